```python
import jax, jax.numpy as jnp
from jax import lax
import numpy as np

D_MODEL = 1024
BATCH = 8
SEQ = 8192
DEPTH = 2

CHUNK = 64
N_MIXERS = 2
N_POOL_LAYERS = (DEPTH + 1) // 2
N_ATT_LAYERS = DEPTH // 2

POOL_WIDTH = 2 * D_MODEL
POOL_WINDOWS = (2, 4, 8, 16)
N_POOL_GROUPS = len(POOL_WINDOWS)
POOL_GROUP = POOL_WIDTH // N_POOL_GROUPS

HEAD_DIM = 64
ATT_WIDTH = D_MODEL
N_HEADS = ATT_WIDTH // HEAD_DIM
LEFT_CHUNKS = 8
BAND = (LEFT_CHUNKS + 1) * CHUNK
MAX_REL = 256

RMS_EPS = 1e-6

kernel_name = "hybrid_pool_chunkattn_sandwich"


def rms_norm(x, g):
    xf = x.astype(jnp.float32)
    y = xf * lax.rsqrt(jnp.mean(xf * xf, axis=-1, keepdims=True) + RMS_EPS)
    return (y * g.astype(jnp.float32)).astype(x.dtype)


def pool_mixer(h, w_in, w_group, scale, w_out):
    b, s, _ = h.shape
    u = h @ w_in
    a, z = jnp.split(u, 2, axis=-1)
    ag = a.astype(jnp.float32).reshape(b, s, N_POOL_GROUPS, POOL_GROUP)
    cs = jnp.cumsum(ag, axis=1)
    cs0 = jnp.concatenate([jnp.zeros((b, 1, N_POOL_GROUPS, POOL_GROUP), jnp.float32), cs], axis=1)
    pos = jnp.arange(s)
    pooled = []
    for gi, w in enumerate(POOL_WINDOWS):
        cg = cs0[:, :, gi]
        lagged = jnp.concatenate([jnp.zeros((b, w, POOL_GROUP), jnp.float32), cg], axis=1)[:, 1:s + 1]
        cnt = jnp.minimum(pos + 1, w).astype(jnp.float32)[None, :, None]
        pooled.append((cg[:, 1:] - lagged) / cnt)
    mixed = (jnp.stack(pooled, axis=2) - ag).astype(a.dtype)
    mixed = jnp.einsum('bsgc,gcd->bsgd', mixed, w_group).reshape(b, s, POOL_WIDTH) * scale
    return (mixed * jax.nn.silu(z)) @ w_out


def chunk_attention(h, w_in, rel_bias, w_out):
    b, s, _ = h.shape
    nc = s // CHUNK
    pad = LEFT_CHUNKS * CHUNK
    u = h @ w_in
    q, k, v, z = jnp.split(u, 4, axis=-1)
    q = q.reshape(b, s, N_HEADS, HEAD_DIM)
    k = k.reshape(b, s, N_HEADS, HEAD_DIM)
    v = v.reshape(b, s, N_HEADS, HEAD_DIM)
    kp = jnp.pad(k, ((0, 0), (pad, 0), (0, 0), (0, 0)))
    vp = jnp.pad(v, ((0, 0), (pad, 0), (0, 0), (0, 0)))
    qc = q.reshape(b, nc, CHUNK, N_HEADS, HEAD_DIM).transpose(1, 0, 2, 3, 4)
    rel = jnp.arange(CHUNK)[:, None] + pad - jnp.arange(BAND)[None, :]
    idx = jnp.clip(rel, -MAX_REL, MAX_REL) + MAX_REL
    bias = rel_bias.astype(jnp.float32)[:, idx]
    qk_scale = HEAD_DIM ** -0.5

    def one_chunk(args):
        c, qb = args
        start = c * CHUNK
        kb = lax.dynamic_slice_in_dim(kp, start, BAND, axis=1)
        vb = lax.dynamic_slice_in_dim(vp, start, BAND, axis=1)
        sc = jnp.einsum('bqhd,bkhd->bhqk', qb, kb).astype(jnp.float32) * qk_scale + bias
        valid = (start - pad + jnp.arange(BAND)) >= 0
        sc = jnp.where(valid[None, None, None, :], sc, -jnp.inf)
        p = jax.nn.softmax(sc, axis=-1).astype(vb.dtype)
        return jnp.einsum('bhqk,bkhd->bqhd', p, vb)

    o = lax.map(one_chunk, (jnp.arange(nc), qc))
    o = o.transpose(1, 0, 2, 3, 4).reshape(b, s, ATT_WIDTH)
    return (o * jax.nn.silu(z)) @ w_out


def _fwd_setup_inputs(seed: int = 0) -> dict:
    key = jax.random.key(seed)
    ks = jax.random.split(key, 12)
    f32 = jnp.float32
    x = jax.random.normal(ks[0], (BATCH, SEQ, D_MODEL), f32)
    norm_pre = 1.0 + 0.05 * jax.random.normal(ks[1], (DEPTH, D_MODEL), f32)
    norm_post = 1.0 + 0.05 * jax.random.normal(ks[2], (DEPTH, D_MODEL), f32)
    pool_w_in = jax.random.normal(ks[3], (N_POOL_LAYERS, D_MODEL, 2 * POOL_WIDTH), f32) * D_MODEL ** -0.5
    pool_w_group = jax.random.normal(ks[4], (N_POOL_LAYERS, N_POOL_GROUPS, POOL_GROUP, POOL_GROUP), f32) * POOL_GROUP ** -0.5
    pool_scale = 1.0 + 0.1 * jax.random.normal(ks[5], (N_POOL_LAYERS, POOL_WIDTH), f32)
    pool_w_out = jax.random.normal(ks[6], (N_POOL_LAYERS, POOL_WIDTH, D_MODEL), f32) * POOL_WIDTH ** -0.5
    att_w_in = jax.random.normal(ks[7], (N_ATT_LAYERS, D_MODEL, 4 * ATT_WIDTH), f32) * D_MODEL ** -0.5
    att_rel_bias = 0.5 * jax.random.normal(ks[8], (N_ATT_LAYERS, N_HEADS, 2 * MAX_REL + 1), f32)
    att_w_out = jax.random.normal(ks[9], (N_ATT_LAYERS, ATT_WIDTH, D_MODEL), f32) * ATT_WIDTH ** -0.5
    return {"x": x, "norm_pre": norm_pre, "norm_post": norm_post,
            "pool_w_in": pool_w_in, "pool_w_group": pool_w_group, "pool_scale": pool_scale,
            "pool_w_out": pool_w_out, "att_w_in": att_w_in, "att_rel_bias": att_rel_bias,
            "att_w_out": att_w_out}


def _fwd_reference(x, norm_pre, norm_post, pool_w_in, pool_w_group, pool_scale, pool_w_out,
              att_w_in, att_rel_bias, att_w_out):
    for i in range(DEPTH):
        h = rms_norm(x, norm_pre[i])
        j = i // N_MIXERS
        if i % N_MIXERS == 0:
            y = pool_mixer(h, pool_w_in[j], pool_w_group[j], pool_scale[j], pool_w_out[j])
        else:
            y = chunk_attention(h, att_w_in[j], att_rel_bias[j], att_w_out[j])
        x = x + rms_norm(y, norm_post[i])
    return x


import jax as _jax
import jax.numpy as _jnp

TWIN_FORMAT = 'train_step'
FWD_PARAMS = ['x', 'norm_pre', 'norm_post', 'pool_w_in', 'pool_w_group', 'pool_scale', 'pool_w_out', 'att_w_in', 'att_rel_bias', 'att_w_out']
TWIN_WEIGHTS = ['norm_pre', 'norm_post', 'pool_w_in', 'pool_w_group', 'pool_scale', 'pool_w_out', 'att_w_in', 'att_rel_bias', 'att_w_out']
TWIN_DIFF_INPUT = 'x'
TWIN_INPUTS = ['x', 'norm_pre', 'norm_post', 'pool_w_in', 'pool_w_group', 'pool_scale', 'pool_w_out', 'att_w_in', 'att_rel_bias', 'att_w_out', 'loss_target', 'm_norm_pre', 'm_norm_post', 'm_pool_w_in', 'm_pool_w_group', 'm_pool_scale', 'm_pool_w_out', 'm_att_w_in', 'm_att_rel_bias', 'm_att_w_out', 'v_norm_pre', 'v_norm_post', 'v_pool_w_in', 'v_pool_w_group', 'v_pool_scale', 'v_pool_w_out', 'v_att_w_in', 'v_att_rel_bias', 'v_att_w_out']
TWIN_OUTPUTS = ['loss', 'grad_x', 'grad_norm_pre', 'grad_norm_post', 'grad_pool_w_in', 'grad_pool_w_group', 'grad_pool_scale', 'grad_pool_w_out', 'grad_att_w_in', 'grad_att_rel_bias', 'grad_att_w_out', 'delta_norm_pre', 'delta_norm_post', 'delta_pool_w_in', 'delta_pool_w_group', 'delta_pool_scale', 'delta_pool_w_out', 'delta_att_w_in', 'delta_att_rel_bias', 'delta_att_w_out', 'new_m_norm_pre', 'new_m_norm_post', 'new_m_pool_w_in', 'new_m_pool_w_group', 'new_m_pool_scale', 'new_m_pool_w_out', 'new_m_att_w_in', 'new_m_att_rel_bias', 'new_m_att_w_out', 'new_v_norm_pre', 'new_v_norm_post', 'new_v_pool_w_in', 'new_v_pool_w_group', 'new_v_pool_scale', 'new_v_pool_w_out', 'new_v_att_w_in', 'new_v_att_rel_bias', 'new_v_att_w_out']
TWIN_LEAF_KINDS = {'loss': 'loss', 'grad_x': 'grad_x', 'grad_norm_pre': 'grad_w', 'grad_norm_post': 'grad_w', 'grad_pool_w_in': 'grad_w', 'grad_pool_w_group': 'grad_w', 'grad_pool_scale': 'grad_w', 'grad_pool_w_out': 'grad_w', 'grad_att_w_in': 'grad_w', 'grad_att_rel_bias': 'grad_w', 'grad_att_w_out': 'grad_w', 'delta_norm_pre': 'delta_w', 'delta_norm_post': 'delta_w', 'delta_pool_w_in': 'delta_w', 'delta_pool_w_group': 'delta_w', 'delta_pool_scale': 'delta_w', 'delta_pool_w_out': 'delta_w', 'delta_att_w_in': 'delta_w', 'delta_att_rel_bias': 'delta_w', 'delta_att_w_out': 'delta_w', 'new_m_norm_pre': 'new_m', 'new_m_norm_post': 'new_m', 'new_m_pool_w_in': 'new_m', 'new_m_pool_w_group': 'new_m', 'new_m_pool_scale': 'new_m', 'new_m_pool_w_out': 'new_m', 'new_m_att_w_in': 'new_m', 'new_m_att_rel_bias': 'new_m', 'new_m_att_w_out': 'new_m', 'new_v_norm_pre': 'new_v', 'new_v_norm_post': 'new_v', 'new_v_pool_w_in': 'new_v', 'new_v_pool_w_group': 'new_v', 'new_v_pool_scale': 'new_v', 'new_v_pool_w_out': 'new_v', 'new_v_att_w_in': 'new_v', 'new_v_att_rel_bias': 'new_v', 'new_v_att_w_out': 'new_v'}


def _forward(args):
    return _fwd_reference(*[args[k] for k in FWD_PARAMS])


def _output_shape():
    def fwd():
        inp = _fwd_setup_inputs(0)
        return _fwd_reference(*[inp[k] for k in FWD_PARAMS])
    out = _jax.eval_shape(fwd)
    return out.shape, out.dtype

N_MICROBATCH = 1
ADAM_LR = 0.001
ADAM_B1 = 0.9
ADAM_B2 = 0.999
ADAM_EPS = 1e-08
ADAM_WD = 0.01
ADAM_STEP = 10
PER_EXAMPLE_BATCH_AXIS = {'x': 0, 'loss_target': 0}
SHARED_INPUTS = []
_WEIGHT_DTYPES = {'norm_pre': _jnp.float32, 'norm_post': _jnp.float32, 'pool_w_in': _jnp.float32, 'pool_w_group': _jnp.float32, 'pool_scale': _jnp.float32, 'pool_w_out': _jnp.float32, 'att_w_in': _jnp.float32, 'att_rel_bias': _jnp.float32, 'att_w_out': _jnp.float32}
MOMENT_SCALE = {'norm_pre': 1.103123e+00, 'norm_post': 6.414676e+01, 'pool_w_in': 5.867444e-01, 'pool_w_group': 6.057581e-01, 'pool_scale': 6.565469e-01, 'pool_w_out': 8.963055e-01, 'att_w_in': 5.297978e-01, 'att_rel_bias': 1.451760e-01, 'att_w_out': 5.958003e-01}


def _to_microbatches(a, axis):
    t = _jnp.moveaxis(a, axis, 0)
    t = t.reshape((N_MICROBATCH, t.shape[0] // N_MICROBATCH) + t.shape[1:])
    return _jnp.moveaxis(t, 1, axis + 1)


def setup_inputs(seed: int = 0) -> dict:
    inp = _fwd_setup_inputs(seed)
    key = _jax.random.fold_in(_jax.random.key(seed), 7919)
    shape, _ = _output_shape()
    out = dict(inp)
    out["loss_target"] = _jax.random.normal(_jax.random.fold_in(key, 0), shape, _jnp.float32)
    for i, name in enumerate(TWIN_WEIGHTS):
        w = inp[name].astype(_jnp.float32)
        if MOMENT_SCALE is None:
            s = _jnp.sqrt(_jnp.mean(_jnp.square(w)) + 1e-30)
        else:
            s = MOMENT_SCALE[name]
        km, kv = _jax.random.split(_jax.random.fold_in(key, i + 1))
        out[name] = w
        out["m_" + name] = s * _jax.random.normal(km, w.shape, _jnp.float32)
        out["v_" + name] = (s * s) * _jax.random.uniform(kv, w.shape, _jnp.float32, 0.5, 1.5)
    if N_MICROBATCH > 1:
        for name, axis in PER_EXAMPLE_BATCH_AXIS.items():
            out[name] = _to_microbatches(out[name], axis)
    return {'x': out['x'], 'norm_pre': out['norm_pre'], 'norm_post': out['norm_post'], 'pool_w_in': out['pool_w_in'], 'pool_w_group': out['pool_w_group'], 'pool_scale': out['pool_scale'], 'pool_w_out': out['pool_w_out'], 'att_w_in': out['att_w_in'], 'att_rel_bias': out['att_rel_bias'], 'att_w_out': out['att_w_out'], 'loss_target': out['loss_target'], 'm_norm_pre': out['m_norm_pre'], 'm_norm_post': out['m_norm_post'], 'm_pool_w_in': out['m_pool_w_in'], 'm_pool_w_group': out['m_pool_w_group'], 'm_pool_scale': out['m_pool_scale'], 'm_pool_w_out': out['m_pool_w_out'], 'm_att_w_in': out['m_att_w_in'], 'm_att_rel_bias': out['m_att_rel_bias'], 'm_att_w_out': out['m_att_w_out'], 'v_norm_pre': out['v_norm_pre'], 'v_norm_post': out['v_norm_post'], 'v_pool_w_in': out['v_pool_w_in'], 'v_pool_w_group': out['v_pool_w_group'], 'v_pool_scale': out['v_pool_scale'], 'v_pool_w_out': out['v_pool_w_out'], 'v_att_w_in': out['v_att_w_in'], 'v_att_rel_bias': out['v_att_rel_bias'], 'v_att_w_out': out['v_att_w_out']}


def _loss(weights, diff, rest, loss_target):
    with _jax.named_scope("forward"):
        args = {**rest, TWIN_DIFF_INPUT: diff, **{k: w.astype(_WEIGHT_DTYPES[k]) for k, w in weights.items()}}
        y = _forward(args)
    with _jax.named_scope("loss_head"):
        err = _jnp.square(y.astype(_jnp.float32) - loss_target)
        return 0.5 * _jnp.sum(_jnp.mean(err, axis=-1)) if err.ndim else 0.5 * err


def _adamw(w, g, m, v):
    m = ADAM_B1 * m + (1.0 - ADAM_B1) * g
    v = ADAM_B2 * v + (1.0 - ADAM_B2) * _jnp.square(g)
    m_hat = m / (1.0 - ADAM_B1 ** ADAM_STEP)
    v_hat = v / (1.0 - ADAM_B2 ** ADAM_STEP)
    delta = -ADAM_LR * (m_hat / (_jnp.sqrt(v_hat) + ADAM_EPS) + ADAM_WD * w)
    return delta, m, v


def reference(x, norm_pre, norm_post, pool_w_in, pool_w_group, pool_scale, pool_w_out, att_w_in, att_rel_bias, att_w_out, loss_target, m_norm_pre, m_norm_post, m_pool_w_in, m_pool_w_group, m_pool_scale, m_pool_w_out, m_att_w_in, m_att_rel_bias, m_att_w_out, v_norm_pre, v_norm_post, v_pool_w_in, v_pool_w_group, v_pool_scale, v_pool_w_out, v_att_w_in, v_att_rel_bias, v_att_w_out):
    given = dict(x=x, norm_pre=norm_pre, norm_post=norm_post, pool_w_in=pool_w_in, pool_w_group=pool_w_group, pool_scale=pool_scale, pool_w_out=pool_w_out, att_w_in=att_w_in, att_rel_bias=att_rel_bias, att_w_out=att_w_out, loss_target=loss_target, m_norm_pre=m_norm_pre, m_norm_post=m_norm_post, m_pool_w_in=m_pool_w_in, m_pool_w_group=m_pool_w_group, m_pool_scale=m_pool_scale, m_pool_w_out=m_pool_w_out, m_att_w_in=m_att_w_in, m_att_rel_bias=m_att_rel_bias, m_att_w_out=m_att_w_out, v_norm_pre=v_norm_pre, v_norm_post=v_norm_post, v_pool_w_in=v_pool_w_in, v_pool_w_group=v_pool_w_group, v_pool_scale=v_pool_scale, v_pool_w_out=v_pool_w_out, v_att_w_in=v_att_w_in, v_att_rel_bias=v_att_rel_bias, v_att_w_out=v_att_w_out)
    weights = {n: given[n] for n in TWIN_WEIGHTS}
    shared = {n: given[n] for n in SHARED_INPUTS}
    per_example = {n: given[n] for n in ['x']}
    grad_fn = _jax.value_and_grad(_loss, argnums=(0, 1))

    def one_microbatch(ex, loss_target):
        ex = dict(ex)
        diff = ex.pop(TWIN_DIFF_INPUT)
        return grad_fn(weights, diff, {**shared, **ex}, loss_target)

    if N_MICROBATCH == 1:
        loss, (grad_w, grad_x) = one_microbatch(per_example, given["loss_target"])
    else:
        def body(carry, xs):
            loss_sum, grad_sum = carry
            l_k, (gw_k, gx_k) = one_microbatch(xs[0], xs[1])
            with _jax.named_scope("update"):
                return (loss_sum + l_k, _jax.tree.map(_jnp.add, grad_sum, gw_k)), gx_k

        init = (_jnp.zeros((), _jnp.float32), _jax.tree.map(_jnp.zeros_like, weights))
        (loss, grad_w), grad_x = _jax.lax.scan(body, init, (per_example, given["loss_target"]))
    with _jax.named_scope("update"):
        delta_w, new_m, new_v = {}, {}, {}
        for n in TWIN_WEIGHTS:
            delta_w[n], new_m[n], new_v[n] = _adamw(weights[n], grad_w[n], given["m_" + n], given["v_" + n])
    return (loss, grad_x, *[grad_w[n] for n in TWIN_WEIGHTS], *[delta_w[n] for n in TWIN_WEIGHTS],
            *[new_m[n] for n in TWIN_WEIGHTS], *[new_v[n] for n in TWIN_WEIGHTS])
```

```python
import functools

import jax
import jax.numpy as jnp
from jax import lax
from jax.experimental import pallas as pl
from jax.experimental.pallas import tpu as pltpu

F32 = jnp.float32
MXU = jnp.bfloat16

D_MODEL = 1024
POOL_WIDTH = 2048
POOL_WINDOWS = (2, 4, 8, 16)
N_GROUPS = 4
GROUP = 512
HALO = 16
N_HEADS = 16
HEAD_DIM = 64
CHUNK = 64
LEFT_CHUNKS = 8
PAD = LEFT_CHUNKS * CHUNK
BAND = PAD + CHUNK
MAX_REL = 256
N_REL = 2 * MAX_REL + 1
REL_PAD = 640
ATT_WIDTH = 1024
N_DEV = 8
W_BLOCK = 512
RMS_EPS = 1e-6
QK_SCALE = 0.125
NEG = -1e30

TM = 256
KB = 3
TKW = KB * TM
ROLL_W = 1024

VMEM_LIMIT = 56 * 1024 * 1024

ADAM_LR = 0.001
ADAM_B1 = 0.9
ADAM_B2 = 0.999
ADAM_EPS = 1e-08
ADAM_WD = 0.01
ADAM_STEP = 10

NT_DIMS = (((1,), (1,)), ((), ()))
TN_DIMS = (((0,), (0,)), ((), ()))


def _params(n_grid):
    return pltpu.CompilerParams(dimension_semantics=("arbitrary",) * n_grid, vmem_limit_bytes=VMEM_LIMIT)


def _const_spec(shape):
    nd = len(shape)
    return pl.BlockSpec(shape, lambda *_: (0,) * nd, pipeline_mode=pl.Buffered(1))


def _dot(a, b):
    return jnp.dot(a, b, preferred_element_type=F32)


def _dot_nt(a, b):
    return lax.dot_general(a, b, NT_DIMS, preferred_element_type=F32)


def _dot_tn(a, b):
    return lax.dot_general(a, b, TN_DIMS, preferred_element_type=F32)


def _sigmoid(z):
    return 1.0 / (1.0 + jnp.exp(-z))


def _rms_fwd(xv):
    r = lax.rsqrt(jnp.mean(xv * xv, axis=-1, keepdims=True) + RMS_EPS)
    return r, xv * r


def _rms_bwd(dn, xhat, r, g):
    dng = dn * g
    return r * (dng - xhat * jnp.mean(dng * xhat, axis=-1, keepdims=True))


def _exchange(items, name):
    n = len(items)
    out_shape = []
    for arr, scatter in items:
        s = arr.shape[1:] if scatter else arr.shape
        out_shape.append(jax.ShapeDtypeStruct((N_DEV,) + tuple(s), arr.dtype))

    def body(*refs):
        ins, outs = refs[:n], refs[n:2 * n]
        send_sems, recv_sems, local_sems = refs[2 * n:]
        x, y, c = lax.axis_index("x"), lax.axis_index("y"), lax.axis_index("c")
        me = 4 * x + 2 * y + c

        def src(t, slot):
            return ins[t].at[slot] if items[t][1] else ins[t]

        local = []
        for t in range(n):
            cp = pltpu.make_async_copy(src(t, me), outs[t].at[me], local_sems.at[t])
            cp.start()
            local.append(cp)
        sends, recvs = [], []
        for k in range(1, N_DEV):
            px = 1 - x if k & 4 else x
            py = 1 - y if k & 2 else y
            pc = 1 - c if k & 1 else c
            peer = 4 * px + 2 * py + pc
            for t in range(n):
                cp = pltpu.make_async_remote_copy(
                    src_ref=src(t, peer), dst_ref=outs[t].at[me],
                    send_sem=send_sems.at[k - 1, t], recv_sem=recv_sems.at[k - 1, t],
                    device_id=(px, py, pc), device_id_type=pl.DeviceIdType.MESH)
                cp.start()
                sends.append(cp)
                recvs.append(pltpu.make_async_remote_copy(
                    src_ref=src(t, peer), dst_ref=outs[t].at[peer],
                    send_sem=send_sems.at[k - 1, t], recv_sem=recv_sems.at[k - 1, t],
                    device_id=(px, py, pc), device_id_type=pl.DeviceIdType.MESH))
        for cp in recvs:
            cp.wait_recv()
        for cp in sends:
            cp.wait_send()
        for cp in local:
            cp.wait()

    any_spec = pl.BlockSpec(memory_space=pl.ANY)
    return pl.pallas_call(
        body, name=name, out_shape=out_shape,
        in_specs=[any_spec] * n, out_specs=[any_spec] * n,
        scratch_shapes=[pltpu.SemaphoreType.DMA((N_DEV - 1, n)), pltpu.SemaphoreType.DMA((N_DEV - 1, n)),
                        pltpu.SemaphoreType.DMA((n,))],
        compiler_params=pltpu.CompilerParams(has_side_effects=True),
    )(*[a for a, _ in items])


def _inv_count(row, window):
    return 1.0 / jnp.minimum(row + 1, window).astype(F32)


def _pool_fwd(x, g_pre, g_post, w_in, w_group, scale, w_out):
    n_tok = x.shape[0]
    nt = n_tok // TM

    def body(x_ref, gpre_ref, gpost_ref, win_ref, wg_ref, sc_ref, wout_ref,
             x1_ref, y_ref, z_ref, mixed_ref, mg_ref, prod_ref, carry_ref):
        i = pl.program_id(0)

        @pl.when(i == 0)
        def _():
            carry_ref[...] = jnp.zeros_like(carry_ref)

        xv = x_ref[...]
        r, xhat = _rms_fwd(xv)
        h = (xhat * gpre_ref[...]).astype(MXU)
        row = i * TM + lax.broadcasted_iota(jnp.int32, (TM, 1), 0)
        y = None
        for g in range(N_GROUPS):
            cols = slice(g * GROUP, (g + 1) * GROUP)
            a = _dot(h, win_ref[g])
            z = _dot(h, win_ref[N_GROUPS + g])
            s = jnp.concatenate([carry_ref[g], a], axis=0)
            carry_ref[g] = a[TM - HALO:, :]
            w = 1
            while w < POOL_WINDOWS[g]:
                s = s + pltpu.roll(s, w, 0)
                w *= 2
            mixed = (s[HALO:, :] * _inv_count(row, POOL_WINDOWS[g]) - a).astype(MXU)
            mg = _dot(mixed, wg_ref[g])
            prod = (mg * sc_ref[:, cols] * (z * _sigmoid(z))).astype(MXU)
            z_ref[:, cols] = z
            mixed_ref[:, cols] = mixed
            mg_ref[:, cols] = mg
            prod_ref[:, cols] = prod
            part = _dot(prod, wout_ref[cols, :])
            y = part if y is None else y + part
        y_ref[...] = y
        _, yhat = _rms_fwd(y)
        x1_ref[...] = xv + yhat * gpost_ref[...]

    tok = lambda w: pl.BlockSpec((TM, w), lambda i: (i, 0))
    return pl.pallas_call(
        body, name="pool_fwd", grid=(nt,),
        in_specs=[tok(D_MODEL), _const_spec((1, D_MODEL)), _const_spec((1, D_MODEL)),
                  _const_spec((N_DEV, D_MODEL, W_BLOCK)), _const_spec((N_GROUPS, GROUP, GROUP)),
                  _const_spec((1, POOL_WIDTH)), _const_spec((POOL_WIDTH, D_MODEL))],
        out_specs=[tok(D_MODEL), tok(D_MODEL), tok(POOL_WIDTH), tok(POOL_WIDTH), tok(POOL_WIDTH), tok(POOL_WIDTH)],
        out_shape=[jax.ShapeDtypeStruct((n_tok, D_MODEL), F32), jax.ShapeDtypeStruct((n_tok, D_MODEL), F32),
                   jax.ShapeDtypeStruct((n_tok, POOL_WIDTH), F32), jax.ShapeDtypeStruct((n_tok, POOL_WIDTH), MXU),
                   jax.ShapeDtypeStruct((n_tok, POOL_WIDTH), F32), jax.ShapeDtypeStruct((n_tok, POOL_WIDTH), MXU)],
        scratch_shapes=[pltpu.VMEM((N_GROUPS, HALO, GROUP), F32)],
        compiler_params=_params(1),
    )(x, g_pre, g_post, w_in, w_group, scale, w_out)


def _pool_bwd(dx1, y, z, mg, mixed, prod, g_post, scale, w_group, w_out):
    n_tok = dx1.shape[0]
    nt = n_tok // TM

    def body(dx1_ref, y_ref, z_ref, mg_ref, mixed_ref, prod_ref, gpost_ref, sc_ref, wg_ref, wout_ref,
             du_ref, dsc_ref, dgpost_ref, dwg_hbm, dwout_hbm, carry_ref, dwg_acc, dwout_acc):
        i = pl.program_id(0)

        @pl.when(i == 0)
        def _():
            carry_ref[...] = jnp.zeros_like(carry_ref)
            dwg_acc[...] = jnp.zeros_like(dwg_acc)
            dwout_acc[...] = jnp.zeros_like(dwout_acc)
            dsc_ref[...] = jnp.zeros_like(dsc_ref)
            dgpost_ref[...] = jnp.zeros_like(dgpost_ref)

        dn = dx1_ref[...]
        r, yhat = _rms_fwd(y_ref[...])
        dgpost_ref[...] += jnp.sum(dn * yhat, axis=0, keepdims=True)
        dy = _rms_bwd(dn, yhat, r, gpost_ref[...]).astype(MXU)
        row = (nt - 1 - i) * TM + lax.broadcasted_iota(jnp.int32, (TM, 1), 0)
        n_ext = TM + HALO
        for g in range(N_GROUPS):
            cols = slice(g * GROUP, (g + 1) * GROUP)
            dwout_acc[cols, :] += _dot_tn(prod_ref[:, cols], dy)
            dprod = _dot_nt(dy, wout_ref[cols, :])
            zv = z_ref[:, cols]
            sig = _sigmoid(zv)
            silu = zv * sig
            mgv = mg_ref[:, cols]
            sc = sc_ref[:, cols]
            dsc_ref[:, cols] += jnp.sum(dprod * silu * mgv, axis=0, keepdims=True)
            dmg = (dprod * silu * sc).astype(MXU)
            dz = dprod * (mgv * sc) * (sig * (1.0 + zv * (1.0 - sig)))
            dwg_acc[g] += _dot_tn(mixed_ref[:, cols], dmg)
            dmixed = _dot_nt(dmg, wg_ref[g])
            e = dmixed * _inv_count(row, POOL_WINDOWS[g])
            s = jnp.concatenate([e, carry_ref[g]], axis=0)
            carry_ref[g] = e[:HALO, :]
            w = 1
            while w < POOL_WINDOWS[g]:
                s = s + pltpu.roll(s, n_ext - w, 0)
                w *= 2
            du_ref[:, cols] = (s[:TM, :] - dmixed).astype(MXU)
            du_ref[:, POOL_WIDTH + g * GROUP:POOL_WIDTH + (g + 1) * GROUP] = dz.astype(MXU)

        @pl.when(i == nt - 1)
        def _():
            pltpu.sync_copy(dwg_acc, dwg_hbm)
            pltpu.sync_copy(dwout_acc, dwout_hbm)

    rev = lambda w: pl.BlockSpec((TM, w), lambda i: (nt - 1 - i, 0))
    any_spec = pl.BlockSpec(memory_space=pl.ANY)
    return pl.pallas_call(
        body, name="pool_bwd", grid=(nt,),
        in_specs=[rev(D_MODEL), rev(D_MODEL), rev(POOL_WIDTH), rev(POOL_WIDTH), rev(POOL_WIDTH), rev(POOL_WIDTH),
                  _const_spec((1, D_MODEL)), _const_spec((1, POOL_WIDTH)),
                  _const_spec((N_GROUPS, GROUP, GROUP)), _const_spec((POOL_WIDTH, D_MODEL))],
        out_specs=[rev(2 * POOL_WIDTH), pl.BlockSpec((1, POOL_WIDTH), lambda i: (0, 0)),
                   pl.BlockSpec((1, D_MODEL), lambda i: (0, 0)), any_spec, any_spec],
        out_shape=[jax.ShapeDtypeStruct((n_tok, 2 * POOL_WIDTH), MXU), jax.ShapeDtypeStruct((1, POOL_WIDTH), F32),
                   jax.ShapeDtypeStruct((1, D_MODEL), F32), jax.ShapeDtypeStruct((N_GROUPS, GROUP, GROUP), F32),
                   jax.ShapeDtypeStruct((POOL_WIDTH, D_MODEL), F32)],
        scratch_shapes=[pltpu.VMEM((N_GROUPS, HALO, GROUP), F32), pltpu.VMEM((N_GROUPS, GROUP, GROUP), F32),
                        pltpu.VMEM((POOL_WIDTH, D_MODEL), F32)],
        compiler_params=_params(1),
    )(dx1, y, z, mg, mixed, prod, g_post, scale, w_group, w_out)


def _in_proj_bwd(parts, x, dres, g_pre, w_in, name):
    n_tok = x.shape[0]
    nt = n_tok // TM
    half = D_MODEL // W_BLOCK

    def body(p0, p1, p2, p3, x_ref, dres_ref, g_ref, w_ref, dx_ref, dg_ref, dw_hbm, acc_ref):
        i = pl.program_id(0)

        @pl.when(i == 0)
        def _():
            acc_ref[...] = jnp.zeros_like(acc_ref)
            dg_ref[...] = jnp.zeros_like(dg_ref)

        r, xhat = _rms_fwd(x_ref[...])
        g = g_ref[...]
        h = (xhat * g).astype(MXU)
        dh = None
        for p, part_ref in enumerate((p0, p1, p2, p3)):
            for jj in range(half):
                j = half * p + jj
                du = part_ref[:, jj * W_BLOCK:(jj + 1) * W_BLOCK]
                t = _dot_nt(du, w_ref[j])
                dh = t if dh is None else dh + t
                acc_ref[j] += _dot_tn(h, du)
        dg_ref[...] += jnp.sum(dh * xhat, axis=0, keepdims=True)
        dx_ref[...] = dres_ref[...] + _rms_bwd(dh, xhat, r, g)

        @pl.when(i == nt - 1)
        def _():
            pltpu.sync_copy(acc_ref, dw_hbm)

    tok = pl.BlockSpec((TM, D_MODEL), lambda i: (i, 0))
    return pl.pallas_call(
        body, name=name, grid=(nt,),
        in_specs=[pl.BlockSpec((TM, D_MODEL), m) for _, m in parts]
        + [tok, tok, _const_spec((1, D_MODEL)), _const_spec((N_DEV, D_MODEL, W_BLOCK))],
        out_specs=[tok, pl.BlockSpec((1, D_MODEL), lambda i: (0, 0)), pl.BlockSpec(memory_space=pl.ANY)],
        out_shape=[jax.ShapeDtypeStruct((n_tok, D_MODEL), F32), jax.ShapeDtypeStruct((1, D_MODEL), F32),
                   jax.ShapeDtypeStruct((N_DEV, D_MODEL, W_BLOCK), F32)],
        scratch_shapes=[pltpu.VMEM((N_DEV, D_MODEL, W_BLOCK), F32)],
        compiler_params=_params(1),
    )(*[a for a, _ in parts], x, dres, g_pre, w_in)


def _rel_onehot():
    rel = lax.broadcasted_iota(jnp.int32, (REL_PAD, ROLL_W), 0)
    col = lax.broadcasted_iota(jnp.int32, (REL_PAD, ROLL_W), 1)
    return (rel == jnp.minimum(BAND + MAX_REL - col, 2 * MAX_REL)).astype(MXU)


def _split3(v):
    hi = v.astype(MXU)
    r1 = v - hi.astype(F32)
    mid = r1.astype(MXU)
    lo = (r1 - mid.astype(F32)).astype(MXU)
    return hi, mid, lo


def _bias_table(rel_bias_padded):
    def body(rb_ref, out_ref):
        onehot = _rel_onehot()
        base = None
        for term in _split3(rb_ref[...]):
            t = _dot(term, onehot)
            base = t if base is None else base + t
        qi = lax.broadcasted_iota(jnp.int32, (CHUNK, ROLL_W), 0)
        kk = lax.broadcasted_iota(jnp.int32, (CHUNK, TKW), 1)
        for h in range(N_HEADS):
            t = jnp.broadcast_to(base[h:h + 1, :], (CHUNK, ROLL_W))
            for bit in range(6):
                t = jnp.where(((qi >> bit) & 1) == 1, pltpu.roll(t, 1 << bit, 1), t)
            for rr in range(TM // CHUNK):
                shifted = pltpu.roll(t, (CHUNK * rr - CHUNK) % ROLL_W, 1)[:, :TKW]
                band = kk - CHUNK * rr
                out_ref[h, rr * CHUNK:(rr + 1) * CHUNK, :] = jnp.where((band >= 0) & (band < BAND), shifted, NEG)

    return pl.pallas_call(
        body, name="bias_table", out_shape=jax.ShapeDtypeStruct((N_HEADS, TM, TKW), F32),
        compiler_params=pltpu.CompilerParams(vmem_limit_bytes=VMEM_LIMIT),
    )(rel_bias_padded)


def _bias_grad(dtab):
    def body(dt_ref, out_ref, dbase_ref):
        qi = lax.broadcasted_iota(jnp.int32, (CHUNK, ROLL_W), 0)
        zeros = jnp.zeros((CHUNK, ROLL_W - TKW), F32)
        for h in range(N_HEADS):
            t = None
            for rr in range(TM // CHUNK):
                blk = jnp.concatenate([dt_ref[h, rr * CHUNK:(rr + 1) * CHUNK, :], zeros], axis=1)
                blk = pltpu.roll(blk, (CHUNK - CHUNK * rr) % ROLL_W, 1)
                t = blk if t is None else t + blk
            for bit in range(6):
                t = jnp.where(((qi >> bit) & 1) == 1, pltpu.roll(t, ROLL_W - (1 << bit), 1), t)
            dbase_ref[h:h + 1, :] = jnp.sum(t, axis=0, keepdims=True)
        onehot = _rel_onehot()
        acc = None
        for term in _split3(dbase_ref[...]):
            t = _dot_nt(term, onehot)
            acc = t if acc is None else acc + t
        out_ref[...] = acc

    return pl.pallas_call(
        body, name="bias_grad", out_shape=jax.ShapeDtypeStruct((N_HEADS, REL_PAD), F32),
        scratch_shapes=[pltpu.VMEM((N_HEADS, ROLL_W), F32)],
        compiler_params=pltpu.CompilerParams(vmem_limit_bytes=VMEM_LIMIT),
    )(dtab)


def _att_in(x1, g_pre, w_in):
    n_tok = x1.shape[0]
    nt = n_tok // TM
    lead = PAD // TM

    def body(x_ref, g_ref, w_ref, q_ref, k_ref, v_ref, z_ref):
        i = pl.program_id(0)

        @pl.when(i < lead)
        def _():
            k_ref[...] = jnp.zeros_like(k_ref)
            v_ref[...] = jnp.zeros_like(v_ref)

        @pl.when(i >= lead)
        def _():
            _, xhat = _rms_fwd(x_ref[...])
            h = (xhat * g_ref[...]).astype(MXU)
            for j in range(N_DEV):
                u = _dot(h, w_ref[j])
                cols = slice((j % 2) * W_BLOCK, (j % 2 + 1) * W_BLOCK)
                if j < 2:
                    q_ref[:, cols] = (u * QK_SCALE).astype(MXU)
                elif j < 4:
                    k_ref[:, cols] = u.astype(MXU)
                elif j < 6:
                    v_ref[:, cols] = u.astype(MXU)
                else:
                    z_ref[:, cols] = u

    late = pl.BlockSpec((TM, D_MODEL), lambda i: (jnp.maximum(i - lead, 0), 0))
    padded = pl.BlockSpec((TM, D_MODEL), lambda i: (i, 0))
    return pl.pallas_call(
        body, name="att_in", grid=(nt + lead,),
        in_specs=[late, _const_spec((1, D_MODEL)), _const_spec((N_DEV, D_MODEL, W_BLOCK))],
        out_specs=[late, padded, padded, late],
        out_shape=[jax.ShapeDtypeStruct((n_tok, ATT_WIDTH), MXU), jax.ShapeDtypeStruct((n_tok + PAD, ATT_WIDTH), MXU),
                   jax.ShapeDtypeStruct((n_tok + PAD, ATT_WIDTH), MXU), jax.ShapeDtypeStruct((n_tok, ATT_WIDTH), F32)],
        compiler_params=_params(1),
    )(x1, g_pre, w_in)


def _head_masks():
    lane = lax.broadcasted_iota(jnp.int32, (1, 2 * HEAD_DIM), 1)
    first = lane < HEAD_DIM
    return first, jnp.logical_not(first)


def _softmax_blocks(qh, k_refs, bias_ref, hh, tile):
    ss = []
    for b in range(KB):
        s = _dot_nt(qh, k_refs[b][...]) + bias_ref[hh, :, b * TM:(b + 1) * TM]
        ss.append(s + jnp.where(tile + b < PAD // TM, NEG, 0.0).astype(F32))
    m = None
    for s in ss:
        mb = jnp.max(s, axis=-1, keepdims=True)
        m = mb if m is None else jnp.maximum(m, mb)
    ps = [jnp.exp(s - m) for s in ss]
    l = None
    for p in ps:
        lb = jnp.sum(p, axis=-1, keepdims=True)
        l = lb if l is None else l + lb
    inv = 1.0 / l
    return [p * inv for p in ps]


def _key_specs():
    return [pl.BlockSpec((TM, 2 * HEAD_DIM), functools.partial(lambda hp, i, b: (i + b, hp), b=b)) for b in range(KB)]


def _att_fwd(q, kpad, vpad, bias_tab):
    n_tok = q.shape[0]
    nt = n_tok // TM

    def body(q_ref, k0, k1, k2, v0, v1, v2, bias_ref, o_ref):
        i = pl.program_id(1)
        qv = q_ref[...]
        masks = _head_masks()
        outs = []
        for hh in range(2):
            qh = jnp.where(masks[hh], qv, jnp.zeros_like(qv))
            ps = _softmax_blocks(qh, (k0, k1, k2), bias_ref, hh, i)
            o = None
            for b, v_ref in enumerate((v0, v1, v2)):
                t = _dot(ps[b].astype(MXU), v_ref[...])
                o = t if o is None else o + t
            outs.append(o)
        o_ref[...] = jnp.where(masks[0], outs[0], outs[1])

    qspec = pl.BlockSpec((TM, 2 * HEAD_DIM), lambda hp, i: (i, hp))
    return pl.pallas_call(
        body, name="att_fwd", grid=(N_HEADS // 2, nt),
        in_specs=[qspec] + _key_specs() + _key_specs() + [pl.BlockSpec((2, TM, TKW), lambda hp, i: (hp, 0, 0))],
        out_specs=qspec,
        out_shape=jax.ShapeDtypeStruct((n_tok, ATT_WIDTH), F32),
        compiler_params=_params(2),
    )(q, kpad, kpad, kpad, vpad, vpad, vpad, bias_tab)


def _att_bwd(q, kpad, vpad, do, bias_tab):
    n_tok = q.shape[0]
    nt = n_tok // TM

    def body(q_ref, do_ref, k0, k1, k2, v0, v1, v2, bias_ref,
             dq_ref, dk_ref, dv_ref, dtab_ref, rk0, rk1, rv0, rv1):
        i = pl.program_id(1)

        @pl.when(i == 0)
        def _():
            for ref in (rk0, rk1, rv0, rv1):
                ref[...] = jnp.zeros_like(ref)
            dtab_ref[...] = jnp.zeros_like(dtab_ref)

        @pl.when(i < nt)
        def _():
            qv = q_ref[...]
            dov = do_ref[...]
            masks = _head_masks()
            dks = [None] * KB
            dvs = [None] * KB
            dqs = []
            for hh in range(2):
                qh = jnp.where(masks[hh], qv, jnp.zeros_like(qv))
                doh = jnp.where(masks[hh], dov, jnp.zeros_like(dov))
                ps = _softmax_blocks(qh, (k0, k1, k2), bias_ref, hh, i)
                dps = [_dot_nt(doh, v_ref[...]) for v_ref in (v0, v1, v2)]
                dsum = None
                for p, dp in zip(ps, dps):
                    t = jnp.sum(p * dp, axis=-1, keepdims=True)
                    dsum = t if dsum is None else dsum + t
                dq = None
                for b, k_ref in enumerate((k0, k1, k2)):
                    ds = ps[b] * (dps[b] - dsum)
                    dtab_ref[hh, :, b * TM:(b + 1) * TM] += ds
                    dsb = ds.astype(MXU)
                    t = _dot(dsb, k_ref[...])
                    dq = t if dq is None else dq + t
                    t = _dot_tn(dsb, qh)
                    dks[b] = t if dks[b] is None else dks[b] + t
                    t = _dot_tn(ps[b].astype(MXU), doh)
                    dvs[b] = t if dvs[b] is None else dvs[b] + t
                dqs.append(dq)
            dq_ref[...] = (jnp.where(masks[0], dqs[0], dqs[1]) * QK_SCALE).astype(MXU)
            dk_ref[...] = (rk0[...] + dks[0]).astype(MXU)
            dv_ref[...] = (rv0[...] + dvs[0]).astype(MXU)
            rk0[...] = rk1[...] + dks[1]
            rv0[...] = rv1[...] + dvs[1]
            rk1[...] = dks[2]
            rv1[...] = dvs[2]

        @pl.when(i >= nt)
        def _():
            dk_ref[...] = rk0[...].astype(MXU)
            dv_ref[...] = rv0[...].astype(MXU)
            rk0[...] = rk1[...]
            rv0[...] = rv1[...]

    last = nt - 1
    qspec = pl.BlockSpec((TM, 2 * HEAD_DIM), lambda hp, i: (jnp.minimum(i, last), hp))
    kspecs = [pl.BlockSpec((TM, 2 * HEAD_DIM), functools.partial(lambda hp, i, b: (jnp.minimum(i, last) + b, hp), b=b))
              for b in range(KB)]
    pspec = pl.BlockSpec((TM, 2 * HEAD_DIM), lambda hp, i: (i, hp))
    tspec = pl.BlockSpec((2, TM, TKW), lambda hp, i: (hp, 0, 0))
    ring = pltpu.VMEM((TM, 2 * HEAD_DIM), F32)
    return pl.pallas_call(
        body, name="att_bwd", grid=(N_HEADS // 2, nt + KB - 1),
        in_specs=[qspec, qspec] + kspecs + kspecs + [tspec],
        out_specs=[qspec, pspec, pspec, tspec],
        out_shape=[jax.ShapeDtypeStruct((n_tok, ATT_WIDTH), MXU), jax.ShapeDtypeStruct((n_tok + PAD, ATT_WIDTH), MXU),
                   jax.ShapeDtypeStruct((n_tok + PAD, ATT_WIDTH), MXU), jax.ShapeDtypeStruct((N_HEADS, TM, TKW), F32)],
        scratch_shapes=[ring, ring, ring, ring],
        compiler_params=_params(2),
    )(q, do, kpad, kpad, kpad, vpad, vpad, vpad, bias_tab)


def _att_out(o, z, x1, target, g_post, w_out):
    n_tok = o.shape[0]
    nt = n_tok // TM

    def body(o_ref, z_ref, x1_ref, tgt_ref, gpost_ref, w_ref,
             loss_ref, dx2_ref, do_ref, dz_ref, dgpost_ref, dw_hbm, acc_ref, loss_acc):
        i = pl.program_id(0)

        @pl.when(i == 0)
        def _():
            acc_ref[...] = jnp.zeros_like(acc_ref)
            loss_acc[...] = jnp.zeros_like(loss_acc)
            dgpost_ref[...] = jnp.zeros_like(dgpost_ref)

        ov = o_ref[...]
        zv = z_ref[...]
        sig = _sigmoid(zv)
        silu = zv * sig
        gated = (ov * silu).astype(MXU)
        y = _dot(gated, w_ref[...])
        r, yhat = _rms_fwd(y)
        gpost = gpost_ref[...]
        diff = x1_ref[...] + yhat * gpost - tgt_ref[...]
        loss_acc[...] += jnp.sum(diff * diff, axis=0, keepdims=True)
        dn = diff * (1.0 / D_MODEL)
        dx2_ref[...] = dn
        dgpost_ref[...] += jnp.sum(dn * yhat, axis=0, keepdims=True)
        dy = _rms_bwd(dn, yhat, r, gpost).astype(MXU)
        acc_ref[...] += _dot_tn(gated, dy)
        dgated = _dot_nt(dy, w_ref[...])
        do_ref[...] = (dgated * silu).astype(MXU)
        dz_ref[...] = (dgated * ov * (sig * (1.0 + zv * (1.0 - sig)))).astype(MXU)

        @pl.when(i == nt - 1)
        def _():
            total = jnp.sum(loss_acc[...], axis=-1, keepdims=True) * (0.5 / D_MODEL)
            loss_ref[...] = jnp.broadcast_to(total, loss_ref.shape)
            pltpu.sync_copy(acc_ref, dw_hbm)

    tok = pl.BlockSpec((TM, D_MODEL), lambda i: (i, 0))
    return pl.pallas_call(
        body, name="att_out", grid=(nt,),
        in_specs=[tok, tok, tok, tok, _const_spec((1, D_MODEL)), _const_spec((ATT_WIDTH, D_MODEL))],
        out_specs=[pl.BlockSpec((1, 128), lambda i: (0, 0)), tok, tok, tok,
                   pl.BlockSpec((1, D_MODEL), lambda i: (0, 0)), pl.BlockSpec(memory_space=pl.ANY)],
        out_shape=[jax.ShapeDtypeStruct((1, 128), F32), jax.ShapeDtypeStruct((n_tok, D_MODEL), F32),
                   jax.ShapeDtypeStruct((n_tok, ATT_WIDTH), MXU), jax.ShapeDtypeStruct((n_tok, ATT_WIDTH), MXU),
                   jax.ShapeDtypeStruct((1, D_MODEL), F32), jax.ShapeDtypeStruct((ATT_WIDTH, D_MODEL), F32)],
        scratch_shapes=[pltpu.VMEM((ATT_WIDTH, D_MODEL), F32), pltpu.VMEM((1, D_MODEL), F32)],
        compiler_params=_params(1),
    )(o, z, x1, target, g_post, w_out)


def _adamw(parts, w, m, v, name):
    rows, cols = w.shape
    tr = min(rows, 256)

    def body(p_ref, w_ref, m_ref, v_ref, g_ref, d_ref, mo_ref, vo_ref):
        g = p_ref[0]
        for s in range(1, N_DEV):
            g = g + p_ref[s]
        m_new = ADAM_B1 * m_ref[...] + (1.0 - ADAM_B1) * g
        v_new = ADAM_B2 * v_ref[...] + (1.0 - ADAM_B2) * (g * g)
        m_hat = m_new / (1.0 - ADAM_B1 ** ADAM_STEP)
        v_hat = v_new / (1.0 - ADAM_B2 ** ADAM_STEP)
        g_ref[...] = g
        d_ref[...] = -ADAM_LR * (m_hat / (jnp.sqrt(v_hat) + ADAM_EPS) + ADAM_WD * w_ref[...])
        mo_ref[...] = m_new
        vo_ref[...] = v_new

    blk = pl.BlockSpec((tr, cols), lambda i: (i, 0))
    shape = jax.ShapeDtypeStruct((rows, cols), F32)
    return pl.pallas_call(
        body, name=name, grid=(rows // tr,),
        in_specs=[pl.BlockSpec((N_DEV, tr, cols), lambda i: (0, i, 0)), blk, blk, blk],
        out_specs=[blk, blk, blk, blk], out_shape=[shape, shape, shape, shape],
        compiler_params=_params(1),
    )(parts, w, m, v)


SMALL_ROWS = 16


def _pack_small(norm_pre, norm_post, pool_scale, rel_bias_padded):
    return jnp.concatenate([norm_pre, norm_post, pool_scale.reshape(2, D_MODEL),
                            rel_bias_padded.reshape(SMALL_ROWS - 6, D_MODEL)], axis=0)


def _unpack_small(packed):
    rel = packed[6:].reshape(N_HEADS, REL_PAD)[:, :N_REL]
    return packed[0:2], packed[2:4], packed[4:6].reshape(1, POOL_WIDTH), rel.reshape(1, N_HEADS, N_REL)


def _pad_rel(rel_bias):
    return jnp.pad(rel_bias.reshape(N_HEADS, N_REL), ((0, 0), (0, REL_PAD - N_REL)))


def _local_grads(xt, target, norm_pre, norm_post, pool_scale, rel_padded, w_in_p, w_group, w_out_p, w_in_a, w_out_a):
    lead = PAD // TM
    bias_tab = _bias_table(rel_padded)

    x1, y0, z0, mixed, mg, prod = _pool_fwd(xt, norm_pre[0:1], norm_post[0:1], w_in_p, w_group, pool_scale, w_out_p)
    q, kpad, vpad, z1 = _att_in(x1, norm_pre[1:2], w_in_a)
    o = _att_fwd(q, kpad, vpad, bias_tab)
    loss_part, dx2, do, dz1, d_gpost1, d_w_out_a = _att_out(o, z1, x1, target, norm_post[1:2], w_out_a)
    dq, dkpad, dvpad, dtab = _att_bwd(q, kpad, vpad, do, bias_tab)
    d_rel = _bias_grad(dtab)
    plain = lambda i: (i, 0)
    shifted = lambda i: (i + lead, 0)
    dx1, d_gpre1, d_w_in_a = _in_proj_bwd([(dq, plain), (dkpad, shifted), (dvpad, shifted), (dz1, plain)],
                                          x1, dx2, norm_pre[1:2], w_in_a, "att_in_bwd")
    du0, d_scale, d_gpost0, d_w_group, d_w_out_p = _pool_bwd(dx1, y0, z0, mg, mixed, prod, norm_post[0:1],
                                                             pool_scale, w_group, w_out_p)
    col = lambda p: (lambda i: (i, p))
    grad_x, d_gpre0, d_w_in_p = _in_proj_bwd([(du0, col(p)) for p in range(4)], xt, dx1, norm_pre[0:1], w_in_p,
                                             "pool_in_bwd")
    d_small = _pack_small(jnp.concatenate([d_gpre0, d_gpre1], axis=0), jnp.concatenate([d_gpost0, d_gpost1], axis=0),
                          d_scale, d_rel)
    return loss_part, grad_x, d_small, d_w_in_a, d_w_out_a, d_w_group, d_w_out_p, d_w_in_p


def kernel(x, norm_pre, norm_post, pool_w_in, pool_w_group, pool_scale, pool_w_out, att_w_in, att_rel_bias, att_w_out, loss_target, m_norm_pre, m_norm_post, m_pool_w_in, m_pool_w_group, m_pool_scale, m_pool_w_out, m_att_w_in, m_att_rel_bias, m_att_w_out, v_norm_pre, v_norm_post, v_pool_w_in, v_pool_w_group, v_pool_scale, v_pool_w_out, v_att_w_in, v_att_rel_bias, v_att_w_out):
    xt = x[0]
    target = loss_target[0]
    n_tok = xt.shape[0]

    gathered = _exchange([(pool_w_in[0].astype(MXU), False), (pool_w_group[0].astype(MXU), False),
                          (pool_w_out[0].astype(MXU), False), (att_w_in[0].astype(MXU), False),
                          (att_w_out[0].astype(MXU), False)], "gather_weights")
    w_in_p = gathered[0]
    w_group = gathered[1].transpose(1, 0, 2, 3).reshape(N_GROUPS, GROUP, GROUP)
    w_out_p = gathered[2].reshape(POOL_WIDTH, D_MODEL)
    w_in_a = gathered[3]
    w_out_a = gathered[4].reshape(ATT_WIDTH, D_MODEL)

    rel_padded = _pad_rel(att_rel_bias[0])
    (loss_part, grad_x, d_small, d_w_in_a, d_w_out_a, d_w_group, d_w_out_p, d_w_in_p) = _local_grads(
        xt, target, norm_pre, norm_post, pool_scale, rel_padded, w_in_p, w_group, w_out_p, w_in_a, w_out_a)

    rows_g = GROUP // N_DEV
    parts = _exchange([(d_w_in_a, True), (d_w_out_a.reshape(N_DEV, ATT_WIDTH // N_DEV, D_MODEL), True),
                       (d_w_group.reshape(N_GROUPS, N_DEV, rows_g, GROUP).transpose(1, 0, 2, 3), True),
                       (d_w_out_p.reshape(N_DEV, POOL_WIDTH // N_DEV, D_MODEL), True),
                       (d_w_in_p, True), (d_small, False)], "exchange_grads")

    def update(part, w, m, v, name):
        shape = w.shape
        flat = lambda a: a.reshape(-1, shape[-1])
        outs = _adamw(part.reshape(N_DEV, -1, shape[-1]), flat(w), flat(m), flat(v), name)
        return [a.reshape(shape) for a in outs]

    u_att_w_in = update(parts[0], att_w_in, m_att_w_in, v_att_w_in, "adamw_att_w_in")
    u_att_w_out = update(parts[1], att_w_out, m_att_w_out, v_att_w_out, "adamw_att_w_out")
    u_pool_w_group = update(parts[2], pool_w_group, m_pool_w_group, v_pool_w_group, "adamw_pool_w_group")
    u_pool_w_out = update(parts[3], pool_w_out, m_pool_w_out, v_pool_w_out, "adamw_pool_w_out")
    u_pool_w_in = update(parts[4], pool_w_in, m_pool_w_in, v_pool_w_in, "adamw_pool_w_in")
    small = _adamw(parts[5], _pack_small(norm_pre, norm_post, pool_scale, rel_padded),
                   _pack_small(m_norm_pre, m_norm_post, m_pool_scale, _pad_rel(m_att_rel_bias[0])),
                   _pack_small(v_norm_pre, v_norm_post, v_pool_scale, _pad_rel(v_att_rel_bias[0])), "adamw_small")
    u_small = [_unpack_small(a) for a in small]

    loss = lax.psum(loss_part[0, 0], ("x", "y", "c"))
    outs = [loss, grad_x.reshape(1, n_tok, D_MODEL)]
    for kind in range(4):
        outs += [u_small[kind][0], u_small[kind][1], u_pool_w_in[kind], u_pool_w_group[kind], u_small[kind][2],
                 u_pool_w_out[kind], u_att_w_in[kind], u_small[kind][3], u_att_w_out[kind]]
    return tuple(outs)
```

```python
import functools

import jax
import jax.numpy as jnp
from jax import lax
from jax.experimental import pallas as pl
from jax.experimental.pallas import tpu as pltpu

F32 = jnp.float32
MXU = jnp.bfloat16

D_MODEL = 1024
POOL_WIDTH = 2048
POOL_WINDOWS = (2, 4, 8, 16)
N_GROUPS = 4
GROUP = 512
HALO = 16
N_HEADS = 16
HEAD_DIM = 64
CHUNK = 64
LEFT_CHUNKS = 8
PAD = LEFT_CHUNKS * CHUNK
BAND = PAD + CHUNK
MAX_REL = 256
N_REL = 2 * MAX_REL + 1
REL_PAD = 640
ATT_WIDTH = 1024
N_DEV = 8
W_BLOCK = 512
RMS_EPS = 1e-6
QK_SCALE = 0.125
NEG = -1e30

TM = 256
KB = 3
TKW = KB * TM
ROLL_W = 1024

VMEM_LIMIT = 56 * 1024 * 1024

ADAM_LR = 0.001
ADAM_B1 = 0.9
ADAM_B2 = 0.999
ADAM_EPS = 1e-08
ADAM_WD = 0.01
ADAM_STEP = 10

NT_DIMS = (((1,), (1,)), ((), ()))
TN_DIMS = (((0,), (0,)), ((), ()))


def _params(n_grid):
    return pltpu.CompilerParams(dimension_semantics=("arbitrary",) * n_grid, vmem_limit_bytes=VMEM_LIMIT)


def _const_spec(shape):
    nd = len(shape)
    return pl.BlockSpec(shape, lambda *_: (0,) * nd, pipeline_mode=pl.Buffered(1))


def _dot(a, b):
    return jnp.dot(a, b, preferred_element_type=F32)


def _dot_nt(a, b):
    return lax.dot_general(a, b, NT_DIMS, preferred_element_type=F32)


def _dot_tn(a, b):
    return lax.dot_general(a, b, TN_DIMS, preferred_element_type=F32)


def _sigmoid(z):
    return 1.0 / (1.0 + jnp.exp(-z))


def _rms_fwd(xv):
    r = lax.rsqrt(jnp.mean(xv * xv, axis=-1, keepdims=True) + RMS_EPS)
    return r, xv * r


def _rms_bwd(dn, xhat, r, g):
    dng = dn * g
    return r * (dng - xhat * jnp.mean(dng * xhat, axis=-1, keepdims=True))


class _Exchange:
    def __init__(self, items):
        self.arrays = [a for a, _ in items]
        self.scatter = [s for _, s in items]
        self.n = len(items)
        self.out_shape = [jax.ShapeDtypeStruct((N_DEV,) + tuple(a.shape[1:] if s else a.shape), a.dtype)
                          for a, s in items]
        self.specs = [pl.BlockSpec(memory_space=pl.ANY)] * self.n
        self.scratch = ([pltpu.SemaphoreType.DMA((N_DEV - 1, self.n)), pltpu.SemaphoreType.DMA((N_DEV - 1, self.n)),
                         pltpu.SemaphoreType.DMA((self.n,))] if self.n else [])

    def _copies(self, ins, outs, sems, with_receives):
        send_sems, recv_sems, local_sems = sems
        x, y, c = lax.axis_index("x"), lax.axis_index("y"), lax.axis_index("c")
        me = 4 * x + 2 * y + c

        def src(t, slot):
            return ins[t].at[slot] if self.scatter[t] else ins[t]

        local = [pltpu.make_async_copy(src(t, me), outs[t].at[me], local_sems.at[t]) for t in range(self.n)]
        sends, recvs = [], []
        for k in range(1, N_DEV):
            px = 1 - x if k & 4 else x
            py = 1 - y if k & 2 else y
            pc = 1 - c if k & 1 else c
            peer = 4 * px + 2 * py + pc
            for t in range(self.n):
                common = dict(src_ref=src(t, peer), send_sem=send_sems.at[k - 1, t], recv_sem=recv_sems.at[k - 1, t],
                              device_id=(px, py, pc), device_id_type=pl.DeviceIdType.MESH)
                sends.append(pltpu.make_async_remote_copy(dst_ref=outs[t].at[me], **common))
                if with_receives:
                    recvs.append(pltpu.make_async_remote_copy(dst_ref=outs[t].at[peer], **common))
        return local, sends, recvs

    def start(self, ins, outs, sems):
        if self.n:
            local, sends, _ = self._copies(ins, outs, sems, False)
            for cp in local + sends:
                cp.start()

    def wait(self, ins, outs, sems):
        if self.n:
            local, sends, recvs = self._copies(ins, outs, sems, True)
            for cp in recvs:
                cp.wait_recv()
            for cp in sends:
                cp.wait_send()
            for cp in local:
                cp.wait()


def _exchange(items, name):
    ex = _Exchange(items)
    n = ex.n

    def body(*refs):
        ins, outs, sems = refs[:n], refs[n:2 * n], refs[2 * n:]
        ex.start(ins, outs, sems)
        ex.wait(ins, outs, sems)

    return pl.pallas_call(
        body, name=name, out_shape=ex.out_shape, in_specs=ex.specs, out_specs=ex.specs, scratch_shapes=ex.scratch,
        compiler_params=pltpu.CompilerParams(has_side_effects=True),
    )(*ex.arrays)


def _inv_count(row, window):
    return 1.0 / jnp.minimum(row + 1, window).astype(F32)


def _pool_fwd(x, g_pre, g_post, w_in, w_group, scale, w_out, exchange_items):
    n_tok = x.shape[0]
    nt = n_tok // TM
    ex = _Exchange(exchange_items)

    def body(x_ref, gpre_ref, gpost_ref, win_ref, wg_ref, sc_ref, wout_ref, *rest):
        ex_in, rest = rest[:ex.n], rest[ex.n:]
        x1_ref, y_ref, z_ref, mixed_ref, mg_ref, prod_ref = rest[:6]
        ex_out, carry_ref, ex_sems = rest[6:6 + ex.n], rest[6 + ex.n], rest[7 + ex.n:]
        i = pl.program_id(0)

        @pl.when(i == 0)
        def _():
            ex.start(ex_in, ex_out, ex_sems)
            carry_ref[...] = jnp.zeros_like(carry_ref)

        xv = x_ref[...]
        r, xhat = _rms_fwd(xv)
        h = (xhat * gpre_ref[...]).astype(MXU)
        row = i * TM + lax.broadcasted_iota(jnp.int32, (TM, 1), 0)
        y = None
        for g in range(N_GROUPS):
            cols = slice(g * GROUP, (g + 1) * GROUP)
            a = _dot(h, win_ref[g])
            z = _dot(h, win_ref[N_GROUPS + g])
            s = jnp.concatenate([carry_ref[g], a], axis=0)
            carry_ref[g] = a[TM - HALO:, :]
            w = 1
            while w < POOL_WINDOWS[g]:
                s = s + pltpu.roll(s, w, 0)
                w *= 2
            mixed = (s[HALO:, :] * _inv_count(row, POOL_WINDOWS[g]) - a).astype(MXU)
            mg = _dot(mixed, wg_ref[g])
            prod = (mg * sc_ref[:, cols] * (z * _sigmoid(z))).astype(MXU)
            z_ref[:, cols] = z
            mixed_ref[:, cols] = mixed
            mg_ref[:, cols] = mg
            prod_ref[:, cols] = prod
            part = _dot(prod, wout_ref[cols, :])
            y = part if y is None else y + part
        y_ref[...] = y
        _, yhat = _rms_fwd(y)
        x1_ref[...] = xv + yhat * gpost_ref[...]

        @pl.when(i == nt - 1)
        def _():
            ex.wait(ex_in, ex_out, ex_sems)

    tok = lambda w: pl.BlockSpec((TM, w), lambda i: (i, 0))
    return pl.pallas_call(
        body, name="pool_fwd", grid=(nt,),
        in_specs=[tok(D_MODEL), _const_spec((1, D_MODEL)), _const_spec((1, D_MODEL)),
                  _const_spec((N_DEV, D_MODEL, W_BLOCK)), _const_spec((N_GROUPS, GROUP, GROUP)),
                  _const_spec((1, POOL_WIDTH)), _const_spec((POOL_WIDTH, D_MODEL))] + ex.specs,
        out_specs=[tok(D_MODEL), tok(D_MODEL), tok(POOL_WIDTH), tok(POOL_WIDTH), tok(POOL_WIDTH), tok(POOL_WIDTH)]
        + ex.specs,
        out_shape=[jax.ShapeDtypeStruct((n_tok, D_MODEL), F32), jax.ShapeDtypeStruct((n_tok, D_MODEL), F32),
                   jax.ShapeDtypeStruct((n_tok, POOL_WIDTH), F32), jax.ShapeDtypeStruct((n_tok, POOL_WIDTH), MXU),
                   jax.ShapeDtypeStruct((n_tok, POOL_WIDTH), F32), jax.ShapeDtypeStruct((n_tok, POOL_WIDTH), MXU)]
        + ex.out_shape,
        scratch_shapes=[pltpu.VMEM((N_GROUPS, HALO, GROUP), F32)] + ex.scratch,
        compiler_params=_params(1),
    )(x, g_pre, g_post, w_in, w_group, scale, w_out, *ex.arrays)


def _flush(acc_ref, out_hbm, stage_ref):
    for j in range(acc_ref.shape[0]):
        stage_ref[...] = acc_ref[j].astype(stage_ref.dtype)
        pltpu.sync_copy(stage_ref, out_hbm.at[j])


def _pool_bwd(dx1, y, z, mg, mixed, prod, g_post, scale, w_group, w_out, exchange_items):
    n_tok = dx1.shape[0]
    nt = n_tok // TM
    ex = _Exchange(exchange_items)

    def body(dx1_ref, y_ref, z_ref, mg_ref, mixed_ref, prod_ref, gpost_ref, sc_ref, wg_ref, wout_ref, *rest):
        ex_in, rest = rest[:ex.n], rest[ex.n:]
        du_ref, dsc_ref, dgpost_ref, dwg_hbm, dwout_hbm = rest[:5]
        ex_out, rest = rest[5:5 + ex.n], rest[5 + ex.n:]
        carry_ref, dwg_acc, dwout_acc, stage_g, stage_o = rest[:5]
        ex_sems = rest[5:]
        i = pl.program_id(0)

        @pl.when(i == 0)
        def _():
            ex.start(ex_in, ex_out, ex_sems)
            carry_ref[...] = jnp.zeros_like(carry_ref)
            dwg_acc[...] = jnp.zeros_like(dwg_acc)
            dwout_acc[...] = jnp.zeros_like(dwout_acc)
            dsc_ref[...] = jnp.zeros_like(dsc_ref)
            dgpost_ref[...] = jnp.zeros_like(dgpost_ref)

        dn = dx1_ref[...]
        r, yhat = _rms_fwd(y_ref[...])
        dgpost_ref[...] += jnp.sum(dn * yhat, axis=0, keepdims=True)
        dy = _rms_bwd(dn, yhat, r, gpost_ref[...]).astype(MXU)
        row = (nt - 1 - i) * TM + lax.broadcasted_iota(jnp.int32, (TM, 1), 0)
        n_ext = TM + HALO
        for g in range(N_GROUPS):
            cols = slice(g * GROUP, (g + 1) * GROUP)
            dwout_acc[g] += _dot_tn(prod_ref[:, cols], dy)
            dprod = _dot_nt(dy, wout_ref[cols, :])
            zv = z_ref[:, cols]
            sig = _sigmoid(zv)
            silu = zv * sig
            mgv = mg_ref[:, cols]
            sc = sc_ref[:, cols]
            dsc_ref[:, cols] += jnp.sum(dprod * silu * mgv, axis=0, keepdims=True)
            dmg = (dprod * silu * sc).astype(MXU)
            dz = dprod * (mgv * sc) * (sig * (1.0 + zv * (1.0 - sig)))
            dwg_acc[g] += _dot_tn(mixed_ref[:, cols], dmg)
            dmixed = _dot_nt(dmg, wg_ref[g])
            e = dmixed * _inv_count(row, POOL_WINDOWS[g])
            s = jnp.concatenate([e, carry_ref[g]], axis=0)
            carry_ref[g] = e[:HALO, :]
            w = 1
            while w < POOL_WINDOWS[g]:
                s = s + pltpu.roll(s, n_ext - w, 0)
                w *= 2
            du_ref[:, cols] = (s[:TM, :] - dmixed).astype(MXU)
            du_ref[:, POOL_WIDTH + g * GROUP:POOL_WIDTH + (g + 1) * GROUP] = dz.astype(MXU)

        @pl.when(i == nt - 1)
        def _():
            _flush(dwg_acc, dwg_hbm, stage_g)
            _flush(dwout_acc, dwout_hbm, stage_o)
            ex.wait(ex_in, ex_out, ex_sems)

    rev = lambda w: pl.BlockSpec((TM, w), lambda i: (nt - 1 - i, 0))
    any_spec = pl.BlockSpec(memory_space=pl.ANY)
    return pl.pallas_call(
        body, name="pool_bwd", grid=(nt,),
        in_specs=[rev(D_MODEL), rev(D_MODEL), rev(POOL_WIDTH), rev(POOL_WIDTH), rev(POOL_WIDTH), rev(POOL_WIDTH),
                  _const_spec((1, D_MODEL)), _const_spec((1, POOL_WIDTH)),
                  _const_spec((N_GROUPS, GROUP, GROUP)), _const_spec((POOL_WIDTH, D_MODEL))] + ex.specs,
        out_specs=[rev(2 * POOL_WIDTH), pl.BlockSpec((1, POOL_WIDTH), lambda i: (0, 0)),
                   pl.BlockSpec((1, D_MODEL), lambda i: (0, 0)), any_spec, any_spec] + ex.specs,
        out_shape=[jax.ShapeDtypeStruct((n_tok, 2 * POOL_WIDTH), MXU), jax.ShapeDtypeStruct((1, POOL_WIDTH), F32),
                   jax.ShapeDtypeStruct((1, D_MODEL), F32), jax.ShapeDtypeStruct((N_GROUPS, GROUP, GROUP), MXU),
                   jax.ShapeDtypeStruct((N_GROUPS, GROUP, D_MODEL), MXU)] + ex.out_shape,
        scratch_shapes=[pltpu.VMEM((N_GROUPS, HALO, GROUP), F32), pltpu.VMEM((N_GROUPS, GROUP, GROUP), F32),
                        pltpu.VMEM((N_GROUPS, GROUP, D_MODEL), F32), pltpu.VMEM((GROUP, GROUP), MXU),
                        pltpu.VMEM((GROUP, D_MODEL), MXU)] + ex.scratch,
        compiler_params=_params(1),
    )(dx1, y, z, mg, mixed, prod, g_post, scale, w_group, w_out, *ex.arrays)


def _in_proj_bwd(parts, x, dres, g_pre, w_in, name, exchange_items):
    n_tok = x.shape[0]
    nt = n_tok // TM
    half = D_MODEL // W_BLOCK
    ex = _Exchange(exchange_items)

    def body(p0, p1, p2, p3, x_ref, dres_ref, g_ref, w_ref, *rest):
        ex_in, rest = rest[:ex.n], rest[ex.n:]
        dx_ref, dg_ref, dw_hbm = rest[:3]
        ex_out, acc_ref, stage_ref, ex_sems = rest[3:3 + ex.n], rest[3 + ex.n], rest[4 + ex.n], rest[5 + ex.n:]
        i = pl.program_id(0)

        @pl.when(i == 0)
        def _():
            ex.start(ex_in, ex_out, ex_sems)
            acc_ref[...] = jnp.zeros_like(acc_ref)
            dg_ref[...] = jnp.zeros_like(dg_ref)

        r, xhat = _rms_fwd(x_ref[...])
        g = g_ref[...]
        h = (xhat * g).astype(MXU)
        dh = None
        for p, part_ref in enumerate((p0, p1, p2, p3)):
            for jj in range(half):
                j = half * p + jj
                du = part_ref[:, jj * W_BLOCK:(jj + 1) * W_BLOCK]
                t = _dot_nt(du, w_ref[j])
                dh = t if dh is None else dh + t
                acc_ref[j] += _dot_tn(h, du)
        dg_ref[...] += jnp.sum(dh * xhat, axis=0, keepdims=True)
        dx_ref[...] = dres_ref[...] + _rms_bwd(dh, xhat, r, g)

        @pl.when(i == nt - 1)
        def _():
            _flush(acc_ref, dw_hbm, stage_ref)
            ex.wait(ex_in, ex_out, ex_sems)

    tok = pl.BlockSpec((TM, D_MODEL), lambda i: (i, 0))
    return pl.pallas_call(
        body, name=name, grid=(nt,),
        in_specs=[pl.BlockSpec((TM, D_MODEL), m) for _, m in parts]
        + [tok, tok, _const_spec((1, D_MODEL)), _const_spec((N_DEV, D_MODEL, W_BLOCK))] + ex.specs,
        out_specs=[tok, pl.BlockSpec((1, D_MODEL), lambda i: (0, 0)), pl.BlockSpec(memory_space=pl.ANY)] + ex.specs,
        out_shape=[jax.ShapeDtypeStruct((n_tok, D_MODEL), F32), jax.ShapeDtypeStruct((1, D_MODEL), F32),
                   jax.ShapeDtypeStruct((N_DEV, D_MODEL, W_BLOCK), MXU)] + ex.out_shape,
        scratch_shapes=[pltpu.VMEM((N_DEV, D_MODEL, W_BLOCK), F32), pltpu.VMEM((D_MODEL, W_BLOCK), MXU)] + ex.scratch,
        compiler_params=_params(1),
    )(*[a for a, _ in parts], x, dres, g_pre, w_in, *ex.arrays)


def _rel_onehot():
    rel = lax.broadcasted_iota(jnp.int32, (REL_PAD, ROLL_W), 0)
    col = lax.broadcasted_iota(jnp.int32, (REL_PAD, ROLL_W), 1)
    return (rel == jnp.minimum(BAND + MAX_REL - col, 2 * MAX_REL)).astype(MXU)


def _split3(v):
    hi = v.astype(MXU)
    r1 = v - hi.astype(F32)
    mid = r1.astype(MXU)
    lo = (r1 - mid.astype(F32)).astype(MXU)
    return hi, mid, lo


def _bias_table(rel_bias_padded):
    def body(rb_ref, out_ref):
        onehot = _rel_onehot()
        base = None
        for term in _split3(rb_ref[...]):
            t = _dot(term, onehot)
            base = t if base is None else base + t
        qi = lax.broadcasted_iota(jnp.int32, (CHUNK, ROLL_W), 0)
        kk = lax.broadcasted_iota(jnp.int32, (CHUNK, TKW), 1)
        for h in range(N_HEADS):
            t = jnp.broadcast_to(base[h:h + 1, :], (CHUNK, ROLL_W))
            for bit in range(6):
                t = jnp.where(((qi >> bit) & 1) == 1, pltpu.roll(t, 1 << bit, 1), t)
            for rr in range(TM // CHUNK):
                shifted = pltpu.roll(t, (CHUNK * rr - CHUNK) % ROLL_W, 1)[:, :TKW]
                band = kk - CHUNK * rr
                out_ref[h, rr * CHUNK:(rr + 1) * CHUNK, :] = jnp.where((band >= 0) & (band < BAND), shifted, NEG)

    return pl.pallas_call(
        body, name="bias_table", out_shape=jax.ShapeDtypeStruct((N_HEADS, TM, TKW), F32),
        compiler_params=pltpu.CompilerParams(vmem_limit_bytes=VMEM_LIMIT),
    )(rel_bias_padded)


def _bias_grad(dtab):
    def body(dt_ref, out_ref, dbase_ref):
        qi = lax.broadcasted_iota(jnp.int32, (CHUNK, ROLL_W), 0)
        zeros = jnp.zeros((CHUNK, ROLL_W - TKW), F32)
        for h in range(N_HEADS):
            t = None
            for rr in range(TM // CHUNK):
                blk = jnp.concatenate([dt_ref[h, rr * CHUNK:(rr + 1) * CHUNK, :], zeros], axis=1)
                blk = pltpu.roll(blk, (CHUNK - CHUNK * rr) % ROLL_W, 1)
                t = blk if t is None else t + blk
            for bit in range(6):
                t = jnp.where(((qi >> bit) & 1) == 1, pltpu.roll(t, ROLL_W - (1 << bit), 1), t)
            dbase_ref[h:h + 1, :] = jnp.sum(t, axis=0, keepdims=True)
        onehot = _rel_onehot()
        acc = None
        for term in _split3(dbase_ref[...]):
            t = _dot_nt(term, onehot)
            acc = t if acc is None else acc + t
        out_ref[...] = acc

    return pl.pallas_call(
        body, name="bias_grad", out_shape=jax.ShapeDtypeStruct((N_HEADS, REL_PAD), F32),
        scratch_shapes=[pltpu.VMEM((N_HEADS, ROLL_W), F32)],
        compiler_params=pltpu.CompilerParams(vmem_limit_bytes=VMEM_LIMIT),
    )(dtab)


def _att_in(x1, g_pre, w_in):
    n_tok = x1.shape[0]
    nt = n_tok // TM
    lead = PAD // TM

    def body(x_ref, g_ref, w_ref, q_ref, k_ref, v_ref, z_ref):
        i = pl.program_id(0)

        @pl.when(i < lead)
        def _():
            k_ref[...] = jnp.zeros_like(k_ref)
            v_ref[...] = jnp.zeros_like(v_ref)

        @pl.when(i >= lead)
        def _():
            _, xhat = _rms_fwd(x_ref[...])
            h = (xhat * g_ref[...]).astype(MXU)
            for j in range(N_DEV):
                u = _dot(h, w_ref[j])
                cols = slice((j % 2) * W_BLOCK, (j % 2 + 1) * W_BLOCK)
                if j < 2:
                    q_ref[:, cols] = (u * QK_SCALE).astype(MXU)
                elif j < 4:
                    k_ref[:, cols] = u.astype(MXU)
                elif j < 6:
                    v_ref[:, cols] = u.astype(MXU)
                else:
                    z_ref[:, cols] = u

    late = pl.BlockSpec((TM, D_MODEL), lambda i: (jnp.maximum(i - lead, 0), 0))
    padded = pl.BlockSpec((TM, D_MODEL), lambda i: (i, 0))
    return pl.pallas_call(
        body, name="att_in", grid=(nt + lead,),
        in_specs=[late, _const_spec((1, D_MODEL)), _const_spec((N_DEV, D_MODEL, W_BLOCK))],
        out_specs=[late, padded, padded, late],
        out_shape=[jax.ShapeDtypeStruct((n_tok, ATT_WIDTH), MXU), jax.ShapeDtypeStruct((n_tok + PAD, ATT_WIDTH), MXU),
                   jax.ShapeDtypeStruct((n_tok + PAD, ATT_WIDTH), MXU), jax.ShapeDtypeStruct((n_tok, ATT_WIDTH), F32)],
        compiler_params=_params(1),
    )(x1, g_pre, w_in)


def _head_masks():
    lane = lax.broadcasted_iota(jnp.int32, (1, 2 * HEAD_DIM), 1)
    first = lane < HEAD_DIM
    return first, jnp.logical_not(first)


def _softmax_blocks(qh, k_refs, bias_ref, hh, tile):
    ss = []
    for b in range(KB):
        s = _dot_nt(qh, k_refs[b][...]) + bias_ref[hh, :, b * TM:(b + 1) * TM]
        ss.append(s + jnp.where(tile + b < PAD // TM, NEG, 0.0).astype(F32))
    m = None
    for s in ss:
        mb = jnp.max(s, axis=-1, keepdims=True)
        m = mb if m is None else jnp.maximum(m, mb)
    ps = [jnp.exp(s - m) for s in ss]
    l = None
    for p in ps:
        lb = jnp.sum(p, axis=-1, keepdims=True)
        l = lb if l is None else l + lb
    inv = 1.0 / l
    return [p * inv for p in ps]


def _key_specs():
    return [pl.BlockSpec((TM, 2 * HEAD_DIM), functools.partial(lambda hp, i, b: (i + b, hp), b=b)) for b in range(KB)]


def _att_fwd(q, kpad, vpad, bias_tab):
    n_tok = q.shape[0]
    nt = n_tok // TM

    def body(q_ref, k0, k1, k2, v0, v1, v2, bias_ref, o_ref):
        i = pl.program_id(1)
        qv = q_ref[...]
        masks = _head_masks()
        outs = []
        for hh in range(2):
            qh = jnp.where(masks[hh], qv, jnp.zeros_like(qv))
            ps = _softmax_blocks(qh, (k0, k1, k2), bias_ref, hh, i)
            o = None
            for b, v_ref in enumerate((v0, v1, v2)):
                t = _dot(ps[b].astype(MXU), v_ref[...])
                o = t if o is None else o + t
            outs.append(o)
        o_ref[...] = jnp.where(masks[0], outs[0], outs[1])

    qspec = pl.BlockSpec((TM, 2 * HEAD_DIM), lambda hp, i: (i, hp))
    return pl.pallas_call(
        body, name="att_fwd", grid=(N_HEADS // 2, nt),
        in_specs=[qspec] + _key_specs() + _key_specs() + [pl.BlockSpec((2, TM, TKW), lambda hp, i: (hp, 0, 0))],
        out_specs=qspec,
        out_shape=jax.ShapeDtypeStruct((n_tok, ATT_WIDTH), F32),
        compiler_params=_params(2),
    )(q, kpad, kpad, kpad, vpad, vpad, vpad, bias_tab)


def _att_bwd(q, kpad, vpad, do, bias_tab):
    n_tok = q.shape[0]
    nt = n_tok // TM

    def body(q_ref, do_ref, k0, k1, k2, v0, v1, v2, bias_ref,
             dq_ref, dk_ref, dv_ref, dtab_ref, rk0, rk1, rv0, rv1):
        i = pl.program_id(1)

        @pl.when(i == 0)
        def _():
            for ref in (rk0, rk1, rv0, rv1):
                ref[...] = jnp.zeros_like(ref)
            dtab_ref[...] = jnp.zeros_like(dtab_ref)

        @pl.when(i < nt)
        def _():
            qv = q_ref[...]
            dov = do_ref[...]
            masks = _head_masks()
            dks = [None] * KB
            dvs = [None] * KB
            dqs = []
            for hh in range(2):
                qh = jnp.where(masks[hh], qv, jnp.zeros_like(qv))
                doh = jnp.where(masks[hh], dov, jnp.zeros_like(dov))
                ps = _softmax_blocks(qh, (k0, k1, k2), bias_ref, hh, i)
                dps = [_dot_nt(doh, v_ref[...]) for v_ref in (v0, v1, v2)]
                dsum = None
                for p, dp in zip(ps, dps):
                    t = jnp.sum(p * dp, axis=-1, keepdims=True)
                    dsum = t if dsum is None else dsum + t
                dq = None
                for b, k_ref in enumerate((k0, k1, k2)):
                    ds = ps[b] * (dps[b] - dsum)
                    dtab_ref[hh, :, b * TM:(b + 1) * TM] += ds
                    dsb = ds.astype(MXU)
                    t = _dot(dsb, k_ref[...])
                    dq = t if dq is None else dq + t
                    t = _dot_tn(dsb, qh)
                    dks[b] = t if dks[b] is None else dks[b] + t
                    t = _dot_tn(ps[b].astype(MXU), doh)
                    dvs[b] = t if dvs[b] is None else dvs[b] + t
                dqs.append(dq)
            dq_ref[...] = (jnp.where(masks[0], dqs[0], dqs[1]) * QK_SCALE).astype(MXU)
            dk_ref[...] = (rk0[...] + dks[0]).astype(MXU)
            dv_ref[...] = (rv0[...] + dvs[0]).astype(MXU)
            rk0[...] = rk1[...] + dks[1]
            rv0[...] = rv1[...] + dvs[1]
            rk1[...] = dks[2]
            rv1[...] = dvs[2]

        @pl.when(i >= nt)
        def _():
            dk_ref[...] = rk0[...].astype(MXU)
            dv_ref[...] = rv0[...].astype(MXU)
            rk0[...] = rk1[...]
            rv0[...] = rv1[...]

    last = nt - 1
    qspec = pl.BlockSpec((TM, 2 * HEAD_DIM), lambda hp, i: (jnp.minimum(i, last), hp))
    kspecs = [pl.BlockSpec((TM, 2 * HEAD_DIM), functools.partial(lambda hp, i, b: (jnp.minimum(i, last) + b, hp), b=b))
              for b in range(KB)]
    pspec = pl.BlockSpec((TM, 2 * HEAD_DIM), lambda hp, i: (i, hp))
    tspec = pl.BlockSpec((2, TM, TKW), lambda hp, i: (hp, 0, 0))
    ring = pltpu.VMEM((TM, 2 * HEAD_DIM), F32)
    return pl.pallas_call(
        body, name="att_bwd", grid=(N_HEADS // 2, nt + KB - 1),
        in_specs=[qspec, qspec] + kspecs + kspecs + [tspec],
        out_specs=[qspec, pspec, pspec, tspec],
        out_shape=[jax.ShapeDtypeStruct((n_tok, ATT_WIDTH), MXU), jax.ShapeDtypeStruct((n_tok + PAD, ATT_WIDTH), MXU),
                   jax.ShapeDtypeStruct((n_tok + PAD, ATT_WIDTH), MXU), jax.ShapeDtypeStruct((N_HEADS, TM, TKW), F32)],
        scratch_shapes=[ring, ring, ring, ring],
        compiler_params=_params(2),
    )(q, do, kpad, kpad, kpad, vpad, vpad, vpad, bias_tab)


def _att_out(o, z, x1, target, g_post, w_out):
    n_tok = o.shape[0]
    nt = n_tok // TM

    def body(o_ref, z_ref, x1_ref, tgt_ref, gpost_ref, w_ref,
             loss_ref, dx2_ref, do_ref, dz_ref, dgpost_ref, dw_hbm, acc_ref, loss_acc, stage_ref):
        i = pl.program_id(0)

        @pl.when(i == 0)
        def _():
            acc_ref[...] = jnp.zeros_like(acc_ref)
            loss_acc[...] = jnp.zeros_like(loss_acc)
            dgpost_ref[...] = jnp.zeros_like(dgpost_ref)

        ov = o_ref[...]
        zv = z_ref[...]
        sig = _sigmoid(zv)
        silu = zv * sig
        gated = (ov * silu).astype(MXU)
        y = _dot(gated, w_ref[...])
        r, yhat = _rms_fwd(y)
        gpost = gpost_ref[...]
        diff = x1_ref[...] + yhat * gpost - tgt_ref[...]
        loss_acc[...] += jnp.sum(diff * diff, axis=0, keepdims=True)
        dn = diff * (1.0 / D_MODEL)
        dx2_ref[...] = dn
        dgpost_ref[...] += jnp.sum(dn * yhat, axis=0, keepdims=True)
        dy = _rms_bwd(dn, yhat, r, gpost).astype(MXU)
        for j in range(ATT_WIDTH // W_BLOCK):
            acc_ref[j] += _dot_tn(gated[:, j * W_BLOCK:(j + 1) * W_BLOCK], dy)
        dgated = _dot_nt(dy, w_ref[...])
        do_ref[...] = (dgated * silu).astype(MXU)
        dz_ref[...] = (dgated * ov * (sig * (1.0 + zv * (1.0 - sig)))).astype(MXU)

        @pl.when(i == nt - 1)
        def _():
            total = jnp.sum(loss_acc[...], axis=-1, keepdims=True) * (0.5 / D_MODEL)
            loss_ref[...] = jnp.broadcast_to(total, loss_ref.shape)
            _flush(acc_ref, dw_hbm, stage_ref)

    tok = pl.BlockSpec((TM, D_MODEL), lambda i: (i, 0))
    return pl.pallas_call(
        body, name="att_out", grid=(nt,),
        in_specs=[tok, tok, tok, tok, _const_spec((1, D_MODEL)), _const_spec((ATT_WIDTH, D_MODEL))],
        out_specs=[pl.BlockSpec((1, 128), lambda i: (0, 0)), tok, tok, tok,
                   pl.BlockSpec((1, D_MODEL), lambda i: (0, 0)), pl.BlockSpec(memory_space=pl.ANY)],
        out_shape=[jax.ShapeDtypeStruct((1, 128), F32), jax.ShapeDtypeStruct((n_tok, D_MODEL), F32),
                   jax.ShapeDtypeStruct((n_tok, ATT_WIDTH), MXU), jax.ShapeDtypeStruct((n_tok, ATT_WIDTH), MXU),
                   jax.ShapeDtypeStruct((1, D_MODEL), F32),
                   jax.ShapeDtypeStruct((ATT_WIDTH // W_BLOCK, W_BLOCK, D_MODEL), MXU)],
        scratch_shapes=[pltpu.VMEM((ATT_WIDTH // W_BLOCK, W_BLOCK, D_MODEL), F32), pltpu.VMEM((1, D_MODEL), F32),
                        pltpu.VMEM((W_BLOCK, D_MODEL), MXU)],
        compiler_params=_params(1),
    )(o, z, x1, target, g_post, w_out)


def _adamw(parts, w, m, v, name):
    rows, cols = w.shape
    tr = min(rows, 256)

    def body(p_ref, w_ref, m_ref, v_ref, g_ref, d_ref, mo_ref, vo_ref):
        g = p_ref[0].astype(F32)
        for s in range(1, N_DEV):
            g = g + p_ref[s].astype(F32)
        m_new = ADAM_B1 * m_ref[...] + (1.0 - ADAM_B1) * g
        v_new = ADAM_B2 * v_ref[...] + (1.0 - ADAM_B2) * (g * g)
        m_hat = m_new / (1.0 - ADAM_B1 ** ADAM_STEP)
        v_hat = v_new / (1.0 - ADAM_B2 ** ADAM_STEP)
        g_ref[...] = g
        d_ref[...] = -ADAM_LR * (m_hat / (jnp.sqrt(v_hat) + ADAM_EPS) + ADAM_WD * w_ref[...])
        mo_ref[...] = m_new
        vo_ref[...] = v_new

    blk = pl.BlockSpec((tr, cols), lambda i: (i, 0))
    shape = jax.ShapeDtypeStruct((rows, cols), F32)
    return pl.pallas_call(
        body, name=name, grid=(rows // tr,),
        in_specs=[pl.BlockSpec((N_DEV, tr, cols), lambda i: (0, i, 0)), blk, blk, blk],
        out_specs=[blk, blk, blk, blk], out_shape=[shape, shape, shape, shape],
        compiler_params=_params(1),
    )(parts, w, m, v)


SMALL_ROWS = 16


def _pack_small(norm_pre, norm_post, pool_scale, rel_bias_padded):
    return jnp.concatenate([norm_pre, norm_post, pool_scale.reshape(2, D_MODEL),
                            rel_bias_padded.reshape(SMALL_ROWS - 6, D_MODEL)], axis=0)


def _unpack_small(packed):
    rel = packed[6:].reshape(N_HEADS, REL_PAD)[:, :N_REL]
    return packed[0:2], packed[2:4], packed[4:6].reshape(1, POOL_WIDTH), rel.reshape(1, N_HEADS, N_REL)


def _pad_rel(rel_bias):
    return jnp.pad(rel_bias.reshape(N_HEADS, N_REL), ((0, 0), (0, REL_PAD - N_REL)))


def kernel(x, norm_pre, norm_post, pool_w_in, pool_w_group, pool_scale, pool_w_out, att_w_in, att_rel_bias, att_w_out, loss_target, m_norm_pre, m_norm_post, m_pool_w_in, m_pool_w_group, m_pool_scale, m_pool_w_out, m_att_w_in, m_att_rel_bias, m_att_w_out, v_norm_pre, v_norm_post, v_pool_w_in, v_pool_w_group, v_pool_scale, v_pool_w_out, v_att_w_in, v_att_rel_bias, v_att_w_out):
    xt = x[0]
    target = loss_target[0]
    n_tok = xt.shape[0]
    lead = PAD // TM
    rows_g = GROUP // N_DEV

    gathered = _exchange([(pool_w_in[0].astype(MXU), False), (pool_w_group[0].astype(MXU), False),
                          (pool_w_out[0].astype(MXU), False)], "gather_pool_weights")
    w_in_p = gathered[0]
    w_group = gathered[1].transpose(1, 0, 2, 3).reshape(N_GROUPS, GROUP, GROUP)
    w_out_p = gathered[2].reshape(POOL_WIDTH, D_MODEL)
    rel_padded = _pad_rel(att_rel_bias[0])
    bias_tab = _bias_table(rel_padded)

    x1, y0, z0, mixed, mg, prod, w_in_a, w_out_a = _pool_fwd(
        xt, norm_pre[0:1], norm_post[0:1], w_in_p, w_group, pool_scale, w_out_p,
        [(att_w_in[0].astype(MXU), False), (att_w_out[0].astype(MXU), False)])
    w_out_a = w_out_a.reshape(ATT_WIDTH, D_MODEL)
    q, kpad, vpad, z1 = _att_in(x1, norm_pre[1:2], w_in_a)
    o = _att_fwd(q, kpad, vpad, bias_tab)
    loss_part, dx2, do, dz1, d_gpost1, d_w_out_a = _att_out(o, z1, x1, target, norm_post[1:2], w_out_a)
    dq, dkpad, dvpad, dtab = _att_bwd(q, kpad, vpad, do, bias_tab)
    d_rel = _bias_grad(dtab)
    plain = lambda i: (i, 0)
    shifted = lambda i: (i + lead, 0)
    dx1, d_gpre1, d_w_in_a = _in_proj_bwd([(dq, plain), (dkpad, shifted), (dvpad, shifted), (dz1, plain)],
                                          x1, dx2, norm_pre[1:2], w_in_a, "att_in_bwd", [])
    du0, d_scale, d_gpost0, d_w_group, d_w_out_p, part_w_in_a, part_w_out_a = _pool_bwd(
        dx1, y0, z0, mg, mixed, prod, norm_post[0:1], pool_scale, w_group, w_out_p,
        [(d_w_in_a, True), (d_w_out_a.reshape(N_DEV, ATT_WIDTH // N_DEV, D_MODEL), True)])
    col = lambda p: (lambda i: (i, p))
    grad_x, d_gpre0, d_w_in_p, part_w_group, part_w_out_p = _in_proj_bwd(
        [(du0, col(p)) for p in range(4)], xt, dx1, norm_pre[0:1], w_in_p, "pool_in_bwd",
        [(d_w_group.reshape(N_GROUPS, N_DEV, rows_g, GROUP).transpose(1, 0, 2, 3), True),
         (d_w_out_p.reshape(N_DEV, POOL_WIDTH // N_DEV, D_MODEL), True)])
    d_small = _pack_small(jnp.concatenate([d_gpre0, d_gpre1], axis=0), jnp.concatenate([d_gpost0, d_gpost1], axis=0),
                          d_scale, d_rel)
    part_w_in_p, part_small = _exchange([(d_w_in_p, True), (d_small, False)], "exchange_tail")

    def update(part, w, m, v, name):
        shape = w.shape
        flat = lambda a: a.reshape(-1, shape[-1])
        outs = _adamw(part.reshape(N_DEV, -1, shape[-1]), flat(w), flat(m), flat(v), name)
        return [a.reshape(shape) for a in outs]

    u_att_w_in = update(part_w_in_a, att_w_in, m_att_w_in, v_att_w_in, "adamw_att_w_in")
    u_att_w_out = update(part_w_out_a, att_w_out, m_att_w_out, v_att_w_out, "adamw_att_w_out")
    u_pool_w_group = update(part_w_group, pool_w_group, m_pool_w_group, v_pool_w_group, "adamw_pool_w_group")
    u_pool_w_out = update(part_w_out_p, pool_w_out, m_pool_w_out, v_pool_w_out, "adamw_pool_w_out")
    u_pool_w_in = update(part_w_in_p, pool_w_in, m_pool_w_in, v_pool_w_in, "adamw_pool_w_in")
    small = _adamw(part_small, _pack_small(norm_pre, norm_post, pool_scale, rel_padded),
                   _pack_small(m_norm_pre, m_norm_post, m_pool_scale, _pad_rel(m_att_rel_bias[0])),
                   _pack_small(v_norm_pre, v_norm_post, v_pool_scale, _pad_rel(v_att_rel_bias[0])), "adamw_small")
    u_small = [_unpack_small(a) for a in small]

    loss = lax.psum(loss_part[0, 0], ("x", "y", "c"))
    outs = [loss, grad_x.reshape(1, n_tok, D_MODEL)]
    for kind in range(4):
        outs += [u_small[kind][0], u_small[kind][1], u_pool_w_in[kind], u_pool_w_group[kind], u_small[kind][2],
                 u_pool_w_out[kind], u_att_w_in[kind], u_small[kind][3], u_att_w_out[kind]]
    return tuple(outs)
```

```python
import functools

import jax
import jax.numpy as jnp
from jax import lax
from jax.experimental import pallas as pl
from jax.experimental.pallas import tpu as pltpu

F32 = jnp.float32
MXU = jnp.bfloat16

D_MODEL = 1024
POOL_WIDTH = 2048
POOL_WINDOWS = (2, 4, 8, 16)
N_GROUPS = 4
GROUP = 512
HALO = 16
N_HEADS = 16
HEAD_DIM = 64
CHUNK = 64
LEFT_CHUNKS = 8
PAD = LEFT_CHUNKS * CHUNK
BAND = PAD + CHUNK
MAX_REL = 256
N_REL = 2 * MAX_REL + 1
REL_PAD = 640
ATT_WIDTH = 1024
N_DEV = 8
W_BLOCK = 512
RMS_EPS = 1e-6
QK_SCALE = 0.125
NEG = -1e30

TM = 256
TMB = 1024
KB = 3
TKW = KB * TM
ROLL_W = 1024

VMEM_LIMIT = 56 * 1024 * 1024

ADAM_LR = 0.001
ADAM_B1 = 0.9
ADAM_B2 = 0.999
ADAM_EPS = 1e-08
ADAM_WD = 0.01
ADAM_STEP = 10

NT_DIMS = (((1,), (1,)), ((), ()))
TN_DIMS = (((0,), (0,)), ((), ()))


def _params(n_grid):
    return pltpu.CompilerParams(dimension_semantics=("arbitrary",) * n_grid, vmem_limit_bytes=VMEM_LIMIT)


def _const_spec(shape):
    nd = len(shape)
    return pl.BlockSpec(shape, lambda *_: (0,) * nd, pipeline_mode=pl.Buffered(1))


def _dot(a, b):
    return jnp.dot(a, b, preferred_element_type=F32)


def _dot_nt(a, b):
    return lax.dot_general(a, b, NT_DIMS, preferred_element_type=F32)


def _dot_tn(a, b):
    return lax.dot_general(a, b, TN_DIMS, preferred_element_type=F32)


def _sigmoid(z):
    return 1.0 / (1.0 + jnp.exp(-z))


def _rms_fwd(xv):
    r = lax.rsqrt(jnp.mean(xv * xv, axis=-1, keepdims=True) + RMS_EPS)
    return r, xv * r


def _rms_bwd(dn, xhat, r, g):
    dng = dn * g
    return r * (dng - xhat * jnp.mean(dng * xhat, axis=-1, keepdims=True))


class _Exchange:
    def __init__(self, items):
        self.arrays = [a for a, _ in items]
        self.scatter = [s for _, s in items]
        self.n = len(items)
        self.out_shape = [jax.ShapeDtypeStruct((N_DEV,) + tuple(a.shape[1:] if s else a.shape), a.dtype)
                          for a, s in items]
        self.specs = [pl.BlockSpec(memory_space=pl.ANY)] * self.n
        self.scratch = ([pltpu.SemaphoreType.DMA((N_DEV - 1, self.n)), pltpu.SemaphoreType.DMA((N_DEV - 1, self.n)),
                         pltpu.SemaphoreType.DMA((self.n,))] if self.n else [])

    def _copies(self, ins, outs, sems, with_receives):
        send_sems, recv_sems, local_sems = sems
        x, y, c = lax.axis_index("x"), lax.axis_index("y"), lax.axis_index("c")
        me = 4 * x + 2 * y + c

        def src(t, slot):
            return ins[t].at[slot] if self.scatter[t] else ins[t]

        local = [pltpu.make_async_copy(src(t, me), outs[t].at[me], local_sems.at[t]) for t in range(self.n)]
        sends, recvs = [], []
        for k in range(1, N_DEV):
            px = 1 - x if k & 4 else x
            py = 1 - y if k & 2 else y
            pc = 1 - c if k & 1 else c
            peer = 4 * px + 2 * py + pc
            for t in range(self.n):
                common = dict(src_ref=src(t, peer), send_sem=send_sems.at[k - 1, t], recv_sem=recv_sems.at[k - 1, t],
                              device_id=(px, py, pc), device_id_type=pl.DeviceIdType.MESH)
                sends.append(pltpu.make_async_remote_copy(dst_ref=outs[t].at[me], **common))
                if with_receives:
                    recvs.append(pltpu.make_async_remote_copy(dst_ref=outs[t].at[peer], **common))
        return local, sends, recvs

    def start(self, ins, outs, sems):
        if self.n:
            local, sends, _ = self._copies(ins, outs, sems, False)
            for cp in local + sends:
                cp.start()

    def wait(self, ins, outs, sems):
        if self.n:
            local, sends, recvs = self._copies(ins, outs, sems, True)
            for cp in recvs:
                cp.wait_recv()
            for cp in sends:
                cp.wait_send()
            for cp in local:
                cp.wait()


def _exchange(items, name):
    ex = _Exchange(items)
    n = ex.n

    def body(*refs):
        ins, outs, sems = refs[:n], refs[n:2 * n], refs[2 * n:]
        ex.start(ins, outs, sems)
        ex.wait(ins, outs, sems)

    return pl.pallas_call(
        body, name=name, out_shape=ex.out_shape, in_specs=ex.specs, out_specs=ex.specs, scratch_shapes=ex.scratch,
        compiler_params=pltpu.CompilerParams(has_side_effects=True),
    )(*ex.arrays)


def _gather_two_level(arrays, name):
    n = len(arrays)
    out_shape = [jax.ShapeDtypeStruct((N_DEV,) + a.shape, a.dtype) for a in arrays]

    def body(*refs):
        ins, outs = refs[:n], refs[n:2 * n]
        send_sems, recv_sems, local_sems = refs[2 * n:]
        x, y, c = lax.axis_index("x"), lax.axis_index("y"), lax.axis_index("c")
        sibling = (x, y, 1 - c)
        chips = [(1 - x, y), (x, 1 - y), (1 - x, 1 - y)]

        def slot(px, py, pc):
            return 4 * px + 2 * py + pc

        def copy(kind, t, block, to, own):
            return pltpu.make_async_remote_copy(
                src_ref=ins[t] if own else outs[t].at[slot(*block)], dst_ref=outs[t].at[slot(*block)],
                send_sem=send_sems.at[kind, t], recv_sem=recv_sems.at[kind, t],
                device_id=to, device_id_type=pl.DeviceIdType.MESH)

        local = [pltpu.make_async_copy(ins[t], outs[t].at[slot(x, y, c)], local_sems.at[t]) for t in range(n)]
        first = [copy(1 + j, t, (x, y, c), (*chip, c), True) for t in range(n) for j, chip in enumerate(chips)]
        first += [copy(0, t, (x, y, c), sibling, True) for t in range(n)]
        for cp in local + first:
            cp.start()
        passed = []
        for t in range(n):
            for j, chip in enumerate(chips):
                copy(1 + j, t, (*chip, c), (x, y, c), False).wait_recv()
                cp = copy(4 + j, t, (*chip, c), sibling, False)
                cp.start()
                passed.append(cp)
        for t in range(n):
            copy(0, t, sibling, (x, y, c), False).wait_recv()
            for j, chip in enumerate(chips):
                copy(4 + j, t, (*chip, 1 - c), (x, y, c), False).wait_recv()
        for cp in first + passed:
            cp.wait_send()
        for cp in local:
            cp.wait()

    any_spec = pl.BlockSpec(memory_space=pl.ANY)
    return pl.pallas_call(
        body, name=name, out_shape=out_shape, in_specs=[any_spec] * n, out_specs=[any_spec] * n,
        scratch_shapes=[pltpu.SemaphoreType.DMA((7, n)), pltpu.SemaphoreType.DMA((7, n)), pltpu.SemaphoreType.DMA((n,))],
        compiler_params=pltpu.CompilerParams(has_side_effects=True),
    )(*arrays)


def _inv_count(row, window):
    return 1.0 / jnp.minimum(row + 1, window).astype(F32)


def _pool_fwd(x, g_pre, g_post, w_in, w_group, scale, w_out, exchange_items):
    n_tok = x.shape[0]
    nt = n_tok // TM
    ex = _Exchange(exchange_items)

    def body(x_ref, gpre_ref, gpost_ref, win_ref, wg_ref, sc_ref, wout_ref, *rest):
        ex_in, rest = rest[:ex.n], rest[ex.n:]
        x1_ref, y_ref, z_ref, mixed_ref, mg_ref, prod_ref, ht_ref = rest[:7]
        ex_out, carry_ref, ex_sems = rest[7:7 + ex.n], rest[7 + ex.n], rest[8 + ex.n:]
        i = pl.program_id(0)

        @pl.when(i == 0)
        def _():
            ex.start(ex_in, ex_out, ex_sems)
            carry_ref[...] = jnp.zeros_like(carry_ref)

        xv = x_ref[...]
        r, xhat = _rms_fwd(xv)
        hf = xhat * gpre_ref[...]
        h = hf.astype(MXU)
        ht_ref[...] = hf.T.astype(MXU)
        row = i * TM + lax.broadcasted_iota(jnp.int32, (TM, 1), 0)
        y = None
        for g in range(N_GROUPS):
            cols = slice(g * GROUP, (g + 1) * GROUP)
            a = _dot(h, win_ref[g])
            z = _dot(h, win_ref[N_GROUPS + g])
            s = jnp.concatenate([carry_ref[g], a], axis=0)
            carry_ref[g] = a[TM - HALO:, :]
            w = 1
            while w < POOL_WINDOWS[g]:
                s = s + pltpu.roll(s, w, 0)
                w *= 2
            mixed = (s[HALO:, :] * _inv_count(row, POOL_WINDOWS[g]) - a).astype(MXU)
            mg = _dot(mixed, wg_ref[g])
            prod = (mg * sc_ref[:, cols] * (z * _sigmoid(z))).astype(MXU)
            z_ref[:, cols] = z
            mixed_ref[:, cols] = mixed
            mg_ref[:, cols] = mg
            prod_ref[:, cols] = prod
            part = _dot(prod, wout_ref[cols, :])
            y = part if y is None else y + part
        y_ref[...] = y
        _, yhat = _rms_fwd(y)
        x1_ref[...] = xv + yhat * gpost_ref[...]

        @pl.when(i == nt - 1)
        def _():
            ex.wait(ex_in, ex_out, ex_sems)

    tok = lambda w: pl.BlockSpec((TM, w), lambda i: (i, 0))
    return pl.pallas_call(
        body, name="pool_fwd", grid=(nt,),
        in_specs=[tok(D_MODEL), _const_spec((1, D_MODEL)), _const_spec((1, D_MODEL)),
                  _const_spec((N_DEV, D_MODEL, W_BLOCK)), _const_spec((N_GROUPS, GROUP, GROUP)),
                  _const_spec((1, POOL_WIDTH)), _const_spec((POOL_WIDTH, D_MODEL))] + ex.specs,
        out_specs=[tok(D_MODEL), tok(D_MODEL), tok(POOL_WIDTH), tok(POOL_WIDTH), tok(POOL_WIDTH), tok(POOL_WIDTH),
                   pl.BlockSpec((D_MODEL, TM), lambda i: (0, i))] + ex.specs,
        out_shape=[jax.ShapeDtypeStruct((n_tok, D_MODEL), F32), jax.ShapeDtypeStruct((n_tok, D_MODEL), F32),
                   jax.ShapeDtypeStruct((n_tok, POOL_WIDTH), F32), jax.ShapeDtypeStruct((n_tok, POOL_WIDTH), MXU),
                   jax.ShapeDtypeStruct((n_tok, POOL_WIDTH), F32), jax.ShapeDtypeStruct((n_tok, POOL_WIDTH), MXU),
                   jax.ShapeDtypeStruct((D_MODEL, n_tok), MXU)] + ex.out_shape,
        scratch_shapes=[pltpu.VMEM((N_GROUPS, HALO, GROUP), F32)] + ex.scratch,
        compiler_params=_params(1),
    )(x, g_pre, g_post, w_in, w_group, scale, w_out, *ex.arrays)


def _flush(acc_ref, out_hbm, stage_ref):
    for j in range(acc_ref.shape[0]):
        stage_ref[...] = acc_ref[j].astype(stage_ref.dtype)
        pltpu.sync_copy(stage_ref, out_hbm.at[j])


def _pool_bwd(dx1, y, z, mg, mixed, prod, g_post, scale, w_group, w_out, exchange_items):
    n_tok = dx1.shape[0]
    nt = n_tok // TM
    ex = _Exchange(exchange_items)

    def body(dx1_ref, y_ref, z_ref, mg_ref, mixed_ref, prod_ref, gpost_ref, sc_ref, wg_ref, wout_ref, *rest):
        ex_in, rest = rest[:ex.n], rest[ex.n:]
        du_ref, dsc_ref, dgpost_ref, dwg_hbm, dwout_hbm = rest[:5]
        ex_out, rest = rest[5:5 + ex.n], rest[5 + ex.n:]
        carry_ref, dwg_acc, dwout_acc, stage_g, stage_o = rest[:5]
        ex_sems = rest[5:]
        i = pl.program_id(0)

        @pl.when(i == 0)
        def _():
            ex.start(ex_in, ex_out, ex_sems)
            carry_ref[...] = jnp.zeros_like(carry_ref)
            dwg_acc[...] = jnp.zeros_like(dwg_acc)
            dwout_acc[...] = jnp.zeros_like(dwout_acc)
            dsc_ref[...] = jnp.zeros_like(dsc_ref)
            dgpost_ref[...] = jnp.zeros_like(dgpost_ref)

        dn = dx1_ref[...]
        r, yhat = _rms_fwd(y_ref[...])
        dgpost_ref[...] += jnp.sum(dn * yhat, axis=0, keepdims=True)
        dy = _rms_bwd(dn, yhat, r, gpost_ref[...]).astype(MXU)
        row = (nt - 1 - i) * TM + lax.broadcasted_iota(jnp.int32, (TM, 1), 0)
        n_ext = TM + HALO
        for g in range(N_GROUPS):
            cols = slice(g * GROUP, (g + 1) * GROUP)
            dwout_acc[g] += _dot_tn(prod_ref[:, cols], dy)
            dprod = _dot_nt(dy, wout_ref[cols, :])
            zv = z_ref[:, cols]
            sig = _sigmoid(zv)
            silu = zv * sig
            mgv = mg_ref[:, cols]
            sc = sc_ref[:, cols]
            dsc_ref[:, cols] += jnp.sum(dprod * silu * mgv, axis=0, keepdims=True)
            dmg = (dprod * silu * sc).astype(MXU)
            dz = dprod * (mgv * sc) * (sig * (1.0 + zv * (1.0 - sig)))
            dwg_acc[g] += _dot_tn(mixed_ref[:, cols], dmg)
            dmixed = _dot_nt(dmg, wg_ref[g])
            e = dmixed * _inv_count(row, POOL_WINDOWS[g])
            s = jnp.concatenate([e, carry_ref[g]], axis=0)
            carry_ref[g] = e[:HALO, :]
            w = 1
            while w < POOL_WINDOWS[g]:
                s = s + pltpu.roll(s, n_ext - w, 0)
                w *= 2
            du_ref[:, cols] = (s[:TM, :] - dmixed).astype(MXU)
            du_ref[:, POOL_WIDTH + g * GROUP:POOL_WIDTH + (g + 1) * GROUP] = dz.astype(MXU)

        @pl.when(i == nt - 1)
        def _():
            _flush(dwg_acc, dwg_hbm, stage_g)
            _flush(dwout_acc, dwout_hbm, stage_o)
            ex.wait(ex_in, ex_out, ex_sems)

    rev = lambda w: pl.BlockSpec((TM, w), lambda i: (nt - 1 - i, 0))
    any_spec = pl.BlockSpec(memory_space=pl.ANY)
    return pl.pallas_call(
        body, name="pool_bwd", grid=(nt,),
        in_specs=[rev(D_MODEL), rev(D_MODEL), rev(POOL_WIDTH), rev(POOL_WIDTH), rev(POOL_WIDTH), rev(POOL_WIDTH),
                  _const_spec((1, D_MODEL)), _const_spec((1, POOL_WIDTH)),
                  _const_spec((N_GROUPS, GROUP, GROUP)), _const_spec((POOL_WIDTH, D_MODEL))] + ex.specs,
        out_specs=[rev(2 * POOL_WIDTH), pl.BlockSpec((1, POOL_WIDTH), lambda i: (0, 0)),
                   pl.BlockSpec((1, D_MODEL), lambda i: (0, 0)), any_spec, any_spec] + ex.specs,
        out_shape=[jax.ShapeDtypeStruct((n_tok, 2 * POOL_WIDTH), MXU), jax.ShapeDtypeStruct((1, POOL_WIDTH), F32),
                   jax.ShapeDtypeStruct((1, D_MODEL), F32), jax.ShapeDtypeStruct((N_GROUPS, GROUP, GROUP), MXU),
                   jax.ShapeDtypeStruct((N_GROUPS, GROUP, D_MODEL), MXU)] + ex.out_shape,
        scratch_shapes=[pltpu.VMEM((N_GROUPS, HALO, GROUP), F32), pltpu.VMEM((N_GROUPS, GROUP, GROUP), F32),
                        pltpu.VMEM((N_GROUPS, GROUP, D_MODEL), F32), pltpu.VMEM((GROUP, GROUP), MXU),
                        pltpu.VMEM((GROUP, D_MODEL), MXU)] + ex.scratch,
        compiler_params=_params(1),
    )(dx1, y, z, mg, mixed, prod, g_post, scale, w_group, w_out, *ex.arrays)


def _in_proj_bwd(parts, x, dres, g_pre, w_in, name, with_dw, exchange_items):
    n_tok = x.shape[0]
    nt = n_tok // TM
    half = D_MODEL // W_BLOCK
    ex = _Exchange(exchange_items)
    n_dw = 1 if with_dw else 0

    def body(p0, p1, p2, p3, x_ref, dres_ref, g_ref, w_ref, *rest):
        ex_in, rest = rest[:ex.n], rest[ex.n:]
        dx_ref, dg_ref = rest[:2]
        dw_hbm = rest[2:2 + n_dw]
        ex_out, rest = rest[2 + n_dw:2 + n_dw + ex.n], rest[2 + n_dw + ex.n:]
        dw_scratch, ex_sems = rest[:2 * n_dw], rest[2 * n_dw:]
        i = pl.program_id(0)

        @pl.when(i == 0)
        def _():
            ex.start(ex_in, ex_out, ex_sems)
            dg_ref[...] = jnp.zeros_like(dg_ref)
            if with_dw:
                dw_scratch[0][...] = jnp.zeros_like(dw_scratch[0])

        r, xhat = _rms_fwd(x_ref[...])
        g = g_ref[...]
        h = (xhat * g).astype(MXU)
        dh = None
        for p, part_ref in enumerate((p0, p1, p2, p3)):
            for jj in range(half):
                j = half * p + jj
                du = part_ref[:, jj * W_BLOCK:(jj + 1) * W_BLOCK]
                t = _dot_nt(du, w_ref[j])
                dh = t if dh is None else dh + t
                if with_dw:
                    dw_scratch[0][j] += _dot_tn(h, du)
        dg_ref[...] += jnp.sum(dh * xhat, axis=0, keepdims=True)
        dx_ref[...] = dres_ref[...] + _rms_bwd(dh, xhat, r, g)

        @pl.when(i == nt - 1)
        def _():
            if with_dw:
                _flush(dw_scratch[0], dw_hbm[0], dw_scratch[1])
            ex.wait(ex_in, ex_out, ex_sems)

    tok = pl.BlockSpec((TM, D_MODEL), lambda i: (i, 0))
    return pl.pallas_call(
        body, name=name, grid=(nt,),
        in_specs=[pl.BlockSpec((TM, D_MODEL), m) for _, m in parts]
        + [tok, tok, _const_spec((1, D_MODEL)), _const_spec((N_DEV, D_MODEL, W_BLOCK))] + ex.specs,
        out_specs=[tok, pl.BlockSpec((1, D_MODEL), lambda i: (0, 0))]
        + [pl.BlockSpec(memory_space=pl.ANY)] * n_dw + ex.specs,
        out_shape=[jax.ShapeDtypeStruct((n_tok, D_MODEL), F32), jax.ShapeDtypeStruct((1, D_MODEL), F32)]
        + [jax.ShapeDtypeStruct((N_DEV, D_MODEL, W_BLOCK), MXU)] * n_dw + ex.out_shape,
        scratch_shapes=[pltpu.VMEM((N_DEV, D_MODEL, W_BLOCK), F32), pltpu.VMEM((D_MODEL, W_BLOCK), MXU)][:2 * n_dw]
        + ex.scratch,
        compiler_params=_params(1),
    )(*[a for a, _ in parts], x, dres, g_pre, w_in, *ex.arrays)


def _w_in_grad_scatter(h_t, du, exchange_items, name):
    n_tok = du.shape[0]
    ni = n_tok // TMB
    ex = _Exchange(exchange_items)
    me_out = 4 * lax.axis_index("x") + 2 * lax.axis_index("y") + lax.axis_index("c")
    order = ((me_out + 1 + jnp.arange(N_DEV, dtype=jnp.int32)) % N_DEV).astype(jnp.int32)

    def body(order_ref, h_ref, du_ref, *rest):
        ex_in, rest = rest[:ex.n], rest[ex.n:]
        part_hbm, ex_out, rest = rest[0], rest[1:1 + ex.n], rest[1 + ex.n:]
        acc_ref, stage_ref, send_sems, recv_sems = rest[:4]
        ex_sems = rest[4:]
        s = pl.program_id(0)
        i = pl.program_id(1)
        x, y, c = lax.axis_index("x"), lax.axis_index("y"), lax.axis_index("c")
        me = 4 * x + 2 * y + c

        def block_copy(step, src_slot, dst_slot, owner):
            return pltpu.make_async_remote_copy(
                src_ref=stage_ref.at[src_slot], dst_ref=part_hbm.at[dst_slot],
                send_sem=send_sems.at[step], recv_sem=recv_sems.at[step],
                device_id=(owner // 4, (owner // 2) % 2, owner % 2), device_id_type=pl.DeviceIdType.MESH)

        @pl.when((s == 0) & (i == 0))
        def _():
            ex.start(ex_in, ex_out, ex_sems)

        @pl.when(i == 0)
        def _():
            acc_ref[...] = jnp.zeros_like(acc_ref)

        acc_ref[...] += _dot(h_ref[:, pl.ds(pl.multiple_of(i * TMB, TMB), TMB)], du_ref[...])

        @pl.when(i == ni - 1)
        def _():
            stage_ref[s] = acc_ref[...].astype(MXU)

            @pl.when(s < N_DEV - 1)
            def _():
                block_copy(s, s, me, (me + 1 + s) % N_DEV).start()

            @pl.when(s == N_DEV - 1)
            def _():
                pltpu.sync_copy(stage_ref.at[s], part_hbm.at[me])
                for step in range(N_DEV - 1):
                    sender = (me + N_DEV - 1 - step) % N_DEV
                    block_copy(step, step, sender, me).wait_recv()
                    block_copy(step, step, me, me).wait_send()
                ex.wait(ex_in, ex_out, ex_sems)

    grid_spec = pltpu.PrefetchScalarGridSpec(
        num_scalar_prefetch=1, grid=(N_DEV, ni),
        in_specs=[pl.BlockSpec((D_MODEL, n_tok), lambda s, i, order: (0, 0), pipeline_mode=pl.Buffered(1)),
                  pl.BlockSpec((TMB, W_BLOCK), lambda s, i, order: (i, order[s]))] + ex.specs,
        out_specs=[pl.BlockSpec(memory_space=pl.ANY)] + ex.specs,
        scratch_shapes=[pltpu.VMEM((D_MODEL, W_BLOCK), F32), pltpu.VMEM((N_DEV, D_MODEL, W_BLOCK), MXU),
                        pltpu.SemaphoreType.DMA((N_DEV - 1,)), pltpu.SemaphoreType.DMA((N_DEV - 1,))] + ex.scratch)
    return pl.pallas_call(
        body, name=name, grid_spec=grid_spec,
        out_shape=[jax.ShapeDtypeStruct((N_DEV, D_MODEL, W_BLOCK), MXU)] + ex.out_shape,
        compiler_params=_params(2),
    )(order, h_t, du, *ex.arrays)


def _rel_onehot():
    rel = lax.broadcasted_iota(jnp.int32, (REL_PAD, ROLL_W), 0)
    col = lax.broadcasted_iota(jnp.int32, (REL_PAD, ROLL_W), 1)
    return (rel == jnp.minimum(BAND + MAX_REL - col, 2 * MAX_REL)).astype(MXU)


def _split3(v):
    hi = v.astype(MXU)
    r1 = v - hi.astype(F32)
    mid = r1.astype(MXU)
    lo = (r1 - mid.astype(F32)).astype(MXU)
    return hi, mid, lo


def _bias_table(rel_bias_padded):
    def body(rb_ref, out_ref):
        onehot = _rel_onehot()
        base = None
        for term in _split3(rb_ref[...]):
            t = _dot(term, onehot)
            base = t if base is None else base + t
        qi = lax.broadcasted_iota(jnp.int32, (CHUNK, ROLL_W), 0)
        kk = lax.broadcasted_iota(jnp.int32, (CHUNK, TKW), 1)
        for h in range(N_HEADS):
            t = jnp.broadcast_to(base[h:h + 1, :], (CHUNK, ROLL_W))
            for bit in range(6):
                t = jnp.where(((qi >> bit) & 1) == 1, pltpu.roll(t, 1 << bit, 1), t)
            for rr in range(TM // CHUNK):
                shifted = pltpu.roll(t, (CHUNK * rr - CHUNK) % ROLL_W, 1)[:, :TKW]
                band = kk - CHUNK * rr
                out_ref[h, rr * CHUNK:(rr + 1) * CHUNK, :] = jnp.where((band >= 0) & (band < BAND), shifted, NEG)

    return pl.pallas_call(
        body, name="bias_table", out_shape=jax.ShapeDtypeStruct((N_HEADS, TM, TKW), F32),
        compiler_params=pltpu.CompilerParams(vmem_limit_bytes=VMEM_LIMIT),
    )(rel_bias_padded)


def _bias_grad(dtab):
    def body(dt_ref, out_ref, dbase_ref):
        qi = lax.broadcasted_iota(jnp.int32, (CHUNK, ROLL_W), 0)
        zeros = jnp.zeros((CHUNK, ROLL_W - TKW), F32)
        for h in range(N_HEADS):
            t = None
            for rr in range(TM // CHUNK):
                blk = jnp.concatenate([dt_ref[h, rr * CHUNK:(rr + 1) * CHUNK, :], zeros], axis=1)
                blk = pltpu.roll(blk, (CHUNK - CHUNK * rr) % ROLL_W, 1)
                t = blk if t is None else t + blk
            for bit in range(6):
                t = jnp.where(((qi >> bit) & 1) == 1, pltpu.roll(t, ROLL_W - (1 << bit), 1), t)
            dbase_ref[h:h + 1, :] = jnp.sum(t, axis=0, keepdims=True)
        onehot = _rel_onehot()
        acc = None
        for term in _split3(dbase_ref[...]):
            t = _dot_nt(term, onehot)
            acc = t if acc is None else acc + t
        out_ref[...] = acc

    return pl.pallas_call(
        body, name="bias_grad", out_shape=jax.ShapeDtypeStruct((N_HEADS, REL_PAD), F32),
        scratch_shapes=[pltpu.VMEM((N_HEADS, ROLL_W), F32)],
        compiler_params=pltpu.CompilerParams(vmem_limit_bytes=VMEM_LIMIT),
    )(dtab)


def _att_in(x1, g_pre, w_in):
    n_tok = x1.shape[0]
    nt = n_tok // TM
    lead = PAD // TM

    def body(x_ref, g_ref, w_ref, q_ref, k_ref, v_ref, z_ref):
        i = pl.program_id(0)

        @pl.when(i < lead)
        def _():
            k_ref[...] = jnp.zeros_like(k_ref)
            v_ref[...] = jnp.zeros_like(v_ref)

        @pl.when(i >= lead)
        def _():
            _, xhat = _rms_fwd(x_ref[...])
            h = (xhat * g_ref[...]).astype(MXU)
            for j in range(N_DEV):
                u = _dot(h, w_ref[j])
                cols = slice((j % 2) * W_BLOCK, (j % 2 + 1) * W_BLOCK)
                if j < 2:
                    q_ref[:, cols] = (u * QK_SCALE).astype(MXU)
                elif j < 4:
                    k_ref[:, cols] = u.astype(MXU)
                elif j < 6:
                    v_ref[:, cols] = u.astype(MXU)
                else:
                    z_ref[:, cols] = u

    late = pl.BlockSpec((TM, D_MODEL), lambda i: (jnp.maximum(i - lead, 0), 0))
    padded = pl.BlockSpec((TM, D_MODEL), lambda i: (i, 0))
    return pl.pallas_call(
        body, name="att_in", grid=(nt + lead,),
        in_specs=[late, _const_spec((1, D_MODEL)), _const_spec((N_DEV, D_MODEL, W_BLOCK))],
        out_specs=[late, padded, padded, late],
        out_shape=[jax.ShapeDtypeStruct((n_tok, ATT_WIDTH), MXU), jax.ShapeDtypeStruct((n_tok + PAD, ATT_WIDTH), MXU),
                   jax.ShapeDtypeStruct((n_tok + PAD, ATT_WIDTH), MXU), jax.ShapeDtypeStruct((n_tok, ATT_WIDTH), F32)],
        compiler_params=_params(1),
    )(x1, g_pre, w_in)


def _head_masks():
    lane = lax.broadcasted_iota(jnp.int32, (1, 2 * HEAD_DIM), 1)
    first = lane < HEAD_DIM
    return first, jnp.logical_not(first)


def _softmax_blocks(qh, k_refs, bias_ref, hh, tile):
    ss = []
    for b in range(KB):
        s = _dot_nt(qh, k_refs[b][...]) + bias_ref[hh, :, b * TM:(b + 1) * TM]
        ss.append(s + jnp.where(tile + b < PAD // TM, NEG, 0.0).astype(F32))
    m = None
    for s in ss:
        mb = jnp.max(s, axis=-1, keepdims=True)
        m = mb if m is None else jnp.maximum(m, mb)
    ps = [jnp.exp(s - m) for s in ss]
    l = None
    for p in ps:
        lb = jnp.sum(p, axis=-1, keepdims=True)
        l = lb if l is None else l + lb
    inv = 1.0 / l
    return [p * inv for p in ps]


def _key_specs():
    return [pl.BlockSpec((TM, 2 * HEAD_DIM), functools.partial(lambda hp, i, b: (i + b, hp), b=b)) for b in range(KB)]


def _att_fwd(q, kpad, vpad, bias_tab):
    n_tok = q.shape[0]
    nt = n_tok // TM

    def body(q_ref, k0, k1, k2, v0, v1, v2, bias_ref, o_ref):
        i = pl.program_id(1)
        qv = q_ref[...]
        masks = _head_masks()
        outs = []
        for hh in range(2):
            qh = jnp.where(masks[hh], qv, jnp.zeros_like(qv))
            ps = _softmax_blocks(qh, (k0, k1, k2), bias_ref, hh, i)
            o = None
            for b, v_ref in enumerate((v0, v1, v2)):
                t = _dot(ps[b].astype(MXU), v_ref[...])
                o = t if o is None else o + t
            outs.append(o)
        o_ref[...] = jnp.where(masks[0], outs[0], outs[1])

    qspec = pl.BlockSpec((TM, 2 * HEAD_DIM), lambda hp, i: (i, hp))
    return pl.pallas_call(
        body, name="att_fwd", grid=(N_HEADS // 2, nt),
        in_specs=[qspec] + _key_specs() + _key_specs() + [pl.BlockSpec((2, TM, TKW), lambda hp, i: (hp, 0, 0))],
        out_specs=qspec,
        out_shape=jax.ShapeDtypeStruct((n_tok, ATT_WIDTH), F32),
        compiler_params=_params(2),
    )(q, kpad, kpad, kpad, vpad, vpad, vpad, bias_tab)


def _att_bwd(q, kpad, vpad, do, bias_tab):
    n_tok = q.shape[0]
    nt = n_tok // TM

    def body(q_ref, do_ref, k0, k1, k2, v0, v1, v2, bias_ref,
             dq_ref, dk_ref, dv_ref, dtab_ref, rk0, rk1, rv0, rv1):
        i = pl.program_id(1)

        @pl.when(i == 0)
        def _():
            for ref in (rk0, rk1, rv0, rv1):
                ref[...] = jnp.zeros_like(ref)
            dtab_ref[...] = jnp.zeros_like(dtab_ref)

        @pl.when(i < nt)
        def _():
            qv = q_ref[...]
            dov = do_ref[...]
            masks = _head_masks()
            dks = [None] * KB
            dvs = [None] * KB
            dqs = []
            for hh in range(2):
                qh = jnp.where(masks[hh], qv, jnp.zeros_like(qv))
                doh = jnp.where(masks[hh], dov, jnp.zeros_like(dov))
                ps = _softmax_blocks(qh, (k0, k1, k2), bias_ref, hh, i)
                dps = [_dot_nt(doh, v_ref[...]) for v_ref in (v0, v1, v2)]
                dsum = None
                for p, dp in zip(ps, dps):
                    t = jnp.sum(p * dp, axis=-1, keepdims=True)
                    dsum = t if dsum is None else dsum + t
                dq = None
                for b, k_ref in enumerate((k0, k1, k2)):
                    ds = ps[b] * (dps[b] - dsum)
                    dtab_ref[hh, :, b * TM:(b + 1) * TM] += ds
                    dsb = ds.astype(MXU)
                    t = _dot(dsb, k_ref[...])
                    dq = t if dq is None else dq + t
                    t = _dot_tn(dsb, qh)
                    dks[b] = t if dks[b] is None else dks[b] + t
                    t = _dot_tn(ps[b].astype(MXU), doh)
                    dvs[b] = t if dvs[b] is None else dvs[b] + t
                dqs.append(dq)
            dq_ref[...] = (jnp.where(masks[0], dqs[0], dqs[1]) * QK_SCALE).astype(MXU)
            dk_ref[...] = (rk0[...] + dks[0]).astype(MXU)
            dv_ref[...] = (rv0[...] + dvs[0]).astype(MXU)
            rk0[...] = rk1[...] + dks[1]
            rv0[...] = rv1[...] + dvs[1]
            rk1[...] = dks[2]
            rv1[...] = dvs[2]

        @pl.when(i >= nt)
        def _():
            dk_ref[...] = rk0[...].astype(MXU)
            dv_ref[...] = rv0[...].astype(MXU)
            rk0[...] = rk1[...]
            rv0[...] = rv1[...]

    last = nt - 1
    qspec = pl.BlockSpec((TM, 2 * HEAD_DIM), lambda hp, i: (jnp.minimum(i, last), hp))
    kspecs = [pl.BlockSpec((TM, 2 * HEAD_DIM), functools.partial(lambda hp, i, b: (jnp.minimum(i, last) + b, hp), b=b))
              for b in range(KB)]
    pspec = pl.BlockSpec((TM, 2 * HEAD_DIM), lambda hp, i: (i, hp))
    tspec = pl.BlockSpec((2, TM, TKW), lambda hp, i: (hp, 0, 0))
    ring = pltpu.VMEM((TM, 2 * HEAD_DIM), F32)
    return pl.pallas_call(
        body, name="att_bwd", grid=(N_HEADS // 2, nt + KB - 1),
        in_specs=[qspec, qspec] + kspecs + kspecs + [tspec],
        out_specs=[qspec, pspec, pspec, tspec],
        out_shape=[jax.ShapeDtypeStruct((n_tok, ATT_WIDTH), MXU), jax.ShapeDtypeStruct((n_tok + PAD, ATT_WIDTH), MXU),
                   jax.ShapeDtypeStruct((n_tok + PAD, ATT_WIDTH), MXU), jax.ShapeDtypeStruct((N_HEADS, TM, TKW), F32)],
        scratch_shapes=[ring, ring, ring, ring],
        compiler_params=_params(2),
    )(q, do, kpad, kpad, kpad, vpad, vpad, vpad, bias_tab)


def _att_out(o, z, x1, target, g_post, w_out):
    n_tok = o.shape[0]
    nt = n_tok // TM

    def body(o_ref, z_ref, x1_ref, tgt_ref, gpost_ref, w_ref,
             loss_ref, dx2_ref, do_ref, dz_ref, dgpost_ref, dw_hbm, acc_ref, loss_acc, stage_ref):
        i = pl.program_id(0)

        @pl.when(i == 0)
        def _():
            acc_ref[...] = jnp.zeros_like(acc_ref)
            loss_acc[...] = jnp.zeros_like(loss_acc)
            dgpost_ref[...] = jnp.zeros_like(dgpost_ref)

        ov = o_ref[...]
        zv = z_ref[...]
        sig = _sigmoid(zv)
        silu = zv * sig
        gated = (ov * silu).astype(MXU)
        y = _dot(gated, w_ref[...])
        r, yhat = _rms_fwd(y)
        gpost = gpost_ref[...]
        diff = x1_ref[...] + yhat * gpost - tgt_ref[...]
        loss_acc[...] += jnp.sum(diff * diff, axis=0, keepdims=True)
        dn = diff * (1.0 / D_MODEL)
        dx2_ref[...] = dn
        dgpost_ref[...] += jnp.sum(dn * yhat, axis=0, keepdims=True)
        dy = _rms_bwd(dn, yhat, r, gpost).astype(MXU)
        for j in range(ATT_WIDTH // W_BLOCK):
            acc_ref[j] += _dot_tn(gated[:, j * W_BLOCK:(j + 1) * W_BLOCK], dy)
        dgated = _dot_nt(dy, w_ref[...])
        do_ref[...] = (dgated * silu).astype(MXU)
        dz_ref[...] = (dgated * ov * (sig * (1.0 + zv * (1.0 - sig)))).astype(MXU)

        @pl.when(i == nt - 1)
        def _():
            total = jnp.sum(loss_acc[...], axis=-1, keepdims=True) * (0.5 / D_MODEL)
            loss_ref[...] = jnp.broadcast_to(total, loss_ref.shape)
            _flush(acc_ref, dw_hbm, stage_ref)

    tok = pl.BlockSpec((TM, D_MODEL), lambda i: (i, 0))
    return pl.pallas_call(
        body, name="att_out", grid=(nt,),
        in_specs=[tok, tok, tok, tok, _const_spec((1, D_MODEL)), _const_spec((ATT_WIDTH, D_MODEL))],
        out_specs=[pl.BlockSpec((1, 128), lambda i: (0, 0)), tok, tok, tok,
                   pl.BlockSpec((1, D_MODEL), lambda i: (0, 0)), pl.BlockSpec(memory_space=pl.ANY)],
        out_shape=[jax.ShapeDtypeStruct((1, 128), F32), jax.ShapeDtypeStruct((n_tok, D_MODEL), F32),
                   jax.ShapeDtypeStruct((n_tok, ATT_WIDTH), MXU), jax.ShapeDtypeStruct((n_tok, ATT_WIDTH), MXU),
                   jax.ShapeDtypeStruct((1, D_MODEL), F32),
                   jax.ShapeDtypeStruct((ATT_WIDTH // W_BLOCK, W_BLOCK, D_MODEL), MXU)],
        scratch_shapes=[pltpu.VMEM((ATT_WIDTH // W_BLOCK, W_BLOCK, D_MODEL), F32), pltpu.VMEM((1, D_MODEL), F32),
                        pltpu.VMEM((W_BLOCK, D_MODEL), MXU)],
        compiler_params=_params(1),
    )(o, z, x1, target, g_post, w_out)


def _adamw(parts, w, m, v, name):
    rows, cols = w.shape
    tr = min(rows, 256)

    def body(p_ref, w_ref, m_ref, v_ref, g_ref, d_ref, mo_ref, vo_ref):
        g = p_ref[0].astype(F32)
        for s in range(1, N_DEV):
            g = g + p_ref[s].astype(F32)
        m_new = ADAM_B1 * m_ref[...] + (1.0 - ADAM_B1) * g
        v_new = ADAM_B2 * v_ref[...] + (1.0 - ADAM_B2) * (g * g)
        m_hat = m_new / (1.0 - ADAM_B1 ** ADAM_STEP)
        v_hat = v_new / (1.0 - ADAM_B2 ** ADAM_STEP)
        g_ref[...] = g
        d_ref[...] = -ADAM_LR * (m_hat / (jnp.sqrt(v_hat) + ADAM_EPS) + ADAM_WD * w_ref[...])
        mo_ref[...] = m_new
        vo_ref[...] = v_new

    blk = pl.BlockSpec((tr, cols), lambda i: (i, 0))
    shape = jax.ShapeDtypeStruct((rows, cols), F32)
    return pl.pallas_call(
        body, name=name, grid=(rows // tr,),
        in_specs=[pl.BlockSpec((N_DEV, tr, cols), lambda i: (0, i, 0)), blk, blk, blk],
        out_specs=[blk, blk, blk, blk], out_shape=[shape, shape, shape, shape],
        compiler_params=_params(1),
    )(parts, w, m, v)


SMALL_ROWS = 16


def _pack_small(norm_pre, norm_post, pool_scale, rel_bias_padded):
    return jnp.concatenate([norm_pre, norm_post, pool_scale.reshape(2, D_MODEL),
                            rel_bias_padded.reshape(SMALL_ROWS - 6, D_MODEL)], axis=0)


def _unpack_small(packed):
    rel = packed[6:].reshape(N_HEADS, REL_PAD)[:, :N_REL]
    return packed[0:2], packed[2:4], packed[4:6].reshape(1, POOL_WIDTH), rel.reshape(1, N_HEADS, N_REL)


def _pad_rel(rel_bias):
    return jnp.pad(rel_bias.reshape(N_HEADS, N_REL), ((0, 0), (0, REL_PAD - N_REL)))


def kernel(x, norm_pre, norm_post, pool_w_in, pool_w_group, pool_scale, pool_w_out, att_w_in, att_rel_bias, att_w_out, loss_target, m_norm_pre, m_norm_post, m_pool_w_in, m_pool_w_group, m_pool_scale, m_pool_w_out, m_att_w_in, m_att_rel_bias, m_att_w_out, v_norm_pre, v_norm_post, v_pool_w_in, v_pool_w_group, v_pool_scale, v_pool_w_out, v_att_w_in, v_att_rel_bias, v_att_w_out):
    xt = x[0]
    target = loss_target[0]
    n_tok = xt.shape[0]
    lead = PAD // TM
    rows_g = GROUP // N_DEV

    gathered = _gather_two_level([pool_w_in[0].astype(MXU), pool_w_group[0].astype(MXU), pool_w_out[0].astype(MXU)],
                                 "gather_pool_weights")
    w_in_p = gathered[0]
    w_group = gathered[1].transpose(1, 0, 2, 3).reshape(N_GROUPS, GROUP, GROUP)
    w_out_p = gathered[2].reshape(POOL_WIDTH, D_MODEL)
    rel_padded = _pad_rel(att_rel_bias[0])
    bias_tab = _bias_table(rel_padded)

    x1, y0, z0, mixed, mg, prod, h0_t, w_in_a, w_out_a = _pool_fwd(
        xt, norm_pre[0:1], norm_post[0:1], w_in_p, w_group, pool_scale, w_out_p,
        [(att_w_in[0].astype(MXU), False), (att_w_out[0].astype(MXU), False)])
    w_out_a = w_out_a.reshape(ATT_WIDTH, D_MODEL)
    q, kpad, vpad, z1 = _att_in(x1, norm_pre[1:2], w_in_a)
    o = _att_fwd(q, kpad, vpad, bias_tab)
    loss_part, dx2, do, dz1, d_gpost1, d_w_out_a = _att_out(o, z1, x1, target, norm_post[1:2], w_out_a)
    dq, dkpad, dvpad, dtab = _att_bwd(q, kpad, vpad, do, bias_tab)
    d_rel = _bias_grad(dtab)
    plain = lambda i: (i, 0)
    shifted = lambda i: (i + lead, 0)
    dx1, d_gpre1, d_w_in_a = _in_proj_bwd([(dq, plain), (dkpad, shifted), (dvpad, shifted), (dz1, plain)],
                                          x1, dx2, norm_pre[1:2], w_in_a, "att_in_bwd", True, [])
    du0, d_scale, d_gpost0, d_w_group, d_w_out_p, part_w_in_a, part_w_out_a = _pool_bwd(
        dx1, y0, z0, mg, mixed, prod, norm_post[0:1], pool_scale, w_group, w_out_p,
        [(d_w_in_a, True), (d_w_out_a.reshape(N_DEV, ATT_WIDTH // N_DEV, D_MODEL), True)])
    col = lambda p: (lambda i: (i, p))
    grad_x, d_gpre0, part_w_group, part_w_out_p = _in_proj_bwd(
        [(du0, col(p)) for p in range(4)], xt, dx1, norm_pre[0:1], w_in_p, "pool_in_bwd", False,
        [(d_w_group.reshape(N_GROUPS, N_DEV, rows_g, GROUP).transpose(1, 0, 2, 3), True),
         (d_w_out_p.reshape(N_DEV, POOL_WIDTH // N_DEV, D_MODEL), True)])
    d_small = _pack_small(jnp.concatenate([d_gpre0, d_gpre1], axis=0), jnp.concatenate([d_gpost0, d_gpost1], axis=0),
                          d_scale, d_rel)
    part_w_in_p, part_small = _w_in_grad_scatter(h0_t, du0, [(d_small, False)], "pool_w_in_grad")

    def update(part, w, m, v, name):
        shape = w.shape
        flat = lambda a: a.reshape(-1, shape[-1])
        outs = _adamw(part.reshape(N_DEV, -1, shape[-1]), flat(w), flat(m), flat(v), name)
        return [a.reshape(shape) for a in outs]

    u_att_w_in = update(part_w_in_a, att_w_in, m_att_w_in, v_att_w_in, "adamw_att_w_in")
    u_att_w_out = update(part_w_out_a, att_w_out, m_att_w_out, v_att_w_out, "adamw_att_w_out")
    u_pool_w_group = update(part_w_group, pool_w_group, m_pool_w_group, v_pool_w_group, "adamw_pool_w_group")
    u_pool_w_out = update(part_w_out_p, pool_w_out, m_pool_w_out, v_pool_w_out, "adamw_pool_w_out")
    u_pool_w_in = update(part_w_in_p, pool_w_in, m_pool_w_in, v_pool_w_in, "adamw_pool_w_in")
    small = _adamw(part_small, _pack_small(norm_pre, norm_post, pool_scale, rel_padded),
                   _pack_small(m_norm_pre, m_norm_post, m_pool_scale, _pad_rel(m_att_rel_bias[0])),
                   _pack_small(v_norm_pre, v_norm_post, v_pool_scale, _pad_rel(v_att_rel_bias[0])), "adamw_small")
    u_small = [_unpack_small(a) for a in small]

    loss = lax.psum(loss_part[0, 0], ("x", "y", "c"))
    outs = [loss, grad_x.reshape(1, n_tok, D_MODEL)]
    for kind in range(4):
        outs += [u_small[kind][0], u_small[kind][1], u_pool_w_in[kind], u_pool_w_group[kind], u_small[kind][2],
                 u_pool_w_out[kind], u_att_w_in[kind], u_small[kind][3], u_att_w_out[kind]]
    return tuple(outs)
```

```python
import functools

import jax
import jax.numpy as jnp
from jax import lax
from jax.experimental import pallas as pl
from jax.experimental.pallas import tpu as pltpu

F32 = jnp.float32
MXU = jnp.bfloat16

D_MODEL = 1024
POOL_WIDTH = 2048
POOL_WINDOWS = (2, 4, 8, 16)
N_GROUPS = 4
GROUP = 512
HALO = 16
N_HEADS = 16
HEAD_DIM = 64
CHUNK = 64
LEFT_CHUNKS = 8
PAD = LEFT_CHUNKS * CHUNK
BAND = PAD + CHUNK
MAX_REL = 256
N_REL = 2 * MAX_REL + 1
REL_PAD = 640
ATT_WIDTH = 1024
PAIR = 2 * HEAD_DIM
N_PAIRS = N_HEADS // 2
N_DEV = 8
W_BLOCK = 512
RMS_EPS = 1e-6
QK_SCALE = 0.125
NEG = -1e30

TM = 256
TMB = 1024
KB = 3
TKW = KB * TM
PAIRS_PER_STEP = 4
FWD_UNROLL = 4
BWD_UNROLL = 2
ROLL_W = 1024

VMEM_LIMIT = 56 * 1024 * 1024

ADAM_LR = 0.001
ADAM_B1 = 0.9
ADAM_B2 = 0.999
ADAM_EPS = 1e-08
ADAM_WD = 0.01
ADAM_STEP = 10

NT_DIMS = (((1,), (1,)), ((), ()))
TN_DIMS = (((0,), (0,)), ((), ()))


def _params(n_grid):
    return pltpu.CompilerParams(dimension_semantics=("arbitrary",) * n_grid, vmem_limit_bytes=VMEM_LIMIT)


def _const_spec(shape):
    nd = len(shape)
    return pl.BlockSpec(shape, lambda *_: (0,) * nd, pipeline_mode=pl.Buffered(1))


def _dot(a, b):
    return jnp.dot(a, b, preferred_element_type=F32)


def _dot_nt(a, b):
    return lax.dot_general(a, b, NT_DIMS, preferred_element_type=F32)


def _dot_tn(a, b):
    return lax.dot_general(a, b, TN_DIMS, preferred_element_type=F32)


def _sigmoid(z):
    return 1.0 / (1.0 + jnp.exp(-z))


def _rms_fwd(xv):
    r = lax.rsqrt(jnp.mean(xv * xv, axis=-1, keepdims=True) + RMS_EPS)
    return r, xv * r


def _rms_bwd(dn, xhat, r, g):
    dng = dn * g
    return r * (dng - xhat * jnp.mean(dng * xhat, axis=-1, keepdims=True))


class _Exchange:
    def __init__(self, items):
        self.arrays = [a for a, _ in items]
        self.scatter = [s for _, s in items]
        self.n = len(items)
        self.out_shape = [jax.ShapeDtypeStruct((N_DEV,) + tuple(a.shape[1:] if s else a.shape), a.dtype)
                          for a, s in items]
        self.specs = [pl.BlockSpec(memory_space=pl.ANY)] * self.n
        self.scratch = ([pltpu.SemaphoreType.DMA((N_DEV - 1, self.n)), pltpu.SemaphoreType.DMA((N_DEV - 1, self.n)),
                         pltpu.SemaphoreType.DMA((self.n,))] if self.n else [])

    def _copies(self, ins, outs, sems, with_receives):
        send_sems, recv_sems, local_sems = sems
        x, y, c = lax.axis_index("x"), lax.axis_index("y"), lax.axis_index("c")
        me = 4 * x + 2 * y + c

        def src(t, slot):
            return ins[t].at[slot] if self.scatter[t] else ins[t]

        local = [pltpu.make_async_copy(src(t, me), outs[t].at[me], local_sems.at[t]) for t in range(self.n)]
        sends, recvs = [], []
        for k in range(1, N_DEV):
            px = 1 - x if k & 4 else x
            py = 1 - y if k & 2 else y
            pc = 1 - c if k & 1 else c
            peer = 4 * px + 2 * py + pc
            for t in range(self.n):
                common = dict(src_ref=src(t, peer), send_sem=send_sems.at[k - 1, t], recv_sem=recv_sems.at[k - 1, t],
                              device_id=(px, py, pc), device_id_type=pl.DeviceIdType.MESH)
                sends.append(pltpu.make_async_remote_copy(dst_ref=outs[t].at[me], **common))
                if with_receives:
                    recvs.append(pltpu.make_async_remote_copy(dst_ref=outs[t].at[peer], **common))
        return local, sends, recvs

    def start(self, ins, outs, sems):
        if self.n:
            local, sends, _ = self._copies(ins, outs, sems, False)
            for cp in local + sends:
                cp.start()

    def wait(self, ins, outs, sems):
        if self.n:
            local, sends, recvs = self._copies(ins, outs, sems, True)
            for cp in recvs:
                cp.wait_recv()
            for cp in sends:
                cp.wait_send()
            for cp in local:
                cp.wait()


def _exchange(items, name):
    ex = _Exchange(items)
    n = ex.n

    def body(*refs):
        ins, outs, sems = refs[:n], refs[n:2 * n], refs[2 * n:]
        ex.start(ins, outs, sems)
        ex.wait(ins, outs, sems)

    return pl.pallas_call(
        body, name=name, out_shape=ex.out_shape, in_specs=ex.specs, out_specs=ex.specs, scratch_shapes=ex.scratch,
        compiler_params=pltpu.CompilerParams(has_side_effects=True),
    )(*ex.arrays)


def _gather_two_level(arrays, name):
    n = len(arrays)
    out_shape = [jax.ShapeDtypeStruct((N_DEV,) + a.shape, a.dtype) for a in arrays]

    def body(*refs):
        ins, outs = refs[:n], refs[n:2 * n]
        send_sems, recv_sems, local_sems = refs[2 * n:]
        x, y, c = lax.axis_index("x"), lax.axis_index("y"), lax.axis_index("c")
        sibling = (x, y, 1 - c)
        chips = [(1 - x, y), (x, 1 - y), (1 - x, 1 - y)]

        def slot(px, py, pc):
            return 4 * px + 2 * py + pc

        def copy(kind, t, block, to, own):
            return pltpu.make_async_remote_copy(
                src_ref=ins[t] if own else outs[t].at[slot(*block)], dst_ref=outs[t].at[slot(*block)],
                send_sem=send_sems.at[kind, t], recv_sem=recv_sems.at[kind, t],
                device_id=to, device_id_type=pl.DeviceIdType.MESH)

        local = [pltpu.make_async_copy(ins[t], outs[t].at[slot(x, y, c)], local_sems.at[t]) for t in range(n)]
        first = [copy(1 + j, t, (x, y, c), (*chip, c), True) for t in range(n) for j, chip in enumerate(chips)]
        first += [copy(0, t, (x, y, c), sibling, True) for t in range(n)]
        for cp in local + first:
            cp.start()
        passed = []
        for t in range(n):
            for j, chip in enumerate(chips):
                copy(1 + j, t, (*chip, c), (x, y, c), False).wait_recv()
                cp = copy(4 + j, t, (*chip, c), sibling, False)
                cp.start()
                passed.append(cp)
        for t in range(n):
            copy(0, t, sibling, (x, y, c), False).wait_recv()
            for j, chip in enumerate(chips):
                copy(4 + j, t, (*chip, 1 - c), (x, y, c), False).wait_recv()
        for cp in first + passed:
            cp.wait_send()
        for cp in local:
            cp.wait()

    any_spec = pl.BlockSpec(memory_space=pl.ANY)
    return pl.pallas_call(
        body, name=name, out_shape=out_shape, in_specs=[any_spec] * n, out_specs=[any_spec] * n,
        scratch_shapes=[pltpu.SemaphoreType.DMA((7, n)), pltpu.SemaphoreType.DMA((7, n)), pltpu.SemaphoreType.DMA((n,))],
        compiler_params=pltpu.CompilerParams(has_side_effects=True),
    )(*arrays)


def _inv_count(row, window):
    return 1.0 / jnp.minimum(row + 1, window).astype(F32)


def _pool_fwd(x, g_pre, g_post, w_in, w_group, scale, w_out, exchange_items):
    n_tok = x.shape[0]
    nt = n_tok // TM
    ex = _Exchange(exchange_items)

    def body(x_ref, gpre_ref, gpost_ref, win_ref, wg_ref, sc_ref, wout_ref, *rest):
        ex_in, rest = rest[:ex.n], rest[ex.n:]
        x1_ref, y_ref, z_ref, mixed_ref, mg_ref, prod_ref, ht_ref = rest[:7]
        ex_out, carry_ref, ex_sems = rest[7:7 + ex.n], rest[7 + ex.n], rest[8 + ex.n:]
        i = pl.program_id(0)

        @pl.when(i == 0)
        def _():
            ex.start(ex_in, ex_out, ex_sems)
            carry_ref[...] = jnp.zeros_like(carry_ref)

        xv = x_ref[...]
        r, xhat = _rms_fwd(xv)
        hf = xhat * gpre_ref[...]
        h = hf.astype(MXU)
        ht_ref[...] = hf.T.astype(MXU)
        row = i * TM + lax.broadcasted_iota(jnp.int32, (TM, 1), 0)
        y = None
        for g in range(N_GROUPS):
            cols = slice(g * GROUP, (g + 1) * GROUP)
            a = _dot(h, win_ref[g])
            z = _dot(h, win_ref[N_GROUPS + g])
            s = jnp.concatenate([carry_ref[g], a], axis=0)
            carry_ref[g] = a[TM - HALO:, :]
            w = 1
            while w < POOL_WINDOWS[g]:
                s = s + pltpu.roll(s, w, 0)
                w *= 2
            mixed = (s[HALO:, :] * _inv_count(row, POOL_WINDOWS[g]) - a).astype(MXU)
            mg = _dot(mixed, wg_ref[g])
            prod = (mg * sc_ref[:, cols] * (z * _sigmoid(z))).astype(MXU)
            z_ref[:, cols] = z
            mixed_ref[:, cols] = mixed
            mg_ref[:, cols] = mg
            prod_ref[:, cols] = prod
            part = _dot(prod, wout_ref[cols, :])
            y = part if y is None else y + part
        y_ref[...] = y
        _, yhat = _rms_fwd(y)
        x1_ref[...] = xv + yhat * gpost_ref[...]

        @pl.when(i == nt - 1)
        def _():
            ex.wait(ex_in, ex_out, ex_sems)

    tok = lambda w: pl.BlockSpec((TM, w), lambda i: (i, 0))
    return pl.pallas_call(
        body, name="pool_fwd", grid=(nt,),
        in_specs=[tok(D_MODEL), _const_spec((1, D_MODEL)), _const_spec((1, D_MODEL)),
                  _const_spec((N_DEV, D_MODEL, W_BLOCK)), _const_spec((N_GROUPS, GROUP, GROUP)),
                  _const_spec((1, POOL_WIDTH)), _const_spec((POOL_WIDTH, D_MODEL))] + ex.specs,
        out_specs=[tok(D_MODEL), tok(D_MODEL), tok(POOL_WIDTH), tok(POOL_WIDTH), tok(POOL_WIDTH), tok(POOL_WIDTH),
                   pl.BlockSpec((D_MODEL, TM), lambda i: (0, i))] + ex.specs,
        out_shape=[jax.ShapeDtypeStruct((n_tok, D_MODEL), F32), jax.ShapeDtypeStruct((n_tok, D_MODEL), F32),
                   jax.ShapeDtypeStruct((n_tok, POOL_WIDTH), F32), jax.ShapeDtypeStruct((n_tok, POOL_WIDTH), MXU),
                   jax.ShapeDtypeStruct((n_tok, POOL_WIDTH), F32), jax.ShapeDtypeStruct((n_tok, POOL_WIDTH), MXU),
                   jax.ShapeDtypeStruct((D_MODEL, n_tok), MXU)] + ex.out_shape,
        scratch_shapes=[pltpu.VMEM((N_GROUPS, HALO, GROUP), F32)] + ex.scratch,
        compiler_params=_params(1),
    )(x, g_pre, g_post, w_in, w_group, scale, w_out, *ex.arrays)


def _flush(acc_ref, out_hbm, stage_ref):
    for j in range(acc_ref.shape[0]):
        stage_ref[...] = acc_ref[j].astype(stage_ref.dtype)
        pltpu.sync_copy(stage_ref, out_hbm.at[j])


def _pool_bwd(dx1, y, z, mg, mixed, prod, g_post, scale, w_group, w_out, exchange_items):
    n_tok = dx1.shape[0]
    nt = n_tok // TM
    ex = _Exchange(exchange_items)

    def body(dx1_ref, y_ref, z_ref, mg_ref, mixed_ref, prod_ref, gpost_ref, sc_ref, wg_ref, wout_ref, *rest):
        ex_in, rest = rest[:ex.n], rest[ex.n:]
        du_ref, dsc_ref, dgpost_ref, dwg_hbm, dwout_hbm = rest[:5]
        ex_out, rest = rest[5:5 + ex.n], rest[5 + ex.n:]
        carry_ref, dwg_acc, dwout_acc, stage_g, stage_o = rest[:5]
        ex_sems = rest[5:]
        i = pl.program_id(0)

        @pl.when(i == 0)
        def _():
            ex.start(ex_in, ex_out, ex_sems)
            carry_ref[...] = jnp.zeros_like(carry_ref)
            dwg_acc[...] = jnp.zeros_like(dwg_acc)
            dwout_acc[...] = jnp.zeros_like(dwout_acc)
            dsc_ref[...] = jnp.zeros_like(dsc_ref)
            dgpost_ref[...] = jnp.zeros_like(dgpost_ref)

        dn = dx1_ref[...]
        r, yhat = _rms_fwd(y_ref[...])
        dgpost_ref[...] += jnp.sum(dn * yhat, axis=0, keepdims=True)
        dy = _rms_bwd(dn, yhat, r, gpost_ref[...]).astype(MXU)
        row = (nt - 1 - i) * TM + lax.broadcasted_iota(jnp.int32, (TM, 1), 0)
        n_ext = TM + HALO
        for g in range(N_GROUPS):
            cols = slice(g * GROUP, (g + 1) * GROUP)
            dwout_acc[g] += _dot_tn(prod_ref[:, cols], dy)
            dprod = _dot_nt(dy, wout_ref[cols, :])
            zv = z_ref[:, cols]
            sig = _sigmoid(zv)
            silu = zv * sig
            mgv = mg_ref[:, cols]
            sc = sc_ref[:, cols]
            dsc_ref[:, cols] += jnp.sum(dprod * silu * mgv, axis=0, keepdims=True)
            dmg = (dprod * silu * sc).astype(MXU)
            dz = dprod * (mgv * sc) * (sig * (1.0 + zv * (1.0 - sig)))
            dwg_acc[g] += _dot_tn(mixed_ref[:, cols], dmg)
            dmixed = _dot_nt(dmg, wg_ref[g])
            e = dmixed * _inv_count(row, POOL_WINDOWS[g])
            s = jnp.concatenate([e, carry_ref[g]], axis=0)
            carry_ref[g] = e[:HALO, :]
            w = 1
            while w < POOL_WINDOWS[g]:
                s = s + pltpu.roll(s, n_ext - w, 0)
                w *= 2
            du_ref[:, cols] = (s[:TM, :] - dmixed).astype(MXU)
            du_ref[:, POOL_WIDTH + g * GROUP:POOL_WIDTH + (g + 1) * GROUP] = dz.astype(MXU)

        @pl.when(i == nt - 1)
        def _():
            _flush(dwg_acc, dwg_hbm, stage_g)
            _flush(dwout_acc, dwout_hbm, stage_o)
            ex.wait(ex_in, ex_out, ex_sems)

    rev = lambda w: pl.BlockSpec((TM, w), lambda i: (nt - 1 - i, 0))
    any_spec = pl.BlockSpec(memory_space=pl.ANY)
    return pl.pallas_call(
        body, name="pool_bwd", grid=(nt,),
        in_specs=[rev(D_MODEL), rev(D_MODEL), rev(POOL_WIDTH), rev(POOL_WIDTH), rev(POOL_WIDTH), rev(POOL_WIDTH),
                  _const_spec((1, D_MODEL)), _const_spec((1, POOL_WIDTH)),
                  _const_spec((N_GROUPS, GROUP, GROUP)), _const_spec((POOL_WIDTH, D_MODEL))] + ex.specs,
        out_specs=[rev(2 * POOL_WIDTH), pl.BlockSpec((1, POOL_WIDTH), lambda i: (0, 0)),
                   pl.BlockSpec((1, D_MODEL), lambda i: (0, 0)), any_spec, any_spec] + ex.specs,
        out_shape=[jax.ShapeDtypeStruct((n_tok, 2 * POOL_WIDTH), MXU), jax.ShapeDtypeStruct((1, POOL_WIDTH), F32),
                   jax.ShapeDtypeStruct((1, D_MODEL), F32), jax.ShapeDtypeStruct((N_GROUPS, GROUP, GROUP), MXU),
                   jax.ShapeDtypeStruct((N_GROUPS, GROUP, D_MODEL), MXU)] + ex.out_shape,
        scratch_shapes=[pltpu.VMEM((N_GROUPS, HALO, GROUP), F32), pltpu.VMEM((N_GROUPS, GROUP, GROUP), F32),
                        pltpu.VMEM((N_GROUPS, GROUP, D_MODEL), F32), pltpu.VMEM((GROUP, GROUP), MXU),
                        pltpu.VMEM((GROUP, D_MODEL), MXU)] + ex.scratch,
        compiler_params=_params(1),
    )(dx1, y, z, mg, mixed, prod, g_post, scale, w_group, w_out, *ex.arrays)


def _in_proj_bwd(parts, x, dres, g_pre, w_in, name, with_dw, exchange_items):
    n_tok = x.shape[0]
    nt = n_tok // TM
    half = D_MODEL // W_BLOCK
    ex = _Exchange(exchange_items)
    n_dw = 1 if with_dw else 0

    def body(p0, p1, p2, p3, x_ref, dres_ref, g_ref, w_ref, *rest):
        ex_in, rest = rest[:ex.n], rest[ex.n:]
        dx_ref, dg_ref = rest[:2]
        dw_hbm = rest[2:2 + n_dw]
        ex_out, rest = rest[2 + n_dw:2 + n_dw + ex.n], rest[2 + n_dw + ex.n:]
        dw_scratch, ex_sems = rest[:2 * n_dw], rest[2 * n_dw:]
        i = pl.program_id(0)

        @pl.when(i == 0)
        def _():
            ex.start(ex_in, ex_out, ex_sems)
            dg_ref[...] = jnp.zeros_like(dg_ref)
            if with_dw:
                dw_scratch[0][...] = jnp.zeros_like(dw_scratch[0])

        r, xhat = _rms_fwd(x_ref[...])
        g = g_ref[...]
        h = (xhat * g).astype(MXU)
        dh = None
        for p, part_ref in enumerate((p0, p1, p2, p3)):
            for jj in range(half):
                j = half * p + jj
                if len(part_ref.shape) == 3:
                    per_block = W_BLOCK // PAIR
                    du = jnp.concatenate([part_ref[jj * per_block + pp] for pp in range(per_block)], axis=1)
                else:
                    du = part_ref[:, jj * W_BLOCK:(jj + 1) * W_BLOCK]
                t = _dot_nt(du, w_ref[j])
                dh = t if dh is None else dh + t
                if with_dw:
                    dw_scratch[0][j] += _dot_tn(h, du)
        dg_ref[...] += jnp.sum(dh * xhat, axis=0, keepdims=True)
        dx_ref[...] = dres_ref[...] + _rms_bwd(dh, xhat, r, g)

        @pl.when(i == nt - 1)
        def _():
            if with_dw:
                _flush(dw_scratch[0], dw_hbm[0], dw_scratch[1])
            ex.wait(ex_in, ex_out, ex_sems)

    tok = pl.BlockSpec((TM, D_MODEL), lambda i: (i, 0))
    return pl.pallas_call(
        body, name=name, grid=(nt,),
        in_specs=[pl.BlockSpec(shape, m) for _, shape, m in parts]
        + [tok, tok, _const_spec((1, D_MODEL)), _const_spec((N_DEV, D_MODEL, W_BLOCK))] + ex.specs,
        out_specs=[tok, pl.BlockSpec((1, D_MODEL), lambda i: (0, 0))]
        + [pl.BlockSpec(memory_space=pl.ANY)] * n_dw + ex.specs,
        out_shape=[jax.ShapeDtypeStruct((n_tok, D_MODEL), F32), jax.ShapeDtypeStruct((1, D_MODEL), F32)]
        + [jax.ShapeDtypeStruct((N_DEV, D_MODEL, W_BLOCK), MXU)] * n_dw + ex.out_shape,
        scratch_shapes=[pltpu.VMEM((N_DEV, D_MODEL, W_BLOCK), F32), pltpu.VMEM((D_MODEL, W_BLOCK), MXU)][:2 * n_dw]
        + ex.scratch,
        compiler_params=_params(1),
    )(*[a for a, _, _ in parts], x, dres, g_pre, w_in, *ex.arrays)


def _w_in_grad_scatter(h_t, du, exchange_items, name):
    n_tok = du.shape[0]
    ni = n_tok // TMB
    ex = _Exchange(exchange_items)
    me_out = 4 * lax.axis_index("x") + 2 * lax.axis_index("y") + lax.axis_index("c")
    order = ((me_out + 1 + jnp.arange(N_DEV, dtype=jnp.int32)) % N_DEV).astype(jnp.int32)

    def body(order_ref, h_ref, du_ref, *rest):
        ex_in, rest = rest[:ex.n], rest[ex.n:]
        part_hbm, ex_out, rest = rest[0], rest[1:1 + ex.n], rest[1 + ex.n:]
        acc_ref, stage_ref, send_sems, recv_sems = rest[:4]
        ex_sems = rest[4:]
        s = pl.program_id(0)
        i = pl.program_id(1)
        x, y, c = lax.axis_index("x"), lax.axis_index("y"), lax.axis_index("c")
        me = 4 * x + 2 * y + c

        def block_copy(step, src_slot, dst_slot, owner):
            return pltpu.make_async_remote_copy(
                src_ref=stage_ref.at[src_slot], dst_ref=part_hbm.at[dst_slot],
                send_sem=send_sems.at[step], recv_sem=recv_sems.at[step],
                device_id=(owner // 4, (owner // 2) % 2, owner % 2), device_id_type=pl.DeviceIdType.MESH)

        @pl.when((s == 0) & (i == 0))
        def _():
            ex.start(ex_in, ex_out, ex_sems)

        @pl.when(i == 0)
        def _():
            acc_ref[...] = jnp.zeros_like(acc_ref)

        acc_ref[...] += _dot(h_ref[:, pl.ds(pl.multiple_of(i * TMB, TMB), TMB)], du_ref[...])

        @pl.when(i == ni - 1)
        def _():
            stage_ref[s] = acc_ref[...].astype(MXU)

            @pl.when(s < N_DEV - 1)
            def _():
                block_copy(s, s, me, (me + 1 + s) % N_DEV).start()

            @pl.when(s == N_DEV - 1)
            def _():
                pltpu.sync_copy(stage_ref.at[s], part_hbm.at[me])
                for step in range(N_DEV - 1):
                    sender = (me + N_DEV - 1 - step) % N_DEV
                    block_copy(step, step, sender, me).wait_recv()
                    block_copy(step, step, me, me).wait_send()
                ex.wait(ex_in, ex_out, ex_sems)

    grid_spec = pltpu.PrefetchScalarGridSpec(
        num_scalar_prefetch=1, grid=(N_DEV, ni),
        in_specs=[pl.BlockSpec((D_MODEL, n_tok), lambda s, i, order: (0, 0), pipeline_mode=pl.Buffered(1)),
                  pl.BlockSpec((TMB, W_BLOCK), lambda s, i, order: (i, order[s]))] + ex.specs,
        out_specs=[pl.BlockSpec(memory_space=pl.ANY)] + ex.specs,
        scratch_shapes=[pltpu.VMEM((D_MODEL, W_BLOCK), F32), pltpu.VMEM((N_DEV, D_MODEL, W_BLOCK), MXU),
                        pltpu.SemaphoreType.DMA((N_DEV - 1,)), pltpu.SemaphoreType.DMA((N_DEV - 1,))] + ex.scratch)
    return pl.pallas_call(
        body, name=name, grid_spec=grid_spec,
        out_shape=[jax.ShapeDtypeStruct((N_DEV, D_MODEL, W_BLOCK), MXU)] + ex.out_shape,
        compiler_params=_params(2),
    )(order, h_t, du, *ex.arrays)


def _rel_onehot():
    rel = lax.broadcasted_iota(jnp.int32, (REL_PAD, ROLL_W), 0)
    col = lax.broadcasted_iota(jnp.int32, (REL_PAD, ROLL_W), 1)
    return (rel == jnp.minimum(BAND + MAX_REL - col, 2 * MAX_REL)).astype(MXU)


def _split3(v):
    hi = v.astype(MXU)
    r1 = v - hi.astype(F32)
    mid = r1.astype(MXU)
    lo = (r1 - mid.astype(F32)).astype(MXU)
    return hi, mid, lo


def _bias_table(rel_bias_padded):
    def body(rb_ref, out_ref):
        onehot = _rel_onehot()
        base = None
        for term in _split3(rb_ref[...]):
            t = _dot(term, onehot)
            base = t if base is None else base + t
        qi = lax.broadcasted_iota(jnp.int32, (CHUNK, ROLL_W), 0)
        kk = lax.broadcasted_iota(jnp.int32, (CHUNK, TKW), 1)
        for h in range(N_HEADS):
            t = jnp.broadcast_to(base[h:h + 1, :], (CHUNK, ROLL_W))
            for bit in range(6):
                t = jnp.where(((qi >> bit) & 1) == 1, pltpu.roll(t, 1 << bit, 1), t)
            for rr in range(TM // CHUNK):
                shifted = pltpu.roll(t, (CHUNK * rr - CHUNK) % ROLL_W, 1)[:, :TKW]
                band = kk - CHUNK * rr
                out_ref[h, rr * CHUNK:(rr + 1) * CHUNK, :] = jnp.where((band >= 0) & (band < BAND), shifted, NEG)

    return pl.pallas_call(
        body, name="bias_table", out_shape=jax.ShapeDtypeStruct((N_HEADS, TM, TKW), F32),
        compiler_params=pltpu.CompilerParams(vmem_limit_bytes=VMEM_LIMIT),
    )(rel_bias_padded)


def _bias_grad(dtab):
    def body(dt_ref, out_ref, dbase_ref):
        qi = lax.broadcasted_iota(jnp.int32, (CHUNK, ROLL_W), 0)
        zeros = jnp.zeros((CHUNK, ROLL_W - TKW), F32)
        for h in range(N_HEADS):
            t = None
            for rr in range(TM // CHUNK):
                blk = jnp.concatenate([dt_ref[h, rr * CHUNK:(rr + 1) * CHUNK, :], zeros], axis=1)
                blk = pltpu.roll(blk, (CHUNK - CHUNK * rr) % ROLL_W, 1)
                t = blk if t is None else t + blk
            for bit in range(6):
                t = jnp.where(((qi >> bit) & 1) == 1, pltpu.roll(t, ROLL_W - (1 << bit), 1), t)
            dbase_ref[h:h + 1, :] = jnp.sum(t, axis=0, keepdims=True)
        onehot = _rel_onehot()
        acc = None
        for term in _split3(dbase_ref[...]):
            t = _dot_nt(term, onehot)
            acc = t if acc is None else acc + t
        out_ref[...] = acc

    return pl.pallas_call(
        body, name="bias_grad", out_shape=jax.ShapeDtypeStruct((N_HEADS, REL_PAD), F32),
        scratch_shapes=[pltpu.VMEM((N_HEADS, ROLL_W), F32)],
        compiler_params=pltpu.CompilerParams(vmem_limit_bytes=VMEM_LIMIT),
    )(dtab)


def _att_in(x1, g_pre, w_in):
    n_tok = x1.shape[0]
    nt = n_tok // TM
    lead = PAD // TM
    per_block = W_BLOCK // PAIR

    def body(x_ref, g_ref, w_ref, q_ref, k_ref, v_ref, z_ref):
        i = pl.program_id(0)

        @pl.when(i < lead)
        def _():
            k_ref[...] = jnp.zeros_like(k_ref)
            v_ref[...] = jnp.zeros_like(v_ref)

        @pl.when(i >= lead)
        def _():
            _, xhat = _rms_fwd(x_ref[...])
            h = (xhat * g_ref[...]).astype(MXU)
            for j in range(N_DEV):
                u = _dot(h, w_ref[j])
                if j >= 6:
                    z_ref[:, (j % 2) * W_BLOCK:(j % 2 + 1) * W_BLOCK] = u
                    continue
                dst = (q_ref, k_ref, v_ref)[j // 2]
                if j < 2:
                    u = u * QK_SCALE
                for pp in range(per_block):
                    dst[(j % 2) * per_block + pp] = u[:, pp * PAIR:(pp + 1) * PAIR].astype(MXU)

    late = pl.BlockSpec((TM, D_MODEL), lambda i: (jnp.maximum(i - lead, 0), 0))
    late3 = pl.BlockSpec((N_PAIRS, TM, PAIR), lambda i: (0, jnp.maximum(i - lead, 0), 0))
    padded3 = pl.BlockSpec((N_PAIRS, TM, PAIR), lambda i: (0, i, 0))
    return pl.pallas_call(
        body, name="att_in", grid=(nt + lead,),
        in_specs=[late, _const_spec((1, D_MODEL)), _const_spec((N_DEV, D_MODEL, W_BLOCK))],
        out_specs=[late3, padded3, padded3, late],
        out_shape=[jax.ShapeDtypeStruct((N_PAIRS, n_tok, PAIR), MXU),
                   jax.ShapeDtypeStruct((N_PAIRS, n_tok + PAD, PAIR), MXU),
                   jax.ShapeDtypeStruct((N_PAIRS, n_tok + PAD, PAIR), MXU),
                   jax.ShapeDtypeStruct((n_tok, ATT_WIDTH), F32)],
        compiler_params=_params(1),
    )(x1, g_pre, w_in)


def _head_masks():
    lane = lax.broadcasted_iota(jnp.int32, (1, PAIR), 1)
    first = lane < HEAD_DIM
    return first, jnp.logical_not(first)


def _pair_loop(one_pair, unroll):
    def loop_pass(t, carry):
        for u in range(unroll):
            one_pair(t * unroll + u)
        return carry

    lax.fori_loop(0, PAIRS_PER_STEP // unroll, loop_pass, 0)


def _scores(qh, k_refs, bias_ref, p, hh, tile, masked):
    ss = []
    for b in range(KB):
        s = _dot_nt(qh, k_refs[b][p]) + bias_ref[2 * p + hh, :, b * TM:(b + 1) * TM]
        if masked:
            s = s + jnp.where(tile + b < PAD // TM, NEG, 0.0).astype(F32)
        ss.append(s)
    return ss


def _pair_specs(index_map):
    return pl.BlockSpec((PAIRS_PER_STEP, TM, PAIR), index_map)


def _att_fwd(q, kpad, vpad, bias_tab):
    n_tok = q.shape[1]
    nt = n_tok // TM
    lead = PAD // TM

    def body(q_ref, k0, k1, k2, v0, v1, v2, bias_ref, o_ref, lse_ref):
        i = pl.program_id(1)
        masks = _head_masks()

        def pairs(masked):
            def one_pair(p):
                qv = q_ref[p]
                outs, ms = [], []
                for hh, mask in enumerate(masks):
                    qh = jnp.where(mask, qv, jnp.zeros_like(qv))
                    ss = _scores(qh, (k0, k1, k2), bias_ref, p, hh, i, masked)
                    m = None
                    for s in ss:
                        mb = jnp.max(s, axis=-1, keepdims=True)
                        m = mb if m is None else jnp.maximum(m, mb)
                    out = None
                    for b, v_ref in enumerate((v0, v1, v2)):
                        vb = v_ref[p]
                        t = _dot(jnp.exp(ss[b] - m).astype(MXU), jnp.where(mask, vb, jnp.ones_like(vb)))
                        out = t if out is None else out + t
                    outs.append(out)
                    ms.append(m)
                num = jnp.where(masks[0], outs[0], outs[1])
                den = jnp.where(masks[0], pltpu.roll(outs[0], HEAD_DIM, 1), pltpu.roll(outs[1], HEAD_DIM, 1))
                o_ref[p] = num / den
                lse_ref[p] = jnp.where(masks[0], ms[0], ms[1]) + jnp.log(den)

            _pair_loop(one_pair, FWD_UNROLL)

        @pl.when(i < lead)
        def _():
            pairs(True)

        @pl.when(i >= lead)
        def _():
            pairs(False)

    qspec = _pair_specs(lambda g, i: (g, i, 0))
    kspecs = [_pair_specs(functools.partial(lambda g, i, b: (g, i + b, 0), b=b)) for b in range(KB)]
    vspecs = [_pair_specs(functools.partial(lambda g, i, b: (g, i + b, 0), b=b)) for b in range(KB)]
    shape = jax.ShapeDtypeStruct((N_PAIRS, n_tok, PAIR), F32)
    return pl.pallas_call(
        body, name="att_fwd", grid=(N_PAIRS // PAIRS_PER_STEP, nt),
        in_specs=[qspec] + kspecs + vspecs + [pl.BlockSpec((2 * PAIRS_PER_STEP, TM, TKW), lambda g, i: (g, 0, 0))],
        out_specs=[qspec, qspec], out_shape=[shape, shape],
        compiler_params=_params(2),
    )(q, kpad, kpad, kpad, vpad, vpad, vpad, bias_tab)


def _att_bwd(q, kpad, vpad, do, o, lse, bias_tab):
    n_tok = q.shape[1]
    nt = n_tok // TM
    lead = PAD // TM

    def body(q_ref, do_ref, o_ref, lse_ref, k0, k1, k2, v0, v1, v2, bias_ref,
             dq_ref, dk_ref, dv_ref, dtab_ref, rk0, rk1, rv0, rv1):
        i = pl.program_id(1)
        masks = _head_masks()

        @pl.when(i == 0)
        def _():
            for ref in (rk0, rk1, rv0, rv1):
                ref[...] = jnp.zeros_like(ref)
            dtab_ref[...] = jnp.zeros_like(dtab_ref)

        def pairs(masked):
            def one_pair(p):
                qv = q_ref[p]
                dov = do_ref[p]
                doo = dov.astype(F32) * o_ref[p]
                lse_pair = lse_ref[p]
                dks = [None] * KB
                dvs = [None] * KB
                dqs = []
                for hh, mask in enumerate(masks):
                    qh = jnp.where(mask, qv, jnp.zeros_like(qv))
                    doh = jnp.where(mask, dov, jnp.zeros_like(dov))
                    dsum = jnp.sum(jnp.where(mask, doo, 0.0), axis=-1, keepdims=True)
                    lse_h = lse_pair[:, hh * HEAD_DIM:hh * HEAD_DIM + 1]
                    ss = _scores(qh, (k0, k1, k2), bias_ref, p, hh, i, masked)
                    dq = None
                    for b, (k_ref, v_ref) in enumerate(zip((k0, k1, k2), (v0, v1, v2))):
                        prob = jnp.exp(ss[b] - lse_h)
                        ds = prob * (_dot_nt(doh, v_ref[p]) - dsum)
                        dtab_ref[2 * p + hh, :, b * TM:(b + 1) * TM] += ds
                        dsb = ds.astype(MXU)
                        t = _dot(dsb, k_ref[p])
                        dq = t if dq is None else dq + t
                        t = _dot_tn(dsb, qh)
                        dks[b] = t if dks[b] is None else dks[b] + t
                        t = _dot_tn(prob.astype(MXU), doh)
                        dvs[b] = t if dvs[b] is None else dvs[b] + t
                    dqs.append(dq)
                dq_ref[p] = (jnp.where(masks[0], dqs[0], dqs[1]) * QK_SCALE).astype(MXU)
                dk_ref[p] = (rk0[p] + dks[0]).astype(MXU)
                dv_ref[p] = (rv0[p] + dvs[0]).astype(MXU)
                rk0[p] = rk1[p] + dks[1]
                rv0[p] = rv1[p] + dvs[1]
                rk1[p] = dks[2]
                rv1[p] = dvs[2]

            _pair_loop(one_pair, BWD_UNROLL)

        @pl.when(i < lead)
        def _():
            pairs(True)

        @pl.when((i >= lead) & (i < nt))
        def _():
            pairs(False)

        @pl.when(i >= nt)
        def _():
            dk_ref[...] = rk0[...].astype(MXU)
            dv_ref[...] = rv0[...].astype(MXU)
            rk0[...] = rk1[...]
            rv0[...] = rv1[...]

    last = nt - 1
    qspec = _pair_specs(lambda g, i: (g, jnp.minimum(i, last), 0))
    kspecs = [_pair_specs(functools.partial(lambda g, i, b: (g, jnp.minimum(i, last) + b, 0), b=b)) for b in range(KB)]
    vspecs = [_pair_specs(functools.partial(lambda g, i, b: (g, jnp.minimum(i, last) + b, 0), b=b)) for b in range(KB)]
    pspec = _pair_specs(lambda g, i: (g, i, 0))
    tspec = pl.BlockSpec((2 * PAIRS_PER_STEP, TM, TKW), lambda g, i: (g, 0, 0))
    ring = pltpu.VMEM((PAIRS_PER_STEP, TM, PAIR), F32)
    return pl.pallas_call(
        body, name="att_bwd", grid=(N_PAIRS // PAIRS_PER_STEP, nt + KB - 1),
        in_specs=[qspec, qspec, qspec, qspec] + kspecs + vspecs + [tspec],
        out_specs=[qspec, pspec, pspec, tspec],
        out_shape=[jax.ShapeDtypeStruct((N_PAIRS, n_tok, PAIR), MXU),
                   jax.ShapeDtypeStruct((N_PAIRS, n_tok + PAD, PAIR), MXU),
                   jax.ShapeDtypeStruct((N_PAIRS, n_tok + PAD, PAIR), MXU),
                   jax.ShapeDtypeStruct((N_HEADS, TM, TKW), F32)],
        scratch_shapes=[ring, ring, ring, ring],
        compiler_params=_params(2),
    )(q, do, o, lse, kpad, kpad, kpad, vpad, vpad, vpad, bias_tab)


def _att_out(o, z, x1, target, g_post, w_out):
    n_tok = z.shape[0]
    nt = n_tok // TM

    def body(o_ref, z_ref, x1_ref, tgt_ref, gpost_ref, w_ref,
             loss_ref, dx2_ref, do_ref, dz_ref, dgpost_ref, dw_hbm, acc_ref, loss_acc, stage_ref):
        i = pl.program_id(0)

        @pl.when(i == 0)
        def _():
            acc_ref[...] = jnp.zeros_like(acc_ref)
            loss_acc[...] = jnp.zeros_like(loss_acc)
            dgpost_ref[...] = jnp.zeros_like(dgpost_ref)

        ov = jnp.concatenate([o_ref[p] for p in range(N_PAIRS)], axis=1)
        zv = z_ref[...]
        sig = _sigmoid(zv)
        silu = zv * sig
        gated = (ov * silu).astype(MXU)
        y = _dot(gated, w_ref[...])
        r, yhat = _rms_fwd(y)
        gpost = gpost_ref[...]
        diff = x1_ref[...] + yhat * gpost - tgt_ref[...]
        loss_acc[...] += jnp.sum(diff * diff, axis=0, keepdims=True)
        dn = diff * (1.0 / D_MODEL)
        dx2_ref[...] = dn
        dgpost_ref[...] += jnp.sum(dn * yhat, axis=0, keepdims=True)
        dy = _rms_bwd(dn, yhat, r, gpost).astype(MXU)
        for j in range(ATT_WIDTH // W_BLOCK):
            acc_ref[j] += _dot_tn(gated[:, j * W_BLOCK:(j + 1) * W_BLOCK], dy)
        dgated = _dot_nt(dy, w_ref[...])
        dob = (dgated * silu).astype(MXU)
        for p in range(N_PAIRS):
            do_ref[p] = dob[:, p * PAIR:(p + 1) * PAIR]
        dz_ref[...] = (dgated * ov * (sig * (1.0 + zv * (1.0 - sig)))).astype(MXU)

        @pl.when(i == nt - 1)
        def _():
            total = jnp.sum(loss_acc[...], axis=-1, keepdims=True) * (0.5 / D_MODEL)
            loss_ref[...] = jnp.broadcast_to(total, loss_ref.shape)
            _flush(acc_ref, dw_hbm, stage_ref)

    tok = pl.BlockSpec((TM, D_MODEL), lambda i: (i, 0))
    tok3 = pl.BlockSpec((N_PAIRS, TM, PAIR), lambda i: (0, i, 0))
    return pl.pallas_call(
        body, name="att_out", grid=(nt,),
        in_specs=[tok3, tok, tok, tok, _const_spec((1, D_MODEL)), _const_spec((ATT_WIDTH, D_MODEL))],
        out_specs=[pl.BlockSpec((1, 128), lambda i: (0, 0)), tok, tok3, tok,
                   pl.BlockSpec((1, D_MODEL), lambda i: (0, 0)), pl.BlockSpec(memory_space=pl.ANY)],
        out_shape=[jax.ShapeDtypeStruct((1, 128), F32), jax.ShapeDtypeStruct((n_tok, D_MODEL), F32),
                   jax.ShapeDtypeStruct((N_PAIRS, n_tok, PAIR), MXU), jax.ShapeDtypeStruct((n_tok, ATT_WIDTH), MXU),
                   jax.ShapeDtypeStruct((1, D_MODEL), F32),
                   jax.ShapeDtypeStruct((ATT_WIDTH // W_BLOCK, W_BLOCK, D_MODEL), MXU)],
        scratch_shapes=[pltpu.VMEM((ATT_WIDTH // W_BLOCK, W_BLOCK, D_MODEL), F32), pltpu.VMEM((1, D_MODEL), F32),
                        pltpu.VMEM((W_BLOCK, D_MODEL), MXU)],
        compiler_params=_params(1),
    )(o, z, x1, target, g_post, w_out)


def _adamw(parts, w, m, v, name):
    rows, cols = w.shape
    tr = min(rows, 256)

    def body(p_ref, w_ref, m_ref, v_ref, g_ref, d_ref, mo_ref, vo_ref):
        g = p_ref[0].astype(F32)
        for s in range(1, N_DEV):
            g = g + p_ref[s].astype(F32)
        m_new = ADAM_B1 * m_ref[...] + (1.0 - ADAM_B1) * g
        v_new = ADAM_B2 * v_ref[...] + (1.0 - ADAM_B2) * (g * g)
        m_hat = m_new / (1.0 - ADAM_B1 ** ADAM_STEP)
        v_hat = v_new / (1.0 - ADAM_B2 ** ADAM_STEP)
        g_ref[...] = g
        d_ref[...] = -ADAM_LR * (m_hat / (jnp.sqrt(v_hat) + ADAM_EPS) + ADAM_WD * w_ref[...])
        mo_ref[...] = m_new
        vo_ref[...] = v_new

    blk = pl.BlockSpec((tr, cols), lambda i: (i, 0))
    shape = jax.ShapeDtypeStruct((rows, cols), F32)
    return pl.pallas_call(
        body, name=name, grid=(rows // tr,),
        in_specs=[pl.BlockSpec((N_DEV, tr, cols), lambda i: (0, i, 0)), blk, blk, blk],
        out_specs=[blk, blk, blk, blk], out_shape=[shape, shape, shape, shape],
        compiler_params=_params(1),
    )(parts, w, m, v)


SMALL_ROWS = 16


def _pack_small(norm_pre, norm_post, pool_scale, rel_bias_padded):
    return jnp.concatenate([norm_pre, norm_post, pool_scale.reshape(2, D_MODEL),
                            rel_bias_padded.reshape(SMALL_ROWS - 6, D_MODEL)], axis=0)


def _unpack_small(packed):
    rel = packed[6:].reshape(N_HEADS, REL_PAD)[:, :N_REL]
    return packed[0:2], packed[2:4], packed[4:6].reshape(1, POOL_WIDTH), rel.reshape(1, N_HEADS, N_REL)


def _pad_rel(rel_bias):
    return jnp.pad(rel_bias.reshape(N_HEADS, N_REL), ((0, 0), (0, REL_PAD - N_REL)))


def kernel(x, norm_pre, norm_post, pool_w_in, pool_w_group, pool_scale, pool_w_out, att_w_in, att_rel_bias, att_w_out, loss_target, m_norm_pre, m_norm_post, m_pool_w_in, m_pool_w_group, m_pool_scale, m_pool_w_out, m_att_w_in, m_att_rel_bias, m_att_w_out, v_norm_pre, v_norm_post, v_pool_w_in, v_pool_w_group, v_pool_scale, v_pool_w_out, v_att_w_in, v_att_rel_bias, v_att_w_out):
    xt = x[0]
    target = loss_target[0]
    n_tok = xt.shape[0]
    lead = PAD // TM
    rows_g = GROUP // N_DEV

    gathered = _gather_two_level([pool_w_in[0].astype(MXU), pool_w_group[0].astype(MXU), pool_w_out[0].astype(MXU)],
                                 "gather_pool_weights")
    w_in_p = gathered[0]
    w_group = gathered[1].transpose(1, 0, 2, 3).reshape(N_GROUPS, GROUP, GROUP)
    w_out_p = gathered[2].reshape(POOL_WIDTH, D_MODEL)
    rel_padded = _pad_rel(att_rel_bias[0])
    bias_tab = _bias_table(rel_padded)

    x1, y0, z0, mixed, mg, prod, h0_t, w_in_a, w_out_a = _pool_fwd(
        xt, norm_pre[0:1], norm_post[0:1], w_in_p, w_group, pool_scale, w_out_p,
        [(att_w_in[0].astype(MXU), False), (att_w_out[0].astype(MXU), False)])
    w_out_a = w_out_a.reshape(ATT_WIDTH, D_MODEL)
    q, kpad, vpad, z1 = _att_in(x1, norm_pre[1:2], w_in_a)
    o, lse = _att_fwd(q, kpad, vpad, bias_tab)
    loss_part, dx2, do, dz1, d_gpost1, d_w_out_a = _att_out(o, z1, x1, target, norm_post[1:2], w_out_a)
    dq, dkpad, dvpad, dtab = _att_bwd(q, kpad, vpad, do, o, lse, bias_tab)
    d_rel = _bias_grad(dtab)
    pairs = (N_PAIRS, TM, PAIR)
    flat = (TM, D_MODEL)
    dx1, d_gpre1, d_w_in_a = _in_proj_bwd(
        [(dq, pairs, lambda i: (0, i, 0)), (dkpad, pairs, lambda i: (0, i + lead, 0)),
         (dvpad, pairs, lambda i: (0, i + lead, 0)), (dz1, flat, lambda i: (i, 0))],
        x1, dx2, norm_pre[1:2], w_in_a, "att_in_bwd", True, [])
    du0, d_scale, d_gpost0, d_w_group, d_w_out_p, part_w_in_a, part_w_out_a = _pool_bwd(
        dx1, y0, z0, mg, mixed, prod, norm_post[0:1], pool_scale, w_group, w_out_p,
        [(d_w_in_a, True), (d_w_out_a.reshape(N_DEV, ATT_WIDTH // N_DEV, D_MODEL), True)])
    col = lambda p: (lambda i: (i, p))
    grad_x, d_gpre0, part_w_group, part_w_out_p = _in_proj_bwd(
        [(du0, flat, col(p)) for p in range(4)], xt, dx1, norm_pre[0:1], w_in_p, "pool_in_bwd", False,
        [(d_w_group.reshape(N_GROUPS, N_DEV, rows_g, GROUP).transpose(1, 0, 2, 3), True),
         (d_w_out_p.reshape(N_DEV, POOL_WIDTH // N_DEV, D_MODEL), True)])
    d_small = _pack_small(jnp.concatenate([d_gpre0, d_gpre1], axis=0), jnp.concatenate([d_gpost0, d_gpost1], axis=0),
                          d_scale, d_rel)
    part_w_in_p, part_small = _w_in_grad_scatter(h0_t, du0, [(d_small, False)], "pool_w_in_grad")

    def update(part, w, m, v, name):
        shape = w.shape
        flat = lambda a: a.reshape(-1, shape[-1])
        outs = _adamw(part.reshape(N_DEV, -1, shape[-1]), flat(w), flat(m), flat(v), name)
        return [a.reshape(shape) for a in outs]

    u_att_w_in = update(part_w_in_a, att_w_in, m_att_w_in, v_att_w_in, "adamw_att_w_in")
    u_att_w_out = update(part_w_out_a, att_w_out, m_att_w_out, v_att_w_out, "adamw_att_w_out")
    u_pool_w_group = update(part_w_group, pool_w_group, m_pool_w_group, v_pool_w_group, "adamw_pool_w_group")
    u_pool_w_out = update(part_w_out_p, pool_w_out, m_pool_w_out, v_pool_w_out, "adamw_pool_w_out")
    u_pool_w_in = update(part_w_in_p, pool_w_in, m_pool_w_in, v_pool_w_in, "adamw_pool_w_in")
    small = _adamw(part_small, _pack_small(norm_pre, norm_post, pool_scale, rel_padded),
                   _pack_small(m_norm_pre, m_norm_post, m_pool_scale, _pad_rel(m_att_rel_bias[0])),
                   _pack_small(v_norm_pre, v_norm_post, v_pool_scale, _pad_rel(v_att_rel_bias[0])), "adamw_small")
    u_small = [_unpack_small(a) for a in small]

    loss = lax.psum(loss_part[0, 0], ("x", "y", "c"))
    outs = [loss, grad_x.reshape(1, n_tok, D_MODEL)]
    for kind in range(4):
        outs += [u_small[kind][0], u_small[kind][1], u_pool_w_in[kind], u_pool_w_group[kind], u_small[kind][2],
                 u_pool_w_out[kind], u_att_w_in[kind], u_small[kind][3], u_att_w_out[kind]]
    return tuple(outs)
```

```python
import functools

import jax
import jax.numpy as jnp
from jax import lax
from jax.experimental import pallas as pl
from jax.experimental.pallas import tpu as pltpu

F32 = jnp.float32
MXU = jnp.bfloat16

D_MODEL = 1024
POOL_WIDTH = 2048
POOL_WINDOWS = (2, 4, 8, 16)
N_GROUPS = 4
GROUP = 512
HALO = 16
N_HEADS = 16
HEAD_DIM = 64
CHUNK = 64
LEFT_CHUNKS = 8
PAD = LEFT_CHUNKS * CHUNK
BAND = PAD + CHUNK
MAX_REL = 256
N_REL = 2 * MAX_REL + 1
REL_PAD = 640
ATT_WIDTH = 1024
PAIR = 2 * HEAD_DIM
N_PAIRS = N_HEADS // 2
N_DEV = 8
W_BLOCK = 512
RMS_EPS = 1e-6
QK_SCALE = 0.125
NEG = -1e30

TM = 256
TMB = 1024
KB = 3
TKW = KB * TM
PAIRS_PER_STEP = 4
FWD_UNROLL = 4
BWD_UNROLL = 2
ROLL_W = 1024

VMEM_LIMIT = 56 * 1024 * 1024

ADAM_LR = 0.001
ADAM_B1 = 0.9
ADAM_B2 = 0.999
ADAM_EPS = 1e-08
ADAM_WD = 0.01
ADAM_STEP = 10

NT_DIMS = (((1,), (1,)), ((), ()))
TN_DIMS = (((0,), (0,)), ((), ()))


def _params(n_grid):
    return pltpu.CompilerParams(dimension_semantics=("arbitrary",) * n_grid, vmem_limit_bytes=VMEM_LIMIT)


def _const_spec(shape):
    nd = len(shape)
    return pl.BlockSpec(shape, lambda *_: (0,) * nd, pipeline_mode=pl.Buffered(1))


def _dot(a, b):
    return jnp.dot(a, b, preferred_element_type=F32)


def _dot_nt(a, b):
    return lax.dot_general(a, b, NT_DIMS, preferred_element_type=F32)


def _dot_tn(a, b):
    return lax.dot_general(a, b, TN_DIMS, preferred_element_type=F32)


def _sigmoid(z):
    return 1.0 / (1.0 + jnp.exp(-z))


def _rms_fwd(xv):
    r = lax.rsqrt(jnp.mean(xv * xv, axis=-1, keepdims=True) + RMS_EPS)
    return r, xv * r


def _rms_bwd(dn, xhat, r, g):
    dng = dn * g
    return r * (dng - xhat * jnp.mean(dng * xhat, axis=-1, keepdims=True))


class _Exchange:
    def __init__(self, items):
        self.arrays = [a for a, _ in items]
        self.scatter = [s for _, s in items]
        self.n = len(items)
        self.out_shape = [jax.ShapeDtypeStruct((N_DEV,) + tuple(a.shape[1:] if s else a.shape), a.dtype)
                          for a, s in items]
        self.specs = [pl.BlockSpec(memory_space=pl.ANY)] * self.n
        self.scratch = ([pltpu.SemaphoreType.DMA((N_DEV - 1, self.n)), pltpu.SemaphoreType.DMA((N_DEV - 1, self.n)),
                         pltpu.SemaphoreType.DMA((self.n,))] if self.n else [])

    def _copies(self, ins, outs, sems, with_receives):
        send_sems, recv_sems, local_sems = sems
        x, y, c = lax.axis_index("x"), lax.axis_index("y"), lax.axis_index("c")
        me = 4 * x + 2 * y + c

        def src(t, slot):
            return ins[t].at[slot] if self.scatter[t] else ins[t]

        local = [pltpu.make_async_copy(src(t, me), outs[t].at[me], local_sems.at[t]) for t in range(self.n)]
        sends, recvs = [], []
        for k in range(1, N_DEV):
            px = 1 - x if k & 4 else x
            py = 1 - y if k & 2 else y
            pc = 1 - c if k & 1 else c
            peer = 4 * px + 2 * py + pc
            for t in range(self.n):
                common = dict(src_ref=src(t, peer), send_sem=send_sems.at[k - 1, t], recv_sem=recv_sems.at[k - 1, t],
                              device_id=(px, py, pc), device_id_type=pl.DeviceIdType.MESH)
                sends.append(pltpu.make_async_remote_copy(dst_ref=outs[t].at[me], **common))
                if with_receives:
                    recvs.append(pltpu.make_async_remote_copy(dst_ref=outs[t].at[peer], **common))
        return local, sends, recvs

    def start(self, ins, outs, sems):
        if self.n:
            local, sends, _ = self._copies(ins, outs, sems, False)
            for cp in local + sends:
                cp.start()

    def wait(self, ins, outs, sems):
        if self.n:
            local, sends, recvs = self._copies(ins, outs, sems, True)
            for cp in recvs:
                cp.wait_recv()
            for cp in sends:
                cp.wait_send()
            for cp in local:
                cp.wait()


def _exchange(items, name):
    ex = _Exchange(items)
    n = ex.n

    def body(*refs):
        ins, outs, sems = refs[:n], refs[n:2 * n], refs[2 * n:]
        ex.start(ins, outs, sems)
        ex.wait(ins, outs, sems)

    return pl.pallas_call(
        body, name=name, out_shape=ex.out_shape, in_specs=ex.specs, out_specs=ex.specs, scratch_shapes=ex.scratch,
        compiler_params=pltpu.CompilerParams(has_side_effects=True),
    )(*ex.arrays)


def _gather_two_level(arrays, rel_bias_padded, name):
    n = len(arrays)
    out_shape = [jax.ShapeDtypeStruct((N_DEV,) + a.shape, a.dtype) for a in arrays]

    def body(*refs):
        ins, rb_ref, outs, tab_ref = refs[:n], refs[n], refs[n + 1:2 * n + 1], refs[2 * n + 1]
        send_sems, recv_sems, local_sems = refs[2 * n + 2:]
        x, y, c = lax.axis_index("x"), lax.axis_index("y"), lax.axis_index("c")
        sibling = (x, y, 1 - c)
        chips = [(1 - x, y), (x, 1 - y), (1 - x, 1 - y)]

        def slot(px, py, pc):
            return 4 * px + 2 * py + pc

        def copy(kind, t, block, to, own):
            return pltpu.make_async_remote_copy(
                src_ref=ins[t] if own else outs[t].at[slot(*block)], dst_ref=outs[t].at[slot(*block)],
                send_sem=send_sems.at[kind, t], recv_sem=recv_sems.at[kind, t],
                device_id=to, device_id_type=pl.DeviceIdType.MESH)

        local = [pltpu.make_async_copy(ins[t], outs[t].at[slot(x, y, c)], local_sems.at[t]) for t in range(n)]
        first = [copy(1 + j, t, (x, y, c), (*chip, c), True) for t in range(n) for j, chip in enumerate(chips)]
        first += [copy(0, t, (x, y, c), sibling, True) for t in range(n)]
        for cp in local + first:
            cp.start()
        _fill_bias_table(rb_ref, tab_ref)
        passed = []
        for t in range(n):
            for j, chip in enumerate(chips):
                copy(1 + j, t, (*chip, c), (x, y, c), False).wait_recv()
                cp = copy(4 + j, t, (*chip, c), sibling, False)
                cp.start()
                passed.append(cp)
        for t in range(n):
            copy(0, t, sibling, (x, y, c), False).wait_recv()
            for j, chip in enumerate(chips):
                copy(4 + j, t, (*chip, 1 - c), (x, y, c), False).wait_recv()
        for cp in first + passed:
            cp.wait_send()
        for cp in local:
            cp.wait()

    any_spec = pl.BlockSpec(memory_space=pl.ANY)
    vmem_spec = pl.BlockSpec(memory_space=pltpu.VMEM)
    return pl.pallas_call(
        body, name=name, out_shape=out_shape + [jax.ShapeDtypeStruct((N_HEADS, TM, TKW), F32)],
        in_specs=[any_spec] * n + [vmem_spec], out_specs=[any_spec] * n + [vmem_spec],
        scratch_shapes=[pltpu.SemaphoreType.DMA((7, n)), pltpu.SemaphoreType.DMA((7, n)), pltpu.SemaphoreType.DMA((n,))],
        compiler_params=pltpu.CompilerParams(has_side_effects=True, vmem_limit_bytes=VMEM_LIMIT),
    )(*arrays, rel_bias_padded)


def _inv_count(row, window):
    return 1.0 / jnp.minimum(row + 1, window).astype(F32)


def _pool_fwd(x, g_pre, g_post, w_in, w_group, scale, w_out, exchange_items):
    n_tok = x.shape[0]
    nt = n_tok // TM
    ex = _Exchange(exchange_items)

    def body(x_ref, gpre_ref, gpost_ref, win_ref, wg_ref, sc_ref, wout_ref, *rest):
        ex_in, rest = rest[:ex.n], rest[ex.n:]
        x1_ref, y_ref, z_ref, mixed_ref, mg_ref, prod_ref, ht_ref = rest[:7]
        ex_out, carry_ref, ex_sems = rest[7:7 + ex.n], rest[7 + ex.n], rest[8 + ex.n:]
        i = pl.program_id(0)

        @pl.when(i == 0)
        def _():
            ex.start(ex_in, ex_out, ex_sems)
            carry_ref[...] = jnp.zeros_like(carry_ref)

        xv = x_ref[...]
        r, xhat = _rms_fwd(xv)
        hf = xhat * gpre_ref[...]
        h = hf.astype(MXU)
        ht_ref[...] = hf.T.astype(MXU)
        row = i * TM + lax.broadcasted_iota(jnp.int32, (TM, 1), 0)
        y = None
        for g in range(N_GROUPS):
            cols = slice(g * GROUP, (g + 1) * GROUP)
            a = _dot(h, win_ref[g])
            z = _dot(h, win_ref[N_GROUPS + g])
            s = jnp.concatenate([carry_ref[g], a], axis=0)
            carry_ref[g] = a[TM - HALO:, :]
            w = 1
            while w < POOL_WINDOWS[g]:
                s = s + pltpu.roll(s, w, 0)
                w *= 2
            mixed = (s[HALO:, :] * _inv_count(row, POOL_WINDOWS[g]) - a).astype(MXU)
            mg = _dot(mixed, wg_ref[g])
            prod = (mg * sc_ref[:, cols] * (z * _sigmoid(z))).astype(MXU)
            z_ref[:, cols] = z
            mixed_ref[:, cols] = mixed
            mg_ref[:, cols] = mg
            prod_ref[:, cols] = prod
            part = _dot(prod, wout_ref[cols, :])
            y = part if y is None else y + part
        y_ref[...] = y
        _, yhat = _rms_fwd(y)
        x1_ref[...] = xv + yhat * gpost_ref[...]

        @pl.when(i == nt - 1)
        def _():
            ex.wait(ex_in, ex_out, ex_sems)

    tok = lambda w: pl.BlockSpec((TM, w), lambda i: (i, 0))
    return pl.pallas_call(
        body, name="pool_fwd", grid=(nt,),
        in_specs=[tok(D_MODEL), _const_spec((1, D_MODEL)), _const_spec((1, D_MODEL)),
                  _const_spec((N_DEV, D_MODEL, W_BLOCK)), _const_spec((N_GROUPS, GROUP, GROUP)),
                  _const_spec((1, POOL_WIDTH)), _const_spec((POOL_WIDTH, D_MODEL))] + ex.specs,
        out_specs=[tok(D_MODEL), tok(D_MODEL), tok(POOL_WIDTH), tok(POOL_WIDTH), tok(POOL_WIDTH), tok(POOL_WIDTH),
                   pl.BlockSpec((D_MODEL, TM), lambda i: (0, i))] + ex.specs,
        out_shape=[jax.ShapeDtypeStruct((n_tok, D_MODEL), F32), jax.ShapeDtypeStruct((n_tok, D_MODEL), F32),
                   jax.ShapeDtypeStruct((n_tok, POOL_WIDTH), F32), jax.ShapeDtypeStruct((n_tok, POOL_WIDTH), MXU),
                   jax.ShapeDtypeStruct((n_tok, POOL_WIDTH), F32), jax.ShapeDtypeStruct((n_tok, POOL_WIDTH), MXU),
                   jax.ShapeDtypeStruct((D_MODEL, n_tok), MXU)] + ex.out_shape,
        scratch_shapes=[pltpu.VMEM((N_GROUPS, HALO, GROUP), F32)] + ex.scratch,
        compiler_params=_params(1),
    )(x, g_pre, g_post, w_in, w_group, scale, w_out, *ex.arrays)


def _flush(acc_ref, out_hbm, stage_ref):
    for j in range(acc_ref.shape[0]):
        stage_ref[...] = acc_ref[j].astype(stage_ref.dtype)
        pltpu.sync_copy(stage_ref, out_hbm.at[j])


def _pool_bwd(dx1, y, z, mg, mixed, prod, g_post, scale, w_group, w_out, exchange_items):
    n_tok = dx1.shape[0]
    nt = n_tok // TM
    ex = _Exchange(exchange_items)

    def body(dx1_ref, y_ref, z_ref, mg_ref, mixed_ref, prod_ref, gpost_ref, sc_ref, wg_ref, wout_ref, *rest):
        ex_in, rest = rest[:ex.n], rest[ex.n:]
        du_ref, dsc_ref, dgpost_ref, dwg_hbm, dwout_hbm = rest[:5]
        ex_out, rest = rest[5:5 + ex.n], rest[5 + ex.n:]
        carry_ref, dwg_acc, dwout_acc, stage_g, stage_o = rest[:5]
        ex_sems = rest[5:]
        i = pl.program_id(0)

        @pl.when(i == 0)
        def _():
            ex.start(ex_in, ex_out, ex_sems)
            carry_ref[...] = jnp.zeros_like(carry_ref)
            dwg_acc[...] = jnp.zeros_like(dwg_acc)
            dwout_acc[...] = jnp.zeros_like(dwout_acc)
            dsc_ref[...] = jnp.zeros_like(dsc_ref)
            dgpost_ref[...] = jnp.zeros_like(dgpost_ref)

        dn = dx1_ref[...]
        r, yhat = _rms_fwd(y_ref[...])
        dgpost_ref[...] += jnp.sum(dn * yhat, axis=0, keepdims=True)
        dy = _rms_bwd(dn, yhat, r, gpost_ref[...]).astype(MXU)
        row = (nt - 1 - i) * TM + lax.broadcasted_iota(jnp.int32, (TM, 1), 0)
        n_ext = TM + HALO
        for g in range(N_GROUPS):
            cols = slice(g * GROUP, (g + 1) * GROUP)
            dwout_acc[g] += _dot_tn(prod_ref[:, cols], dy)
            dprod = _dot_nt(dy, wout_ref[cols, :])
            zv = z_ref[:, cols]
            sig = _sigmoid(zv)
            silu = zv * sig
            mgv = mg_ref[:, cols]
            sc = sc_ref[:, cols]
            dsc_ref[:, cols] += jnp.sum(dprod * silu * mgv, axis=0, keepdims=True)
            dmg = (dprod * silu * sc).astype(MXU)
            dz = dprod * (mgv * sc) * (sig * (1.0 + zv * (1.0 - sig)))
            dwg_acc[g] += _dot_tn(mixed_ref[:, cols], dmg)
            dmixed = _dot_nt(dmg, wg_ref[g])
            e = dmixed * _inv_count(row, POOL_WINDOWS[g])
            s = jnp.concatenate([e, carry_ref[g]], axis=0)
            carry_ref[g] = e[:HALO, :]
            w = 1
            while w < POOL_WINDOWS[g]:
                s = s + pltpu.roll(s, n_ext - w, 0)
                w *= 2
            du_ref[:, cols] = (s[:TM, :] - dmixed).astype(MXU)
            du_ref[:, POOL_WIDTH + g * GROUP:POOL_WIDTH + (g + 1) * GROUP] = dz.astype(MXU)

        @pl.when(i == nt - 1)
        def _():
            _flush(dwg_acc, dwg_hbm, stage_g)
            _flush(dwout_acc, dwout_hbm, stage_o)
            ex.wait(ex_in, ex_out, ex_sems)

    rev = lambda w: pl.BlockSpec((TM, w), lambda i: (nt - 1 - i, 0))
    any_spec = pl.BlockSpec(memory_space=pl.ANY)
    return pl.pallas_call(
        body, name="pool_bwd", grid=(nt,),
        in_specs=[rev(D_MODEL), rev(D_MODEL), rev(POOL_WIDTH), rev(POOL_WIDTH), rev(POOL_WIDTH), rev(POOL_WIDTH),
                  _const_spec((1, D_MODEL)), _const_spec((1, POOL_WIDTH)),
                  _const_spec((N_GROUPS, GROUP, GROUP)), _const_spec((POOL_WIDTH, D_MODEL))] + ex.specs,
        out_specs=[rev(2 * POOL_WIDTH), pl.BlockSpec((1, POOL_WIDTH), lambda i: (0, 0)),
                   pl.BlockSpec((1, D_MODEL), lambda i: (0, 0)), any_spec, any_spec] + ex.specs,
        out_shape=[jax.ShapeDtypeStruct((n_tok, 2 * POOL_WIDTH), MXU), jax.ShapeDtypeStruct((1, POOL_WIDTH), F32),
                   jax.ShapeDtypeStruct((1, D_MODEL), F32), jax.ShapeDtypeStruct((N_GROUPS, GROUP, GROUP), MXU),
                   jax.ShapeDtypeStruct((N_GROUPS, GROUP, D_MODEL), MXU)] + ex.out_shape,
        scratch_shapes=[pltpu.VMEM((N_GROUPS, HALO, GROUP), F32), pltpu.VMEM((N_GROUPS, GROUP, GROUP), F32),
                        pltpu.VMEM((N_GROUPS, GROUP, D_MODEL), F32), pltpu.VMEM((GROUP, GROUP), MXU),
                        pltpu.VMEM((GROUP, D_MODEL), MXU)] + ex.scratch,
        compiler_params=_params(1),
    )(dx1, y, z, mg, mixed, prod, g_post, scale, w_group, w_out, *ex.arrays)


def _in_proj_bwd(parts, x, dres, g_pre, w_in, name, with_dw, exchange_items):
    n_tok = x.shape[0]
    nt = n_tok // TM
    half = D_MODEL // W_BLOCK
    ex = _Exchange(exchange_items)
    n_dw = 1 if with_dw else 0

    def body(p0, p1, p2, p3, x_ref, dres_ref, g_ref, w_ref, *rest):
        ex_in, rest = rest[:ex.n], rest[ex.n:]
        dx_ref, dg_ref = rest[:2]
        dw_hbm = rest[2:2 + n_dw]
        ex_out, rest = rest[2 + n_dw:2 + n_dw + ex.n], rest[2 + n_dw + ex.n:]
        dw_scratch, ex_sems = rest[:2 * n_dw], rest[2 * n_dw:]
        i = pl.program_id(0)

        @pl.when(i == 0)
        def _():
            ex.start(ex_in, ex_out, ex_sems)
            dg_ref[...] = jnp.zeros_like(dg_ref)
            if with_dw:
                dw_scratch[0][...] = jnp.zeros_like(dw_scratch[0])

        r, xhat = _rms_fwd(x_ref[...])
        g = g_ref[...]
        h = (xhat * g).astype(MXU)
        dh = None
        for p, part_ref in enumerate((p0, p1, p2, p3)):
            for jj in range(half):
                j = half * p + jj
                if len(part_ref.shape) == 3:
                    per_block = W_BLOCK // PAIR
                    du = jnp.concatenate([part_ref[jj * per_block + pp] for pp in range(per_block)], axis=1)
                else:
                    du = part_ref[:, jj * W_BLOCK:(jj + 1) * W_BLOCK]
                t = _dot_nt(du, w_ref[j])
                dh = t if dh is None else dh + t
                if with_dw:
                    dw_scratch[0][j] += _dot_tn(h, du)
        dg_ref[...] += jnp.sum(dh * xhat, axis=0, keepdims=True)
        dx_ref[...] = dres_ref[...] + _rms_bwd(dh, xhat, r, g)

        @pl.when(i == nt - 1)
        def _():
            if with_dw:
                _flush(dw_scratch[0], dw_hbm[0], dw_scratch[1])
            ex.wait(ex_in, ex_out, ex_sems)

    tok = pl.BlockSpec((TM, D_MODEL), lambda i: (i, 0))
    return pl.pallas_call(
        body, name=name, grid=(nt,),
        in_specs=[pl.BlockSpec(shape, m) for _, shape, m in parts]
        + [tok, tok, _const_spec((1, D_MODEL)), _const_spec((N_DEV, D_MODEL, W_BLOCK))] + ex.specs,
        out_specs=[tok, pl.BlockSpec((1, D_MODEL), lambda i: (0, 0))]
        + [pl.BlockSpec(memory_space=pl.ANY)] * n_dw + ex.specs,
        out_shape=[jax.ShapeDtypeStruct((n_tok, D_MODEL), F32), jax.ShapeDtypeStruct((1, D_MODEL), F32)]
        + [jax.ShapeDtypeStruct((N_DEV, D_MODEL, W_BLOCK), MXU)] * n_dw + ex.out_shape,
        scratch_shapes=[pltpu.VMEM((N_DEV, D_MODEL, W_BLOCK), F32), pltpu.VMEM((D_MODEL, W_BLOCK), MXU)][:2 * n_dw]
        + ex.scratch,
        compiler_params=_params(1),
    )(*[a for a, _, _ in parts], x, dres, g_pre, w_in, *ex.arrays)


def _w_in_grad_scatter(h_t, du, exchange_items, name):
    n_tok = du.shape[0]
    ni = n_tok // TMB
    ex = _Exchange(exchange_items)
    me_out = 4 * lax.axis_index("x") + 2 * lax.axis_index("y") + lax.axis_index("c")
    order = ((me_out + 1 + jnp.arange(N_DEV, dtype=jnp.int32)) % N_DEV).astype(jnp.int32)

    def body(order_ref, h_ref, du_ref, *rest):
        ex_in, rest = rest[:ex.n], rest[ex.n:]
        part_hbm, ex_out, rest = rest[0], rest[1:1 + ex.n], rest[1 + ex.n:]
        acc_ref, stage_ref, send_sems, recv_sems = rest[:4]
        ex_sems = rest[4:]
        s = pl.program_id(0)
        i = pl.program_id(1)
        x, y, c = lax.axis_index("x"), lax.axis_index("y"), lax.axis_index("c")
        me = 4 * x + 2 * y + c

        def block_copy(step, src_slot, dst_slot, owner):
            return pltpu.make_async_remote_copy(
                src_ref=stage_ref.at[src_slot], dst_ref=part_hbm.at[dst_slot],
                send_sem=send_sems.at[step], recv_sem=recv_sems.at[step],
                device_id=(owner // 4, (owner // 2) % 2, owner % 2), device_id_type=pl.DeviceIdType.MESH)

        @pl.when((s == 0) & (i == 0))
        def _():
            ex.start(ex_in, ex_out, ex_sems)

        @pl.when(i == 0)
        def _():
            acc_ref[...] = jnp.zeros_like(acc_ref)

        acc_ref[...] += _dot(h_ref[:, pl.ds(pl.multiple_of(i * TMB, TMB), TMB)], du_ref[...])

        @pl.when(i == ni - 1)
        def _():
            stage_ref[s] = acc_ref[...].astype(MXU)

            @pl.when(s < N_DEV - 1)
            def _():
                block_copy(s, s, me, (me + 1 + s) % N_DEV).start()

            @pl.when(s == N_DEV - 1)
            def _():
                pltpu.sync_copy(stage_ref.at[s], part_hbm.at[me])
                for step in range(N_DEV - 1):
                    sender = (me + N_DEV - 1 - step) % N_DEV
                    block_copy(step, step, sender, me).wait_recv()
                    block_copy(step, step, me, me).wait_send()
                ex.wait(ex_in, ex_out, ex_sems)

    grid_spec = pltpu.PrefetchScalarGridSpec(
        num_scalar_prefetch=1, grid=(N_DEV, ni),
        in_specs=[pl.BlockSpec((D_MODEL, n_tok), lambda s, i, order: (0, 0), pipeline_mode=pl.Buffered(1)),
                  pl.BlockSpec((TMB, W_BLOCK), lambda s, i, order: (i, order[s]))] + ex.specs,
        out_specs=[pl.BlockSpec(memory_space=pl.ANY)] + ex.specs,
        scratch_shapes=[pltpu.VMEM((D_MODEL, W_BLOCK), F32), pltpu.VMEM((N_DEV, D_MODEL, W_BLOCK), MXU),
                        pltpu.SemaphoreType.DMA((N_DEV - 1,)), pltpu.SemaphoreType.DMA((N_DEV - 1,))] + ex.scratch)
    return pl.pallas_call(
        body, name=name, grid_spec=grid_spec,
        out_shape=[jax.ShapeDtypeStruct((N_DEV, D_MODEL, W_BLOCK), MXU)] + ex.out_shape,
        compiler_params=_params(2),
    )(order, h_t, du, *ex.arrays)


def _rel_onehot():
    rel = lax.broadcasted_iota(jnp.int32, (REL_PAD, ROLL_W), 0)
    col = lax.broadcasted_iota(jnp.int32, (REL_PAD, ROLL_W), 1)
    return (rel == jnp.minimum(BAND + MAX_REL - col, 2 * MAX_REL)).astype(MXU)


def _split3(v):
    hi = v.astype(MXU)
    r1 = v - hi.astype(F32)
    mid = r1.astype(MXU)
    lo = (r1 - mid.astype(F32)).astype(MXU)
    return hi, mid, lo


def _fill_bias_table(rb_ref, out_ref):
    onehot = _rel_onehot()
    base = None
    for term in _split3(rb_ref[...]):
        t = _dot(term, onehot)
        base = t if base is None else base + t
    qi = lax.broadcasted_iota(jnp.int32, (CHUNK, ROLL_W), 0)
    kk = lax.broadcasted_iota(jnp.int32, (CHUNK, TKW), 1)
    for h in range(N_HEADS):
        t = jnp.broadcast_to(base[h:h + 1, :], (CHUNK, ROLL_W))
        for bit in range(6):
            t = jnp.where(((qi >> bit) & 1) == 1, pltpu.roll(t, 1 << bit, 1), t)
        for rr in range(TM // CHUNK):
            shifted = pltpu.roll(t, (CHUNK * rr - CHUNK) % ROLL_W, 1)[:, :TKW]
            band = kk - CHUNK * rr
            out_ref[h, rr * CHUNK:(rr + 1) * CHUNK, :] = jnp.where((band >= 0) & (band < BAND), shifted, NEG)


def _bias_grad(dtab):
    def body(dt_ref, out_ref, dbase_ref):
        qi = lax.broadcasted_iota(jnp.int32, (CHUNK, ROLL_W), 0)
        zeros = jnp.zeros((CHUNK, ROLL_W - TKW), F32)
        for h in range(N_HEADS):
            t = None
            for rr in range(TM // CHUNK):
                blk = jnp.concatenate([dt_ref[h, rr * CHUNK:(rr + 1) * CHUNK, :], zeros], axis=1)
                blk = pltpu.roll(blk, (CHUNK - CHUNK * rr) % ROLL_W, 1)
                t = blk if t is None else t + blk
            for bit in range(6):
                t = jnp.where(((qi >> bit) & 1) == 1, pltpu.roll(t, ROLL_W - (1 << bit), 1), t)
            dbase_ref[h:h + 1, :] = jnp.sum(t, axis=0, keepdims=True)
        onehot = _rel_onehot()
        acc = None
        for term in _split3(dbase_ref[...]):
            t = _dot_nt(term, onehot)
            acc = t if acc is None else acc + t
        out_ref[...] = acc

    return pl.pallas_call(
        body, name="bias_grad", out_shape=jax.ShapeDtypeStruct((N_HEADS, REL_PAD), F32),
        scratch_shapes=[pltpu.VMEM((N_HEADS, ROLL_W), F32)],
        compiler_params=pltpu.CompilerParams(vmem_limit_bytes=VMEM_LIMIT),
    )(dtab)


def _att_in(x1, g_pre, w_in):
    n_tok = x1.shape[0]
    nt = n_tok // TM
    lead = PAD // TM
    per_block = W_BLOCK // PAIR

    def body(x_ref, g_ref, w_ref, q_ref, k_ref, v_ref, z_ref):
        i = pl.program_id(0)

        @pl.when(i < lead)
        def _():
            k_ref[...] = jnp.zeros_like(k_ref)
            v_ref[...] = jnp.zeros_like(v_ref)

        @pl.when(i >= lead)
        def _():
            _, xhat = _rms_fwd(x_ref[...])
            h = (xhat * g_ref[...]).astype(MXU)
            for j in range(N_DEV):
                u = _dot(h, w_ref[j])
                if j >= 6:
                    z_ref[:, (j % 2) * W_BLOCK:(j % 2 + 1) * W_BLOCK] = u
                    continue
                dst = (q_ref, k_ref, v_ref)[j // 2]
                if j < 2:
                    u = u * QK_SCALE
                for pp in range(per_block):
                    dst[(j % 2) * per_block + pp] = u[:, pp * PAIR:(pp + 1) * PAIR].astype(MXU)

    late = pl.BlockSpec((TM, D_MODEL), lambda i: (jnp.maximum(i - lead, 0), 0))
    late3 = pl.BlockSpec((N_PAIRS, TM, PAIR), lambda i: (0, jnp.maximum(i - lead, 0), 0))
    padded3 = pl.BlockSpec((N_PAIRS, TM, PAIR), lambda i: (0, i, 0))
    return pl.pallas_call(
        body, name="att_in", grid=(nt + lead,),
        in_specs=[late, _const_spec((1, D_MODEL)), _const_spec((N_DEV, D_MODEL, W_BLOCK))],
        out_specs=[late3, padded3, padded3, late],
        out_shape=[jax.ShapeDtypeStruct((N_PAIRS, n_tok, PAIR), MXU),
                   jax.ShapeDtypeStruct((N_PAIRS, n_tok + PAD, PAIR), MXU),
                   jax.ShapeDtypeStruct((N_PAIRS, n_tok + PAD, PAIR), MXU),
                   jax.ShapeDtypeStruct((n_tok, ATT_WIDTH), F32)],
        compiler_params=_params(1),
    )(x1, g_pre, w_in)


def _head_masks():
    lane = lax.broadcasted_iota(jnp.int32, (1, PAIR), 1)
    first = lane < HEAD_DIM
    return first, jnp.logical_not(first)


def _pair_loop(one_pair, unroll):
    def loop_pass(t, carry):
        for u in range(unroll):
            one_pair(t * unroll + u)
        return carry

    lax.fori_loop(0, PAIRS_PER_STEP // unroll, loop_pass, 0)


def _scores(qh, k_refs, bias_ref, p, hh, tile, masked):
    ss = []
    for b in range(KB):
        s = _dot_nt(qh, k_refs[b][p]) + bias_ref[2 * p + hh, :, b * TM:(b + 1) * TM]
        if masked:
            s = s + jnp.where(tile + b < PAD // TM, NEG, 0.0).astype(F32)
        ss.append(s)
    return ss


def _pair_specs(index_map):
    return pl.BlockSpec((PAIRS_PER_STEP, TM, PAIR), index_map)


def _att_fwd(q, kpad, vpad, bias_tab):
    n_tok = q.shape[1]
    nt = n_tok // TM
    lead = PAD // TM

    def body(q_ref, k0, k1, k2, v0, v1, v2, bias_ref, o_ref, lse_ref):
        i = pl.program_id(1)
        masks = _head_masks()

        def pairs(masked):
            def one_pair(p):
                qv = q_ref[p]
                outs, ms = [], []
                for hh, mask in enumerate(masks):
                    qh = jnp.where(mask, qv, jnp.zeros_like(qv))
                    ss = _scores(qh, (k0, k1, k2), bias_ref, p, hh, i, masked)
                    m = None
                    for s in ss:
                        mb = jnp.max(s, axis=-1, keepdims=True)
                        m = mb if m is None else jnp.maximum(m, mb)
                    out = None
                    for b, v_ref in enumerate((v0, v1, v2)):
                        vb = v_ref[p]
                        t = _dot(jnp.exp(ss[b] - m).astype(MXU), jnp.where(mask, vb, jnp.ones_like(vb)))
                        out = t if out is None else out + t
                    outs.append(out)
                    ms.append(m)
                num = jnp.where(masks[0], outs[0], outs[1])
                den = jnp.where(masks[0], pltpu.roll(outs[0], HEAD_DIM, 1), pltpu.roll(outs[1], HEAD_DIM, 1))
                o_ref[p] = num / den
                lse_ref[p] = jnp.where(masks[0], ms[0], ms[1]) + jnp.log(den)

            _pair_loop(one_pair, FWD_UNROLL)

        @pl.when(i < lead)
        def _():
            pairs(True)

        @pl.when(i >= lead)
        def _():
            pairs(False)

    qspec = _pair_specs(lambda g, i: (g, i, 0))
    kspecs = [_pair_specs(functools.partial(lambda g, i, b: (g, i + b, 0), b=b)) for b in range(KB)]
    vspecs = [_pair_specs(functools.partial(lambda g, i, b: (g, i + b, 0), b=b)) for b in range(KB)]
    shape = jax.ShapeDtypeStruct((N_PAIRS, n_tok, PAIR), F32)
    return pl.pallas_call(
        body, name="att_fwd", grid=(N_PAIRS // PAIRS_PER_STEP, nt),
        in_specs=[qspec] + kspecs + vspecs + [pl.BlockSpec((2 * PAIRS_PER_STEP, TM, TKW), lambda g, i: (g, 0, 0))],
        out_specs=[qspec, qspec], out_shape=[shape, shape],
        compiler_params=_params(2),
    )(q, kpad, kpad, kpad, vpad, vpad, vpad, bias_tab)


def _att_bwd(q, kpad, vpad, do, o, lse, bias_tab):
    n_tok = q.shape[1]
    nt = n_tok // TM
    lead = PAD // TM

    def body(q_ref, do_ref, o_ref, lse_ref, k0, k1, k2, v0, v1, v2, bias_ref,
             dq_ref, dk_ref, dv_ref, dtab_ref, rk0, rk1, rv0, rv1):
        i = pl.program_id(1)
        masks = _head_masks()

        @pl.when(i == 0)
        def _():
            for ref in (rk0, rk1, rv0, rv1):
                ref[...] = jnp.zeros_like(ref)
            dtab_ref[...] = jnp.zeros_like(dtab_ref)

        def pairs(masked):
            def one_pair(p):
                qv = q_ref[p]
                dov = do_ref[p]
                doo = dov.astype(F32) * o_ref[p]
                lse_pair = lse_ref[p]
                dks = [None] * KB
                dvs = [None] * KB
                dqs = []
                for hh, mask in enumerate(masks):
                    qh = jnp.where(mask, qv, jnp.zeros_like(qv))
                    doh = jnp.where(mask, dov, jnp.zeros_like(dov))
                    dsum = jnp.sum(jnp.where(mask, doo, 0.0), axis=-1, keepdims=True)
                    lse_h = lse_pair[:, hh * HEAD_DIM:hh * HEAD_DIM + 1]
                    ss = _scores(qh, (k0, k1, k2), bias_ref, p, hh, i, masked)
                    dq = None
                    for b, (k_ref, v_ref) in enumerate(zip((k0, k1, k2), (v0, v1, v2))):
                        prob = jnp.exp(ss[b] - lse_h)
                        ds = prob * (_dot_nt(doh, v_ref[p]) - dsum)
                        dtab_ref[2 * p + hh, :, b * TM:(b + 1) * TM] += ds
                        dsb = ds.astype(MXU)
                        t = lax.dot_general(k_ref[p], dsb, (((0,), (1,)), ((), ())), preferred_element_type=F32)
                        dq = t if dq is None else dq + t
                        t = _dot_tn(qh, dsb)
                        dks[b] = t if dks[b] is None else dks[b] + t
                        t = _dot_tn(doh, prob.astype(MXU))
                        dvs[b] = t if dvs[b] is None else dvs[b] + t
                    dqs.append(dq)
                first_rows = lax.broadcasted_iota(jnp.int32, (PAIR, 1), 0) < HEAD_DIM
                dq_ref[p] = (jnp.where(first_rows, dqs[0], dqs[1]).T * QK_SCALE).astype(MXU)
                dk_ref[p] = (rk0[p] + dks[0].T).astype(MXU)
                dv_ref[p] = (rv0[p] + dvs[0].T).astype(MXU)
                rk0[p] = rk1[p] + dks[1].T
                rv0[p] = rv1[p] + dvs[1].T
                rk1[p] = dks[2].T
                rv1[p] = dvs[2].T

            _pair_loop(one_pair, BWD_UNROLL)

        @pl.when(i < lead)
        def _():
            pairs(True)

        @pl.when((i >= lead) & (i < nt))
        def _():
            pairs(False)

        @pl.when(i >= nt)
        def _():
            dk_ref[...] = rk0[...].astype(MXU)
            dv_ref[...] = rv0[...].astype(MXU)
            rk0[...] = rk1[...]
            rv0[...] = rv1[...]

    last = nt - 1
    qspec = _pair_specs(lambda g, i: (g, jnp.minimum(i, last), 0))
    kspecs = [_pair_specs(functools.partial(lambda g, i, b: (g, jnp.minimum(i, last) + b, 0), b=b)) for b in range(KB)]
    vspecs = [_pair_specs(functools.partial(lambda g, i, b: (g, jnp.minimum(i, last) + b, 0), b=b)) for b in range(KB)]
    pspec = _pair_specs(lambda g, i: (g, i, 0))
    tspec = pl.BlockSpec((2 * PAIRS_PER_STEP, TM, TKW), lambda g, i: (g, 0, 0))
    ring = pltpu.VMEM((PAIRS_PER_STEP, TM, PAIR), F32)
    return pl.pallas_call(
        body, name="att_bwd", grid=(N_PAIRS // PAIRS_PER_STEP, nt + KB - 1),
        in_specs=[qspec, qspec, qspec, qspec] + kspecs + vspecs + [tspec],
        out_specs=[qspec, pspec, pspec, tspec],
        out_shape=[jax.ShapeDtypeStruct((N_PAIRS, n_tok, PAIR), MXU),
                   jax.ShapeDtypeStruct((N_PAIRS, n_tok + PAD, PAIR), MXU),
                   jax.ShapeDtypeStruct((N_PAIRS, n_tok + PAD, PAIR), MXU),
                   jax.ShapeDtypeStruct((N_HEADS, TM, TKW), F32)],
        scratch_shapes=[ring, ring, ring, ring],
        compiler_params=_params(2),
    )(q, do, o, lse, kpad, kpad, kpad, vpad, vpad, vpad, bias_tab)


def _att_out(o, z, x1, target, g_post, w_out):
    n_tok = z.shape[0]
    nt = n_tok // TM

    def body(o_ref, z_ref, x1_ref, tgt_ref, gpost_ref, w_ref,
             loss_ref, dx2_ref, do_ref, dz_ref, dgpost_ref, dw_hbm, acc_ref, loss_acc, stage_ref):
        i = pl.program_id(0)

        @pl.when(i == 0)
        def _():
            acc_ref[...] = jnp.zeros_like(acc_ref)
            loss_acc[...] = jnp.zeros_like(loss_acc)
            dgpost_ref[...] = jnp.zeros_like(dgpost_ref)

        ov = jnp.concatenate([o_ref[p] for p in range(N_PAIRS)], axis=1)
        zv = z_ref[...]
        sig = _sigmoid(zv)
        silu = zv * sig
        gated = (ov * silu).astype(MXU)
        y = _dot(gated, w_ref[...])
        r, yhat = _rms_fwd(y)
        gpost = gpost_ref[...]
        diff = x1_ref[...] + yhat * gpost - tgt_ref[...]
        loss_acc[...] += jnp.sum(diff * diff, axis=0, keepdims=True)
        dn = diff * (1.0 / D_MODEL)
        dx2_ref[...] = dn
        dgpost_ref[...] += jnp.sum(dn * yhat, axis=0, keepdims=True)
        dy = _rms_bwd(dn, yhat, r, gpost).astype(MXU)
        for j in range(ATT_WIDTH // W_BLOCK):
            acc_ref[j] += _dot_tn(gated[:, j * W_BLOCK:(j + 1) * W_BLOCK], dy)
        dgated = _dot_nt(dy, w_ref[...])
        dob = (dgated * silu).astype(MXU)
        for p in range(N_PAIRS):
            do_ref[p] = dob[:, p * PAIR:(p + 1) * PAIR]
        dz_ref[...] = (dgated * ov * (sig * (1.0 + zv * (1.0 - sig)))).astype(MXU)

        @pl.when(i == nt - 1)
        def _():
            total = jnp.sum(loss_acc[...], axis=-1, keepdims=True) * (0.5 / D_MODEL)
            loss_ref[...] = jnp.broadcast_to(total, loss_ref.shape)
            _flush(acc_ref, dw_hbm, stage_ref)

    tok = pl.BlockSpec((TM, D_MODEL), lambda i: (i, 0))
    tok3 = pl.BlockSpec((N_PAIRS, TM, PAIR), lambda i: (0, i, 0))
    return pl.pallas_call(
        body, name="att_out", grid=(nt,),
        in_specs=[tok3, tok, tok, tok, _const_spec((1, D_MODEL)), _const_spec((ATT_WIDTH, D_MODEL))],
        out_specs=[pl.BlockSpec((1, 128), lambda i: (0, 0)), tok, tok3, tok,
                   pl.BlockSpec((1, D_MODEL), lambda i: (0, 0)), pl.BlockSpec(memory_space=pl.ANY)],
        out_shape=[jax.ShapeDtypeStruct((1, 128), F32), jax.ShapeDtypeStruct((n_tok, D_MODEL), F32),
                   jax.ShapeDtypeStruct((N_PAIRS, n_tok, PAIR), MXU), jax.ShapeDtypeStruct((n_tok, ATT_WIDTH), MXU),
                   jax.ShapeDtypeStruct((1, D_MODEL), F32),
                   jax.ShapeDtypeStruct((ATT_WIDTH // W_BLOCK, W_BLOCK, D_MODEL), MXU)],
        scratch_shapes=[pltpu.VMEM((ATT_WIDTH // W_BLOCK, W_BLOCK, D_MODEL), F32), pltpu.VMEM((1, D_MODEL), F32),
                        pltpu.VMEM((W_BLOCK, D_MODEL), MXU)],
        compiler_params=_params(1),
    )(o, z, x1, target, g_post, w_out)


def _adamw(parts, w, m, v, name):
    rows, cols = w.shape
    tr = min(rows, 256)

    def body(p_ref, w_ref, m_ref, v_ref, g_ref, d_ref, mo_ref, vo_ref):
        g = p_ref[0].astype(F32)
        for s in range(1, N_DEV):
            g = g + p_ref[s].astype(F32)
        m_new = ADAM_B1 * m_ref[...] + (1.0 - ADAM_B1) * g
        v_new = ADAM_B2 * v_ref[...] + (1.0 - ADAM_B2) * (g * g)
        m_hat = m_new / (1.0 - ADAM_B1 ** ADAM_STEP)
        v_hat = v_new / (1.0 - ADAM_B2 ** ADAM_STEP)
        g_ref[...] = g
        d_ref[...] = -ADAM_LR * (m_hat / (jnp.sqrt(v_hat) + ADAM_EPS) + ADAM_WD * w_ref[...])
        mo_ref[...] = m_new
        vo_ref[...] = v_new

    blk = pl.BlockSpec((tr, cols), lambda i: (i, 0))
    shape = jax.ShapeDtypeStruct((rows, cols), F32)
    return pl.pallas_call(
        body, name=name, grid=(rows // tr,),
        in_specs=[pl.BlockSpec((N_DEV, tr, cols), lambda i: (0, i, 0)), blk, blk, blk],
        out_specs=[blk, blk, blk, blk], out_shape=[shape, shape, shape, shape],
        compiler_params=_params(1),
    )(parts, w, m, v)


SMALL_ROWS = 16


def _pack_small(norm_pre, norm_post, pool_scale, rel_bias_padded):
    return jnp.concatenate([norm_pre, norm_post, pool_scale.reshape(2, D_MODEL),
                            rel_bias_padded.reshape(SMALL_ROWS - 6, D_MODEL)], axis=0)


def _unpack_small(packed):
    rel = packed[6:].reshape(N_HEADS, REL_PAD)[:, :N_REL]
    return packed[0:2], packed[2:4], packed[4:6].reshape(1, POOL_WIDTH), rel.reshape(1, N_HEADS, N_REL)


def _pad_rel(rel_bias):
    return jnp.pad(rel_bias.reshape(N_HEADS, N_REL), ((0, 0), (0, REL_PAD - N_REL)))


def kernel(x, norm_pre, norm_post, pool_w_in, pool_w_group, pool_scale, pool_w_out, att_w_in, att_rel_bias, att_w_out, loss_target, m_norm_pre, m_norm_post, m_pool_w_in, m_pool_w_group, m_pool_scale, m_pool_w_out, m_att_w_in, m_att_rel_bias, m_att_w_out, v_norm_pre, v_norm_post, v_pool_w_in, v_pool_w_group, v_pool_scale, v_pool_w_out, v_att_w_in, v_att_rel_bias, v_att_w_out):
    xt = x[0]
    target = loss_target[0]
    n_tok = xt.shape[0]
    lead = PAD // TM
    rows_g = GROUP // N_DEV

    rel_padded = _pad_rel(att_rel_bias[0])
    gathered = _gather_two_level([pool_w_in[0].astype(MXU), pool_w_group[0].astype(MXU), pool_w_out[0].astype(MXU)],
                                 rel_padded, "gather_pool_weights")
    bias_tab = gathered[3]
    w_in_p = gathered[0]
    w_group = gathered[1].transpose(1, 0, 2, 3).reshape(N_GROUPS, GROUP, GROUP)
    w_out_p = gathered[2].reshape(POOL_WIDTH, D_MODEL)

    x1, y0, z0, mixed, mg, prod, h0_t, w_in_a, w_out_a = _pool_fwd(
        xt, norm_pre[0:1], norm_post[0:1], w_in_p, w_group, pool_scale, w_out_p,
        [(att_w_in[0].astype(MXU), False), (att_w_out[0].astype(MXU), False)])
    w_out_a = w_out_a.reshape(ATT_WIDTH, D_MODEL)
    q, kpad, vpad, z1 = _att_in(x1, norm_pre[1:2], w_in_a)
    o, lse = _att_fwd(q, kpad, vpad, bias_tab)
    loss_part, dx2, do, dz1, d_gpost1, d_w_out_a = _att_out(o, z1, x1, target, norm_post[1:2], w_out_a)
    dq, dkpad, dvpad, dtab = _att_bwd(q, kpad, vpad, do, o, lse, bias_tab)
    d_rel = _bias_grad(dtab)
    pairs = (N_PAIRS, TM, PAIR)
    flat = (TM, D_MODEL)
    dx1, d_gpre1, d_w_in_a = _in_proj_bwd(
        [(dq, pairs, lambda i: (0, i, 0)), (dkpad, pairs, lambda i: (0, i + lead, 0)),
         (dvpad, pairs, lambda i: (0, i + lead, 0)), (dz1, flat, lambda i: (i, 0))],
        x1, dx2, norm_pre[1:2], w_in_a, "att_in_bwd", True, [])
    du0, d_scale, d_gpost0, d_w_group, d_w_out_p, part_w_in_a, part_w_out_a = _pool_bwd(
        dx1, y0, z0, mg, mixed, prod, norm_post[0:1], pool_scale, w_group, w_out_p,
        [(d_w_in_a, True), (d_w_out_a.reshape(N_DEV, ATT_WIDTH // N_DEV, D_MODEL), True)])
    col = lambda p: (lambda i: (i, p))
    grad_x, d_gpre0, part_w_group, part_w_out_p = _in_proj_bwd(
        [(du0, flat, col(p)) for p in range(4)], xt, dx1, norm_pre[0:1], w_in_p, "pool_in_bwd", False,
        [(d_w_group.reshape(N_GROUPS, N_DEV, rows_g, GROUP).transpose(1, 0, 2, 3), True),
         (d_w_out_p.reshape(N_DEV, POOL_WIDTH // N_DEV, D_MODEL), True)])
    d_small = _pack_small(jnp.concatenate([d_gpre0, d_gpre1], axis=0), jnp.concatenate([d_gpost0, d_gpost1], axis=0),
                          d_scale, d_rel)
    part_w_in_p, part_small = _w_in_grad_scatter(h0_t, du0, [(d_small, False)], "pool_w_in_grad")

    def update(part, w, m, v, name):
        shape = w.shape
        flat = lambda a: a.reshape(-1, shape[-1])
        outs = _adamw(part.reshape(N_DEV, -1, shape[-1]), flat(w), flat(m), flat(v), name)
        return [a.reshape(shape) for a in outs]

    u_att_w_in = update(part_w_in_a, att_w_in, m_att_w_in, v_att_w_in, "adamw_att_w_in")
    u_att_w_out = update(part_w_out_a, att_w_out, m_att_w_out, v_att_w_out, "adamw_att_w_out")
    u_pool_w_group = update(part_w_group, pool_w_group, m_pool_w_group, v_pool_w_group, "adamw_pool_w_group")
    u_pool_w_out = update(part_w_out_p, pool_w_out, m_pool_w_out, v_pool_w_out, "adamw_pool_w_out")
    u_pool_w_in = update(part_w_in_p, pool_w_in, m_pool_w_in, v_pool_w_in, "adamw_pool_w_in")
    small = _adamw(part_small, _pack_small(norm_pre, norm_post, pool_scale, rel_padded),
                   _pack_small(m_norm_pre, m_norm_post, m_pool_scale, _pad_rel(m_att_rel_bias[0])),
                   _pack_small(v_norm_pre, v_norm_post, v_pool_scale, _pad_rel(v_att_rel_bias[0])), "adamw_small")
    u_small = [_unpack_small(a) for a in small]

    loss = lax.psum(loss_part[0, 0], ("x", "y", "c"))
    outs = [loss, grad_x.reshape(1, n_tok, D_MODEL)]
    for kind in range(4):
        outs += [u_small[kind][0], u_small[kind][1], u_pool_w_in[kind], u_pool_w_group[kind], u_small[kind][2],
                 u_pool_w_out[kind], u_att_w_in[kind], u_small[kind][3], u_att_w_out[kind]]
    return tuple(outs)
```

```python
import functools

import jax
import jax.numpy as jnp
from jax import lax
from jax.experimental import pallas as pl
from jax.experimental.pallas import tpu as pltpu

F32 = jnp.float32
MXU = jnp.bfloat16

D_MODEL = 1024
POOL_WIDTH = 2048
POOL_WINDOWS = (2, 4, 8, 16)
N_GROUPS = 4
GROUP = 512
HALO = 16
N_HEADS = 16
HEAD_DIM = 64
CHUNK = 64
LEFT_CHUNKS = 8
PAD = LEFT_CHUNKS * CHUNK
BAND = PAD + CHUNK
MAX_REL = 256
N_REL = 2 * MAX_REL + 1
REL_PAD = 640
ATT_WIDTH = 1024
PAIR = 2 * HEAD_DIM
N_PAIRS = N_HEADS // 2
N_DEV = 8
W_BLOCK = 512
RMS_EPS = 1e-6
QK_SCALE = 0.125
NEG = -1e30

TM = 256
TMB = 1024
TAIL_PEER_BITS = ((6, 7, 4, 2, 5, 3, 1, 0), (6, 7, 2, 4, 3, 5, 1, 0))
KB = 3
TKW = KB * TM
PAIRS_PER_STEP = 4
FWD_UNROLL = 4
BWD_UNROLL = 2
ROLL_W = 1024

VMEM_LIMIT = 56 * 1024 * 1024

ADAM_LR = 0.001
ADAM_B1 = 0.9
ADAM_B2 = 0.999
ADAM_EPS = 1e-08
ADAM_WD = 0.01
ADAM_STEP = 10

NT_DIMS = (((1,), (1,)), ((), ()))
TN_DIMS = (((0,), (0,)), ((), ()))


def _params(n_grid):
    return pltpu.CompilerParams(dimension_semantics=("arbitrary",) * n_grid, vmem_limit_bytes=VMEM_LIMIT)


def _const_spec(shape):
    nd = len(shape)
    return pl.BlockSpec(shape, lambda *_: (0,) * nd, pipeline_mode=pl.Buffered(1))


def _dot(a, b):
    return jnp.dot(a, b, preferred_element_type=F32)


def _dot_nt(a, b):
    return lax.dot_general(a, b, NT_DIMS, preferred_element_type=F32)


def _dot_tn(a, b):
    return lax.dot_general(a, b, TN_DIMS, preferred_element_type=F32)


def _sigmoid(z):
    return 1.0 / (1.0 + jnp.exp(-z))


def _rms_fwd(xv):
    r = lax.rsqrt(jnp.mean(xv * xv, axis=-1, keepdims=True) + RMS_EPS)
    return r, xv * r


def _rms_bwd(dn, xhat, r, g):
    dng = dn * g
    return r * (dng - xhat * jnp.mean(dng * xhat, axis=-1, keepdims=True))


class _Exchange:
    def __init__(self, items):
        self.arrays = [a for a, _ in items]
        self.scatter = [s for _, s in items]
        self.n = len(items)
        self.out_shape = [jax.ShapeDtypeStruct((N_DEV,) + tuple(a.shape[1:] if s else a.shape), a.dtype)
                          for a, s in items]
        self.specs = [pl.BlockSpec(memory_space=pl.ANY)] * self.n
        self.scratch = ([pltpu.SemaphoreType.DMA((N_DEV - 1, self.n)), pltpu.SemaphoreType.DMA((N_DEV - 1, self.n)),
                         pltpu.SemaphoreType.DMA((self.n,))] if self.n else [])

    def _copies(self, ins, outs, sems, with_receives):
        send_sems, recv_sems, local_sems = sems
        x, y, c = lax.axis_index("x"), lax.axis_index("y"), lax.axis_index("c")
        me = 4 * x + 2 * y + c

        def src(t, slot):
            return ins[t].at[slot] if self.scatter[t] else ins[t]

        local = [pltpu.make_async_copy(src(t, me), outs[t].at[me], local_sems.at[t]) for t in range(self.n)]
        sends, recvs = [], []
        for k in range(1, N_DEV):
            px = 1 - x if k & 4 else x
            py = 1 - y if k & 2 else y
            pc = 1 - c if k & 1 else c
            peer = 4 * px + 2 * py + pc
            for t in range(self.n):
                common = dict(src_ref=src(t, peer), send_sem=send_sems.at[k - 1, t], recv_sem=recv_sems.at[k - 1, t],
                              device_id=(px, py, pc), device_id_type=pl.DeviceIdType.MESH)
                sends.append(pltpu.make_async_remote_copy(dst_ref=outs[t].at[me], **common))
                if with_receives:
                    recvs.append(pltpu.make_async_remote_copy(dst_ref=outs[t].at[peer], **common))
        return local, sends, recvs

    def start(self, ins, outs, sems):
        if self.n:
            local, sends, _ = self._copies(ins, outs, sems, False)
            for cp in local + sends:
                cp.start()

    def wait(self, ins, outs, sems):
        if self.n:
            local, sends, recvs = self._copies(ins, outs, sems, True)
            for cp in recvs:
                cp.wait_recv()
            for cp in sends:
                cp.wait_send()
            for cp in local:
                cp.wait()


def _exchange(items, name):
    ex = _Exchange(items)
    n = ex.n

    def body(*refs):
        ins, outs, sems = refs[:n], refs[n:2 * n], refs[2 * n:]
        ex.start(ins, outs, sems)
        ex.wait(ins, outs, sems)

    return pl.pallas_call(
        body, name=name, out_shape=ex.out_shape, in_specs=ex.specs, out_specs=ex.specs, scratch_shapes=ex.scratch,
        compiler_params=pltpu.CompilerParams(has_side_effects=True),
    )(*ex.arrays)


def _gather_two_level(arrays, rel_bias_padded, name):
    n = len(arrays)
    out_shape = [jax.ShapeDtypeStruct((N_DEV,) + a.shape, a.dtype) for a in arrays]

    def body(*refs):
        ins, rb_ref, outs, tab_ref = refs[:n], refs[n], refs[n + 1:2 * n + 1], refs[2 * n + 1]
        send_sems, recv_sems, local_sems = refs[2 * n + 2:]
        x, y, c = lax.axis_index("x"), lax.axis_index("y"), lax.axis_index("c")
        sibling = (x, y, 1 - c)
        chips = [(1 - x, y), (x, 1 - y), (1 - x, 1 - y)]

        def slot(px, py, pc):
            return 4 * px + 2 * py + pc

        def copy(kind, t, block, to, own):
            return pltpu.make_async_remote_copy(
                src_ref=ins[t] if own else outs[t].at[slot(*block)], dst_ref=outs[t].at[slot(*block)],
                send_sem=send_sems.at[kind, t], recv_sem=recv_sems.at[kind, t],
                device_id=to, device_id_type=pl.DeviceIdType.MESH)

        local = [pltpu.make_async_copy(ins[t], outs[t].at[slot(x, y, c)], local_sems.at[t]) for t in range(n)]
        first = [copy(1 + j, t, (x, y, c), (*chip, c), True) for t in range(n) for j, chip in enumerate(chips)]
        first += [copy(0, t, (x, y, c), sibling, True) for t in range(n)]
        for cp in local + first:
            cp.start()
        _fill_bias_table(rb_ref, tab_ref)
        passed = []
        for t in range(n):
            for j, chip in enumerate(chips):
                copy(1 + j, t, (*chip, c), (x, y, c), False).wait_recv()
                cp = copy(4 + j, t, (*chip, c), sibling, False)
                cp.start()
                passed.append(cp)
        for t in range(n):
            copy(0, t, sibling, (x, y, c), False).wait_recv()
            for j, chip in enumerate(chips):
                copy(4 + j, t, (*chip, 1 - c), (x, y, c), False).wait_recv()
        for cp in first + passed:
            cp.wait_send()
        for cp in local:
            cp.wait()

    any_spec = pl.BlockSpec(memory_space=pl.ANY)
    vmem_spec = pl.BlockSpec(memory_space=pltpu.VMEM)
    return pl.pallas_call(
        body, name=name, out_shape=out_shape + [jax.ShapeDtypeStruct((N_HEADS, TM, TKW), F32)],
        in_specs=[any_spec] * n + [vmem_spec], out_specs=[any_spec] * n + [vmem_spec],
        scratch_shapes=[pltpu.SemaphoreType.DMA((7, n)), pltpu.SemaphoreType.DMA((7, n)), pltpu.SemaphoreType.DMA((n,))],
        compiler_params=pltpu.CompilerParams(has_side_effects=True, vmem_limit_bytes=VMEM_LIMIT),
    )(*arrays, rel_bias_padded)


def _inv_count(row, window):
    return 1.0 / jnp.minimum(row + 1, window).astype(F32)


def _pool_fwd(x, g_pre, g_post, w_in, w_group, scale, w_out, exchange_items):
    n_tok = x.shape[0]
    nt = n_tok // TM
    ex = _Exchange(exchange_items)

    def body(x_ref, gpre_ref, gpost_ref, win_ref, wg_ref, sc_ref, wout_ref, *rest):
        ex_in, rest = rest[:ex.n], rest[ex.n:]
        x1_ref, y_ref, z_ref, mixed_ref, mg_ref, prod_ref, ht_ref = rest[:7]
        ex_out, carry_ref, ex_sems = rest[7:7 + ex.n], rest[7 + ex.n], rest[8 + ex.n:]
        i = pl.program_id(0)

        @pl.when(i == 0)
        def _():
            ex.start(ex_in, ex_out, ex_sems)
            carry_ref[...] = jnp.zeros_like(carry_ref)

        xv = x_ref[...]
        r, xhat = _rms_fwd(xv)
        hf = xhat * gpre_ref[...]
        h = hf.astype(MXU)
        ht_ref[...] = hf.T.astype(MXU)
        row = i * TM + lax.broadcasted_iota(jnp.int32, (TM, 1), 0)
        y = None
        for g in range(N_GROUPS):
            cols = slice(g * GROUP, (g + 1) * GROUP)
            a = _dot(h, win_ref[g])
            z = _dot(h, win_ref[N_GROUPS + g])
            s = jnp.concatenate([carry_ref[g], a], axis=0)
            carry_ref[g] = a[TM - HALO:, :]
            w = 1
            while w < POOL_WINDOWS[g]:
                s = s + pltpu.roll(s, w, 0)
                w *= 2
            mixed = (s[HALO:, :] * _inv_count(row, POOL_WINDOWS[g]) - a).astype(MXU)
            mg = _dot(mixed, wg_ref[g])
            prod = (mg * sc_ref[:, cols] * (z * _sigmoid(z))).astype(MXU)
            z_ref[:, cols] = z
            mixed_ref[:, cols] = mixed
            mg_ref[:, cols] = mg
            prod_ref[:, cols] = prod
            part = _dot(prod, wout_ref[cols, :])
            y = part if y is None else y + part
        y_ref[...] = y
        _, yhat = _rms_fwd(y)
        x1_ref[...] = xv + yhat * gpost_ref[...]

        @pl.when(i == nt - 1)
        def _():
            ex.wait(ex_in, ex_out, ex_sems)

    tok = lambda w: pl.BlockSpec((TM, w), lambda i: (i, 0))
    return pl.pallas_call(
        body, name="pool_fwd", grid=(nt,),
        in_specs=[tok(D_MODEL), _const_spec((1, D_MODEL)), _const_spec((1, D_MODEL)),
                  _const_spec((N_DEV, D_MODEL, W_BLOCK)), _const_spec((N_GROUPS, GROUP, GROUP)),
                  _const_spec((1, POOL_WIDTH)), _const_spec((POOL_WIDTH, D_MODEL))] + ex.specs,
        out_specs=[tok(D_MODEL), tok(D_MODEL), tok(POOL_WIDTH), tok(POOL_WIDTH), tok(POOL_WIDTH), tok(POOL_WIDTH),
                   pl.BlockSpec((D_MODEL, TM), lambda i: (0, i))] + ex.specs,
        out_shape=[jax.ShapeDtypeStruct((n_tok, D_MODEL), F32), jax.ShapeDtypeStruct((n_tok, D_MODEL), F32),
                   jax.ShapeDtypeStruct((n_tok, POOL_WIDTH), F32), jax.ShapeDtypeStruct((n_tok, POOL_WIDTH), MXU),
                   jax.ShapeDtypeStruct((n_tok, POOL_WIDTH), F32), jax.ShapeDtypeStruct((n_tok, POOL_WIDTH), MXU),
                   jax.ShapeDtypeStruct((D_MODEL, n_tok), MXU)] + ex.out_shape,
        scratch_shapes=[pltpu.VMEM((N_GROUPS, HALO, GROUP), F32)] + ex.scratch,
        compiler_params=_params(1),
    )(x, g_pre, g_post, w_in, w_group, scale, w_out, *ex.arrays)


def _flush(acc_ref, out_hbm, stage_ref):
    for j in range(acc_ref.shape[0]):
        stage_ref[...] = acc_ref[j].astype(stage_ref.dtype)
        pltpu.sync_copy(stage_ref, out_hbm.at[j])


def _pool_bwd(dx1, y, z, mg, mixed, prod, g_post, scale, w_group, w_out, exchange_items):
    n_tok = dx1.shape[0]
    nt = n_tok // TM
    ex = _Exchange(exchange_items)

    def body(dx1_ref, y_ref, z_ref, mg_ref, mixed_ref, prod_ref, gpost_ref, sc_ref, wg_ref, wout_ref, *rest):
        ex_in, rest = rest[:ex.n], rest[ex.n:]
        du_ref, dsc_ref, dgpost_ref, dwg_hbm, dwout_hbm = rest[:5]
        ex_out, rest = rest[5:5 + ex.n], rest[5 + ex.n:]
        carry_ref, dwg_acc, dwout_acc, stage_g, stage_o = rest[:5]
        ex_sems = rest[5:]
        i = pl.program_id(0)

        @pl.when(i == 0)
        def _():
            ex.start(ex_in, ex_out, ex_sems)
            carry_ref[...] = jnp.zeros_like(carry_ref)
            dwg_acc[...] = jnp.zeros_like(dwg_acc)
            dwout_acc[...] = jnp.zeros_like(dwout_acc)
            dsc_ref[...] = jnp.zeros_like(dsc_ref)
            dgpost_ref[...] = jnp.zeros_like(dgpost_ref)

        dn = dx1_ref[...]
        r, yhat = _rms_fwd(y_ref[...])
        dgpost_ref[...] += jnp.sum(dn * yhat, axis=0, keepdims=True)
        dy = _rms_bwd(dn, yhat, r, gpost_ref[...]).astype(MXU)
        row = (nt - 1 - i) * TM + lax.broadcasted_iota(jnp.int32, (TM, 1), 0)
        n_ext = TM + HALO
        for g in range(N_GROUPS):
            cols = slice(g * GROUP, (g + 1) * GROUP)
            dwout_acc[g] += _dot_tn(prod_ref[:, cols], dy)
            dprod = _dot_nt(dy, wout_ref[cols, :])
            zv = z_ref[:, cols]
            sig = _sigmoid(zv)
            silu = zv * sig
            mgv = mg_ref[:, cols]
            sc = sc_ref[:, cols]
            dsc_ref[:, cols] += jnp.sum(dprod * silu * mgv, axis=0, keepdims=True)
            dmg = (dprod * silu * sc).astype(MXU)
            dz = dprod * (mgv * sc) * (sig * (1.0 + zv * (1.0 - sig)))
            dwg_acc[g] += _dot_tn(mixed_ref[:, cols], dmg)
            dmixed = _dot_nt(dmg, wg_ref[g])
            e = dmixed * _inv_count(row, POOL_WINDOWS[g])
            s = jnp.concatenate([e, carry_ref[g]], axis=0)
            carry_ref[g] = e[:HALO, :]
            w = 1
            while w < POOL_WINDOWS[g]:
                s = s + pltpu.roll(s, n_ext - w, 0)
                w *= 2
            du_ref[:, cols] = (s[:TM, :] - dmixed).astype(MXU)
            du_ref[:, POOL_WIDTH + g * GROUP:POOL_WIDTH + (g + 1) * GROUP] = dz.astype(MXU)

        @pl.when(i == nt - 1)
        def _():
            _flush(dwg_acc, dwg_hbm, stage_g)
            _flush(dwout_acc, dwout_hbm, stage_o)
            ex.wait(ex_in, ex_out, ex_sems)

    rev = lambda w: pl.BlockSpec((TM, w), lambda i: (nt - 1 - i, 0))
    any_spec = pl.BlockSpec(memory_space=pl.ANY)
    return pl.pallas_call(
        body, name="pool_bwd", grid=(nt,),
        in_specs=[rev(D_MODEL), rev(D_MODEL), rev(POOL_WIDTH), rev(POOL_WIDTH), rev(POOL_WIDTH), rev(POOL_WIDTH),
                  _const_spec((1, D_MODEL)), _const_spec((1, POOL_WIDTH)),
                  _const_spec((N_GROUPS, GROUP, GROUP)), _const_spec((POOL_WIDTH, D_MODEL))] + ex.specs,
        out_specs=[rev(2 * POOL_WIDTH), pl.BlockSpec((1, POOL_WIDTH), lambda i: (0, 0)),
                   pl.BlockSpec((1, D_MODEL), lambda i: (0, 0)), any_spec, any_spec] + ex.specs,
        out_shape=[jax.ShapeDtypeStruct((n_tok, 2 * POOL_WIDTH), MXU), jax.ShapeDtypeStruct((1, POOL_WIDTH), F32),
                   jax.ShapeDtypeStruct((1, D_MODEL), F32), jax.ShapeDtypeStruct((N_GROUPS, GROUP, GROUP), MXU),
                   jax.ShapeDtypeStruct((N_GROUPS, GROUP, D_MODEL), MXU)] + ex.out_shape,
        scratch_shapes=[pltpu.VMEM((N_GROUPS, HALO, GROUP), F32), pltpu.VMEM((N_GROUPS, GROUP, GROUP), F32),
                        pltpu.VMEM((N_GROUPS, GROUP, D_MODEL), F32), pltpu.VMEM((GROUP, GROUP), MXU),
                        pltpu.VMEM((GROUP, D_MODEL), MXU)] + ex.scratch,
        compiler_params=_params(1),
    )(dx1, y, z, mg, mixed, prod, g_post, scale, w_group, w_out, *ex.arrays)


def _in_proj_bwd(parts, x, dres, g_pre, w_in, name, with_dw, exchange_items):
    n_tok = x.shape[0]
    nt = n_tok // TM
    half = D_MODEL // W_BLOCK
    ex = _Exchange(exchange_items)
    n_dw = 1 if with_dw else 0

    def body(p0, p1, p2, p3, x_ref, dres_ref, g_ref, w_ref, *rest):
        ex_in, rest = rest[:ex.n], rest[ex.n:]
        dx_ref, dg_ref = rest[:2]
        dw_hbm = rest[2:2 + n_dw]
        ex_out, rest = rest[2 + n_dw:2 + n_dw + ex.n], rest[2 + n_dw + ex.n:]
        dw_scratch, ex_sems = rest[:2 * n_dw], rest[2 * n_dw:]
        i = pl.program_id(0)

        @pl.when(i == 0)
        def _():
            ex.start(ex_in, ex_out, ex_sems)
            dg_ref[...] = jnp.zeros_like(dg_ref)
            if with_dw:
                dw_scratch[0][...] = jnp.zeros_like(dw_scratch[0])

        r, xhat = _rms_fwd(x_ref[...])
        g = g_ref[...]
        h = (xhat * g).astype(MXU)
        dh = None
        for p, part_ref in enumerate((p0, p1, p2, p3)):
            for jj in range(half):
                j = half * p + jj
                if len(part_ref.shape) == 3:
                    per_block = W_BLOCK // PAIR
                    du = jnp.concatenate([part_ref[jj * per_block + pp] for pp in range(per_block)], axis=1)
                else:
                    du = part_ref[:, jj * W_BLOCK:(jj + 1) * W_BLOCK]
                t = _dot_nt(du, w_ref[j])
                dh = t if dh is None else dh + t
                if with_dw:
                    dw_scratch[0][j] += _dot_tn(h, du)
        dg_ref[...] += jnp.sum(dh * xhat, axis=0, keepdims=True)
        dx_ref[...] = dres_ref[...] + _rms_bwd(dh, xhat, r, g)

        @pl.when(i == nt - 1)
        def _():
            if with_dw:
                _flush(dw_scratch[0], dw_hbm[0], dw_scratch[1])
            ex.wait(ex_in, ex_out, ex_sems)

    tok = pl.BlockSpec((TM, D_MODEL), lambda i: (i, 0))
    return pl.pallas_call(
        body, name=name, grid=(nt,),
        in_specs=[pl.BlockSpec(shape, m) for _, shape, m in parts]
        + [tok, tok, _const_spec((1, D_MODEL)), _const_spec((N_DEV, D_MODEL, W_BLOCK))] + ex.specs,
        out_specs=[tok, pl.BlockSpec((1, D_MODEL), lambda i: (0, 0))]
        + [pl.BlockSpec(memory_space=pl.ANY)] * n_dw + ex.specs,
        out_shape=[jax.ShapeDtypeStruct((n_tok, D_MODEL), F32), jax.ShapeDtypeStruct((1, D_MODEL), F32)]
        + [jax.ShapeDtypeStruct((N_DEV, D_MODEL, W_BLOCK), MXU)] * n_dw + ex.out_shape,
        scratch_shapes=[pltpu.VMEM((N_DEV, D_MODEL, W_BLOCK), F32), pltpu.VMEM((D_MODEL, W_BLOCK), MXU)][:2 * n_dw]
        + ex.scratch,
        compiler_params=_params(1),
    )(*[a for a, _, _ in parts], x, dres, g_pre, w_in, *ex.arrays)


def _flip(x, y, c, bits):
    return (1 - x if bits & 4 else x, 1 - y if bits & 2 else y, 1 - c if bits & 1 else c)


def _tail_peer(x, y, c, step, receive):
    south, north = TAIL_PEER_BITS[0][step], TAIL_PEER_BITS[1][step]
    if receive and south & 1:
        south, north = north, south
    if south == north:
        return _flip(x, y, c, south)
    return tuple(jnp.where(c == 0, u, v) for u, v in zip(_flip(x, y, c, south), _flip(x, y, c, north)))


def _w_in_grad_scatter(h_t, du, exchange_items, name):
    n_tok = du.shape[0]
    ni = n_tok // TMB
    ex = _Exchange(exchange_items)
    c_out = lax.axis_index("c")
    me_out = 4 * lax.axis_index("x") + 2 * lax.axis_index("y") + c_out
    order = jnp.where(c_out == 0, me_out ^ jnp.array(TAIL_PEER_BITS[0], jnp.int32),
                      me_out ^ jnp.array(TAIL_PEER_BITS[1], jnp.int32)).astype(jnp.int32)

    def body(order_ref, h_ref, du_ref, *rest):
        ex_in, rest = rest[:ex.n], rest[ex.n:]
        part_hbm, ex_out, rest = rest[0], rest[1:1 + ex.n], rest[1 + ex.n:]
        acc_ref, stage_ref, send_sems, recv_sems = rest[:4]
        ex_sems = rest[4:]
        s = pl.program_id(0)
        i = pl.program_id(1)
        x, y, c = lax.axis_index("x"), lax.axis_index("y"), lax.axis_index("c")
        me = 4 * x + 2 * y + c

        def block_copy(step, dst_slot, to):
            return pltpu.make_async_remote_copy(
                src_ref=stage_ref.at[step], dst_ref=part_hbm.at[dst_slot],
                send_sem=send_sems.at[step], recv_sem=recv_sems.at[step],
                device_id=to, device_id_type=pl.DeviceIdType.MESH)

        @pl.when((s == 0) & (i == 0))
        def _():
            ex.start(ex_in, ex_out, ex_sems)

        @pl.when(i == 0)
        def _():
            acc_ref[...] = jnp.zeros_like(acc_ref)

        acc_ref[...] += _dot(h_ref[:, pl.ds(pl.multiple_of(i * TMB, TMB), TMB)], du_ref[...])

        @pl.when(i == ni - 1)
        def _():
            stage_ref[s] = acc_ref[...].astype(MXU)
            for step in range(N_DEV - 1):
                @pl.when(s == step)
                def _(step=step):
                    block_copy(step, me, _tail_peer(x, y, c, step, False)).start()

            @pl.when(s == N_DEV - 1)
            def _():
                pltpu.sync_copy(stage_ref.at[s], part_hbm.at[me])
                for step in range(N_DEV - 1):
                    sx, sy, sc = _tail_peer(x, y, c, step, True)
                    block_copy(step, 4 * sx + 2 * sy + sc, (x, y, c)).wait_recv()
                    block_copy(step, me, (x, y, c)).wait_send()
                ex.wait(ex_in, ex_out, ex_sems)

    grid_spec = pltpu.PrefetchScalarGridSpec(
        num_scalar_prefetch=1, grid=(N_DEV, ni),
        in_specs=[pl.BlockSpec((D_MODEL, n_tok), lambda s, i, order: (0, 0), pipeline_mode=pl.Buffered(1)),
                  pl.BlockSpec((TMB, W_BLOCK), lambda s, i, order: (i, order[s]))] + ex.specs,
        out_specs=[pl.BlockSpec(memory_space=pl.ANY)] + ex.specs,
        scratch_shapes=[pltpu.VMEM((D_MODEL, W_BLOCK), F32), pltpu.VMEM((N_DEV, D_MODEL, W_BLOCK), MXU),
                        pltpu.SemaphoreType.DMA((N_DEV - 1,)), pltpu.SemaphoreType.DMA((N_DEV - 1,))] + ex.scratch)
    return pl.pallas_call(
        body, name=name, grid_spec=grid_spec,
        out_shape=[jax.ShapeDtypeStruct((N_DEV, D_MODEL, W_BLOCK), MXU)] + ex.out_shape,
        compiler_params=_params(2),
    )(order, h_t, du, *ex.arrays)


def _rel_onehot():
    rel = lax.broadcasted_iota(jnp.int32, (REL_PAD, ROLL_W), 0)
    col = lax.broadcasted_iota(jnp.int32, (REL_PAD, ROLL_W), 1)
    return (rel == jnp.minimum(BAND + MAX_REL - col, 2 * MAX_REL)).astype(MXU)


def _split3(v):
    hi = v.astype(MXU)
    r1 = v - hi.astype(F32)
    mid = r1.astype(MXU)
    lo = (r1 - mid.astype(F32)).astype(MXU)
    return hi, mid, lo


def _fill_bias_table(rb_ref, out_ref):
    onehot = _rel_onehot()
    base = None
    for term in _split3(rb_ref[...]):
        t = _dot(term, onehot)
        base = t if base is None else base + t
    qi = lax.broadcasted_iota(jnp.int32, (CHUNK, ROLL_W), 0)
    kk = lax.broadcasted_iota(jnp.int32, (CHUNK, TKW), 1)
    for h in range(N_HEADS):
        t = jnp.broadcast_to(base[h:h + 1, :], (CHUNK, ROLL_W))
        for bit in range(6):
            t = jnp.where(((qi >> bit) & 1) == 1, pltpu.roll(t, 1 << bit, 1), t)
        for rr in range(TM // CHUNK):
            shifted = pltpu.roll(t, (CHUNK * rr - CHUNK) % ROLL_W, 1)[:, :TKW]
            band = kk - CHUNK * rr
            out_ref[h, rr * CHUNK:(rr + 1) * CHUNK, :] = jnp.where((band >= 0) & (band < BAND), shifted, NEG)


def _bias_grad(dtab):
    def body(dt_ref, out_ref, dbase_ref):
        qi = lax.broadcasted_iota(jnp.int32, (CHUNK, ROLL_W), 0)
        zeros = jnp.zeros((CHUNK, ROLL_W - TKW), F32)
        for h in range(N_HEADS):
            t = None
            for rr in range(TM // CHUNK):
                blk = jnp.concatenate([dt_ref[h, rr * CHUNK:(rr + 1) * CHUNK, :], zeros], axis=1)
                blk = pltpu.roll(blk, (CHUNK - CHUNK * rr) % ROLL_W, 1)
                t = blk if t is None else t + blk
            for bit in range(6):
                t = jnp.where(((qi >> bit) & 1) == 1, pltpu.roll(t, ROLL_W - (1 << bit), 1), t)
            dbase_ref[h:h + 1, :] = jnp.sum(t, axis=0, keepdims=True)
        onehot = _rel_onehot()
        acc = None
        for term in _split3(dbase_ref[...]):
            t = _dot_nt(term, onehot)
            acc = t if acc is None else acc + t
        out_ref[...] = acc

    return pl.pallas_call(
        body, name="bias_grad", out_shape=jax.ShapeDtypeStruct((N_HEADS, REL_PAD), F32),
        scratch_shapes=[pltpu.VMEM((N_HEADS, ROLL_W), F32)],
        compiler_params=pltpu.CompilerParams(vmem_limit_bytes=VMEM_LIMIT),
    )(dtab)


def _att_in(x1, g_pre, w_in):
    n_tok = x1.shape[0]
    nt = n_tok // TM
    lead = PAD // TM
    per_block = W_BLOCK // PAIR

    def body(x_ref, g_ref, w_ref, q_ref, k_ref, v_ref, z_ref):
        i = pl.program_id(0)

        @pl.when(i < lead)
        def _():
            k_ref[...] = jnp.zeros_like(k_ref)
            v_ref[...] = jnp.zeros_like(v_ref)

        @pl.when(i >= lead)
        def _():
            _, xhat = _rms_fwd(x_ref[...])
            h = (xhat * g_ref[...]).astype(MXU)
            for j in range(N_DEV):
                u = _dot(h, w_ref[j])
                if j >= 6:
                    z_ref[:, (j % 2) * W_BLOCK:(j % 2 + 1) * W_BLOCK] = u
                    continue
                dst = (q_ref, k_ref, v_ref)[j // 2]
                if j < 2:
                    u = u * QK_SCALE
                for pp in range(per_block):
                    dst[(j % 2) * per_block + pp] = u[:, pp * PAIR:(pp + 1) * PAIR].astype(MXU)

    late = pl.BlockSpec((TM, D_MODEL), lambda i: (jnp.maximum(i - lead, 0), 0))
    late3 = pl.BlockSpec((N_PAIRS, TM, PAIR), lambda i: (0, jnp.maximum(i - lead, 0), 0))
    padded3 = pl.BlockSpec((N_PAIRS, TM, PAIR), lambda i: (0, i, 0))
    return pl.pallas_call(
        body, name="att_in", grid=(nt + lead,),
        in_specs=[late, _const_spec((1, D_MODEL)), _const_spec((N_DEV, D_MODEL, W_BLOCK))],
        out_specs=[late3, padded3, padded3, late],
        out_shape=[jax.ShapeDtypeStruct((N_PAIRS, n_tok, PAIR), MXU),
                   jax.ShapeDtypeStruct((N_PAIRS, n_tok + PAD, PAIR), MXU),
                   jax.ShapeDtypeStruct((N_PAIRS, n_tok + PAD, PAIR), MXU),
                   jax.ShapeDtypeStruct((n_tok, ATT_WIDTH), F32)],
        compiler_params=_params(1),
    )(x1, g_pre, w_in)


def _head_masks():
    lane = lax.broadcasted_iota(jnp.int32, (1, PAIR), 1)
    first = lane < HEAD_DIM
    return first, jnp.logical_not(first)


def _pair_loop(one_pair, unroll):
    def loop_pass(t, carry):
        for u in range(unroll):
            one_pair(t * unroll + u)
        return carry

    lax.fori_loop(0, PAIRS_PER_STEP // unroll, loop_pass, 0)


def _scores(qh, k_refs, bias_ref, p, hh, tile, masked):
    ss = []
    for b in range(KB):
        s = _dot_nt(qh, k_refs[b][p]) + bias_ref[2 * p + hh, :, b * TM:(b + 1) * TM]
        if masked:
            s = s + jnp.where(tile + b < PAD // TM, NEG, 0.0).astype(F32)
        ss.append(s)
    return ss


def _pair_specs(index_map):
    return pl.BlockSpec((PAIRS_PER_STEP, TM, PAIR), index_map)


def _att_fwd(q, kpad, vpad, bias_tab):
    n_tok = q.shape[1]
    nt = n_tok // TM
    lead = PAD // TM

    def body(q_ref, k0, k1, k2, v0, v1, v2, bias_ref, o_ref, lse_ref):
        i = pl.program_id(1)
        masks = _head_masks()

        def pairs(masked):
            def one_pair(p):
                qv = q_ref[p]
                outs, ms = [], []
                for hh, mask in enumerate(masks):
                    qh = jnp.where(mask, qv, jnp.zeros_like(qv))
                    ss = _scores(qh, (k0, k1, k2), bias_ref, p, hh, i, masked)
                    m = None
                    for s in ss:
                        mb = jnp.max(s, axis=-1, keepdims=True)
                        m = mb if m is None else jnp.maximum(m, mb)
                    out = None
                    for b, v_ref in enumerate((v0, v1, v2)):
                        vb = v_ref[p]
                        t = _dot(jnp.exp(ss[b] - m).astype(MXU), jnp.where(mask, vb, jnp.ones_like(vb)))
                        out = t if out is None else out + t
                    outs.append(out)
                    ms.append(m)
                num = jnp.where(masks[0], outs[0], outs[1])
                den = jnp.where(masks[0], pltpu.roll(outs[0], HEAD_DIM, 1), pltpu.roll(outs[1], HEAD_DIM, 1))
                o_ref[p] = num / den
                lse_ref[p] = jnp.where(masks[0], ms[0], ms[1]) + jnp.log(den)

            _pair_loop(one_pair, FWD_UNROLL)

        @pl.when(i < lead)
        def _():
            pairs(True)

        @pl.when(i >= lead)
        def _():
            pairs(False)

    qspec = _pair_specs(lambda g, i: (g, i, 0))
    kspecs = [_pair_specs(functools.partial(lambda g, i, b: (g, i + b, 0), b=b)) for b in range(KB)]
    vspecs = [_pair_specs(functools.partial(lambda g, i, b: (g, i + b, 0), b=b)) for b in range(KB)]
    shape = jax.ShapeDtypeStruct((N_PAIRS, n_tok, PAIR), F32)
    return pl.pallas_call(
        body, name="att_fwd", grid=(N_PAIRS // PAIRS_PER_STEP, nt),
        in_specs=[qspec] + kspecs + vspecs + [pl.BlockSpec((2 * PAIRS_PER_STEP, TM, TKW), lambda g, i: (g, 0, 0))],
        out_specs=[qspec, qspec], out_shape=[shape, shape],
        compiler_params=_params(2),
    )(q, kpad, kpad, kpad, vpad, vpad, vpad, bias_tab)


def _att_bwd(q, kpad, vpad, do, o, lse, bias_tab):
    n_tok = q.shape[1]
    nt = n_tok // TM
    lead = PAD // TM

    def body(q_ref, do_ref, o_ref, lse_ref, k0, k1, k2, v0, v1, v2, bias_ref,
             dq_ref, dk_ref, dv_ref, dtab_ref, rk0, rk1, rv0, rv1):
        i = pl.program_id(1)
        masks = _head_masks()

        @pl.when(i == 0)
        def _():
            for ref in (rk0, rk1, rv0, rv1):
                ref[...] = jnp.zeros_like(ref)
            dtab_ref[...] = jnp.zeros_like(dtab_ref)

        def pairs(masked):
            def one_pair(p):
                qv = q_ref[p]
                dov = do_ref[p]
                doo = dov.astype(F32) * o_ref[p]
                lse_pair = lse_ref[p]
                dks = [None] * KB
                dvs = [None] * KB
                dqs = []
                for hh, mask in enumerate(masks):
                    qh = jnp.where(mask, qv, jnp.zeros_like(qv))
                    doh = jnp.where(mask, dov, jnp.zeros_like(dov))
                    dsum = jnp.sum(jnp.where(mask, doo, 0.0), axis=-1, keepdims=True)
                    lse_h = lse_pair[:, hh * HEAD_DIM:hh * HEAD_DIM + 1]
                    ss = _scores(qh, (k0, k1, k2), bias_ref, p, hh, i, masked)
                    dq = None
                    for b, (k_ref, v_ref) in enumerate(zip((k0, k1, k2), (v0, v1, v2))):
                        prob = jnp.exp(ss[b] - lse_h)
                        ds = prob * (_dot_nt(doh, v_ref[p]) - dsum)
                        dtab_ref[2 * p + hh, :, b * TM:(b + 1) * TM] += ds
                        dsb = ds.astype(MXU)
                        t = lax.dot_general(k_ref[p], dsb, (((0,), (1,)), ((), ())), preferred_element_type=F32)
                        dq = t if dq is None else dq + t
                        t = _dot_tn(qh, dsb)
                        dks[b] = t if dks[b] is None else dks[b] + t
                        t = _dot_tn(doh, prob.astype(MXU))
                        dvs[b] = t if dvs[b] is None else dvs[b] + t
                    dqs.append(dq)
                first_rows = lax.broadcasted_iota(jnp.int32, (PAIR, 1), 0) < HEAD_DIM
                dq_ref[p] = (jnp.where(first_rows, dqs[0], dqs[1]).T * QK_SCALE).astype(MXU)
                dk_ref[p] = (rk0[p] + dks[0].T).astype(MXU)
                dv_ref[p] = (rv0[p] + dvs[0].T).astype(MXU)
                rk0[p] = rk1[p] + dks[1].T
                rv0[p] = rv1[p] + dvs[1].T
                rk1[p] = dks[2].T
                rv1[p] = dvs[2].T

            _pair_loop(one_pair, BWD_UNROLL)

        @pl.when(i < lead)
        def _():
            pairs(True)

        @pl.when((i >= lead) & (i < nt))
        def _():
            pairs(False)

        @pl.when(i >= nt)
        def _():
            dk_ref[...] = rk0[...].astype(MXU)
            dv_ref[...] = rv0[...].astype(MXU)
            rk0[...] = rk1[...]
            rv0[...] = rv1[...]

    last = nt - 1
    qspec = _pair_specs(lambda g, i: (g, jnp.minimum(i, last), 0))
    kspecs = [_pair_specs(functools.partial(lambda g, i, b: (g, jnp.minimum(i, last) + b, 0), b=b)) for b in range(KB)]
    vspecs = [_pair_specs(functools.partial(lambda g, i, b: (g, jnp.minimum(i, last) + b, 0), b=b)) for b in range(KB)]
    pspec = _pair_specs(lambda g, i: (g, i, 0))
    tspec = pl.BlockSpec((2 * PAIRS_PER_STEP, TM, TKW), lambda g, i: (g, 0, 0))
    ring = pltpu.VMEM((PAIRS_PER_STEP, TM, PAIR), F32)
    return pl.pallas_call(
        body, name="att_bwd", grid=(N_PAIRS // PAIRS_PER_STEP, nt + KB - 1),
        in_specs=[qspec, qspec, qspec, qspec] + kspecs + vspecs + [tspec],
        out_specs=[qspec, pspec, pspec, tspec],
        out_shape=[jax.ShapeDtypeStruct((N_PAIRS, n_tok, PAIR), MXU),
                   jax.ShapeDtypeStruct((N_PAIRS, n_tok + PAD, PAIR), MXU),
                   jax.ShapeDtypeStruct((N_PAIRS, n_tok + PAD, PAIR), MXU),
                   jax.ShapeDtypeStruct((N_HEADS, TM, TKW), F32)],
        scratch_shapes=[ring, ring, ring, ring],
        compiler_params=_params(2),
    )(q, do, o, lse, kpad, kpad, kpad, vpad, vpad, vpad, bias_tab)


def _att_out(o, z, x1, target, g_post, w_out):
    n_tok = z.shape[0]
    nt = n_tok // TM

    def body(o_ref, z_ref, x1_ref, tgt_ref, gpost_ref, w_ref,
             loss_ref, dx2_ref, do_ref, dz_ref, dgpost_ref, dw_hbm, acc_ref, loss_acc, stage_ref):
        i = pl.program_id(0)

        @pl.when(i == 0)
        def _():
            acc_ref[...] = jnp.zeros_like(acc_ref)
            loss_acc[...] = jnp.zeros_like(loss_acc)
            dgpost_ref[...] = jnp.zeros_like(dgpost_ref)

        ov = jnp.concatenate([o_ref[p] for p in range(N_PAIRS)], axis=1)
        zv = z_ref[...]
        sig = _sigmoid(zv)
        silu = zv * sig
        gated = (ov * silu).astype(MXU)
        y = _dot(gated, w_ref[...])
        r, yhat = _rms_fwd(y)
        gpost = gpost_ref[...]
        diff = x1_ref[...] + yhat * gpost - tgt_ref[...]
        loss_acc[...] += jnp.sum(diff * diff, axis=0, keepdims=True)
        dn = diff * (1.0 / D_MODEL)
        dx2_ref[...] = dn
        dgpost_ref[...] += jnp.sum(dn * yhat, axis=0, keepdims=True)
        dy = _rms_bwd(dn, yhat, r, gpost).astype(MXU)
        for j in range(ATT_WIDTH // W_BLOCK):
            acc_ref[j] += _dot_tn(gated[:, j * W_BLOCK:(j + 1) * W_BLOCK], dy)
        dgated = _dot_nt(dy, w_ref[...])
        dob = (dgated * silu).astype(MXU)
        for p in range(N_PAIRS):
            do_ref[p] = dob[:, p * PAIR:(p + 1) * PAIR]
        dz_ref[...] = (dgated * ov * (sig * (1.0 + zv * (1.0 - sig)))).astype(MXU)

        @pl.when(i == nt - 1)
        def _():
            total = jnp.sum(loss_acc[...], axis=-1, keepdims=True) * (0.5 / D_MODEL)
            loss_ref[...] = jnp.broadcast_to(total, loss_ref.shape)
            _flush(acc_ref, dw_hbm, stage_ref)

    tok = pl.BlockSpec((TM, D_MODEL), lambda i: (i, 0))
    tok3 = pl.BlockSpec((N_PAIRS, TM, PAIR), lambda i: (0, i, 0))
    return pl.pallas_call(
        body, name="att_out", grid=(nt,),
        in_specs=[tok3, tok, tok, tok, _const_spec((1, D_MODEL)), _const_spec((ATT_WIDTH, D_MODEL))],
        out_specs=[pl.BlockSpec((1, 128), lambda i: (0, 0)), tok, tok3, tok,
                   pl.BlockSpec((1, D_MODEL), lambda i: (0, 0)), pl.BlockSpec(memory_space=pl.ANY)],
        out_shape=[jax.ShapeDtypeStruct((1, 128), F32), jax.ShapeDtypeStruct((n_tok, D_MODEL), F32),
                   jax.ShapeDtypeStruct((N_PAIRS, n_tok, PAIR), MXU), jax.ShapeDtypeStruct((n_tok, ATT_WIDTH), MXU),
                   jax.ShapeDtypeStruct((1, D_MODEL), F32),
                   jax.ShapeDtypeStruct((ATT_WIDTH // W_BLOCK, W_BLOCK, D_MODEL), MXU)],
        scratch_shapes=[pltpu.VMEM((ATT_WIDTH // W_BLOCK, W_BLOCK, D_MODEL), F32), pltpu.VMEM((1, D_MODEL), F32),
                        pltpu.VMEM((W_BLOCK, D_MODEL), MXU)],
        compiler_params=_params(1),
    )(o, z, x1, target, g_post, w_out)


def _adamw(parts, w, m, v, name):
    rows, cols = w.shape
    tr = min(rows, 256)

    def body(p_ref, w_ref, m_ref, v_ref, g_ref, d_ref, mo_ref, vo_ref):
        g = p_ref[0].astype(F32)
        for s in range(1, N_DEV):
            g = g + p_ref[s].astype(F32)
        m_new = ADAM_B1 * m_ref[...] + (1.0 - ADAM_B1) * g
        v_new = ADAM_B2 * v_ref[...] + (1.0 - ADAM_B2) * (g * g)
        m_hat = m_new / (1.0 - ADAM_B1 ** ADAM_STEP)
        v_hat = v_new / (1.0 - ADAM_B2 ** ADAM_STEP)
        g_ref[...] = g
        d_ref[...] = -ADAM_LR * (m_hat / (jnp.sqrt(v_hat) + ADAM_EPS) + ADAM_WD * w_ref[...])
        mo_ref[...] = m_new
        vo_ref[...] = v_new

    blk = pl.BlockSpec((tr, cols), lambda i: (i, 0))
    shape = jax.ShapeDtypeStruct((rows, cols), F32)
    return pl.pallas_call(
        body, name=name, grid=(rows // tr,),
        in_specs=[pl.BlockSpec((N_DEV, tr, cols), lambda i: (0, i, 0)), blk, blk, blk],
        out_specs=[blk, blk, blk, blk], out_shape=[shape, shape, shape, shape],
        compiler_params=_params(1),
    )(parts, w, m, v)


SMALL_ROWS = 16
LOSS_AT = (6, N_REL)


def _pack_small(norm_pre, norm_post, pool_scale, rel_bias_padded):
    return jnp.concatenate([norm_pre, norm_post, pool_scale.reshape(2, D_MODEL),
                            rel_bias_padded.reshape(SMALL_ROWS - 6, D_MODEL)], axis=0)


def _unpack_small(packed):
    rel = packed[6:].reshape(N_HEADS, REL_PAD)[:, :N_REL]
    return packed[0:2], packed[2:4], packed[4:6].reshape(1, POOL_WIDTH), rel.reshape(1, N_HEADS, N_REL)


def _pad_rel(rel_bias):
    return jnp.pad(rel_bias.reshape(N_HEADS, N_REL), ((0, 0), (0, REL_PAD - N_REL)))


def kernel(x, norm_pre, norm_post, pool_w_in, pool_w_group, pool_scale, pool_w_out, att_w_in, att_rel_bias, att_w_out, loss_target, m_norm_pre, m_norm_post, m_pool_w_in, m_pool_w_group, m_pool_scale, m_pool_w_out, m_att_w_in, m_att_rel_bias, m_att_w_out, v_norm_pre, v_norm_post, v_pool_w_in, v_pool_w_group, v_pool_scale, v_pool_w_out, v_att_w_in, v_att_rel_bias, v_att_w_out):
    xt = x[0]
    target = loss_target[0]
    n_tok = xt.shape[0]
    lead = PAD // TM
    rows_g = GROUP // N_DEV

    rel_padded = _pad_rel(att_rel_bias[0])
    gathered = _gather_two_level([pool_w_in[0].astype(MXU), pool_w_group[0].astype(MXU), pool_w_out[0].astype(MXU)],
                                 rel_padded, "gather_pool_weights")
    bias_tab = gathered[3]
    w_in_p = gathered[0]
    w_group = gathered[1].transpose(1, 0, 2, 3).reshape(N_GROUPS, GROUP, GROUP)
    w_out_p = gathered[2].reshape(POOL_WIDTH, D_MODEL)

    x1, y0, z0, mixed, mg, prod, h0_t, w_in_a, w_out_a = _pool_fwd(
        xt, norm_pre[0:1], norm_post[0:1], w_in_p, w_group, pool_scale, w_out_p,
        [(att_w_in[0].astype(MXU), False), (att_w_out[0].astype(MXU), False)])
    w_out_a = w_out_a.reshape(ATT_WIDTH, D_MODEL)
    q, kpad, vpad, z1 = _att_in(x1, norm_pre[1:2], w_in_a)
    o, lse = _att_fwd(q, kpad, vpad, bias_tab)
    loss_part, dx2, do, dz1, d_gpost1, d_w_out_a = _att_out(o, z1, x1, target, norm_post[1:2], w_out_a)
    dq, dkpad, dvpad, dtab = _att_bwd(q, kpad, vpad, do, o, lse, bias_tab)
    d_rel = _bias_grad(dtab)
    pairs = (N_PAIRS, TM, PAIR)
    flat = (TM, D_MODEL)
    dx1, d_gpre1, d_w_in_a = _in_proj_bwd(
        [(dq, pairs, lambda i: (0, i, 0)), (dkpad, pairs, lambda i: (0, i + lead, 0)),
         (dvpad, pairs, lambda i: (0, i + lead, 0)), (dz1, flat, lambda i: (i, 0))],
        x1, dx2, norm_pre[1:2], w_in_a, "att_in_bwd", True, [])
    du0, d_scale, d_gpost0, d_w_group, d_w_out_p, part_w_in_a, part_w_out_a = _pool_bwd(
        dx1, y0, z0, mg, mixed, prod, norm_post[0:1], pool_scale, w_group, w_out_p,
        [(d_w_in_a, True), (d_w_out_a.reshape(N_DEV, ATT_WIDTH // N_DEV, D_MODEL), True)])
    col = lambda p: (lambda i: (i, p))
    grad_x, d_gpre0, part_w_group, part_w_out_p = _in_proj_bwd(
        [(du0, flat, col(p)) for p in range(4)], xt, dx1, norm_pre[0:1], w_in_p, "pool_in_bwd", False,
        [(d_w_group.reshape(N_GROUPS, N_DEV, rows_g, GROUP).transpose(1, 0, 2, 3), True),
         (d_w_out_p.reshape(N_DEV, POOL_WIDTH // N_DEV, D_MODEL), True)])
    d_small = _pack_small(jnp.concatenate([d_gpre0, d_gpre1], axis=0), jnp.concatenate([d_gpost0, d_gpost1], axis=0),
                          d_scale, d_rel).at[LOSS_AT].set(loss_part[0, 0])
    part_w_in_p, part_small = _w_in_grad_scatter(h0_t, du0, [(d_small, False)], "pool_w_in_grad")

    def update(part, w, m, v, name):
        shape = w.shape
        flat = lambda a: a.reshape(-1, shape[-1])
        outs = _adamw(part.reshape(N_DEV, -1, shape[-1]), flat(w), flat(m), flat(v), name)
        return [a.reshape(shape) for a in outs]

    u_att_w_in = update(part_w_in_a, att_w_in, m_att_w_in, v_att_w_in, "adamw_att_w_in")
    u_att_w_out = update(part_w_out_a, att_w_out, m_att_w_out, v_att_w_out, "adamw_att_w_out")
    u_pool_w_group = update(part_w_group, pool_w_group, m_pool_w_group, v_pool_w_group, "adamw_pool_w_group")
    u_pool_w_out = update(part_w_out_p, pool_w_out, m_pool_w_out, v_pool_w_out, "adamw_pool_w_out")
    u_pool_w_in = update(part_w_in_p, pool_w_in, m_pool_w_in, v_pool_w_in, "adamw_pool_w_in")
    small = _adamw(part_small, _pack_small(norm_pre, norm_post, pool_scale, rel_padded),
                   _pack_small(m_norm_pre, m_norm_post, m_pool_scale, _pad_rel(m_att_rel_bias[0])),
                   _pack_small(v_norm_pre, v_norm_post, v_pool_scale, _pad_rel(v_att_rel_bias[0])), "adamw_small")
    u_small = [_unpack_small(a) for a in small]

    loss = small[0][LOSS_AT]
    outs = [loss, grad_x.reshape(1, n_tok, D_MODEL)]
    for kind in range(4):
        outs += [u_small[kind][0], u_small[kind][1], u_pool_w_in[kind], u_pool_w_group[kind], u_small[kind][2],
                 u_pool_w_out[kind], u_att_w_in[kind], u_small[kind][3], u_att_w_out[kind]]
    return tuple(outs)
```

```python
import functools

import jax
import jax.numpy as jnp
from jax import lax
from jax.experimental import pallas as pl
from jax.experimental.pallas import tpu as pltpu

F32 = jnp.float32
MXU = jnp.bfloat16

D_MODEL = 1024
POOL_WIDTH = 2048
POOL_WINDOWS = (2, 4, 8, 16)
N_GROUPS = 4
GROUP = 512
HALO = 16
N_HEADS = 16
HEAD_DIM = 64
CHUNK = 64
LEFT_CHUNKS = 8
PAD = LEFT_CHUNKS * CHUNK
BAND = PAD + CHUNK
MAX_REL = 256
N_REL = 2 * MAX_REL + 1
REL_PAD = 640
ATT_WIDTH = 1024
PAIR = 2 * HEAD_DIM
N_PAIRS = N_HEADS // 2
N_DEV = 8
W_BLOCK = 512
RMS_EPS = 1e-6
QK_SCALE = 0.125
NEG = -1e30

TM = 256
TMB = 1024
TAIL_PEER_BITS = ((6, 7, 4, 2, 5, 3, 1, 0), (6, 7, 2, 4, 3, 5, 1, 0))
KB = 3
TKW = KB * TM
PAIRS_PER_STEP = 4
FWD_UNROLL = 4
BWD_UNROLL = 2
ROLL_W = 1024

VMEM_LIMIT = 56 * 1024 * 1024

ADAM_LR = 0.001
ADAM_B1 = 0.9
ADAM_B2 = 0.999
ADAM_EPS = 1e-08
ADAM_WD = 0.01
ADAM_STEP = 10

NT_DIMS = (((1,), (1,)), ((), ()))
TN_DIMS = (((0,), (0,)), ((), ()))


def _params(n_grid):
    return pltpu.CompilerParams(dimension_semantics=("arbitrary",) * n_grid, vmem_limit_bytes=VMEM_LIMIT)


def _const_spec(shape):
    nd = len(shape)
    return pl.BlockSpec(shape, lambda *_: (0,) * nd, pipeline_mode=pl.Buffered(1))


def _dot(a, b):
    return jnp.dot(a, b, preferred_element_type=F32)


def _dot_nt(a, b):
    return lax.dot_general(a, b, NT_DIMS, preferred_element_type=F32)


def _dot_tn(a, b):
    return lax.dot_general(a, b, TN_DIMS, preferred_element_type=F32)


def _sigmoid(z):
    return 1.0 / (1.0 + jnp.exp(-z))


def _rms_fwd(xv):
    r = lax.rsqrt(jnp.mean(xv * xv, axis=-1, keepdims=True) + RMS_EPS)
    return r, xv * r


def _rms_bwd(dn, xhat, r, g):
    dng = dn * g
    return r * (dng - xhat * jnp.mean(dng * xhat, axis=-1, keepdims=True))


class _Exchange:
    def __init__(self, items):
        self.arrays = [a for a, _ in items]
        self.scatter = [s for _, s in items]
        self.n = len(items)
        self.out_shape = [jax.ShapeDtypeStruct((N_DEV,) + tuple(a.shape[1:] if s else a.shape), a.dtype)
                          for a, s in items]
        self.specs = [pl.BlockSpec(memory_space=pl.ANY)] * self.n
        self.scratch = ([pltpu.SemaphoreType.DMA((N_DEV - 1, self.n)), pltpu.SemaphoreType.DMA((N_DEV - 1, self.n)),
                         pltpu.SemaphoreType.DMA((self.n,))] if self.n else [])

    def _copies(self, ins, outs, sems, with_receives):
        send_sems, recv_sems, local_sems = sems
        x, y, c = lax.axis_index("x"), lax.axis_index("y"), lax.axis_index("c")
        me = 4 * x + 2 * y + c

        def src(t, slot):
            return ins[t].at[slot] if self.scatter[t] else ins[t]

        local = [pltpu.make_async_copy(src(t, me), outs[t].at[me], local_sems.at[t]) for t in range(self.n)]
        sends, recvs = [], []
        for k in range(1, N_DEV):
            px = 1 - x if k & 4 else x
            py = 1 - y if k & 2 else y
            pc = 1 - c if k & 1 else c
            peer = 4 * px + 2 * py + pc
            for t in range(self.n):
                common = dict(src_ref=src(t, peer), send_sem=send_sems.at[k - 1, t], recv_sem=recv_sems.at[k - 1, t],
                              device_id=(px, py, pc), device_id_type=pl.DeviceIdType.MESH)
                sends.append(pltpu.make_async_remote_copy(dst_ref=outs[t].at[me], **common))
                if with_receives:
                    recvs.append(pltpu.make_async_remote_copy(dst_ref=outs[t].at[peer], **common))
        return local, sends, recvs

    def start(self, ins, outs, sems):
        if self.n:
            local, sends, _ = self._copies(ins, outs, sems, False)
            for cp in local + sends:
                cp.start()

    def wait(self, ins, outs, sems):
        if self.n:
            local, sends, recvs = self._copies(ins, outs, sems, True)
            for cp in recvs:
                cp.wait_recv()
            for cp in sends:
                cp.wait_send()
            for cp in local:
                cp.wait()


def _exchange(items, name):
    ex = _Exchange(items)
    n = ex.n

    def body(*refs):
        ins, outs, sems = refs[:n], refs[n:2 * n], refs[2 * n:]
        ex.start(ins, outs, sems)
        ex.wait(ins, outs, sems)

    return pl.pallas_call(
        body, name=name, out_shape=ex.out_shape, in_specs=ex.specs, out_specs=ex.specs, scratch_shapes=ex.scratch,
        compiler_params=pltpu.CompilerParams(has_side_effects=True),
    )(*ex.arrays)


def _gather_two_level(arrays, rel_bias_padded, name):
    n = len(arrays)
    out_shape = [jax.ShapeDtypeStruct((N_DEV,) + a.shape, a.dtype) for a in arrays]
    own_sib, own_x, own_y, half_via_x, half_via_y, x_sib, y_sib, diag_sib = range(8)

    def body(*refs):
        ins, rb_ref, outs, tab_ref = refs[:n], refs[n], refs[n + 1:2 * n + 1], refs[2 * n + 1]
        send_sems, recv_sems, local_sems = refs[2 * n + 2:]
        x, y, c = lax.axis_index("x"), lax.axis_index("y"), lax.axis_index("c")
        me, sibling = (x, y, c), (x, y, 1 - c)
        x_nbr, y_nbr, diag = (1 - x, y, c), (x, 1 - y, c), (1 - x, 1 - y, c)

        def slot(pos):
            return 4 * pos[0] + 2 * pos[1] + pos[2]

        def other_core(pos):
            return (pos[0], pos[1], 1 - pos[2])

        def copy(kind, t, block, to, own=False, half=None):
            dst = outs[t].at[slot(block)]
            if half is not None:
                rows = arrays[t].shape[0] // 2
                dst = dst.at[pl.ds(half * rows, rows)]
            return pltpu.make_async_remote_copy(
                src_ref=ins[t] if own else dst, dst_ref=dst,
                send_sem=send_sems.at[kind, t], recv_sem=recv_sems.at[kind, t],
                device_id=to, device_id_type=pl.DeviceIdType.MESH)

        local = [pltpu.make_async_copy(ins[t], outs[t].at[slot(me)], local_sems.at[t]) for t in range(n)]
        sent = [copy(kind, t, me, to, own=True)
                for t in range(n) for kind, to in ((own_x, x_nbr), (own_y, y_nbr), (own_sib, sibling))]
        for cp in local + sent:
            cp.start()
        _fill_bias_table(rb_ref, tab_ref)

        def start(cp):
            cp.start()
            sent.append(cp)

        for t in range(n):
            copy(own_x, t, x_nbr, me).wait_recv()
            start(copy(half_via_x, t, x_nbr, y_nbr, half=0))
            start(copy(x_sib, t, x_nbr, sibling))
        for t in range(n):
            copy(own_y, t, y_nbr, me).wait_recv()
            start(copy(half_via_y, t, y_nbr, x_nbr, half=1))
            start(copy(y_sib, t, y_nbr, sibling))
        for t in range(n):
            copy(half_via_x, t, diag, me, half=0).wait_recv()
            copy(half_via_y, t, diag, me, half=1).wait_recv()
            start(copy(diag_sib, t, diag, sibling))
        for t in range(n):
            for kind, block in ((own_sib, sibling), (x_sib, other_core(x_nbr)), (y_sib, other_core(y_nbr)),
                                (diag_sib, other_core(diag))):
                copy(kind, t, block, me).wait_recv()
        for cp in sent:
            cp.wait_send()
        for cp in local:
            cp.wait()

    any_spec = pl.BlockSpec(memory_space=pl.ANY)
    vmem_spec = pl.BlockSpec(memory_space=pltpu.VMEM)
    return pl.pallas_call(
        body, name=name, out_shape=out_shape + [jax.ShapeDtypeStruct((N_HEADS, TM, TKW), F32)],
        in_specs=[any_spec] * n + [vmem_spec], out_specs=[any_spec] * n + [vmem_spec],
        scratch_shapes=[pltpu.SemaphoreType.DMA((8, n)), pltpu.SemaphoreType.DMA((8, n)), pltpu.SemaphoreType.DMA((n,))],
        compiler_params=pltpu.CompilerParams(has_side_effects=True, vmem_limit_bytes=VMEM_LIMIT),
    )(*arrays, rel_bias_padded)


def _inv_count(row, window):
    return 1.0 / jnp.minimum(row + 1, window).astype(F32)


def _pool_fwd(x, g_pre, g_post, w_in, w_group, scale, w_out, exchange_items):
    n_tok = x.shape[0]
    nt = n_tok // TM
    ex = _Exchange(exchange_items)

    def body(x_ref, gpre_ref, gpost_ref, win_ref, wg_ref, sc_ref, wout_ref, *rest):
        ex_in, rest = rest[:ex.n], rest[ex.n:]
        x1_ref, y_ref, z_ref, mixed_ref, mg_ref, prod_ref, ht_ref = rest[:7]
        ex_out, carry_ref, ex_sems = rest[7:7 + ex.n], rest[7 + ex.n], rest[8 + ex.n:]
        i = pl.program_id(0)

        @pl.when(i == 0)
        def _():
            ex.start(ex_in, ex_out, ex_sems)
            carry_ref[...] = jnp.zeros_like(carry_ref)

        xv = x_ref[...]
        r, xhat = _rms_fwd(xv)
        hf = xhat * gpre_ref[...]
        h = hf.astype(MXU)
        ht_ref[...] = hf.T.astype(MXU)
        row = i * TM + lax.broadcasted_iota(jnp.int32, (TM, 1), 0)
        y = None
        for g in range(N_GROUPS):
            cols = slice(g * GROUP, (g + 1) * GROUP)
            a = _dot(h, win_ref[g])
            z = _dot(h, win_ref[N_GROUPS + g])
            s = jnp.concatenate([carry_ref[g], a], axis=0)
            carry_ref[g] = a[TM - HALO:, :]
            w = 1
            while w < POOL_WINDOWS[g]:
                s = s + pltpu.roll(s, w, 0)
                w *= 2
            mixed = (s[HALO:, :] * _inv_count(row, POOL_WINDOWS[g]) - a).astype(MXU)
            mg = _dot(mixed, wg_ref[g])
            prod = (mg * sc_ref[:, cols] * (z * _sigmoid(z))).astype(MXU)
            z_ref[:, cols] = z
            mixed_ref[:, cols] = mixed
            mg_ref[:, cols] = mg
            prod_ref[:, cols] = prod
            part = _dot(prod, wout_ref[cols, :])
            y = part if y is None else y + part
        y_ref[...] = y
        _, yhat = _rms_fwd(y)
        x1_ref[...] = xv + yhat * gpost_ref[...]

        @pl.when(i == nt - 1)
        def _():
            ex.wait(ex_in, ex_out, ex_sems)

    tok = lambda w: pl.BlockSpec((TM, w), lambda i: (i, 0))
    return pl.pallas_call(
        body, name="pool_fwd", grid=(nt,),
        in_specs=[tok(D_MODEL), _const_spec((1, D_MODEL)), _const_spec((1, D_MODEL)),
                  _const_spec((N_DEV, D_MODEL, W_BLOCK)), _const_spec((N_GROUPS, GROUP, GROUP)),
                  _const_spec((1, POOL_WIDTH)), _const_spec((POOL_WIDTH, D_MODEL))] + ex.specs,
        out_specs=[tok(D_MODEL), tok(D_MODEL), tok(POOL_WIDTH), tok(POOL_WIDTH), tok(POOL_WIDTH), tok(POOL_WIDTH),
                   pl.BlockSpec((D_MODEL, TM), lambda i: (0, i))] + ex.specs,
        out_shape=[jax.ShapeDtypeStruct((n_tok, D_MODEL), F32), jax.ShapeDtypeStruct((n_tok, D_MODEL), F32),
                   jax.ShapeDtypeStruct((n_tok, POOL_WIDTH), F32), jax.ShapeDtypeStruct((n_tok, POOL_WIDTH), MXU),
                   jax.ShapeDtypeStruct((n_tok, POOL_WIDTH), F32), jax.ShapeDtypeStruct((n_tok, POOL_WIDTH), MXU),
                   jax.ShapeDtypeStruct((D_MODEL, n_tok), MXU)] + ex.out_shape,
        scratch_shapes=[pltpu.VMEM((N_GROUPS, HALO, GROUP), F32)] + ex.scratch,
        compiler_params=_params(1),
    )(x, g_pre, g_post, w_in, w_group, scale, w_out, *ex.arrays)


def _flush(acc_ref, out_hbm, stage_ref):
    for j in range(acc_ref.shape[0]):
        stage_ref[...] = acc_ref[j].astype(stage_ref.dtype)
        pltpu.sync_copy(stage_ref, out_hbm.at[j])


def _pool_bwd(dx1, y, z, mg, mixed, prod, g_post, scale, w_group, w_out, exchange_items):
    n_tok = dx1.shape[0]
    nt = n_tok // TM
    ex = _Exchange(exchange_items)

    def body(dx1_ref, y_ref, z_ref, mg_ref, mixed_ref, prod_ref, gpost_ref, sc_ref, wg_ref, wout_ref, *rest):
        ex_in, rest = rest[:ex.n], rest[ex.n:]
        du_ref, dsc_ref, dgpost_ref, dwg_hbm, dwout_hbm = rest[:5]
        ex_out, rest = rest[5:5 + ex.n], rest[5 + ex.n:]
        carry_ref, dwg_acc, dwout_acc, stage_g, stage_o = rest[:5]
        ex_sems = rest[5:]
        i = pl.program_id(0)

        @pl.when(i == 0)
        def _():
            ex.start(ex_in, ex_out, ex_sems)
            carry_ref[...] = jnp.zeros_like(carry_ref)
            dwg_acc[...] = jnp.zeros_like(dwg_acc)
            dwout_acc[...] = jnp.zeros_like(dwout_acc)
            dsc_ref[...] = jnp.zeros_like(dsc_ref)
            dgpost_ref[...] = jnp.zeros_like(dgpost_ref)

        dn = dx1_ref[...]
        r, yhat = _rms_fwd(y_ref[...])
        dgpost_ref[...] += jnp.sum(dn * yhat, axis=0, keepdims=True)
        dy = _rms_bwd(dn, yhat, r, gpost_ref[...]).astype(MXU)
        row = (nt - 1 - i) * TM + lax.broadcasted_iota(jnp.int32, (TM, 1), 0)
        n_ext = TM + HALO
        for g in range(N_GROUPS):
            cols = slice(g * GROUP, (g + 1) * GROUP)
            dwout_acc[g] += _dot_tn(prod_ref[:, cols], dy)
            dprod = _dot_nt(dy, wout_ref[cols, :])
            zv = z_ref[:, cols]
            sig = _sigmoid(zv)
            silu = zv * sig
            mgv = mg_ref[:, cols]
            sc = sc_ref[:, cols]
            dsc_ref[:, cols] += jnp.sum(dprod * silu * mgv, axis=0, keepdims=True)
            dmg = (dprod * silu * sc).astype(MXU)
            dz = dprod * (mgv * sc) * (sig * (1.0 + zv * (1.0 - sig)))
            dwg_acc[g] += _dot_tn(mixed_ref[:, cols], dmg)
            dmixed = _dot_nt(dmg, wg_ref[g])
            e = dmixed * _inv_count(row, POOL_WINDOWS[g])
            s = jnp.concatenate([e, carry_ref[g]], axis=0)
            carry_ref[g] = e[:HALO, :]
            w = 1
            while w < POOL_WINDOWS[g]:
                s = s + pltpu.roll(s, n_ext - w, 0)
                w *= 2
            du_ref[:, cols] = (s[:TM, :] - dmixed).astype(MXU)
            du_ref[:, POOL_WIDTH + g * GROUP:POOL_WIDTH + (g + 1) * GROUP] = dz.astype(MXU)

        @pl.when(i == nt - 1)
        def _():
            _flush(dwg_acc, dwg_hbm, stage_g)
            _flush(dwout_acc, dwout_hbm, stage_o)
            ex.wait(ex_in, ex_out, ex_sems)

    rev = lambda w: pl.BlockSpec((TM, w), lambda i: (nt - 1 - i, 0))
    any_spec = pl.BlockSpec(memory_space=pl.ANY)
    return pl.pallas_call(
        body, name="pool_bwd", grid=(nt,),
        in_specs=[rev(D_MODEL), rev(D_MODEL), rev(POOL_WIDTH), rev(POOL_WIDTH), rev(POOL_WIDTH), rev(POOL_WIDTH),
                  _const_spec((1, D_MODEL)), _const_spec((1, POOL_WIDTH)),
                  _const_spec((N_GROUPS, GROUP, GROUP)), _const_spec((POOL_WIDTH, D_MODEL))] + ex.specs,
        out_specs=[rev(2 * POOL_WIDTH), pl.BlockSpec((1, POOL_WIDTH), lambda i: (0, 0)),
                   pl.BlockSpec((1, D_MODEL), lambda i: (0, 0)), any_spec, any_spec] + ex.specs,
        out_shape=[jax.ShapeDtypeStruct((n_tok, 2 * POOL_WIDTH), MXU), jax.ShapeDtypeStruct((1, POOL_WIDTH), F32),
                   jax.ShapeDtypeStruct((1, D_MODEL), F32), jax.ShapeDtypeStruct((N_GROUPS, GROUP, GROUP), MXU),
                   jax.ShapeDtypeStruct((N_GROUPS, GROUP, D_MODEL), MXU)] + ex.out_shape,
        scratch_shapes=[pltpu.VMEM((N_GROUPS, HALO, GROUP), F32), pltpu.VMEM((N_GROUPS, GROUP, GROUP), F32),
                        pltpu.VMEM((N_GROUPS, GROUP, D_MODEL), F32), pltpu.VMEM((GROUP, GROUP), MXU),
                        pltpu.VMEM((GROUP, D_MODEL), MXU)] + ex.scratch,
        compiler_params=_params(1),
    )(dx1, y, z, mg, mixed, prod, g_post, scale, w_group, w_out, *ex.arrays)


def _in_proj_bwd(parts, x, dres, g_pre, w_in, name, with_dw, exchange_items):
    n_tok = x.shape[0]
    nt = n_tok // TM
    half = D_MODEL // W_BLOCK
    ex = _Exchange(exchange_items)
    n_dw = 1 if with_dw else 0

    def body(p0, p1, p2, p3, x_ref, dres_ref, g_ref, w_ref, *rest):
        ex_in, rest = rest[:ex.n], rest[ex.n:]
        dx_ref, dg_ref = rest[:2]
        dw_hbm = rest[2:2 + n_dw]
        ex_out, rest = rest[2 + n_dw:2 + n_dw + ex.n], rest[2 + n_dw + ex.n:]
        dw_scratch, ex_sems = rest[:2 * n_dw], rest[2 * n_dw:]
        i = pl.program_id(0)

        @pl.when(i == 0)
        def _():
            ex.start(ex_in, ex_out, ex_sems)
            dg_ref[...] = jnp.zeros_like(dg_ref)
            if with_dw:
                dw_scratch[0][...] = jnp.zeros_like(dw_scratch[0])

        r, xhat = _rms_fwd(x_ref[...])
        g = g_ref[...]
        h = (xhat * g).astype(MXU)
        dh = None
        for p, part_ref in enumerate((p0, p1, p2, p3)):
            for jj in range(half):
                j = half * p + jj
                if len(part_ref.shape) == 3:
                    per_block = W_BLOCK // PAIR
                    du = jnp.concatenate([part_ref[jj * per_block + pp] for pp in range(per_block)], axis=1)
                else:
                    du = part_ref[:, jj * W_BLOCK:(jj + 1) * W_BLOCK]
                t = _dot_nt(du, w_ref[j])
                dh = t if dh is None else dh + t
                if with_dw:
                    dw_scratch[0][j] += _dot_tn(h, du)
        dg_ref[...] += jnp.sum(dh * xhat, axis=0, keepdims=True)
        dx_ref[...] = dres_ref[...] + _rms_bwd(dh, xhat, r, g)

        @pl.when(i == nt - 1)
        def _():
            if with_dw:
                _flush(dw_scratch[0], dw_hbm[0], dw_scratch[1])
            ex.wait(ex_in, ex_out, ex_sems)

    tok = pl.BlockSpec((TM, D_MODEL), lambda i: (i, 0))
    return pl.pallas_call(
        body, name=name, grid=(nt,),
        in_specs=[pl.BlockSpec(shape, m) for _, shape, m in parts]
        + [tok, tok, _const_spec((1, D_MODEL)), _const_spec((N_DEV, D_MODEL, W_BLOCK))] + ex.specs,
        out_specs=[tok, pl.BlockSpec((1, D_MODEL), lambda i: (0, 0))]
        + [pl.BlockSpec(memory_space=pl.ANY)] * n_dw + ex.specs,
        out_shape=[jax.ShapeDtypeStruct((n_tok, D_MODEL), F32), jax.ShapeDtypeStruct((1, D_MODEL), F32)]
        + [jax.ShapeDtypeStruct((N_DEV, D_MODEL, W_BLOCK), MXU)] * n_dw + ex.out_shape,
        scratch_shapes=[pltpu.VMEM((N_DEV, D_MODEL, W_BLOCK), F32), pltpu.VMEM((D_MODEL, W_BLOCK), MXU)][:2 * n_dw]
        + ex.scratch,
        compiler_params=_params(1),
    )(*[a for a, _, _ in parts], x, dres, g_pre, w_in, *ex.arrays)


def _flip(x, y, c, bits):
    return (1 - x if bits & 4 else x, 1 - y if bits & 2 else y, 1 - c if bits & 1 else c)


def _tail_peer(x, y, c, step, receive):
    south, north = TAIL_PEER_BITS[0][step], TAIL_PEER_BITS[1][step]
    if receive and south & 1:
        south, north = north, south
    if south == north:
        return _flip(x, y, c, south)
    return tuple(jnp.where(c == 0, u, v) for u, v in zip(_flip(x, y, c, south), _flip(x, y, c, north)))


def _w_in_grad_scatter(h_t, du, exchange_items, name):
    n_tok = du.shape[0]
    ni = n_tok // TMB
    ex = _Exchange(exchange_items)
    c_out = lax.axis_index("c")
    me_out = 4 * lax.axis_index("x") + 2 * lax.axis_index("y") + c_out
    order = jnp.where(c_out == 0, me_out ^ jnp.array(TAIL_PEER_BITS[0], jnp.int32),
                      me_out ^ jnp.array(TAIL_PEER_BITS[1], jnp.int32)).astype(jnp.int32)

    def body(order_ref, h_ref, du_ref, *rest):
        ex_in, rest = rest[:ex.n], rest[ex.n:]
        part_hbm, ex_out, rest = rest[0], rest[1:1 + ex.n], rest[1 + ex.n:]
        acc_ref, stage_ref, send_sems, recv_sems = rest[:4]
        ex_sems = rest[4:]
        s = pl.program_id(0)
        i = pl.program_id(1)
        x, y, c = lax.axis_index("x"), lax.axis_index("y"), lax.axis_index("c")
        me = 4 * x + 2 * y + c

        def block_copy(step, dst_slot, to):
            return pltpu.make_async_remote_copy(
                src_ref=stage_ref.at[step], dst_ref=part_hbm.at[dst_slot],
                send_sem=send_sems.at[step], recv_sem=recv_sems.at[step],
                device_id=to, device_id_type=pl.DeviceIdType.MESH)

        @pl.when((s == 0) & (i == 0))
        def _():
            ex.start(ex_in, ex_out, ex_sems)

        @pl.when(i == 0)
        def _():
            acc_ref[...] = jnp.zeros_like(acc_ref)

        acc_ref[...] += _dot(h_ref[:, pl.ds(pl.multiple_of(i * TMB, TMB), TMB)], du_ref[...])

        @pl.when(i == ni - 1)
        def _():
            stage_ref[s] = acc_ref[...].astype(MXU)
            for step in range(N_DEV - 1):
                @pl.when(s == step)
                def _(step=step):
                    block_copy(step, me, _tail_peer(x, y, c, step, False)).start()

            @pl.when(s == N_DEV - 1)
            def _():
                pltpu.sync_copy(stage_ref.at[s], part_hbm.at[me])
                for step in range(N_DEV - 1):
                    sx, sy, sc = _tail_peer(x, y, c, step, True)
                    block_copy(step, 4 * sx + 2 * sy + sc, (x, y, c)).wait_recv()
                    block_copy(step, me, (x, y, c)).wait_send()
                ex.wait(ex_in, ex_out, ex_sems)

    grid_spec = pltpu.PrefetchScalarGridSpec(
        num_scalar_prefetch=1, grid=(N_DEV, ni),
        in_specs=[pl.BlockSpec((D_MODEL, n_tok), lambda s, i, order: (0, 0), pipeline_mode=pl.Buffered(1)),
                  pl.BlockSpec((TMB, W_BLOCK), lambda s, i, order: (i, order[s]))] + ex.specs,
        out_specs=[pl.BlockSpec(memory_space=pl.ANY)] + ex.specs,
        scratch_shapes=[pltpu.VMEM((D_MODEL, W_BLOCK), F32), pltpu.VMEM((N_DEV, D_MODEL, W_BLOCK), MXU),
                        pltpu.SemaphoreType.DMA((N_DEV - 1,)), pltpu.SemaphoreType.DMA((N_DEV - 1,))] + ex.scratch)
    return pl.pallas_call(
        body, name=name, grid_spec=grid_spec,
        out_shape=[jax.ShapeDtypeStruct((N_DEV, D_MODEL, W_BLOCK), MXU)] + ex.out_shape,
        compiler_params=_params(2),
    )(order, h_t, du, *ex.arrays)


def _rel_onehot():
    rel = lax.broadcasted_iota(jnp.int32, (REL_PAD, ROLL_W), 0)
    col = lax.broadcasted_iota(jnp.int32, (REL_PAD, ROLL_W), 1)
    return (rel == jnp.minimum(BAND + MAX_REL - col, 2 * MAX_REL)).astype(MXU)


def _split3(v):
    hi = v.astype(MXU)
    r1 = v - hi.astype(F32)
    mid = r1.astype(MXU)
    lo = (r1 - mid.astype(F32)).astype(MXU)
    return hi, mid, lo


def _fill_bias_table(rb_ref, out_ref):
    onehot = _rel_onehot()
    base = None
    for term in _split3(rb_ref[...]):
        t = _dot(term, onehot)
        base = t if base is None else base + t
    qi = lax.broadcasted_iota(jnp.int32, (CHUNK, ROLL_W), 0)
    kk = lax.broadcasted_iota(jnp.int32, (CHUNK, TKW), 1)
    for h in range(N_HEADS):
        t = jnp.broadcast_to(base[h:h + 1, :], (CHUNK, ROLL_W))
        for bit in range(6):
            t = jnp.where(((qi >> bit) & 1) == 1, pltpu.roll(t, 1 << bit, 1), t)
        for rr in range(TM // CHUNK):
            shifted = pltpu.roll(t, (CHUNK * rr - CHUNK) % ROLL_W, 1)[:, :TKW]
            band = kk - CHUNK * rr
            out_ref[h, rr * CHUNK:(rr + 1) * CHUNK, :] = jnp.where((band >= 0) & (band < BAND), shifted, NEG)


def _bias_grad(dtab):
    def body(dt_ref, out_ref, dbase_ref):
        qi = lax.broadcasted_iota(jnp.int32, (CHUNK, ROLL_W), 0)
        zeros = jnp.zeros((CHUNK, ROLL_W - TKW), F32)
        for h in range(N_HEADS):
            t = None
            for rr in range(TM // CHUNK):
                blk = jnp.concatenate([dt_ref[h, rr * CHUNK:(rr + 1) * CHUNK, :], zeros], axis=1)
                blk = pltpu.roll(blk, (CHUNK - CHUNK * rr) % ROLL_W, 1)
                t = blk if t is None else t + blk
            for bit in range(6):
                t = jnp.where(((qi >> bit) & 1) == 1, pltpu.roll(t, ROLL_W - (1 << bit), 1), t)
            dbase_ref[h:h + 1, :] = jnp.sum(t, axis=0, keepdims=True)
        onehot = _rel_onehot()
        acc = None
        for term in _split3(dbase_ref[...]):
            t = _dot_nt(term, onehot)
            acc = t if acc is None else acc + t
        out_ref[...] = acc

    return pl.pallas_call(
        body, name="bias_grad", out_shape=jax.ShapeDtypeStruct((N_HEADS, REL_PAD), F32),
        scratch_shapes=[pltpu.VMEM((N_HEADS, ROLL_W), F32)],
        compiler_params=pltpu.CompilerParams(vmem_limit_bytes=VMEM_LIMIT),
    )(dtab)


def _att_in(x1, g_pre, w_in):
    n_tok = x1.shape[0]
    nt = n_tok // TM
    lead = PAD // TM
    per_block = W_BLOCK // PAIR

    def body(x_ref, g_ref, w_ref, q_ref, k_ref, v_ref, z_ref):
        i = pl.program_id(0)

        @pl.when(i < lead)
        def _():
            k_ref[...] = jnp.zeros_like(k_ref)
            v_ref[...] = jnp.zeros_like(v_ref)

        @pl.when(i >= lead)
        def _():
            _, xhat = _rms_fwd(x_ref[...])
            h = (xhat * g_ref[...]).astype(MXU)
            for j in range(N_DEV):
                u = _dot(h, w_ref[j])
                if j >= 6:
                    z_ref[:, (j % 2) * W_BLOCK:(j % 2 + 1) * W_BLOCK] = u
                    continue
                dst = (q_ref, k_ref, v_ref)[j // 2]
                if j < 2:
                    u = u * QK_SCALE
                for pp in range(per_block):
                    dst[(j % 2) * per_block + pp] = u[:, pp * PAIR:(pp + 1) * PAIR].astype(MXU)

    late = pl.BlockSpec((TM, D_MODEL), lambda i: (jnp.maximum(i - lead, 0), 0))
    late3 = pl.BlockSpec((N_PAIRS, TM, PAIR), lambda i: (0, jnp.maximum(i - lead, 0), 0))
    padded3 = pl.BlockSpec((N_PAIRS, TM, PAIR), lambda i: (0, i, 0))
    return pl.pallas_call(
        body, name="att_in", grid=(nt + lead,),
        in_specs=[late, _const_spec((1, D_MODEL)), _const_spec((N_DEV, D_MODEL, W_BLOCK))],
        out_specs=[late3, padded3, padded3, late],
        out_shape=[jax.ShapeDtypeStruct((N_PAIRS, n_tok, PAIR), MXU),
                   jax.ShapeDtypeStruct((N_PAIRS, n_tok + PAD, PAIR), MXU),
                   jax.ShapeDtypeStruct((N_PAIRS, n_tok + PAD, PAIR), MXU),
                   jax.ShapeDtypeStruct((n_tok, ATT_WIDTH), F32)],
        compiler_params=_params(1),
    )(x1, g_pre, w_in)


def _head_masks():
    lane = lax.broadcasted_iota(jnp.int32, (1, PAIR), 1)
    first = lane < HEAD_DIM
    return first, jnp.logical_not(first)


def _pair_loop(one_pair, unroll):
    def loop_pass(t, carry):
        for u in range(unroll):
            one_pair(t * unroll + u)
        return carry

    lax.fori_loop(0, PAIRS_PER_STEP // unroll, loop_pass, 0)


def _scores(qh, k_refs, bias_ref, p, hh, tile, masked):
    ss = []
    for b in range(KB):
        s = _dot_nt(qh, k_refs[b][p]) + bias_ref[2 * p + hh, :, b * TM:(b + 1) * TM]
        if masked:
            s = s + jnp.where(tile + b < PAD // TM, NEG, 0.0).astype(F32)
        ss.append(s)
    return ss


def _pair_specs(index_map):
    return pl.BlockSpec((PAIRS_PER_STEP, TM, PAIR), index_map)


def _att_fwd(q, kpad, vpad, bias_tab):
    n_tok = q.shape[1]
    nt = n_tok // TM
    lead = PAD // TM

    def body(q_ref, k0, k1, k2, v0, v1, v2, bias_ref, o_ref, lse_ref):
        i = pl.program_id(1)
        masks = _head_masks()

        def pairs(masked):
            def one_pair(p):
                qv = q_ref[p]
                outs, ms = [], []
                for hh, mask in enumerate(masks):
                    qh = jnp.where(mask, qv, jnp.zeros_like(qv))
                    ss = _scores(qh, (k0, k1, k2), bias_ref, p, hh, i, masked)
                    m = None
                    for s in ss:
                        mb = jnp.max(s, axis=-1, keepdims=True)
                        m = mb if m is None else jnp.maximum(m, mb)
                    out = None
                    for b, v_ref in enumerate((v0, v1, v2)):
                        vb = v_ref[p]
                        t = _dot(jnp.exp(ss[b] - m).astype(MXU), jnp.where(mask, vb, jnp.ones_like(vb)))
                        out = t if out is None else out + t
                    outs.append(out)
                    ms.append(m)
                num = jnp.where(masks[0], outs[0], outs[1])
                den = jnp.where(masks[0], pltpu.roll(outs[0], HEAD_DIM, 1), pltpu.roll(outs[1], HEAD_DIM, 1))
                o_ref[p] = num / den
                lse_ref[p] = jnp.where(masks[0], ms[0], ms[1]) + jnp.log(den)

            _pair_loop(one_pair, FWD_UNROLL)

        @pl.when(i < lead)
        def _():
            pairs(True)

        @pl.when(i >= lead)
        def _():
            pairs(False)

    qspec = _pair_specs(lambda g, i: (g, i, 0))
    kspecs = [_pair_specs(functools.partial(lambda g, i, b: (g, i + b, 0), b=b)) for b in range(KB)]
    vspecs = [_pair_specs(functools.partial(lambda g, i, b: (g, i + b, 0), b=b)) for b in range(KB)]
    shape = jax.ShapeDtypeStruct((N_PAIRS, n_tok, PAIR), F32)
    return pl.pallas_call(
        body, name="att_fwd", grid=(N_PAIRS // PAIRS_PER_STEP, nt),
        in_specs=[qspec] + kspecs + vspecs + [pl.BlockSpec((2 * PAIRS_PER_STEP, TM, TKW), lambda g, i: (g, 0, 0))],
        out_specs=[qspec, qspec], out_shape=[shape, shape],
        compiler_params=_params(2),
    )(q, kpad, kpad, kpad, vpad, vpad, vpad, bias_tab)


def _att_bwd(q, kpad, vpad, do, o, lse, bias_tab):
    n_tok = q.shape[1]
    nt = n_tok // TM
    lead = PAD // TM

    def body(q_ref, do_ref, o_ref, lse_ref, k0, k1, k2, v0, v1, v2, bias_ref,
             dq_ref, dk_ref, dv_ref, dtab_ref, rk0, rk1, rv0, rv1):
        i = pl.program_id(1)
        masks = _head_masks()

        @pl.when(i == 0)
        def _():
            for ref in (rk0, rk1, rv0, rv1):
                ref[...] = jnp.zeros_like(ref)
            dtab_ref[...] = jnp.zeros_like(dtab_ref)

        def pairs(masked):
            def one_pair(p):
                qv = q_ref[p]
                dov = do_ref[p]
                doo = dov.astype(F32) * o_ref[p]
                lse_pair = lse_ref[p]
                dks = [None] * KB
                dvs = [None] * KB
                dqs = []
                for hh, mask in enumerate(masks):
                    qh = jnp.where(mask, qv, jnp.zeros_like(qv))
                    doh = jnp.where(mask, dov, jnp.zeros_like(dov))
                    dsum = jnp.sum(jnp.where(mask, doo, 0.0), axis=-1, keepdims=True)
                    lse_h = lse_pair[:, hh * HEAD_DIM:hh * HEAD_DIM + 1]
                    ss = _scores(qh, (k0, k1, k2), bias_ref, p, hh, i, masked)
                    dq = None
                    for b, (k_ref, v_ref) in enumerate(zip((k0, k1, k2), (v0, v1, v2))):
                        prob = jnp.exp(ss[b] - lse_h)
                        ds = prob * (_dot_nt(doh, v_ref[p]) - dsum)
                        dtab_ref[2 * p + hh, :, b * TM:(b + 1) * TM] += ds
                        dsb = ds.astype(MXU)
                        t = lax.dot_general(k_ref[p], dsb, (((0,), (1,)), ((), ())), preferred_element_type=F32)
                        dq = t if dq is None else dq + t
                        t = _dot_tn(qh, dsb)
                        dks[b] = t if dks[b] is None else dks[b] + t
                        t = _dot_tn(doh, prob.astype(MXU))
                        dvs[b] = t if dvs[b] is None else dvs[b] + t
                    dqs.append(dq)
                first_rows = lax.broadcasted_iota(jnp.int32, (PAIR, 1), 0) < HEAD_DIM
                dq_ref[p] = (jnp.where(first_rows, dqs[0], dqs[1]).T * QK_SCALE).astype(MXU)
                dk_ref[p] = (rk0[p] + dks[0].T).astype(MXU)
                dv_ref[p] = (rv0[p] + dvs[0].T).astype(MXU)
                rk0[p] = rk1[p] + dks[1].T
                rv0[p] = rv1[p] + dvs[1].T
                rk1[p] = dks[2].T
                rv1[p] = dvs[2].T

            _pair_loop(one_pair, BWD_UNROLL)

        @pl.when(i < lead)
        def _():
            pairs(True)

        @pl.when((i >= lead) & (i < nt))
        def _():
            pairs(False)

        @pl.when(i >= nt)
        def _():
            dk_ref[...] = rk0[...].astype(MXU)
            dv_ref[...] = rv0[...].astype(MXU)
            rk0[...] = rk1[...]
            rv0[...] = rv1[...]

    last = nt - 1
    qspec = _pair_specs(lambda g, i: (g, jnp.minimum(i, last), 0))
    kspecs = [_pair_specs(functools.partial(lambda g, i, b: (g, jnp.minimum(i, last) + b, 0), b=b)) for b in range(KB)]
    vspecs = [_pair_specs(functools.partial(lambda g, i, b: (g, jnp.minimum(i, last) + b, 0), b=b)) for b in range(KB)]
    pspec = _pair_specs(lambda g, i: (g, i, 0))
    tspec = pl.BlockSpec((2 * PAIRS_PER_STEP, TM, TKW), lambda g, i: (g, 0, 0))
    ring = pltpu.VMEM((PAIRS_PER_STEP, TM, PAIR), F32)
    return pl.pallas_call(
        body, name="att_bwd", grid=(N_PAIRS // PAIRS_PER_STEP, nt + KB - 1),
        in_specs=[qspec, qspec, qspec, qspec] + kspecs + vspecs + [tspec],
        out_specs=[qspec, pspec, pspec, tspec],
        out_shape=[jax.ShapeDtypeStruct((N_PAIRS, n_tok, PAIR), MXU),
                   jax.ShapeDtypeStruct((N_PAIRS, n_tok + PAD, PAIR), MXU),
                   jax.ShapeDtypeStruct((N_PAIRS, n_tok + PAD, PAIR), MXU),
                   jax.ShapeDtypeStruct((N_HEADS, TM, TKW), F32)],
        scratch_shapes=[ring, ring, ring, ring],
        compiler_params=_params(2),
    )(q, do, o, lse, kpad, kpad, kpad, vpad, vpad, vpad, bias_tab)


def _att_out(o, z, x1, target, g_post, w_out):
    n_tok = z.shape[0]
    nt = n_tok // TM

    def body(o_ref, z_ref, x1_ref, tgt_ref, gpost_ref, w_ref,
             loss_ref, dx2_ref, do_ref, dz_ref, dgpost_ref, dw_hbm, acc_ref, loss_acc, stage_ref):
        i = pl.program_id(0)

        @pl.when(i == 0)
        def _():
            acc_ref[...] = jnp.zeros_like(acc_ref)
            loss_acc[...] = jnp.zeros_like(loss_acc)
            dgpost_ref[...] = jnp.zeros_like(dgpost_ref)

        ov = jnp.concatenate([o_ref[p] for p in range(N_PAIRS)], axis=1)
        zv = z_ref[...]
        sig = _sigmoid(zv)
        silu = zv * sig
        gated = (ov * silu).astype(MXU)
        y = _dot(gated, w_ref[...])
        r, yhat = _rms_fwd(y)
        gpost = gpost_ref[...]
        diff = x1_ref[...] + yhat * gpost - tgt_ref[...]
        loss_acc[...] += jnp.sum(diff * diff, axis=0, keepdims=True)
        dn = diff * (1.0 / D_MODEL)
        dx2_ref[...] = dn
        dgpost_ref[...] += jnp.sum(dn * yhat, axis=0, keepdims=True)
        dy = _rms_bwd(dn, yhat, r, gpost).astype(MXU)
        for j in range(ATT_WIDTH // W_BLOCK):
            acc_ref[j] += _dot_tn(gated[:, j * W_BLOCK:(j + 1) * W_BLOCK], dy)
        dgated = _dot_nt(dy, w_ref[...])
        dob = (dgated * silu).astype(MXU)
        for p in range(N_PAIRS):
            do_ref[p] = dob[:, p * PAIR:(p + 1) * PAIR]
        dz_ref[...] = (dgated * ov * (sig * (1.0 + zv * (1.0 - sig)))).astype(MXU)

        @pl.when(i == nt - 1)
        def _():
            total = jnp.sum(loss_acc[...], axis=-1, keepdims=True) * (0.5 / D_MODEL)
            loss_ref[...] = jnp.broadcast_to(total, loss_ref.shape)
            _flush(acc_ref, dw_hbm, stage_ref)

    tok = pl.BlockSpec((TM, D_MODEL), lambda i: (i, 0))
    tok3 = pl.BlockSpec((N_PAIRS, TM, PAIR), lambda i: (0, i, 0))
    return pl.pallas_call(
        body, name="att_out", grid=(nt,),
        in_specs=[tok3, tok, tok, tok, _const_spec((1, D_MODEL)), _const_spec((ATT_WIDTH, D_MODEL))],
        out_specs=[pl.BlockSpec((1, 128), lambda i: (0, 0)), tok, tok3, tok,
                   pl.BlockSpec((1, D_MODEL), lambda i: (0, 0)), pl.BlockSpec(memory_space=pl.ANY)],
        out_shape=[jax.ShapeDtypeStruct((1, 128), F32), jax.ShapeDtypeStruct((n_tok, D_MODEL), F32),
                   jax.ShapeDtypeStruct((N_PAIRS, n_tok, PAIR), MXU), jax.ShapeDtypeStruct((n_tok, ATT_WIDTH), MXU),
                   jax.ShapeDtypeStruct((1, D_MODEL), F32),
                   jax.ShapeDtypeStruct((ATT_WIDTH // W_BLOCK, W_BLOCK, D_MODEL), MXU)],
        scratch_shapes=[pltpu.VMEM((ATT_WIDTH // W_BLOCK, W_BLOCK, D_MODEL), F32), pltpu.VMEM((1, D_MODEL), F32),
                        pltpu.VMEM((W_BLOCK, D_MODEL), MXU)],
        compiler_params=_params(1),
    )(o, z, x1, target, g_post, w_out)


def _adamw(parts, w, m, v, name):
    rows, cols = w.shape
    tr = min(rows, 256)

    def body(p_ref, w_ref, m_ref, v_ref, g_ref, d_ref, mo_ref, vo_ref):
        g = p_ref[0].astype(F32)
        for s in range(1, N_DEV):
            g = g + p_ref[s].astype(F32)
        m_new = ADAM_B1 * m_ref[...] + (1.0 - ADAM_B1) * g
        v_new = ADAM_B2 * v_ref[...] + (1.0 - ADAM_B2) * (g * g)
        m_hat = m_new / (1.0 - ADAM_B1 ** ADAM_STEP)
        v_hat = v_new / (1.0 - ADAM_B2 ** ADAM_STEP)
        g_ref[...] = g
        d_ref[...] = -ADAM_LR * (m_hat / (jnp.sqrt(v_hat) + ADAM_EPS) + ADAM_WD * w_ref[...])
        mo_ref[...] = m_new
        vo_ref[...] = v_new

    blk = pl.BlockSpec((tr, cols), lambda i: (i, 0))
    shape = jax.ShapeDtypeStruct((rows, cols), F32)
    return pl.pallas_call(
        body, name=name, grid=(rows // tr,),
        in_specs=[pl.BlockSpec((N_DEV, tr, cols), lambda i: (0, i, 0)), blk, blk, blk],
        out_specs=[blk, blk, blk, blk], out_shape=[shape, shape, shape, shape],
        compiler_params=_params(1),
    )(parts, w, m, v)


SMALL_ROWS = 16
LOSS_AT = (6, N_REL)


def _pack_small(norm_pre, norm_post, pool_scale, rel_bias_padded):
    return jnp.concatenate([norm_pre, norm_post, pool_scale.reshape(2, D_MODEL),
                            rel_bias_padded.reshape(SMALL_ROWS - 6, D_MODEL)], axis=0)


def _unpack_small(packed):
    rel = packed[6:].reshape(N_HEADS, REL_PAD)[:, :N_REL]
    return packed[0:2], packed[2:4], packed[4:6].reshape(1, POOL_WIDTH), rel.reshape(1, N_HEADS, N_REL)


def _pad_rel(rel_bias):
    return jnp.pad(rel_bias.reshape(N_HEADS, N_REL), ((0, 0), (0, REL_PAD - N_REL)))


def kernel(x, norm_pre, norm_post, pool_w_in, pool_w_group, pool_scale, pool_w_out, att_w_in, att_rel_bias, att_w_out, loss_target, m_norm_pre, m_norm_post, m_pool_w_in, m_pool_w_group, m_pool_scale, m_pool_w_out, m_att_w_in, m_att_rel_bias, m_att_w_out, v_norm_pre, v_norm_post, v_pool_w_in, v_pool_w_group, v_pool_scale, v_pool_w_out, v_att_w_in, v_att_rel_bias, v_att_w_out):
    xt = x[0]
    target = loss_target[0]
    n_tok = xt.shape[0]
    lead = PAD // TM
    rows_g = GROUP // N_DEV

    rel_padded = _pad_rel(att_rel_bias[0])
    gathered = _gather_two_level([pool_w_in[0].astype(MXU), pool_w_group[0].astype(MXU), pool_w_out[0].astype(MXU)],
                                 rel_padded, "gather_pool_weights")
    bias_tab = gathered[3]
    w_in_p = gathered[0]
    w_group = gathered[1].transpose(1, 0, 2, 3).reshape(N_GROUPS, GROUP, GROUP)
    w_out_p = gathered[2].reshape(POOL_WIDTH, D_MODEL)

    x1, y0, z0, mixed, mg, prod, h0_t, w_in_a, w_out_a = _pool_fwd(
        xt, norm_pre[0:1], norm_post[0:1], w_in_p, w_group, pool_scale, w_out_p,
        [(att_w_in[0].astype(MXU), False), (att_w_out[0].astype(MXU), False)])
    w_out_a = w_out_a.reshape(ATT_WIDTH, D_MODEL)
    q, kpad, vpad, z1 = _att_in(x1, norm_pre[1:2], w_in_a)
    o, lse = _att_fwd(q, kpad, vpad, bias_tab)
    loss_part, dx2, do, dz1, d_gpost1, d_w_out_a = _att_out(o, z1, x1, target, norm_post[1:2], w_out_a)
    dq, dkpad, dvpad, dtab = _att_bwd(q, kpad, vpad, do, o, lse, bias_tab)
    d_rel = _bias_grad(dtab)
    pairs = (N_PAIRS, TM, PAIR)
    flat = (TM, D_MODEL)
    dx1, d_gpre1, d_w_in_a = _in_proj_bwd(
        [(dq, pairs, lambda i: (0, i, 0)), (dkpad, pairs, lambda i: (0, i + lead, 0)),
         (dvpad, pairs, lambda i: (0, i + lead, 0)), (dz1, flat, lambda i: (i, 0))],
        x1, dx2, norm_pre[1:2], w_in_a, "att_in_bwd", True, [])
    du0, d_scale, d_gpost0, d_w_group, d_w_out_p, part_w_in_a, part_w_out_a = _pool_bwd(
        dx1, y0, z0, mg, mixed, prod, norm_post[0:1], pool_scale, w_group, w_out_p,
        [(d_w_in_a, True), (d_w_out_a.reshape(N_DEV, ATT_WIDTH // N_DEV, D_MODEL), True)])
    col = lambda p: (lambda i: (i, p))
    grad_x, d_gpre0, part_w_group, part_w_out_p = _in_proj_bwd(
        [(du0, flat, col(p)) for p in range(4)], xt, dx1, norm_pre[0:1], w_in_p, "pool_in_bwd", False,
        [(d_w_group.reshape(N_GROUPS, N_DEV, rows_g, GROUP).transpose(1, 0, 2, 3), True),
         (d_w_out_p.reshape(N_DEV, POOL_WIDTH // N_DEV, D_MODEL), True)])
    d_small = _pack_small(jnp.concatenate([d_gpre0, d_gpre1], axis=0), jnp.concatenate([d_gpost0, d_gpost1], axis=0),
                          d_scale, d_rel).at[LOSS_AT].set(loss_part[0, 0])
    part_w_in_p, part_small = _w_in_grad_scatter(h0_t, du0, [(d_small, False)], "pool_w_in_grad")

    def update(part, w, m, v, name):
        shape = w.shape
        flat = lambda a: a.reshape(-1, shape[-1])
        outs = _adamw(part.reshape(N_DEV, -1, shape[-1]), flat(w), flat(m), flat(v), name)
        return [a.reshape(shape) for a in outs]

    u_att_w_in = update(part_w_in_a, att_w_in, m_att_w_in, v_att_w_in, "adamw_att_w_in")
    u_att_w_out = update(part_w_out_a, att_w_out, m_att_w_out, v_att_w_out, "adamw_att_w_out")
    u_pool_w_group = update(part_w_group, pool_w_group, m_pool_w_group, v_pool_w_group, "adamw_pool_w_group")
    u_pool_w_out = update(part_w_out_p, pool_w_out, m_pool_w_out, v_pool_w_out, "adamw_pool_w_out")
    u_pool_w_in = update(part_w_in_p, pool_w_in, m_pool_w_in, v_pool_w_in, "adamw_pool_w_in")
    small = _adamw(part_small, _pack_small(norm_pre, norm_post, pool_scale, rel_padded),
                   _pack_small(m_norm_pre, m_norm_post, m_pool_scale, _pad_rel(m_att_rel_bias[0])),
                   _pack_small(v_norm_pre, v_norm_post, v_pool_scale, _pad_rel(v_att_rel_bias[0])), "adamw_small")
    u_small = [_unpack_small(a) for a in small]

    loss = small[0][LOSS_AT]
    outs = [loss, grad_x.reshape(1, n_tok, D_MODEL)]
    for kind in range(4):
        outs += [u_small[kind][0], u_small[kind][1], u_pool_w_in[kind], u_pool_w_group[kind], u_small[kind][2],
                 u_pool_w_out[kind], u_att_w_in[kind], u_small[kind][3], u_att_w_out[kind]]
    return tuple(outs)
```

```python
import functools

import jax
import jax.numpy as jnp
from jax import lax
from jax.experimental import pallas as pl
from jax.experimental.pallas import tpu as pltpu

F32 = jnp.float32
MXU = jnp.bfloat16

D_MODEL = 1024
POOL_WIDTH = 2048
POOL_WINDOWS = (2, 4, 8, 16)
N_GROUPS = 4
GROUP = 512
HALO = 16
N_HEADS = 16
HEAD_DIM = 64
CHUNK = 64
LEFT_CHUNKS = 8
PAD = LEFT_CHUNKS * CHUNK
BAND = PAD + CHUNK
MAX_REL = 256
N_REL = 2 * MAX_REL + 1
REL_PAD = 640
ATT_WIDTH = 1024
PAIR = 2 * HEAD_DIM
N_PAIRS = N_HEADS // 2
N_DEV = 8
W_BLOCK = 512
RMS_EPS = 1e-6
QK_SCALE = 0.125
NEG = -1e30

TM = 256
TM_WIDE = 512
TMB = 1024
TAIL_PEER_BITS = ((6, 7, 4, 2, 5, 3, 1, 0), (6, 7, 2, 4, 3, 5, 1, 0))
KB = 3
TKW = KB * TM
PAIRS_PER_STEP = 4
FWD_UNROLL = 4
BWD_UNROLL = 2
ROLL_W = 1024

VMEM_LIMIT = 56 * 1024 * 1024

ADAM_LR = 0.001
ADAM_B1 = 0.9
ADAM_B2 = 0.999
ADAM_EPS = 1e-08
ADAM_WD = 0.01
ADAM_STEP = 10

NT_DIMS = (((1,), (1,)), ((), ()))
TN_DIMS = (((0,), (0,)), ((), ()))


def _params(n_grid):
    return pltpu.CompilerParams(dimension_semantics=("arbitrary",) * n_grid, vmem_limit_bytes=VMEM_LIMIT)


def _const_spec(shape):
    nd = len(shape)
    return pl.BlockSpec(shape, lambda *_: (0,) * nd, pipeline_mode=pl.Buffered(1))


def _dot(a, b):
    return jnp.dot(a, b, preferred_element_type=F32)


def _dot_nt(a, b):
    return lax.dot_general(a, b, NT_DIMS, preferred_element_type=F32)


def _dot_tn(a, b):
    return lax.dot_general(a, b, TN_DIMS, preferred_element_type=F32)


def _sigmoid(z):
    return 1.0 / (1.0 + jnp.exp(-z))


def _rms_fwd(xv):
    r = lax.rsqrt(jnp.mean(xv * xv, axis=-1, keepdims=True) + RMS_EPS)
    return r, xv * r


def _rms_bwd(dn, xhat, r, g):
    dng = dn * g
    return r * (dng - xhat * jnp.mean(dng * xhat, axis=-1, keepdims=True))


class _Exchange:
    def __init__(self, items):
        self.arrays = [a for a, _ in items]
        self.scatter = [s for _, s in items]
        self.n = len(items)
        self.out_shape = [jax.ShapeDtypeStruct((N_DEV,) + tuple(a.shape[1:] if s else a.shape), a.dtype)
                          for a, s in items]
        self.specs = [pl.BlockSpec(memory_space=pl.ANY)] * self.n
        self.scratch = ([pltpu.SemaphoreType.DMA((N_DEV - 1, self.n)), pltpu.SemaphoreType.DMA((N_DEV - 1, self.n)),
                         pltpu.SemaphoreType.DMA((self.n,))] if self.n else [])

    def _copies(self, ins, outs, sems, with_receives):
        send_sems, recv_sems, local_sems = sems
        x, y, c = lax.axis_index("x"), lax.axis_index("y"), lax.axis_index("c")
        me = 4 * x + 2 * y + c

        def src(t, slot):
            return ins[t].at[slot] if self.scatter[t] else ins[t]

        local = [pltpu.make_async_copy(src(t, me), outs[t].at[me], local_sems.at[t]) for t in range(self.n)]
        sends, recvs = [], []
        for k in range(1, N_DEV):
            px = 1 - x if k & 4 else x
            py = 1 - y if k & 2 else y
            pc = 1 - c if k & 1 else c
            peer = 4 * px + 2 * py + pc
            for t in range(self.n):
                common = dict(src_ref=src(t, peer), send_sem=send_sems.at[k - 1, t], recv_sem=recv_sems.at[k - 1, t],
                              device_id=(px, py, pc), device_id_type=pl.DeviceIdType.MESH)
                sends.append(pltpu.make_async_remote_copy(dst_ref=outs[t].at[me], **common))
                if with_receives:
                    recvs.append(pltpu.make_async_remote_copy(dst_ref=outs[t].at[peer], **common))
        return local, sends, recvs

    def start(self, ins, outs, sems):
        if self.n:
            local, sends, _ = self._copies(ins, outs, sems, False)
            for cp in local + sends:
                cp.start()

    def wait(self, ins, outs, sems):
        if self.n:
            local, sends, recvs = self._copies(ins, outs, sems, True)
            for cp in recvs:
                cp.wait_recv()
            for cp in sends:
                cp.wait_send()
            for cp in local:
                cp.wait()


def _exchange(items, name):
    ex = _Exchange(items)
    n = ex.n

    def body(*refs):
        ins, outs, sems = refs[:n], refs[n:2 * n], refs[2 * n:]
        ex.start(ins, outs, sems)
        ex.wait(ins, outs, sems)

    return pl.pallas_call(
        body, name=name, out_shape=ex.out_shape, in_specs=ex.specs, out_specs=ex.specs, scratch_shapes=ex.scratch,
        compiler_params=pltpu.CompilerParams(has_side_effects=True),
    )(*ex.arrays)


def _gather_two_level(arrays, rel_bias_padded, name):
    n = len(arrays)
    out_shape = [jax.ShapeDtypeStruct((N_DEV,) + a.shape, a.dtype) for a in arrays]
    own_sib, own_x, own_y, half_via_x, half_via_y, x_sib, y_sib, diag_sib = range(8)

    def body(*refs):
        ins, rb_ref, outs, tab_ref = refs[:n], refs[n], refs[n + 1:2 * n + 1], refs[2 * n + 1]
        send_sems, recv_sems, local_sems = refs[2 * n + 2:]
        x, y, c = lax.axis_index("x"), lax.axis_index("y"), lax.axis_index("c")
        me, sibling = (x, y, c), (x, y, 1 - c)
        x_nbr, y_nbr, diag = (1 - x, y, c), (x, 1 - y, c), (1 - x, 1 - y, c)

        def slot(pos):
            return 4 * pos[0] + 2 * pos[1] + pos[2]

        def other_core(pos):
            return (pos[0], pos[1], 1 - pos[2])

        def copy(kind, t, block, to, own=False, half=None):
            dst = outs[t].at[slot(block)]
            if half is not None:
                rows = arrays[t].shape[0] // 2
                dst = dst.at[pl.ds(half * rows, rows)]
            return pltpu.make_async_remote_copy(
                src_ref=ins[t] if own else dst, dst_ref=dst,
                send_sem=send_sems.at[kind, t], recv_sem=recv_sems.at[kind, t],
                device_id=to, device_id_type=pl.DeviceIdType.MESH)

        local = [pltpu.make_async_copy(ins[t], outs[t].at[slot(me)], local_sems.at[t]) for t in range(n)]
        sent = [copy(kind, t, me, to, own=True)
                for t in range(n) for kind, to in ((own_x, x_nbr), (own_y, y_nbr), (own_sib, sibling))]
        for cp in local + sent:
            cp.start()
        _fill_bias_table(rb_ref, tab_ref)

        def start(cp):
            cp.start()
            sent.append(cp)

        for t in range(n):
            copy(own_x, t, x_nbr, me).wait_recv()
            start(copy(half_via_x, t, x_nbr, y_nbr, half=0))
            start(copy(x_sib, t, x_nbr, sibling))
        for t in range(n):
            copy(own_y, t, y_nbr, me).wait_recv()
            start(copy(half_via_y, t, y_nbr, x_nbr, half=1))
            start(copy(y_sib, t, y_nbr, sibling))
        for t in range(n):
            copy(half_via_x, t, diag, me, half=0).wait_recv()
            copy(half_via_y, t, diag, me, half=1).wait_recv()
            start(copy(diag_sib, t, diag, sibling))
        for t in range(n):
            for kind, block in ((own_sib, sibling), (x_sib, other_core(x_nbr)), (y_sib, other_core(y_nbr)),
                                (diag_sib, other_core(diag))):
                copy(kind, t, block, me).wait_recv()
        for cp in sent:
            cp.wait_send()
        for cp in local:
            cp.wait()

    any_spec = pl.BlockSpec(memory_space=pl.ANY)
    vmem_spec = pl.BlockSpec(memory_space=pltpu.VMEM)
    return pl.pallas_call(
        body, name=name, out_shape=out_shape + [jax.ShapeDtypeStruct((N_HEADS, TM, TKW), F32)],
        in_specs=[any_spec] * n + [vmem_spec], out_specs=[any_spec] * n + [vmem_spec],
        scratch_shapes=[pltpu.SemaphoreType.DMA((8, n)), pltpu.SemaphoreType.DMA((8, n)), pltpu.SemaphoreType.DMA((n,))],
        compiler_params=pltpu.CompilerParams(has_side_effects=True, vmem_limit_bytes=VMEM_LIMIT),
    )(*arrays, rel_bias_padded)


def _inv_count(row, window):
    return 1.0 / jnp.minimum(row + 1, window).astype(F32)


def _pool_fwd(x, g_pre, g_post, w_in, w_group, scale, w_out, exchange_items):
    n_tok = x.shape[0]
    nt = n_tok // TM
    ex = _Exchange(exchange_items)

    def body(x_ref, gpre_ref, gpost_ref, win_ref, wg_ref, sc_ref, wout_ref, *rest):
        ex_in, rest = rest[:ex.n], rest[ex.n:]
        x1_ref, y_ref, z_ref, mixed_ref, mg_ref, prod_ref, ht_ref = rest[:7]
        ex_out, carry_ref, ex_sems = rest[7:7 + ex.n], rest[7 + ex.n], rest[8 + ex.n:]
        i = pl.program_id(0)

        @pl.when(i == 0)
        def _():
            ex.start(ex_in, ex_out, ex_sems)
            carry_ref[...] = jnp.zeros_like(carry_ref)

        xv = x_ref[...]
        r, xhat = _rms_fwd(xv)
        hf = xhat * gpre_ref[...]
        h = hf.astype(MXU)
        ht_ref[...] = hf.T.astype(MXU)
        row = i * TM + lax.broadcasted_iota(jnp.int32, (TM, 1), 0)
        y = None
        for g in range(N_GROUPS):
            cols = slice(g * GROUP, (g + 1) * GROUP)
            a = _dot(h, win_ref[g])
            z = _dot(h, win_ref[N_GROUPS + g])
            s = jnp.concatenate([carry_ref[g], a], axis=0)
            carry_ref[g] = a[TM - HALO:, :]
            w = 1
            while w < POOL_WINDOWS[g]:
                s = s + pltpu.roll(s, w, 0)
                w *= 2
            mixed = (s[HALO:, :] * _inv_count(row, POOL_WINDOWS[g]) - a).astype(MXU)
            mg = _dot(mixed, wg_ref[g])
            prod = (mg * sc_ref[:, cols] * (z * _sigmoid(z))).astype(MXU)
            z_ref[:, cols] = z
            mixed_ref[:, cols] = mixed
            mg_ref[:, cols] = mg
            prod_ref[:, cols] = prod
            part = _dot(prod, wout_ref[cols, :])
            y = part if y is None else y + part
        y_ref[...] = y
        _, yhat = _rms_fwd(y)
        x1_ref[...] = xv + yhat * gpost_ref[...]

        @pl.when(i == nt - 1)
        def _():
            ex.wait(ex_in, ex_out, ex_sems)

    tok = lambda w: pl.BlockSpec((TM, w), lambda i: (i, 0))
    return pl.pallas_call(
        body, name="pool_fwd", grid=(nt,),
        in_specs=[tok(D_MODEL), _const_spec((1, D_MODEL)), _const_spec((1, D_MODEL)),
                  _const_spec((N_DEV, D_MODEL, W_BLOCK)), _const_spec((N_GROUPS, GROUP, GROUP)),
                  _const_spec((1, POOL_WIDTH)), _const_spec((POOL_WIDTH, D_MODEL))] + ex.specs,
        out_specs=[tok(D_MODEL), tok(D_MODEL), tok(POOL_WIDTH), tok(POOL_WIDTH), tok(POOL_WIDTH), tok(POOL_WIDTH),
                   pl.BlockSpec((D_MODEL, TM), lambda i: (0, i))] + ex.specs,
        out_shape=[jax.ShapeDtypeStruct((n_tok, D_MODEL), F32), jax.ShapeDtypeStruct((n_tok, D_MODEL), F32),
                   jax.ShapeDtypeStruct((n_tok, POOL_WIDTH), F32), jax.ShapeDtypeStruct((n_tok, POOL_WIDTH), MXU),
                   jax.ShapeDtypeStruct((n_tok, POOL_WIDTH), F32), jax.ShapeDtypeStruct((n_tok, POOL_WIDTH), MXU),
                   jax.ShapeDtypeStruct((D_MODEL, n_tok), MXU)] + ex.out_shape,
        scratch_shapes=[pltpu.VMEM((N_GROUPS, HALO, GROUP), F32)] + ex.scratch,
        compiler_params=_params(1),
    )(x, g_pre, g_post, w_in, w_group, scale, w_out, *ex.arrays)


def _flush(acc_ref, out_hbm, stage_ref):
    for j in range(acc_ref.shape[0]):
        stage_ref[...] = acc_ref[j].astype(stage_ref.dtype)
        pltpu.sync_copy(stage_ref, out_hbm.at[j])


def _pool_bwd(dx1, y, z, mg, mixed, prod, g_post, scale, w_group, w_out, exchange_items):
    n_tok = dx1.shape[0]
    nt = n_tok // TM
    ex = _Exchange(exchange_items)

    def body(dx1_ref, y_ref, z_ref, mg_ref, mixed_ref, prod_ref, gpost_ref, sc_ref, wg_ref, wout_ref, *rest):
        ex_in, rest = rest[:ex.n], rest[ex.n:]
        du_ref, dsc_ref, dgpost_ref, dwg_hbm, dwout_hbm = rest[:5]
        ex_out, rest = rest[5:5 + ex.n], rest[5 + ex.n:]
        carry_ref, dwg_acc, dwout_acc, stage_g, stage_o = rest[:5]
        ex_sems = rest[5:]
        i = pl.program_id(0)

        @pl.when(i == 0)
        def _():
            ex.start(ex_in, ex_out, ex_sems)
            carry_ref[...] = jnp.zeros_like(carry_ref)
            dwg_acc[...] = jnp.zeros_like(dwg_acc)
            dwout_acc[...] = jnp.zeros_like(dwout_acc)
            dsc_ref[...] = jnp.zeros_like(dsc_ref)
            dgpost_ref[...] = jnp.zeros_like(dgpost_ref)

        dn = dx1_ref[...]
        r, yhat = _rms_fwd(y_ref[...])
        dgpost_ref[...] += jnp.sum(dn * yhat, axis=0, keepdims=True)
        dy = _rms_bwd(dn, yhat, r, gpost_ref[...]).astype(MXU)
        row = (nt - 1 - i) * TM + lax.broadcasted_iota(jnp.int32, (TM, 1), 0)
        n_ext = TM + HALO
        for g in range(N_GROUPS):
            cols = slice(g * GROUP, (g + 1) * GROUP)
            dwout_acc[g] += _dot_tn(prod_ref[:, cols], dy)
            dprod = _dot_nt(dy, wout_ref[cols, :])
            zv = z_ref[:, cols]
            sig = _sigmoid(zv)
            silu = zv * sig
            mgv = mg_ref[:, cols]
            sc = sc_ref[:, cols]
            dsc_ref[:, cols] += jnp.sum(dprod * silu * mgv, axis=0, keepdims=True)
            dmg = (dprod * silu * sc).astype(MXU)
            dz = dprod * (mgv * sc) * (sig * (1.0 + zv * (1.0 - sig)))
            dwg_acc[g] += _dot_tn(mixed_ref[:, cols], dmg)
            dmixed = _dot_nt(dmg, wg_ref[g])
            e = dmixed * _inv_count(row, POOL_WINDOWS[g])
            s = jnp.concatenate([e, carry_ref[g]], axis=0)
            carry_ref[g] = e[:HALO, :]
            w = 1
            while w < POOL_WINDOWS[g]:
                s = s + pltpu.roll(s, n_ext - w, 0)
                w *= 2
            du_ref[:, cols] = (s[:TM, :] - dmixed).astype(MXU)
            du_ref[:, POOL_WIDTH + g * GROUP:POOL_WIDTH + (g + 1) * GROUP] = dz.astype(MXU)

        @pl.when(i == nt - 1)
        def _():
            _flush(dwg_acc, dwg_hbm, stage_g)
            _flush(dwout_acc, dwout_hbm, stage_o)
            ex.wait(ex_in, ex_out, ex_sems)

    rev = lambda w: pl.BlockSpec((TM, w), lambda i: (nt - 1 - i, 0))
    any_spec = pl.BlockSpec(memory_space=pl.ANY)
    return pl.pallas_call(
        body, name="pool_bwd", grid=(nt,),
        in_specs=[rev(D_MODEL), rev(D_MODEL), rev(POOL_WIDTH), rev(POOL_WIDTH), rev(POOL_WIDTH), rev(POOL_WIDTH),
                  _const_spec((1, D_MODEL)), _const_spec((1, POOL_WIDTH)),
                  _const_spec((N_GROUPS, GROUP, GROUP)), _const_spec((POOL_WIDTH, D_MODEL))] + ex.specs,
        out_specs=[rev(2 * POOL_WIDTH), pl.BlockSpec((1, POOL_WIDTH), lambda i: (0, 0)),
                   pl.BlockSpec((1, D_MODEL), lambda i: (0, 0)), any_spec, any_spec] + ex.specs,
        out_shape=[jax.ShapeDtypeStruct((n_tok, 2 * POOL_WIDTH), MXU), jax.ShapeDtypeStruct((1, POOL_WIDTH), F32),
                   jax.ShapeDtypeStruct((1, D_MODEL), F32), jax.ShapeDtypeStruct((N_GROUPS, GROUP, GROUP), MXU),
                   jax.ShapeDtypeStruct((N_GROUPS, GROUP, D_MODEL), MXU)] + ex.out_shape,
        scratch_shapes=[pltpu.VMEM((N_GROUPS, HALO, GROUP), F32), pltpu.VMEM((N_GROUPS, GROUP, GROUP), F32),
                        pltpu.VMEM((N_GROUPS, GROUP, D_MODEL), F32), pltpu.VMEM((GROUP, GROUP), MXU),
                        pltpu.VMEM((GROUP, D_MODEL), MXU)] + ex.scratch,
        compiler_params=_params(1),
    )(dx1, y, z, mg, mixed, prod, g_post, scale, w_group, w_out, *ex.arrays)


def _in_proj_bwd(parts, x, dres, g_pre, w_in, name, tm, with_dw, exchange_items):
    n_tok = x.shape[0]
    nt = n_tok // tm
    half = D_MODEL // W_BLOCK
    ex = _Exchange(exchange_items)
    n_dw = 1 if with_dw else 0

    def body(p0, p1, p2, p3, x_ref, dres_ref, g_ref, w_ref, *rest):
        ex_in, rest = rest[:ex.n], rest[ex.n:]
        dx_ref, dg_ref = rest[:2]
        dw_hbm = rest[2:2 + n_dw]
        ex_out, rest = rest[2 + n_dw:2 + n_dw + ex.n], rest[2 + n_dw + ex.n:]
        dw_scratch, ex_sems = rest[:2 * n_dw], rest[2 * n_dw:]
        i = pl.program_id(0)

        @pl.when(i == 0)
        def _():
            ex.start(ex_in, ex_out, ex_sems)
            dg_ref[...] = jnp.zeros_like(dg_ref)
            if with_dw:
                dw_scratch[0][...] = jnp.zeros_like(dw_scratch[0])

        r, xhat = _rms_fwd(x_ref[...])
        g = g_ref[...]
        h = (xhat * g).astype(MXU)
        dh = None
        for p, part_ref in enumerate((p0, p1, p2, p3)):
            for jj in range(half):
                j = half * p + jj
                if len(part_ref.shape) == 3:
                    per_block = W_BLOCK // PAIR
                    du = jnp.concatenate([part_ref[jj * per_block + pp] for pp in range(per_block)], axis=1)
                else:
                    du = part_ref[:, jj * W_BLOCK:(jj + 1) * W_BLOCK]
                t = _dot_nt(du, w_ref[j])
                dh = t if dh is None else dh + t
                if with_dw:
                    dw_scratch[0][j] += _dot_tn(h, du)
        dg_ref[...] += jnp.sum(dh * xhat, axis=0, keepdims=True)
        dx_ref[...] = dres_ref[...] + _rms_bwd(dh, xhat, r, g)

        @pl.when(i == nt - 1)
        def _():
            if with_dw:
                _flush(dw_scratch[0], dw_hbm[0], dw_scratch[1])
            ex.wait(ex_in, ex_out, ex_sems)

    tok = pl.BlockSpec((tm, D_MODEL), lambda i: (i, 0))
    return pl.pallas_call(
        body, name=name, grid=(nt,),
        in_specs=[pl.BlockSpec(shape, m) for _, shape, m in parts]
        + [tok, tok, _const_spec((1, D_MODEL)), _const_spec((N_DEV, D_MODEL, W_BLOCK))] + ex.specs,
        out_specs=[tok, pl.BlockSpec((1, D_MODEL), lambda i: (0, 0))]
        + [pl.BlockSpec(memory_space=pl.ANY)] * n_dw + ex.specs,
        out_shape=[jax.ShapeDtypeStruct((n_tok, D_MODEL), F32), jax.ShapeDtypeStruct((1, D_MODEL), F32)]
        + [jax.ShapeDtypeStruct((N_DEV, D_MODEL, W_BLOCK), MXU)] * n_dw + ex.out_shape,
        scratch_shapes=[pltpu.VMEM((N_DEV, D_MODEL, W_BLOCK), F32), pltpu.VMEM((D_MODEL, W_BLOCK), MXU)][:2 * n_dw]
        + ex.scratch,
        compiler_params=_params(1),
    )(*[a for a, _, _ in parts], x, dres, g_pre, w_in, *ex.arrays)


def _flip(x, y, c, bits):
    return (1 - x if bits & 4 else x, 1 - y if bits & 2 else y, 1 - c if bits & 1 else c)


def _tail_peer(x, y, c, step, receive):
    south, north = TAIL_PEER_BITS[0][step], TAIL_PEER_BITS[1][step]
    if receive and south & 1:
        south, north = north, south
    if south == north:
        return _flip(x, y, c, south)
    return tuple(jnp.where(c == 0, u, v) for u, v in zip(_flip(x, y, c, south), _flip(x, y, c, north)))


def _w_in_grad_scatter(h_t, du, exchange_items, name):
    n_tok = du.shape[0]
    ni = n_tok // TMB
    ex = _Exchange(exchange_items)
    c_out = lax.axis_index("c")
    me_out = 4 * lax.axis_index("x") + 2 * lax.axis_index("y") + c_out
    order = jnp.where(c_out == 0, me_out ^ jnp.array(TAIL_PEER_BITS[0], jnp.int32),
                      me_out ^ jnp.array(TAIL_PEER_BITS[1], jnp.int32)).astype(jnp.int32)

    def body(order_ref, h_ref, du_ref, *rest):
        ex_in, rest = rest[:ex.n], rest[ex.n:]
        part_hbm, ex_out, rest = rest[0], rest[1:1 + ex.n], rest[1 + ex.n:]
        acc_ref, stage_ref, send_sems, recv_sems = rest[:4]
        ex_sems = rest[4:]
        s = pl.program_id(0)
        i = pl.program_id(1)
        x, y, c = lax.axis_index("x"), lax.axis_index("y"), lax.axis_index("c")
        me = 4 * x + 2 * y + c

        def block_copy(step, dst_slot, to):
            return pltpu.make_async_remote_copy(
                src_ref=stage_ref.at[step], dst_ref=part_hbm.at[dst_slot],
                send_sem=send_sems.at[step], recv_sem=recv_sems.at[step],
                device_id=to, device_id_type=pl.DeviceIdType.MESH)

        @pl.when((s == 0) & (i == 0))
        def _():
            ex.start(ex_in, ex_out, ex_sems)

        @pl.when(i == 0)
        def _():
            acc_ref[...] = jnp.zeros_like(acc_ref)

        acc_ref[...] += _dot(h_ref[:, pl.ds(pl.multiple_of(i * TMB, TMB), TMB)], du_ref[...])

        @pl.when(i == ni - 1)
        def _():
            stage_ref[s] = acc_ref[...].astype(MXU)
            for step in range(N_DEV - 1):
                @pl.when(s == step)
                def _(step=step):
                    block_copy(step, me, _tail_peer(x, y, c, step, False)).start()

            @pl.when(s == N_DEV - 1)
            def _():
                pltpu.sync_copy(stage_ref.at[s], part_hbm.at[me])
                for step in range(N_DEV - 1):
                    sx, sy, sc = _tail_peer(x, y, c, step, True)
                    block_copy(step, 4 * sx + 2 * sy + sc, (x, y, c)).wait_recv()
                    block_copy(step, me, (x, y, c)).wait_send()
                ex.wait(ex_in, ex_out, ex_sems)

    grid_spec = pltpu.PrefetchScalarGridSpec(
        num_scalar_prefetch=1, grid=(N_DEV, ni),
        in_specs=[pl.BlockSpec((D_MODEL, n_tok), lambda s, i, order: (0, 0), pipeline_mode=pl.Buffered(1)),
                  pl.BlockSpec((TMB, W_BLOCK), lambda s, i, order: (i, order[s]))] + ex.specs,
        out_specs=[pl.BlockSpec(memory_space=pl.ANY)] + ex.specs,
        scratch_shapes=[pltpu.VMEM((D_MODEL, W_BLOCK), F32), pltpu.VMEM((N_DEV, D_MODEL, W_BLOCK), MXU),
                        pltpu.SemaphoreType.DMA((N_DEV - 1,)), pltpu.SemaphoreType.DMA((N_DEV - 1,))] + ex.scratch)
    return pl.pallas_call(
        body, name=name, grid_spec=grid_spec,
        out_shape=[jax.ShapeDtypeStruct((N_DEV, D_MODEL, W_BLOCK), MXU)] + ex.out_shape,
        compiler_params=_params(2),
    )(order, h_t, du, *ex.arrays)


def _rel_onehot():
    rel = lax.broadcasted_iota(jnp.int32, (REL_PAD, ROLL_W), 0)
    col = lax.broadcasted_iota(jnp.int32, (REL_PAD, ROLL_W), 1)
    return (rel == jnp.minimum(BAND + MAX_REL - col, 2 * MAX_REL)).astype(MXU)


def _split3(v):
    hi = v.astype(MXU)
    r1 = v - hi.astype(F32)
    mid = r1.astype(MXU)
    lo = (r1 - mid.astype(F32)).astype(MXU)
    return hi, mid, lo


def _fill_bias_table(rb_ref, out_ref):
    onehot = _rel_onehot()
    base = None
    for term in _split3(rb_ref[...]):
        t = _dot(term, onehot)
        base = t if base is None else base + t
    qi = lax.broadcasted_iota(jnp.int32, (CHUNK, ROLL_W), 0)
    kk = lax.broadcasted_iota(jnp.int32, (CHUNK, TKW), 1)
    for h in range(N_HEADS):
        t = jnp.broadcast_to(base[h:h + 1, :], (CHUNK, ROLL_W))
        for bit in range(6):
            t = jnp.where(((qi >> bit) & 1) == 1, pltpu.roll(t, 1 << bit, 1), t)
        for rr in range(TM // CHUNK):
            shifted = pltpu.roll(t, (CHUNK * rr - CHUNK) % ROLL_W, 1)[:, :TKW]
            band = kk - CHUNK * rr
            out_ref[h, rr * CHUNK:(rr + 1) * CHUNK, :] = jnp.where((band >= 0) & (band < BAND), shifted, NEG)


def _bias_grad(dtab):
    def body(dt_ref, out_ref, dbase_ref):
        qi = lax.broadcasted_iota(jnp.int32, (CHUNK, ROLL_W), 0)
        zeros = jnp.zeros((CHUNK, ROLL_W - TKW), F32)
        for h in range(N_HEADS):
            t = None
            for rr in range(TM // CHUNK):
                blk = jnp.concatenate([dt_ref[h, rr * CHUNK:(rr + 1) * CHUNK, :], zeros], axis=1)
                blk = pltpu.roll(blk, (CHUNK - CHUNK * rr) % ROLL_W, 1)
                t = blk if t is None else t + blk
            for bit in range(6):
                t = jnp.where(((qi >> bit) & 1) == 1, pltpu.roll(t, ROLL_W - (1 << bit), 1), t)
            dbase_ref[h:h + 1, :] = jnp.sum(t, axis=0, keepdims=True)
        onehot = _rel_onehot()
        acc = None
        for term in _split3(dbase_ref[...]):
            t = _dot_nt(term, onehot)
            acc = t if acc is None else acc + t
        out_ref[...] = acc

    return pl.pallas_call(
        body, name="bias_grad", out_shape=jax.ShapeDtypeStruct((N_HEADS, REL_PAD), F32),
        scratch_shapes=[pltpu.VMEM((N_HEADS, ROLL_W), F32)],
        compiler_params=pltpu.CompilerParams(vmem_limit_bytes=VMEM_LIMIT),
    )(dtab)


def _att_in(x1, g_pre, w_in):
    n_tok = x1.shape[0]
    tm = TM_WIDE
    nt = n_tok // tm
    lead = PAD // tm
    per_block = W_BLOCK // PAIR

    def body(x_ref, g_ref, w_ref, q_ref, k_ref, v_ref, z_ref):
        i = pl.program_id(0)

        @pl.when(i < lead)
        def _():
            k_ref[...] = jnp.zeros_like(k_ref)
            v_ref[...] = jnp.zeros_like(v_ref)

        @pl.when(i >= lead)
        def _():
            _, xhat = _rms_fwd(x_ref[...])
            h = (xhat * g_ref[...]).astype(MXU)
            for j in range(N_DEV):
                u = _dot(h, w_ref[j])
                if j >= 6:
                    z_ref[:, (j % 2) * W_BLOCK:(j % 2 + 1) * W_BLOCK] = u
                    continue
                dst = (q_ref, k_ref, v_ref)[j // 2]
                if j < 2:
                    u = u * QK_SCALE
                for pp in range(per_block):
                    dst[(j % 2) * per_block + pp] = u[:, pp * PAIR:(pp + 1) * PAIR].astype(MXU)

    late = pl.BlockSpec((tm, D_MODEL), lambda i: (jnp.maximum(i - lead, 0), 0))
    late3 = pl.BlockSpec((N_PAIRS, tm, PAIR), lambda i: (0, jnp.maximum(i - lead, 0), 0))
    padded3 = pl.BlockSpec((N_PAIRS, tm, PAIR), lambda i: (0, i, 0))
    return pl.pallas_call(
        body, name="att_in", grid=(nt + lead,),
        in_specs=[late, _const_spec((1, D_MODEL)), _const_spec((N_DEV, D_MODEL, W_BLOCK))],
        out_specs=[late3, padded3, padded3, late],
        out_shape=[jax.ShapeDtypeStruct((N_PAIRS, n_tok, PAIR), MXU),
                   jax.ShapeDtypeStruct((N_PAIRS, n_tok + PAD, PAIR), MXU),
                   jax.ShapeDtypeStruct((N_PAIRS, n_tok + PAD, PAIR), MXU),
                   jax.ShapeDtypeStruct((n_tok, ATT_WIDTH), F32)],
        compiler_params=_params(1),
    )(x1, g_pre, w_in)


def _head_masks():
    lane = lax.broadcasted_iota(jnp.int32, (1, PAIR), 1)
    first = lane < HEAD_DIM
    return first, jnp.logical_not(first)


def _pair_loop(one_pair, unroll):
    def loop_pass(t, carry):
        for u in range(unroll):
            one_pair(t * unroll + u)
        return carry

    lax.fori_loop(0, PAIRS_PER_STEP // unroll, loop_pass, 0)


def _scores(qh, k_refs, bias_ref, p, hh, tile, masked):
    ss = []
    for b in range(KB):
        s = _dot_nt(qh, k_refs[b][p]) + bias_ref[2 * p + hh, :, b * TM:(b + 1) * TM]
        if masked:
            s = s + jnp.where(tile + b < PAD // TM, NEG, 0.0).astype(F32)
        ss.append(s)
    return ss


def _pair_specs(index_map):
    return pl.BlockSpec((PAIRS_PER_STEP, TM, PAIR), index_map)


def _att_fwd(q, kpad, vpad, bias_tab):
    n_tok = q.shape[1]
    nt = n_tok // TM
    lead = PAD // TM

    def body(q_ref, k0, k1, k2, v0, v1, v2, bias_ref, o_ref, lse_ref):
        i = pl.program_id(1)
        masks = _head_masks()

        def pairs(masked):
            def one_pair(p):
                qv = q_ref[p]
                outs, ms = [], []
                for hh, mask in enumerate(masks):
                    qh = jnp.where(mask, qv, jnp.zeros_like(qv))
                    ss = _scores(qh, (k0, k1, k2), bias_ref, p, hh, i, masked)
                    m = None
                    for s in ss:
                        mb = jnp.max(s, axis=-1, keepdims=True)
                        m = mb if m is None else jnp.maximum(m, mb)
                    out = None
                    for b, v_ref in enumerate((v0, v1, v2)):
                        vb = v_ref[p]
                        t = _dot(jnp.exp(ss[b] - m).astype(MXU), jnp.where(mask, vb, jnp.ones_like(vb)))
                        out = t if out is None else out + t
                    outs.append(out)
                    ms.append(m)
                num = jnp.where(masks[0], outs[0], outs[1])
                den = jnp.where(masks[0], pltpu.roll(outs[0], HEAD_DIM, 1), pltpu.roll(outs[1], HEAD_DIM, 1))
                o_ref[p] = num / den
                lse_ref[p] = jnp.where(masks[0], ms[0], ms[1]) + jnp.log(den)

            _pair_loop(one_pair, FWD_UNROLL)

        @pl.when(i < lead)
        def _():
            pairs(True)

        @pl.when(i >= lead)
        def _():
            pairs(False)

    qspec = _pair_specs(lambda g, i: (g, i, 0))
    kspecs = [_pair_specs(functools.partial(lambda g, i, b: (g, i + b, 0), b=b)) for b in range(KB)]
    vspecs = [_pair_specs(functools.partial(lambda g, i, b: (g, i + b, 0), b=b)) for b in range(KB)]
    shape = jax.ShapeDtypeStruct((N_PAIRS, n_tok, PAIR), F32)
    return pl.pallas_call(
        body, name="att_fwd", grid=(N_PAIRS // PAIRS_PER_STEP, nt),
        in_specs=[qspec] + kspecs + vspecs + [pl.BlockSpec((2 * PAIRS_PER_STEP, TM, TKW), lambda g, i: (g, 0, 0))],
        out_specs=[qspec, qspec], out_shape=[shape, shape],
        compiler_params=_params(2),
    )(q, kpad, kpad, kpad, vpad, vpad, vpad, bias_tab)


def _att_bwd(q, kpad, vpad, do, o, lse, bias_tab):
    n_tok = q.shape[1]
    nt = n_tok // TM
    lead = PAD // TM

    def body(q_ref, do_ref, o_ref, lse_ref, k0, k1, k2, v0, v1, v2, bias_ref,
             dq_ref, dk_ref, dv_ref, dtab_ref, rk0, rk1, rv0, rv1):
        i = pl.program_id(1)
        masks = _head_masks()

        @pl.when(i == 0)
        def _():
            for ref in (rk0, rk1, rv0, rv1):
                ref[...] = jnp.zeros_like(ref)
            dtab_ref[...] = jnp.zeros_like(dtab_ref)

        def pairs(masked):
            def one_pair(p):
                qv = q_ref[p]
                dov = do_ref[p]
                doo = dov.astype(F32) * o_ref[p]
                lse_pair = lse_ref[p]
                dks = [None] * KB
                dvs = [None] * KB
                dqs = []
                for hh, mask in enumerate(masks):
                    qh = jnp.where(mask, qv, jnp.zeros_like(qv))
                    doh = jnp.where(mask, dov, jnp.zeros_like(dov))
                    dsum = jnp.sum(jnp.where(mask, doo, 0.0), axis=-1, keepdims=True)
                    lse_h = lse_pair[:, hh * HEAD_DIM:hh * HEAD_DIM + 1]
                    ss = _scores(qh, (k0, k1, k2), bias_ref, p, hh, i, masked)
                    dq = None
                    for b, (k_ref, v_ref) in enumerate(zip((k0, k1, k2), (v0, v1, v2))):
                        prob = jnp.exp(ss[b] - lse_h)
                        ds = prob * (_dot_nt(doh, v_ref[p]) - dsum)
                        dtab_ref[2 * p + hh, :, b * TM:(b + 1) * TM] += ds
                        dsb = ds.astype(MXU)
                        t = lax.dot_general(k_ref[p], dsb, (((0,), (1,)), ((), ())), preferred_element_type=F32)
                        dq = t if dq is None else dq + t
                        t = _dot_tn(qh, dsb)
                        dks[b] = t if dks[b] is None else dks[b] + t
                        t = _dot_tn(doh, prob.astype(MXU))
                        dvs[b] = t if dvs[b] is None else dvs[b] + t
                    dqs.append(dq)
                first_rows = lax.broadcasted_iota(jnp.int32, (PAIR, 1), 0) < HEAD_DIM
                dq_ref[p] = (jnp.where(first_rows, dqs[0], dqs[1]).T * QK_SCALE).astype(MXU)
                dk_ref[p] = (rk0[p] + dks[0].T).astype(MXU)
                dv_ref[p] = (rv0[p] + dvs[0].T).astype(MXU)
                rk0[p] = rk1[p] + dks[1].T
                rv0[p] = rv1[p] + dvs[1].T
                rk1[p] = dks[2].T
                rv1[p] = dvs[2].T

            _pair_loop(one_pair, BWD_UNROLL)

        @pl.when(i < lead)
        def _():
            pairs(True)

        @pl.when((i >= lead) & (i < nt))
        def _():
            pairs(False)

        @pl.when(i >= nt)
        def _():
            dk_ref[...] = rk0[...].astype(MXU)
            dv_ref[...] = rv0[...].astype(MXU)
            rk0[...] = rk1[...]
            rv0[...] = rv1[...]

    last = nt - 1
    qspec = _pair_specs(lambda g, i: (g, jnp.minimum(i, last), 0))
    kspecs = [_pair_specs(functools.partial(lambda g, i, b: (g, jnp.minimum(i, last) + b, 0), b=b)) for b in range(KB)]
    vspecs = [_pair_specs(functools.partial(lambda g, i, b: (g, jnp.minimum(i, last) + b, 0), b=b)) for b in range(KB)]
    pspec = _pair_specs(lambda g, i: (g, i, 0))
    tspec = pl.BlockSpec((2 * PAIRS_PER_STEP, TM, TKW), lambda g, i: (g, 0, 0))
    ring = pltpu.VMEM((PAIRS_PER_STEP, TM, PAIR), F32)
    return pl.pallas_call(
        body, name="att_bwd", grid=(N_PAIRS // PAIRS_PER_STEP, nt + KB - 1),
        in_specs=[qspec, qspec, qspec, qspec] + kspecs + vspecs + [tspec],
        out_specs=[qspec, pspec, pspec, tspec],
        out_shape=[jax.ShapeDtypeStruct((N_PAIRS, n_tok, PAIR), MXU),
                   jax.ShapeDtypeStruct((N_PAIRS, n_tok + PAD, PAIR), MXU),
                   jax.ShapeDtypeStruct((N_PAIRS, n_tok + PAD, PAIR), MXU),
                   jax.ShapeDtypeStruct((N_HEADS, TM, TKW), F32)],
        scratch_shapes=[ring, ring, ring, ring],
        compiler_params=_params(2),
    )(q, do, o, lse, kpad, kpad, kpad, vpad, vpad, vpad, bias_tab)


def _att_out(o, z, x1, target, g_post, w_out):
    n_tok = z.shape[0]
    nt = n_tok // TM

    def body(o_ref, z_ref, x1_ref, tgt_ref, gpost_ref, w_ref,
             loss_ref, dx2_ref, do_ref, dz_ref, dgpost_ref, dw_hbm, acc_ref, loss_acc, stage_ref):
        i = pl.program_id(0)

        @pl.when(i == 0)
        def _():
            acc_ref[...] = jnp.zeros_like(acc_ref)
            loss_acc[...] = jnp.zeros_like(loss_acc)
            dgpost_ref[...] = jnp.zeros_like(dgpost_ref)

        ov = jnp.concatenate([o_ref[p] for p in range(N_PAIRS)], axis=1)
        zv = z_ref[...]
        sig = _sigmoid(zv)
        silu = zv * sig
        gated = (ov * silu).astype(MXU)
        y = _dot(gated, w_ref[...])
        r, yhat = _rms_fwd(y)
        gpost = gpost_ref[...]
        diff = x1_ref[...] + yhat * gpost - tgt_ref[...]
        loss_acc[...] += jnp.sum(diff * diff, axis=0, keepdims=True)
        dn = diff * (1.0 / D_MODEL)
        dx2_ref[...] = dn
        dgpost_ref[...] += jnp.sum(dn * yhat, axis=0, keepdims=True)
        dy = _rms_bwd(dn, yhat, r, gpost).astype(MXU)
        for j in range(ATT_WIDTH // W_BLOCK):
            acc_ref[j] += _dot_tn(gated[:, j * W_BLOCK:(j + 1) * W_BLOCK], dy)
        dgated = _dot_nt(dy, w_ref[...])
        dob = (dgated * silu).astype(MXU)
        for p in range(N_PAIRS):
            do_ref[p] = dob[:, p * PAIR:(p + 1) * PAIR]
        dz_ref[...] = (dgated * ov * (sig * (1.0 + zv * (1.0 - sig)))).astype(MXU)

        @pl.when(i == nt - 1)
        def _():
            total = jnp.sum(loss_acc[...], axis=-1, keepdims=True) * (0.5 / D_MODEL)
            loss_ref[...] = jnp.broadcast_to(total, loss_ref.shape)
            _flush(acc_ref, dw_hbm, stage_ref)

    tok = pl.BlockSpec((TM, D_MODEL), lambda i: (i, 0))
    tok3 = pl.BlockSpec((N_PAIRS, TM, PAIR), lambda i: (0, i, 0))
    return pl.pallas_call(
        body, name="att_out", grid=(nt,),
        in_specs=[tok3, tok, tok, tok, _const_spec((1, D_MODEL)), _const_spec((ATT_WIDTH, D_MODEL))],
        out_specs=[pl.BlockSpec((1, 128), lambda i: (0, 0)), tok, tok3, tok,
                   pl.BlockSpec((1, D_MODEL), lambda i: (0, 0)), pl.BlockSpec(memory_space=pl.ANY)],
        out_shape=[jax.ShapeDtypeStruct((1, 128), F32), jax.ShapeDtypeStruct((n_tok, D_MODEL), F32),
                   jax.ShapeDtypeStruct((N_PAIRS, n_tok, PAIR), MXU), jax.ShapeDtypeStruct((n_tok, ATT_WIDTH), MXU),
                   jax.ShapeDtypeStruct((1, D_MODEL), F32),
                   jax.ShapeDtypeStruct((ATT_WIDTH // W_BLOCK, W_BLOCK, D_MODEL), MXU)],
        scratch_shapes=[pltpu.VMEM((ATT_WIDTH // W_BLOCK, W_BLOCK, D_MODEL), F32), pltpu.VMEM((1, D_MODEL), F32),
                        pltpu.VMEM((W_BLOCK, D_MODEL), MXU)],
        compiler_params=_params(1),
    )(o, z, x1, target, g_post, w_out)


def _adamw(parts, w, m, v, name):
    rows, cols = w.shape
    tr = min(rows, 256)

    def body(p_ref, w_ref, m_ref, v_ref, g_ref, d_ref, mo_ref, vo_ref):
        g = p_ref[0].astype(F32)
        for s in range(1, N_DEV):
            g = g + p_ref[s].astype(F32)
        m_new = ADAM_B1 * m_ref[...] + (1.0 - ADAM_B1) * g
        v_new = ADAM_B2 * v_ref[...] + (1.0 - ADAM_B2) * (g * g)
        m_hat = m_new / (1.0 - ADAM_B1 ** ADAM_STEP)
        v_hat = v_new / (1.0 - ADAM_B2 ** ADAM_STEP)
        g_ref[...] = g
        d_ref[...] = -ADAM_LR * (m_hat / (jnp.sqrt(v_hat) + ADAM_EPS) + ADAM_WD * w_ref[...])
        mo_ref[...] = m_new
        vo_ref[...] = v_new

    blk = pl.BlockSpec((tr, cols), lambda i: (i, 0))
    shape = jax.ShapeDtypeStruct((rows, cols), F32)
    return pl.pallas_call(
        body, name=name, grid=(rows // tr,),
        in_specs=[pl.BlockSpec((N_DEV, tr, cols), lambda i: (0, i, 0)), blk, blk, blk],
        out_specs=[blk, blk, blk, blk], out_shape=[shape, shape, shape, shape],
        compiler_params=_params(1),
    )(parts, w, m, v)


SMALL_ROWS = 16
LOSS_AT = (6, N_REL)


def _pack_small(norm_pre, norm_post, pool_scale, rel_bias_padded):
    return jnp.concatenate([norm_pre, norm_post, pool_scale.reshape(2, D_MODEL),
                            rel_bias_padded.reshape(SMALL_ROWS - 6, D_MODEL)], axis=0)


def _unpack_small(packed):
    rel = packed[6:].reshape(N_HEADS, REL_PAD)[:, :N_REL]
    return packed[0:2], packed[2:4], packed[4:6].reshape(1, POOL_WIDTH), rel.reshape(1, N_HEADS, N_REL)


def _pad_rel(rel_bias):
    return jnp.pad(rel_bias.reshape(N_HEADS, N_REL), ((0, 0), (0, REL_PAD - N_REL)))


def kernel(x, norm_pre, norm_post, pool_w_in, pool_w_group, pool_scale, pool_w_out, att_w_in, att_rel_bias, att_w_out, loss_target, m_norm_pre, m_norm_post, m_pool_w_in, m_pool_w_group, m_pool_scale, m_pool_w_out, m_att_w_in, m_att_rel_bias, m_att_w_out, v_norm_pre, v_norm_post, v_pool_w_in, v_pool_w_group, v_pool_scale, v_pool_w_out, v_att_w_in, v_att_rel_bias, v_att_w_out):
    xt = x[0]
    target = loss_target[0]
    n_tok = xt.shape[0]
    lead = PAD // TM
    rows_g = GROUP // N_DEV

    rel_padded = _pad_rel(att_rel_bias[0])
    gathered = _gather_two_level([pool_w_in[0].astype(MXU), pool_w_group[0].astype(MXU), pool_w_out[0].astype(MXU)],
                                 rel_padded, "gather_pool_weights")
    bias_tab = gathered[3]
    w_in_p = gathered[0]
    w_group = gathered[1].transpose(1, 0, 2, 3).reshape(N_GROUPS, GROUP, GROUP)
    w_out_p = gathered[2].reshape(POOL_WIDTH, D_MODEL)

    x1, y0, z0, mixed, mg, prod, h0_t, w_in_a, w_out_a = _pool_fwd(
        xt, norm_pre[0:1], norm_post[0:1], w_in_p, w_group, pool_scale, w_out_p,
        [(att_w_in[0].astype(MXU), False), (att_w_out[0].astype(MXU), False)])
    w_out_a = w_out_a.reshape(ATT_WIDTH, D_MODEL)
    q, kpad, vpad, z1 = _att_in(x1, norm_pre[1:2], w_in_a)
    o, lse = _att_fwd(q, kpad, vpad, bias_tab)
    loss_part, dx2, do, dz1, d_gpost1, d_w_out_a = _att_out(o, z1, x1, target, norm_post[1:2], w_out_a)
    dq, dkpad, dvpad, dtab = _att_bwd(q, kpad, vpad, do, o, lse, bias_tab)
    d_rel = _bias_grad(dtab)
    pairs = (N_PAIRS, TM, PAIR)
    flat = (TM, D_MODEL)
    dx1, d_gpre1, d_w_in_a = _in_proj_bwd(
        [(dq, pairs, lambda i: (0, i, 0)), (dkpad, pairs, lambda i: (0, i + lead, 0)),
         (dvpad, pairs, lambda i: (0, i + lead, 0)), (dz1, flat, lambda i: (i, 0))],
        x1, dx2, norm_pre[1:2], w_in_a, "att_in_bwd", TM, True, [])
    du0, d_scale, d_gpost0, d_w_group, d_w_out_p, part_w_in_a, part_w_out_a = _pool_bwd(
        dx1, y0, z0, mg, mixed, prod, norm_post[0:1], pool_scale, w_group, w_out_p,
        [(d_w_in_a, True), (d_w_out_a.reshape(N_DEV, ATT_WIDTH // N_DEV, D_MODEL), True)])
    col = lambda p: (lambda i: (i, p))
    grad_x, d_gpre0, part_w_group, part_w_out_p = _in_proj_bwd(
        [(du0, (TM_WIDE, D_MODEL), col(p)) for p in range(4)], xt, dx1, norm_pre[0:1], w_in_p, "pool_in_bwd", TM_WIDE,
        False,
        [(d_w_group.reshape(N_GROUPS, N_DEV, rows_g, GROUP).transpose(1, 0, 2, 3), True),
         (d_w_out_p.reshape(N_DEV, POOL_WIDTH // N_DEV, D_MODEL), True)])
    d_small = _pack_small(jnp.concatenate([d_gpre0, d_gpre1], axis=0), jnp.concatenate([d_gpost0, d_gpost1], axis=0),
                          d_scale, d_rel).at[LOSS_AT].set(loss_part[0, 0])
    part_w_in_p, part_small = _w_in_grad_scatter(h0_t, du0, [(d_small, False)], "pool_w_in_grad")

    def update(part, w, m, v, name):
        shape = w.shape
        flat = lambda a: a.reshape(-1, shape[-1])
        outs = _adamw(part.reshape(N_DEV, -1, shape[-1]), flat(w), flat(m), flat(v), name)
        return [a.reshape(shape) for a in outs]

    u_att_w_in = update(part_w_in_a, att_w_in, m_att_w_in, v_att_w_in, "adamw_att_w_in")
    u_att_w_out = update(part_w_out_a, att_w_out, m_att_w_out, v_att_w_out, "adamw_att_w_out")
    u_pool_w_group = update(part_w_group, pool_w_group, m_pool_w_group, v_pool_w_group, "adamw_pool_w_group")
    u_pool_w_out = update(part_w_out_p, pool_w_out, m_pool_w_out, v_pool_w_out, "adamw_pool_w_out")
    u_pool_w_in = update(part_w_in_p, pool_w_in, m_pool_w_in, v_pool_w_in, "adamw_pool_w_in")
    small = _adamw(part_small, _pack_small(norm_pre, norm_post, pool_scale, rel_padded),
                   _pack_small(m_norm_pre, m_norm_post, m_pool_scale, _pad_rel(m_att_rel_bias[0])),
                   _pack_small(v_norm_pre, v_norm_post, v_pool_scale, _pad_rel(v_att_rel_bias[0])), "adamw_small")
    u_small = [_unpack_small(a) for a in small]

    loss = small[0][LOSS_AT]
    outs = [loss, grad_x.reshape(1, n_tok, D_MODEL)]
    for kind in range(4):
        outs += [u_small[kind][0], u_small[kind][1], u_pool_w_in[kind], u_pool_w_group[kind], u_small[kind][2],
                 u_pool_w_out[kind], u_att_w_in[kind], u_small[kind][3], u_att_w_out[kind]]
    return tuple(outs)
```

```python
import functools

import jax
import jax.numpy as jnp
from jax import lax
from jax.experimental import pallas as pl
from jax.experimental.pallas import tpu as pltpu

F32 = jnp.float32
MXU = jnp.bfloat16

D_MODEL = 1024
POOL_WIDTH = 2048
POOL_WINDOWS = (2, 4, 8, 16)
N_GROUPS = 4
GROUP = 512
HALO = 16
N_HEADS = 16
HEAD_DIM = 64
CHUNK = 64
LEFT_CHUNKS = 8
PAD = LEFT_CHUNKS * CHUNK
BAND = PAD + CHUNK
MAX_REL = 256
N_REL = 2 * MAX_REL + 1
REL_PAD = 640
ATT_WIDTH = 1024
PAIR = 2 * HEAD_DIM
N_PAIRS = N_HEADS // 2
N_DEV = 8
W_BLOCK = 512
RMS_EPS = 1e-6
QK_SCALE = 0.125
NEG = -1e30

TM = 256
TM_WIDE = 512
TMB = 1024
TAIL_OWNER_BITS = (7, 6, 5, 4, 3, 2, 1, 0)
KB = 3
TKW = KB * TM
PAIRS_PER_STEP = 4
FWD_UNROLL = 4
BWD_UNROLL = 2
ROLL_W = 1024

VMEM_LIMIT = 56 * 1024 * 1024

ADAM_LR = 0.001
ADAM_B1 = 0.9
ADAM_B2 = 0.999
ADAM_EPS = 1e-08
ADAM_WD = 0.01
ADAM_STEP = 10

NT_DIMS = (((1,), (1,)), ((), ()))
TN_DIMS = (((0,), (0,)), ((), ()))


def _params(n_grid):
    return pltpu.CompilerParams(dimension_semantics=("arbitrary",) * n_grid, vmem_limit_bytes=VMEM_LIMIT)


def _const_spec(shape):
    nd = len(shape)
    return pl.BlockSpec(shape, lambda *_: (0,) * nd, pipeline_mode=pl.Buffered(1))


def _dot(a, b):
    return jnp.dot(a, b, preferred_element_type=F32)


def _dot_nt(a, b):
    return lax.dot_general(a, b, NT_DIMS, preferred_element_type=F32)


def _dot_tn(a, b):
    return lax.dot_general(a, b, TN_DIMS, preferred_element_type=F32)


def _sigmoid(z):
    return 1.0 / (1.0 + jnp.exp(-z))


def _rms_fwd(xv):
    r = lax.rsqrt(jnp.mean(xv * xv, axis=-1, keepdims=True) + RMS_EPS)
    return r, xv * r


def _rms_bwd(dn, xhat, r, g):
    dng = dn * g
    return r * (dng - xhat * jnp.mean(dng * xhat, axis=-1, keepdims=True))


class _Exchange:
    def __init__(self, items):
        self.arrays = [a for a, _ in items]
        self.scatter = [s for _, s in items]
        self.n = len(items)
        self.out_shape = [jax.ShapeDtypeStruct((N_DEV,) + tuple(a.shape[1:] if s else a.shape), a.dtype)
                          for a, s in items]
        self.specs = [pl.BlockSpec(memory_space=pl.ANY)] * self.n
        self.scratch = ([pltpu.SemaphoreType.DMA((N_DEV - 1, self.n)), pltpu.SemaphoreType.DMA((N_DEV - 1, self.n)),
                         pltpu.SemaphoreType.DMA((self.n,))] if self.n else [])

    def _copies(self, ins, outs, sems, with_receives):
        send_sems, recv_sems, local_sems = sems
        x, y, c = lax.axis_index("x"), lax.axis_index("y"), lax.axis_index("c")
        me = 4 * x + 2 * y + c

        def src(t, slot):
            return ins[t].at[slot] if self.scatter[t] else ins[t]

        local = [pltpu.make_async_copy(src(t, me), outs[t].at[me], local_sems.at[t]) for t in range(self.n)]
        sends, recvs = [], []
        for k in range(1, N_DEV):
            px = 1 - x if k & 4 else x
            py = 1 - y if k & 2 else y
            pc = 1 - c if k & 1 else c
            peer = 4 * px + 2 * py + pc
            for t in range(self.n):
                common = dict(src_ref=src(t, peer), send_sem=send_sems.at[k - 1, t], recv_sem=recv_sems.at[k - 1, t],
                              device_id=(px, py, pc), device_id_type=pl.DeviceIdType.MESH)
                sends.append(pltpu.make_async_remote_copy(dst_ref=outs[t].at[me], **common))
                if with_receives:
                    recvs.append(pltpu.make_async_remote_copy(dst_ref=outs[t].at[peer], **common))
        return local, sends, recvs

    def start(self, ins, outs, sems):
        if self.n:
            local, sends, _ = self._copies(ins, outs, sems, False)
            for cp in local + sends:
                cp.start()

    def wait(self, ins, outs, sems):
        if self.n:
            local, sends, recvs = self._copies(ins, outs, sems, True)
            for cp in recvs:
                cp.wait_recv()
            for cp in sends:
                cp.wait_send()
            for cp in local:
                cp.wait()


def _exchange(items, name):
    ex = _Exchange(items)
    n = ex.n

    def body(*refs):
        ins, outs, sems = refs[:n], refs[n:2 * n], refs[2 * n:]
        ex.start(ins, outs, sems)
        ex.wait(ins, outs, sems)

    return pl.pallas_call(
        body, name=name, out_shape=ex.out_shape, in_specs=ex.specs, out_specs=ex.specs, scratch_shapes=ex.scratch,
        compiler_params=pltpu.CompilerParams(has_side_effects=True),
    )(*ex.arrays)


def _gather_two_level(arrays, rel_bias_padded, name):
    n = len(arrays)
    out_shape = [jax.ShapeDtypeStruct((N_DEV,) + a.shape, a.dtype) for a in arrays]
    own_sib, own_x, own_y, half_via_x, half_via_y, x_sib, y_sib, diag_sib = range(8)

    def body(*refs):
        ins, rb_ref, outs, tab_ref = refs[:n], refs[n], refs[n + 1:2 * n + 1], refs[2 * n + 1]
        send_sems, recv_sems, local_sems = refs[2 * n + 2:]
        x, y, c = lax.axis_index("x"), lax.axis_index("y"), lax.axis_index("c")
        me, sibling = (x, y, c), (x, y, 1 - c)
        x_nbr, y_nbr, diag = (1 - x, y, c), (x, 1 - y, c), (1 - x, 1 - y, c)

        def slot(pos):
            return 4 * pos[0] + 2 * pos[1] + pos[2]

        def other_core(pos):
            return (pos[0], pos[1], 1 - pos[2])

        def copy(kind, t, block, to, own=False, half=None):
            dst = outs[t].at[slot(block)]
            if half is not None:
                rows = arrays[t].shape[0] // 2
                dst = dst.at[pl.ds(half * rows, rows)]
            return pltpu.make_async_remote_copy(
                src_ref=ins[t] if own else dst, dst_ref=dst,
                send_sem=send_sems.at[kind, t], recv_sem=recv_sems.at[kind, t],
                device_id=to, device_id_type=pl.DeviceIdType.MESH)

        local = [pltpu.make_async_copy(ins[t], outs[t].at[slot(me)], local_sems.at[t]) for t in range(n)]
        sent = [copy(kind, t, me, to, own=True)
                for t in range(n) for kind, to in ((own_x, x_nbr), (own_y, y_nbr), (own_sib, sibling))]
        for cp in local + sent:
            cp.start()
        _fill_bias_table(rb_ref, tab_ref)

        def start(cp):
            cp.start()
            sent.append(cp)

        for t in range(n):
            copy(own_x, t, x_nbr, me).wait_recv()
            start(copy(half_via_x, t, x_nbr, y_nbr, half=0))
            start(copy(x_sib, t, x_nbr, sibling))
        for t in range(n):
            copy(own_y, t, y_nbr, me).wait_recv()
            start(copy(half_via_y, t, y_nbr, x_nbr, half=1))
            start(copy(y_sib, t, y_nbr, sibling))
        for t in range(n):
            copy(half_via_x, t, diag, me, half=0).wait_recv()
            copy(half_via_y, t, diag, me, half=1).wait_recv()
            start(copy(diag_sib, t, diag, sibling))
        for t in range(n):
            for kind, block in ((own_sib, sibling), (x_sib, other_core(x_nbr)), (y_sib, other_core(y_nbr)),
                                (diag_sib, other_core(diag))):
                copy(kind, t, block, me).wait_recv()
        for cp in sent:
            cp.wait_send()
        for cp in local:
            cp.wait()

    any_spec = pl.BlockSpec(memory_space=pl.ANY)
    vmem_spec = pl.BlockSpec(memory_space=pltpu.VMEM)
    return pl.pallas_call(
        body, name=name, out_shape=out_shape + [jax.ShapeDtypeStruct((N_HEADS, TM, TKW), F32)],
        in_specs=[any_spec] * n + [vmem_spec], out_specs=[any_spec] * n + [vmem_spec],
        scratch_shapes=[pltpu.SemaphoreType.DMA((8, n)), pltpu.SemaphoreType.DMA((8, n)), pltpu.SemaphoreType.DMA((n,))],
        compiler_params=pltpu.CompilerParams(has_side_effects=True, vmem_limit_bytes=VMEM_LIMIT),
    )(*arrays, rel_bias_padded)


def _inv_count(row, window):
    return 1.0 / jnp.minimum(row + 1, window).astype(F32)


def _pool_fwd(x, g_pre, g_post, w_in, w_group, scale, w_out, exchange_items):
    n_tok = x.shape[0]
    nt = n_tok // TM
    ex = _Exchange(exchange_items)

    def body(x_ref, gpre_ref, gpost_ref, win_ref, wg_ref, sc_ref, wout_ref, *rest):
        ex_in, rest = rest[:ex.n], rest[ex.n:]
        x1_ref, y_ref, z_ref, mixed_ref, mg_ref, prod_ref, ht_ref = rest[:7]
        ex_out, carry_ref, ex_sems = rest[7:7 + ex.n], rest[7 + ex.n], rest[8 + ex.n:]
        i = pl.program_id(0)

        @pl.when(i == 0)
        def _():
            ex.start(ex_in, ex_out, ex_sems)
            carry_ref[...] = jnp.zeros_like(carry_ref)

        xv = x_ref[...]
        r, xhat = _rms_fwd(xv)
        hf = xhat * gpre_ref[...]
        h = hf.astype(MXU)
        ht_ref[...] = hf.T.astype(MXU)
        row = i * TM + lax.broadcasted_iota(jnp.int32, (TM, 1), 0)
        y = None
        for g in range(N_GROUPS):
            cols = slice(g * GROUP, (g + 1) * GROUP)
            a = _dot(h, win_ref[g])
            z = _dot(h, win_ref[N_GROUPS + g])
            s = jnp.concatenate([carry_ref[g], a], axis=0)
            carry_ref[g] = a[TM - HALO:, :]
            w = 1
            while w < POOL_WINDOWS[g]:
                s = s + pltpu.roll(s, w, 0)
                w *= 2
            mixed = (s[HALO:, :] * _inv_count(row, POOL_WINDOWS[g]) - a).astype(MXU)
            mg = _dot(mixed, wg_ref[g])
            prod = (mg * sc_ref[:, cols] * (z * _sigmoid(z))).astype(MXU)
            z_ref[:, cols] = z
            mixed_ref[:, cols] = mixed
            mg_ref[:, cols] = mg
            prod_ref[:, cols] = prod
            part = _dot(prod, wout_ref[cols, :])
            y = part if y is None else y + part
        y_ref[...] = y
        _, yhat = _rms_fwd(y)
        x1_ref[...] = xv + yhat * gpost_ref[...]

        @pl.when(i == nt - 1)
        def _():
            ex.wait(ex_in, ex_out, ex_sems)

    tok = lambda w: pl.BlockSpec((TM, w), lambda i: (i, 0))
    return pl.pallas_call(
        body, name="pool_fwd", grid=(nt,),
        in_specs=[tok(D_MODEL), _const_spec((1, D_MODEL)), _const_spec((1, D_MODEL)),
                  _const_spec((N_DEV, D_MODEL, W_BLOCK)), _const_spec((N_GROUPS, GROUP, GROUP)),
                  _const_spec((1, POOL_WIDTH)), _const_spec((POOL_WIDTH, D_MODEL))] + ex.specs,
        out_specs=[tok(D_MODEL), tok(D_MODEL), tok(POOL_WIDTH), tok(POOL_WIDTH), tok(POOL_WIDTH), tok(POOL_WIDTH),
                   pl.BlockSpec((D_MODEL, TM), lambda i: (0, i))] + ex.specs,
        out_shape=[jax.ShapeDtypeStruct((n_tok, D_MODEL), F32), jax.ShapeDtypeStruct((n_tok, D_MODEL), F32),
                   jax.ShapeDtypeStruct((n_tok, POOL_WIDTH), F32), jax.ShapeDtypeStruct((n_tok, POOL_WIDTH), MXU),
                   jax.ShapeDtypeStruct((n_tok, POOL_WIDTH), F32), jax.ShapeDtypeStruct((n_tok, POOL_WIDTH), MXU),
                   jax.ShapeDtypeStruct((D_MODEL, n_tok), MXU)] + ex.out_shape,
        scratch_shapes=[pltpu.VMEM((N_GROUPS, HALO, GROUP), F32)] + ex.scratch,
        compiler_params=_params(1),
    )(x, g_pre, g_post, w_in, w_group, scale, w_out, *ex.arrays)


def _flush(acc_ref, out_hbm, stage_ref):
    for j in range(acc_ref.shape[0]):
        stage_ref[...] = acc_ref[j].astype(stage_ref.dtype)
        pltpu.sync_copy(stage_ref, out_hbm.at[j])


def _pool_bwd(dx1, y, z, mg, mixed, prod, g_post, scale, w_group, w_out, exchange_items):
    n_tok = dx1.shape[0]
    nt = n_tok // TM
    ex = _Exchange(exchange_items)

    def body(dx1_ref, y_ref, z_ref, mg_ref, mixed_ref, prod_ref, gpost_ref, sc_ref, wg_ref, wout_ref, *rest):
        ex_in, rest = rest[:ex.n], rest[ex.n:]
        du_ref, dsc_ref, dgpost_ref, dwg_hbm, dwout_hbm = rest[:5]
        ex_out, rest = rest[5:5 + ex.n], rest[5 + ex.n:]
        carry_ref, dwg_acc, dwout_acc, stage_g, stage_o = rest[:5]
        ex_sems = rest[5:]
        i = pl.program_id(0)

        @pl.when(i == 0)
        def _():
            ex.start(ex_in, ex_out, ex_sems)
            carry_ref[...] = jnp.zeros_like(carry_ref)
            dwg_acc[...] = jnp.zeros_like(dwg_acc)
            dwout_acc[...] = jnp.zeros_like(dwout_acc)
            dsc_ref[...] = jnp.zeros_like(dsc_ref)
            dgpost_ref[...] = jnp.zeros_like(dgpost_ref)

        dn = dx1_ref[...]
        r, yhat = _rms_fwd(y_ref[...])
        dgpost_ref[...] += jnp.sum(dn * yhat, axis=0, keepdims=True)
        dy = _rms_bwd(dn, yhat, r, gpost_ref[...]).astype(MXU)
        row = (nt - 1 - i) * TM + lax.broadcasted_iota(jnp.int32, (TM, 1), 0)
        n_ext = TM + HALO
        for g in range(N_GROUPS):
            cols = slice(g * GROUP, (g + 1) * GROUP)
            dwout_acc[g] += _dot_tn(prod_ref[:, cols], dy)
            dprod = _dot_nt(dy, wout_ref[cols, :])
            zv = z_ref[:, cols]
            sig = _sigmoid(zv)
            silu = zv * sig
            mgv = mg_ref[:, cols]
            sc = sc_ref[:, cols]
            dsc_ref[:, cols] += jnp.sum(dprod * silu * mgv, axis=0, keepdims=True)
            dmg = (dprod * silu * sc).astype(MXU)
            dz = dprod * (mgv * sc) * (sig * (1.0 + zv * (1.0 - sig)))
            dwg_acc[g] += _dot_tn(mixed_ref[:, cols], dmg)
            dmixed = _dot_nt(dmg, wg_ref[g])
            e = dmixed * _inv_count(row, POOL_WINDOWS[g])
            s = jnp.concatenate([e, carry_ref[g]], axis=0)
            carry_ref[g] = e[:HALO, :]
            w = 1
            while w < POOL_WINDOWS[g]:
                s = s + pltpu.roll(s, n_ext - w, 0)
                w *= 2
            du_ref[:, cols] = (s[:TM, :] - dmixed).astype(MXU)
            du_ref[:, POOL_WIDTH + g * GROUP:POOL_WIDTH + (g + 1) * GROUP] = dz.astype(MXU)

        @pl.when(i == nt - 1)
        def _():
            _flush(dwg_acc, dwg_hbm, stage_g)
            _flush(dwout_acc, dwout_hbm, stage_o)
            ex.wait(ex_in, ex_out, ex_sems)

    rev = lambda w: pl.BlockSpec((TM, w), lambda i: (nt - 1 - i, 0))
    any_spec = pl.BlockSpec(memory_space=pl.ANY)
    return pl.pallas_call(
        body, name="pool_bwd", grid=(nt,),
        in_specs=[rev(D_MODEL), rev(D_MODEL), rev(POOL_WIDTH), rev(POOL_WIDTH), rev(POOL_WIDTH), rev(POOL_WIDTH),
                  _const_spec((1, D_MODEL)), _const_spec((1, POOL_WIDTH)),
                  _const_spec((N_GROUPS, GROUP, GROUP)), _const_spec((POOL_WIDTH, D_MODEL))] + ex.specs,
        out_specs=[rev(2 * POOL_WIDTH), pl.BlockSpec((1, POOL_WIDTH), lambda i: (0, 0)),
                   pl.BlockSpec((1, D_MODEL), lambda i: (0, 0)), any_spec, any_spec] + ex.specs,
        out_shape=[jax.ShapeDtypeStruct((n_tok, 2 * POOL_WIDTH), MXU), jax.ShapeDtypeStruct((1, POOL_WIDTH), F32),
                   jax.ShapeDtypeStruct((1, D_MODEL), F32), jax.ShapeDtypeStruct((N_GROUPS, GROUP, GROUP), MXU),
                   jax.ShapeDtypeStruct((N_GROUPS, GROUP, D_MODEL), MXU)] + ex.out_shape,
        scratch_shapes=[pltpu.VMEM((N_GROUPS, HALO, GROUP), F32), pltpu.VMEM((N_GROUPS, GROUP, GROUP), F32),
                        pltpu.VMEM((N_GROUPS, GROUP, D_MODEL), F32), pltpu.VMEM((GROUP, GROUP), MXU),
                        pltpu.VMEM((GROUP, D_MODEL), MXU)] + ex.scratch,
        compiler_params=_params(1),
    )(dx1, y, z, mg, mixed, prod, g_post, scale, w_group, w_out, *ex.arrays)


def _in_proj_bwd(parts, x, dres, g_pre, w_in, name, tm, with_dw, exchange_items):
    n_tok = x.shape[0]
    nt = n_tok // tm
    half = D_MODEL // W_BLOCK
    ex = _Exchange(exchange_items)
    n_dw = 1 if with_dw else 0

    def body(p0, p1, p2, p3, x_ref, dres_ref, g_ref, w_ref, *rest):
        ex_in, rest = rest[:ex.n], rest[ex.n:]
        dx_ref, dg_ref = rest[:2]
        dw_hbm = rest[2:2 + n_dw]
        ex_out, rest = rest[2 + n_dw:2 + n_dw + ex.n], rest[2 + n_dw + ex.n:]
        dw_scratch, ex_sems = rest[:2 * n_dw], rest[2 * n_dw:]
        i = pl.program_id(0)

        @pl.when(i == 0)
        def _():
            ex.start(ex_in, ex_out, ex_sems)
            dg_ref[...] = jnp.zeros_like(dg_ref)
            if with_dw:
                dw_scratch[0][...] = jnp.zeros_like(dw_scratch[0])

        r, xhat = _rms_fwd(x_ref[...])
        g = g_ref[...]
        h = (xhat * g).astype(MXU)
        dh = None
        for p, part_ref in enumerate((p0, p1, p2, p3)):
            for jj in range(half):
                j = half * p + jj
                if len(part_ref.shape) == 3:
                    per_block = W_BLOCK // PAIR
                    du = jnp.concatenate([part_ref[jj * per_block + pp] for pp in range(per_block)], axis=1)
                else:
                    du = part_ref[:, jj * W_BLOCK:(jj + 1) * W_BLOCK]
                t = _dot_nt(du, w_ref[j])
                dh = t if dh is None else dh + t
                if with_dw:
                    dw_scratch[0][j] += _dot_tn(h, du)
        dg_ref[...] += jnp.sum(dh * xhat, axis=0, keepdims=True)
        dx_ref[...] = dres_ref[...] + _rms_bwd(dh, xhat, r, g)

        @pl.when(i == nt - 1)
        def _():
            if with_dw:
                _flush(dw_scratch[0], dw_hbm[0], dw_scratch[1])
            ex.wait(ex_in, ex_out, ex_sems)

    tok = pl.BlockSpec((tm, D_MODEL), lambda i: (i, 0))
    return pl.pallas_call(
        body, name=name, grid=(nt,),
        in_specs=[pl.BlockSpec(shape, m) for _, shape, m in parts]
        + [tok, tok, _const_spec((1, D_MODEL)), _const_spec((N_DEV, D_MODEL, W_BLOCK))] + ex.specs,
        out_specs=[tok, pl.BlockSpec((1, D_MODEL), lambda i: (0, 0))]
        + [pl.BlockSpec(memory_space=pl.ANY)] * n_dw + ex.specs,
        out_shape=[jax.ShapeDtypeStruct((n_tok, D_MODEL), F32), jax.ShapeDtypeStruct((1, D_MODEL), F32)]
        + [jax.ShapeDtypeStruct((N_DEV, D_MODEL, W_BLOCK), MXU)] * n_dw + ex.out_shape,
        scratch_shapes=[pltpu.VMEM((N_DEV, D_MODEL, W_BLOCK), F32), pltpu.VMEM((D_MODEL, W_BLOCK), MXU)][:2 * n_dw]
        + ex.scratch,
        compiler_params=_params(1),
    )(*[a for a, _, _ in parts], x, dres, g_pre, w_in, *ex.arrays)


def _flip(x, y, c, bits):
    return (1 - x if bits & 4 else x, 1 - y if bits & 2 else y, 1 - c if bits & 1 else c)


def _w_in_grad_scatter(h_t, du, exchange_items, name):
    n_tok = du.shape[0]
    ni = n_tok // TMB
    ex = _Exchange(exchange_items)
    me_out = 4 * lax.axis_index("x") + 2 * lax.axis_index("y") + lax.axis_index("c")
    order = (me_out ^ jnp.array(TAIL_OWNER_BITS, jnp.int32)).astype(jnp.int32)
    n_pairs = N_DEV // 2

    def body(order_ref, h_ref, du_ref, *rest):
        ex_in, rest = rest[:ex.n], rest[ex.n:]
        part_hbm, ex_out, rest = rest[0], rest[1:1 + ex.n], rest[1 + ex.n:]
        acc_ref, stage_ref, pair_ref, pair_send, pair_recv, chip_send, chip_recv = rest[:7]
        ex_sems = rest[7:]
        s = pl.program_id(0)
        i = pl.program_id(1)
        x, y, c = lax.axis_index("x"), lax.axis_index("y"), lax.axis_index("c")
        my_chip = 2 * x + y

        def to_sibling(j):
            return pltpu.make_async_remote_copy(
                src_ref=stage_ref.at[2 * j], dst_ref=pair_ref.at[j], send_sem=pair_send.at[j], recv_sem=pair_recv.at[j],
                device_id=(x, y, 1 - c), device_id_type=pl.DeviceIdType.MESH)

        def to_owner(j, from_chip):
            return pltpu.make_async_remote_copy(
                src_ref=stage_ref.at[2 * j + 1], dst_ref=part_hbm.at[from_chip],
                send_sem=chip_send.at[j], recv_sem=chip_recv.at[j],
                device_id=_flip(x, y, c, TAIL_OWNER_BITS[2 * j + 1]), device_id_type=pl.DeviceIdType.MESH)

        @pl.when((s == 0) & (i == 0))
        def _():
            ex.start(ex_in, ex_out, ex_sems)

        @pl.when(i == 0)
        def _():
            acc_ref[...] = jnp.zeros_like(acc_ref)

        acc_ref[...] += _dot(h_ref[:, pl.ds(pl.multiple_of(i * TMB, TMB), TMB)], du_ref[...])

        @pl.when(i == ni - 1)
        def _():
            for j in range(n_pairs):
                @pl.when(s == 2 * j)
                def _(j=j):
                    stage_ref[2 * j] = acc_ref[...].astype(MXU)
                    to_sibling(j).start()

                @pl.when(s == 2 * j + 1)
                def _(j=j):
                    to_sibling(j).wait_recv()
                    stage_ref[2 * j + 1] = (acc_ref[...] + pair_ref[j].astype(F32)).astype(MXU)
                    if j < n_pairs - 1:
                        to_owner(j, my_chip).start()
                    else:
                        pltpu.sync_copy(stage_ref.at[2 * j + 1], part_hbm.at[my_chip])
                        for jj in range(n_pairs - 1):
                            sx, sy, _ = _flip(x, y, c, TAIL_OWNER_BITS[2 * jj + 1])
                            to_owner(jj, 2 * sx + sy).wait_recv()
                            to_owner(jj, my_chip).wait_send()
                        for jj in range(n_pairs):
                            to_sibling(jj).wait_send()
                        ex.wait(ex_in, ex_out, ex_sems)

    grid_spec = pltpu.PrefetchScalarGridSpec(
        num_scalar_prefetch=1, grid=(N_DEV, ni),
        in_specs=[pl.BlockSpec((D_MODEL, n_tok), lambda s, i, order: (0, 0), pipeline_mode=pl.Buffered(1)),
                  pl.BlockSpec((TMB, W_BLOCK), lambda s, i, order: (i, order[s]))] + ex.specs,
        out_specs=[pl.BlockSpec(memory_space=pl.ANY)] + ex.specs,
        scratch_shapes=[pltpu.VMEM((D_MODEL, W_BLOCK), F32), pltpu.VMEM((N_DEV, D_MODEL, W_BLOCK), MXU),
                        pltpu.VMEM((n_pairs, D_MODEL, W_BLOCK), MXU),
                        pltpu.SemaphoreType.DMA((n_pairs,)), pltpu.SemaphoreType.DMA((n_pairs,)),
                        pltpu.SemaphoreType.DMA((n_pairs - 1,)), pltpu.SemaphoreType.DMA((n_pairs - 1,))] + ex.scratch)
    return pl.pallas_call(
        body, name=name, grid_spec=grid_spec,
        out_shape=[jax.ShapeDtypeStruct((n_pairs, D_MODEL, W_BLOCK), MXU)] + ex.out_shape,
        compiler_params=_params(2),
    )(order, h_t, du, *ex.arrays)


def _rel_onehot():
    rel = lax.broadcasted_iota(jnp.int32, (REL_PAD, ROLL_W), 0)
    col = lax.broadcasted_iota(jnp.int32, (REL_PAD, ROLL_W), 1)
    return (rel == jnp.minimum(BAND + MAX_REL - col, 2 * MAX_REL)).astype(MXU)


def _split3(v):
    hi = v.astype(MXU)
    r1 = v - hi.astype(F32)
    mid = r1.astype(MXU)
    lo = (r1 - mid.astype(F32)).astype(MXU)
    return hi, mid, lo


def _fill_bias_table(rb_ref, out_ref):
    onehot = _rel_onehot()
    base = None
    for term in _split3(rb_ref[...]):
        t = _dot(term, onehot)
        base = t if base is None else base + t
    qi = lax.broadcasted_iota(jnp.int32, (CHUNK, ROLL_W), 0)
    kk = lax.broadcasted_iota(jnp.int32, (CHUNK, TKW), 1)
    for h in range(N_HEADS):
        t = jnp.broadcast_to(base[h:h + 1, :], (CHUNK, ROLL_W))
        for bit in range(6):
            t = jnp.where(((qi >> bit) & 1) == 1, pltpu.roll(t, 1 << bit, 1), t)
        for rr in range(TM // CHUNK):
            shifted = pltpu.roll(t, (CHUNK * rr - CHUNK) % ROLL_W, 1)[:, :TKW]
            band = kk - CHUNK * rr
            out_ref[h, rr * CHUNK:(rr + 1) * CHUNK, :] = jnp.where((band >= 0) & (band < BAND), shifted, NEG)


def _bias_grad(dtab):
    def body(dt_ref, out_ref, dbase_ref):
        qi = lax.broadcasted_iota(jnp.int32, (CHUNK, ROLL_W), 0)
        zeros = jnp.zeros((CHUNK, ROLL_W - TKW), F32)
        for h in range(N_HEADS):
            t = None
            for rr in range(TM // CHUNK):
                blk = jnp.concatenate([dt_ref[h, rr * CHUNK:(rr + 1) * CHUNK, :], zeros], axis=1)
                blk = pltpu.roll(blk, (CHUNK - CHUNK * rr) % ROLL_W, 1)
                t = blk if t is None else t + blk
            for bit in range(6):
                t = jnp.where(((qi >> bit) & 1) == 1, pltpu.roll(t, ROLL_W - (1 << bit), 1), t)
            dbase_ref[h:h + 1, :] = jnp.sum(t, axis=0, keepdims=True)
        onehot = _rel_onehot()
        acc = None
        for term in _split3(dbase_ref[...]):
            t = _dot_nt(term, onehot)
            acc = t if acc is None else acc + t
        out_ref[...] = acc

    return pl.pallas_call(
        body, name="bias_grad", out_shape=jax.ShapeDtypeStruct((N_HEADS, REL_PAD), F32),
        scratch_shapes=[pltpu.VMEM((N_HEADS, ROLL_W), F32)],
        compiler_params=pltpu.CompilerParams(vmem_limit_bytes=VMEM_LIMIT),
    )(dtab)


def _att_in(x1, g_pre, w_in):
    n_tok = x1.shape[0]
    tm = TM_WIDE
    nt = n_tok // tm
    lead = PAD // tm
    per_block = W_BLOCK // PAIR

    def body(x_ref, g_ref, w_ref, q_ref, k_ref, v_ref, z_ref):
        i = pl.program_id(0)

        @pl.when(i < lead)
        def _():
            k_ref[...] = jnp.zeros_like(k_ref)
            v_ref[...] = jnp.zeros_like(v_ref)

        @pl.when(i >= lead)
        def _():
            _, xhat = _rms_fwd(x_ref[...])
            h = (xhat * g_ref[...]).astype(MXU)
            for j in range(N_DEV):
                u = _dot(h, w_ref[j])
                if j >= 6:
                    z_ref[:, (j % 2) * W_BLOCK:(j % 2 + 1) * W_BLOCK] = u
                    continue
                dst = (q_ref, k_ref, v_ref)[j // 2]
                if j < 2:
                    u = u * QK_SCALE
                for pp in range(per_block):
                    dst[(j % 2) * per_block + pp] = u[:, pp * PAIR:(pp + 1) * PAIR].astype(MXU)

    late = pl.BlockSpec((tm, D_MODEL), lambda i: (jnp.maximum(i - lead, 0), 0))
    late3 = pl.BlockSpec((N_PAIRS, tm, PAIR), lambda i: (0, jnp.maximum(i - lead, 0), 0))
    padded3 = pl.BlockSpec((N_PAIRS, tm, PAIR), lambda i: (0, i, 0))
    return pl.pallas_call(
        body, name="att_in", grid=(nt + lead,),
        in_specs=[late, _const_spec((1, D_MODEL)), _const_spec((N_DEV, D_MODEL, W_BLOCK))],
        out_specs=[late3, padded3, padded3, late],
        out_shape=[jax.ShapeDtypeStruct((N_PAIRS, n_tok, PAIR), MXU),
                   jax.ShapeDtypeStruct((N_PAIRS, n_tok + PAD, PAIR), MXU),
                   jax.ShapeDtypeStruct((N_PAIRS, n_tok + PAD, PAIR), MXU),
                   jax.ShapeDtypeStruct((n_tok, ATT_WIDTH), F32)],
        compiler_params=_params(1),
    )(x1, g_pre, w_in)


def _head_masks():
    lane = lax.broadcasted_iota(jnp.int32, (1, PAIR), 1)
    first = lane < HEAD_DIM
    return first, jnp.logical_not(first)


def _pair_loop(one_pair, unroll):
    def loop_pass(t, carry):
        for u in range(unroll):
            one_pair(t * unroll + u)
        return carry

    lax.fori_loop(0, PAIRS_PER_STEP // unroll, loop_pass, 0)


def _scores(qh, k_refs, bias_ref, p, hh, tile, masked):
    ss = []
    for b in range(KB):
        s = _dot_nt(qh, k_refs[b][p]) + bias_ref[2 * p + hh, :, b * TM:(b + 1) * TM]
        if masked:
            s = s + jnp.where(tile + b < PAD // TM, NEG, 0.0).astype(F32)
        ss.append(s)
    return ss


def _pair_specs(index_map):
    return pl.BlockSpec((PAIRS_PER_STEP, TM, PAIR), index_map)


def _att_fwd(q, kpad, vpad, bias_tab):
    n_tok = q.shape[1]
    nt = n_tok // TM
    lead = PAD // TM

    def body(q_ref, k0, k1, k2, v0, v1, v2, bias_ref, o_ref, lse_ref):
        i = pl.program_id(1)
        masks = _head_masks()

        def pairs(masked):
            def one_pair(p):
                qv = q_ref[p]
                outs, ms = [], []
                for hh, mask in enumerate(masks):
                    qh = jnp.where(mask, qv, jnp.zeros_like(qv))
                    ss = _scores(qh, (k0, k1, k2), bias_ref, p, hh, i, masked)
                    m = None
                    for s in ss:
                        mb = jnp.max(s, axis=-1, keepdims=True)
                        m = mb if m is None else jnp.maximum(m, mb)
                    out = None
                    for b, v_ref in enumerate((v0, v1, v2)):
                        vb = v_ref[p]
                        t = _dot(jnp.exp(ss[b] - m).astype(MXU), jnp.where(mask, vb, jnp.ones_like(vb)))
                        out = t if out is None else out + t
                    outs.append(out)
                    ms.append(m)
                num = jnp.where(masks[0], outs[0], outs[1])
                den = jnp.where(masks[0], pltpu.roll(outs[0], HEAD_DIM, 1), pltpu.roll(outs[1], HEAD_DIM, 1))
                o_ref[p] = num / den
                lse_ref[p] = jnp.where(masks[0], ms[0], ms[1]) + jnp.log(den)

            _pair_loop(one_pair, FWD_UNROLL)

        @pl.when(i < lead)
        def _():
            pairs(True)

        @pl.when(i >= lead)
        def _():
            pairs(False)

    qspec = _pair_specs(lambda g, i: (g, i, 0))
    kspecs = [_pair_specs(functools.partial(lambda g, i, b: (g, i + b, 0), b=b)) for b in range(KB)]
    vspecs = [_pair_specs(functools.partial(lambda g, i, b: (g, i + b, 0), b=b)) for b in range(KB)]
    shape = jax.ShapeDtypeStruct((N_PAIRS, n_tok, PAIR), F32)
    return pl.pallas_call(
        body, name="att_fwd", grid=(N_PAIRS // PAIRS_PER_STEP, nt),
        in_specs=[qspec] + kspecs + vspecs + [pl.BlockSpec((2 * PAIRS_PER_STEP, TM, TKW), lambda g, i: (g, 0, 0))],
        out_specs=[qspec, qspec], out_shape=[shape, shape],
        compiler_params=_params(2),
    )(q, kpad, kpad, kpad, vpad, vpad, vpad, bias_tab)


def _att_bwd(q, kpad, vpad, do, o, lse, bias_tab):
    n_tok = q.shape[1]
    nt = n_tok // TM
    lead = PAD // TM

    def body(q_ref, do_ref, o_ref, lse_ref, k0, k1, k2, v0, v1, v2, bias_ref,
             dq_ref, dk_ref, dv_ref, dtab_ref, rk0, rk1, rv0, rv1):
        i = pl.program_id(1)
        masks = _head_masks()

        @pl.when(i == 0)
        def _():
            for ref in (rk0, rk1, rv0, rv1):
                ref[...] = jnp.zeros_like(ref)
            dtab_ref[...] = jnp.zeros_like(dtab_ref)

        def pairs(masked):
            def one_pair(p):
                qv = q_ref[p]
                dov = do_ref[p]
                doo = dov.astype(F32) * o_ref[p]
                lse_pair = lse_ref[p]
                dks = [None] * KB
                dvs = [None] * KB
                dqs = []
                for hh, mask in enumerate(masks):
                    qh = jnp.where(mask, qv, jnp.zeros_like(qv))
                    doh = jnp.where(mask, dov, jnp.zeros_like(dov))
                    dsum = jnp.sum(jnp.where(mask, doo, 0.0), axis=-1, keepdims=True)
                    lse_h = lse_pair[:, hh * HEAD_DIM:hh * HEAD_DIM + 1]
                    ss = _scores(qh, (k0, k1, k2), bias_ref, p, hh, i, masked)
                    dq = None
                    for b, (k_ref, v_ref) in enumerate(zip((k0, k1, k2), (v0, v1, v2))):
                        prob = jnp.exp(ss[b] - lse_h)
                        ds = prob * (_dot_nt(doh, v_ref[p]) - dsum)
                        dtab_ref[2 * p + hh, :, b * TM:(b + 1) * TM] += ds
                        dsb = ds.astype(MXU)
                        t = lax.dot_general(k_ref[p], dsb, (((0,), (1,)), ((), ())), preferred_element_type=F32)
                        dq = t if dq is None else dq + t
                        t = _dot_tn(qh, dsb)
                        dks[b] = t if dks[b] is None else dks[b] + t
                        t = _dot_tn(doh, prob.astype(MXU))
                        dvs[b] = t if dvs[b] is None else dvs[b] + t
                    dqs.append(dq)
                first_rows = lax.broadcasted_iota(jnp.int32, (PAIR, 1), 0) < HEAD_DIM
                dq_ref[p] = (jnp.where(first_rows, dqs[0], dqs[1]).T * QK_SCALE).astype(MXU)
                dk_ref[p] = (rk0[p] + dks[0].T).astype(MXU)
                dv_ref[p] = (rv0[p] + dvs[0].T).astype(MXU)
                rk0[p] = rk1[p] + dks[1].T
                rv0[p] = rv1[p] + dvs[1].T
                rk1[p] = dks[2].T
                rv1[p] = dvs[2].T

            _pair_loop(one_pair, BWD_UNROLL)

        @pl.when(i < lead)
        def _():
            pairs(True)

        @pl.when((i >= lead) & (i < nt))
        def _():
            pairs(False)

        @pl.when(i >= nt)
        def _():
            dk_ref[...] = rk0[...].astype(MXU)
            dv_ref[...] = rv0[...].astype(MXU)
            rk0[...] = rk1[...]
            rv0[...] = rv1[...]

    last = nt - 1
    qspec = _pair_specs(lambda g, i: (g, jnp.minimum(i, last), 0))
    kspecs = [_pair_specs(functools.partial(lambda g, i, b: (g, jnp.minimum(i, last) + b, 0), b=b)) for b in range(KB)]
    vspecs = [_pair_specs(functools.partial(lambda g, i, b: (g, jnp.minimum(i, last) + b, 0), b=b)) for b in range(KB)]
    pspec = _pair_specs(lambda g, i: (g, i, 0))
    tspec = pl.BlockSpec((2 * PAIRS_PER_STEP, TM, TKW), lambda g, i: (g, 0, 0))
    ring = pltpu.VMEM((PAIRS_PER_STEP, TM, PAIR), F32)
    return pl.pallas_call(
        body, name="att_bwd", grid=(N_PAIRS // PAIRS_PER_STEP, nt + KB - 1),
        in_specs=[qspec, qspec, qspec, qspec] + kspecs + vspecs + [tspec],
        out_specs=[qspec, pspec, pspec, tspec],
        out_shape=[jax.ShapeDtypeStruct((N_PAIRS, n_tok, PAIR), MXU),
                   jax.ShapeDtypeStruct((N_PAIRS, n_tok + PAD, PAIR), MXU),
                   jax.ShapeDtypeStruct((N_PAIRS, n_tok + PAD, PAIR), MXU),
                   jax.ShapeDtypeStruct((N_HEADS, TM, TKW), F32)],
        scratch_shapes=[ring, ring, ring, ring],
        compiler_params=_params(2),
    )(q, do, o, lse, kpad, kpad, kpad, vpad, vpad, vpad, bias_tab)


def _att_out(o, z, x1, target, g_post, w_out):
    n_tok = z.shape[0]
    nt = n_tok // TM

    def body(o_ref, z_ref, x1_ref, tgt_ref, gpost_ref, w_ref,
             loss_ref, dx2_ref, do_ref, dz_ref, dgpost_ref, dw_hbm, acc_ref, loss_acc, stage_ref):
        i = pl.program_id(0)

        @pl.when(i == 0)
        def _():
            acc_ref[...] = jnp.zeros_like(acc_ref)
            loss_acc[...] = jnp.zeros_like(loss_acc)
            dgpost_ref[...] = jnp.zeros_like(dgpost_ref)

        ov = jnp.concatenate([o_ref[p] for p in range(N_PAIRS)], axis=1)
        zv = z_ref[...]
        sig = _sigmoid(zv)
        silu = zv * sig
        gated = (ov * silu).astype(MXU)
        y = _dot(gated, w_ref[...])
        r, yhat = _rms_fwd(y)
        gpost = gpost_ref[...]
        diff = x1_ref[...] + yhat * gpost - tgt_ref[...]
        loss_acc[...] += jnp.sum(diff * diff, axis=0, keepdims=True)
        dn = diff * (1.0 / D_MODEL)
        dx2_ref[...] = dn
        dgpost_ref[...] += jnp.sum(dn * yhat, axis=0, keepdims=True)
        dy = _rms_bwd(dn, yhat, r, gpost).astype(MXU)
        for j in range(ATT_WIDTH // W_BLOCK):
            acc_ref[j] += _dot_tn(gated[:, j * W_BLOCK:(j + 1) * W_BLOCK], dy)
        dgated = _dot_nt(dy, w_ref[...])
        dob = (dgated * silu).astype(MXU)
        for p in range(N_PAIRS):
            do_ref[p] = dob[:, p * PAIR:(p + 1) * PAIR]
        dz_ref[...] = (dgated * ov * (sig * (1.0 + zv * (1.0 - sig)))).astype(MXU)

        @pl.when(i == nt - 1)
        def _():
            total = jnp.sum(loss_acc[...], axis=-1, keepdims=True) * (0.5 / D_MODEL)
            loss_ref[...] = jnp.broadcast_to(total, loss_ref.shape)
            _flush(acc_ref, dw_hbm, stage_ref)

    tok = pl.BlockSpec((TM, D_MODEL), lambda i: (i, 0))
    tok3 = pl.BlockSpec((N_PAIRS, TM, PAIR), lambda i: (0, i, 0))
    return pl.pallas_call(
        body, name="att_out", grid=(nt,),
        in_specs=[tok3, tok, tok, tok, _const_spec((1, D_MODEL)), _const_spec((ATT_WIDTH, D_MODEL))],
        out_specs=[pl.BlockSpec((1, 128), lambda i: (0, 0)), tok, tok3, tok,
                   pl.BlockSpec((1, D_MODEL), lambda i: (0, 0)), pl.BlockSpec(memory_space=pl.ANY)],
        out_shape=[jax.ShapeDtypeStruct((1, 128), F32), jax.ShapeDtypeStruct((n_tok, D_MODEL), F32),
                   jax.ShapeDtypeStruct((N_PAIRS, n_tok, PAIR), MXU), jax.ShapeDtypeStruct((n_tok, ATT_WIDTH), MXU),
                   jax.ShapeDtypeStruct((1, D_MODEL), F32),
                   jax.ShapeDtypeStruct((ATT_WIDTH // W_BLOCK, W_BLOCK, D_MODEL), MXU)],
        scratch_shapes=[pltpu.VMEM((ATT_WIDTH // W_BLOCK, W_BLOCK, D_MODEL), F32), pltpu.VMEM((1, D_MODEL), F32),
                        pltpu.VMEM((W_BLOCK, D_MODEL), MXU)],
        compiler_params=_params(1),
    )(o, z, x1, target, g_post, w_out)


def _adamw(parts, w, m, v, name):
    rows, cols = w.shape
    tr = min(rows, 256)

    def body(p_ref, w_ref, m_ref, v_ref, g_ref, d_ref, mo_ref, vo_ref):
        g = p_ref[0].astype(F32)
        for s in range(1, parts.shape[0]):
            g = g + p_ref[s].astype(F32)
        m_new = ADAM_B1 * m_ref[...] + (1.0 - ADAM_B1) * g
        v_new = ADAM_B2 * v_ref[...] + (1.0 - ADAM_B2) * (g * g)
        m_hat = m_new / (1.0 - ADAM_B1 ** ADAM_STEP)
        v_hat = v_new / (1.0 - ADAM_B2 ** ADAM_STEP)
        g_ref[...] = g
        d_ref[...] = -ADAM_LR * (m_hat / (jnp.sqrt(v_hat) + ADAM_EPS) + ADAM_WD * w_ref[...])
        mo_ref[...] = m_new
        vo_ref[...] = v_new

    blk = pl.BlockSpec((tr, cols), lambda i: (i, 0))
    shape = jax.ShapeDtypeStruct((rows, cols), F32)
    return pl.pallas_call(
        body, name=name, grid=(rows // tr,),
        in_specs=[pl.BlockSpec((parts.shape[0], tr, cols), lambda i: (0, i, 0)), blk, blk, blk],
        out_specs=[blk, blk, blk, blk], out_shape=[shape, shape, shape, shape],
        compiler_params=_params(1),
    )(parts, w, m, v)


SMALL_ROWS = 16
LOSS_AT = (6, N_REL)


def _pack_small(norm_pre, norm_post, pool_scale, rel_bias_padded):
    return jnp.concatenate([norm_pre, norm_post, pool_scale.reshape(2, D_MODEL),
                            rel_bias_padded.reshape(SMALL_ROWS - 6, D_MODEL)], axis=0)


def _unpack_small(packed):
    rel = packed[6:].reshape(N_HEADS, REL_PAD)[:, :N_REL]
    return packed[0:2], packed[2:4], packed[4:6].reshape(1, POOL_WIDTH), rel.reshape(1, N_HEADS, N_REL)


def _pad_rel(rel_bias):
    return jnp.pad(rel_bias.reshape(N_HEADS, N_REL), ((0, 0), (0, REL_PAD - N_REL)))


def kernel(x, norm_pre, norm_post, pool_w_in, pool_w_group, pool_scale, pool_w_out, att_w_in, att_rel_bias, att_w_out, loss_target, m_norm_pre, m_norm_post, m_pool_w_in, m_pool_w_group, m_pool_scale, m_pool_w_out, m_att_w_in, m_att_rel_bias, m_att_w_out, v_norm_pre, v_norm_post, v_pool_w_in, v_pool_w_group, v_pool_scale, v_pool_w_out, v_att_w_in, v_att_rel_bias, v_att_w_out):
    xt = x[0]
    target = loss_target[0]
    n_tok = xt.shape[0]
    lead = PAD // TM
    rows_g = GROUP // N_DEV

    rel_padded = _pad_rel(att_rel_bias[0])
    gathered = _gather_two_level([pool_w_in[0].astype(MXU), pool_w_group[0].astype(MXU), pool_w_out[0].astype(MXU)],
                                 rel_padded, "gather_pool_weights")
    bias_tab = gathered[3]
    w_in_p = gathered[0]
    w_group = gathered[1].transpose(1, 0, 2, 3).reshape(N_GROUPS, GROUP, GROUP)
    w_out_p = gathered[2].reshape(POOL_WIDTH, D_MODEL)

    x1, y0, z0, mixed, mg, prod, h0_t, w_in_a, w_out_a = _pool_fwd(
        xt, norm_pre[0:1], norm_post[0:1], w_in_p, w_group, pool_scale, w_out_p,
        [(att_w_in[0].astype(MXU), False), (att_w_out[0].astype(MXU), False)])
    w_out_a = w_out_a.reshape(ATT_WIDTH, D_MODEL)
    q, kpad, vpad, z1 = _att_in(x1, norm_pre[1:2], w_in_a)
    o, lse = _att_fwd(q, kpad, vpad, bias_tab)
    loss_part, dx2, do, dz1, d_gpost1, d_w_out_a = _att_out(o, z1, x1, target, norm_post[1:2], w_out_a)
    dq, dkpad, dvpad, dtab = _att_bwd(q, kpad, vpad, do, o, lse, bias_tab)
    d_rel = _bias_grad(dtab)
    pairs = (N_PAIRS, TM, PAIR)
    flat = (TM, D_MODEL)
    dx1, d_gpre1, d_w_in_a = _in_proj_bwd(
        [(dq, pairs, lambda i: (0, i, 0)), (dkpad, pairs, lambda i: (0, i + lead, 0)),
         (dvpad, pairs, lambda i: (0, i + lead, 0)), (dz1, flat, lambda i: (i, 0))],
        x1, dx2, norm_pre[1:2], w_in_a, "att_in_bwd", TM, True, [])
    du0, d_scale, d_gpost0, d_w_group, d_w_out_p, part_w_in_a, part_w_out_a = _pool_bwd(
        dx1, y0, z0, mg, mixed, prod, norm_post[0:1], pool_scale, w_group, w_out_p,
        [(d_w_in_a, True), (d_w_out_a.reshape(N_DEV, ATT_WIDTH // N_DEV, D_MODEL), True)])
    col = lambda p: (lambda i: (i, p))
    grad_x, d_gpre0, part_w_group, part_w_out_p = _in_proj_bwd(
        [(du0, (TM_WIDE, D_MODEL), col(p)) for p in range(4)], xt, dx1, norm_pre[0:1], w_in_p, "pool_in_bwd", TM_WIDE,
        False,
        [(d_w_group.reshape(N_GROUPS, N_DEV, rows_g, GROUP).transpose(1, 0, 2, 3), True),
         (d_w_out_p.reshape(N_DEV, POOL_WIDTH // N_DEV, D_MODEL), True)])
    d_small = _pack_small(jnp.concatenate([d_gpre0, d_gpre1], axis=0), jnp.concatenate([d_gpost0, d_gpost1], axis=0),
                          d_scale, d_rel).at[LOSS_AT].set(loss_part[0, 0])
    part_w_in_p, part_small = _w_in_grad_scatter(h0_t, du0, [(d_small, False)], "pool_w_in_grad")

    def update(part, w, m, v, name):
        shape = w.shape
        flat = lambda a: a.reshape(-1, shape[-1])
        outs = _adamw(part.reshape(part.shape[0], -1, shape[-1]), flat(w), flat(m), flat(v), name)
        return [a.reshape(shape) for a in outs]

    u_att_w_in = update(part_w_in_a, att_w_in, m_att_w_in, v_att_w_in, "adamw_att_w_in")
    u_att_w_out = update(part_w_out_a, att_w_out, m_att_w_out, v_att_w_out, "adamw_att_w_out")
    u_pool_w_group = update(part_w_group, pool_w_group, m_pool_w_group, v_pool_w_group, "adamw_pool_w_group")
    u_pool_w_out = update(part_w_out_p, pool_w_out, m_pool_w_out, v_pool_w_out, "adamw_pool_w_out")
    u_pool_w_in = update(part_w_in_p, pool_w_in, m_pool_w_in, v_pool_w_in, "adamw_pool_w_in")
    small = _adamw(part_small, _pack_small(norm_pre, norm_post, pool_scale, rel_padded),
                   _pack_small(m_norm_pre, m_norm_post, m_pool_scale, _pad_rel(m_att_rel_bias[0])),
                   _pack_small(v_norm_pre, v_norm_post, v_pool_scale, _pad_rel(v_att_rel_bias[0])), "adamw_small")
    u_small = [_unpack_small(a) for a in small]

    loss = small[0][LOSS_AT]
    outs = [loss, grad_x.reshape(1, n_tok, D_MODEL)]
    for kind in range(4):
        outs += [u_small[kind][0], u_small[kind][1], u_pool_w_in[kind], u_pool_w_group[kind], u_small[kind][2],
                 u_pool_w_out[kind], u_att_w_in[kind], u_small[kind][3], u_att_w_out[kind]]
    return tuple(outs)
```

```python
import functools

import jax
import jax.numpy as jnp
from jax import lax
from jax.experimental import pallas as pl
from jax.experimental.pallas import tpu as pltpu

F32 = jnp.float32
MXU = jnp.bfloat16

D_MODEL = 1024
POOL_WIDTH = 2048
POOL_WINDOWS = (2, 4, 8, 16)
N_GROUPS = 4
GROUP = 512
HALO = 16
N_HEADS = 16
HEAD_DIM = 64
CHUNK = 64
LEFT_CHUNKS = 8
PAD = LEFT_CHUNKS * CHUNK
BAND = PAD + CHUNK
MAX_REL = 256
N_REL = 2 * MAX_REL + 1
REL_PAD = 640
ATT_WIDTH = 1024
PAIR = 2 * HEAD_DIM
N_PAIRS = N_HEADS // 2
N_DEV = 8
W_BLOCK = 512
RMS_EPS = 1e-6
QK_SCALE = 0.125
NEG = -1e30

TM = 256
TM_WIDE = 512
TMB = 1024
TAIL_OWNER_BITS = (7, 6, 5, 4, 3, 2, 1, 0)
KB = 3
TKW = KB * TM
PAIRS_PER_STEP = 4
FWD_UNROLL = 4
BWD_UNROLL = 2
ROLL_W = 1024

VMEM_LIMIT = 56 * 1024 * 1024

ADAM_LR = 0.001
ADAM_B1 = 0.9
ADAM_B2 = 0.999
ADAM_EPS = 1e-08
ADAM_WD = 0.01
ADAM_STEP = 10

NT_DIMS = (((1,), (1,)), ((), ()))
TN_DIMS = (((0,), (0,)), ((), ()))


def _params(n_grid):
    return pltpu.CompilerParams(dimension_semantics=("arbitrary",) * n_grid, vmem_limit_bytes=VMEM_LIMIT)


def _const_spec(shape):
    nd = len(shape)
    return pl.BlockSpec(shape, lambda *_: (0,) * nd, pipeline_mode=pl.Buffered(1))


def _dot(a, b):
    return jnp.dot(a, b, preferred_element_type=F32)


def _dot_nt(a, b):
    return lax.dot_general(a, b, NT_DIMS, preferred_element_type=F32)


def _dot_tn(a, b):
    return lax.dot_general(a, b, TN_DIMS, preferred_element_type=F32)


def _sigmoid(z):
    return 1.0 / (1.0 + jnp.exp(-z))


def _rms_fwd(xv):
    r = lax.rsqrt(jnp.mean(xv * xv, axis=-1, keepdims=True) + RMS_EPS)
    return r, xv * r


def _rms_bwd(dn, xhat, r, g):
    dng = dn * g
    return r * (dng - xhat * jnp.mean(dng * xhat, axis=-1, keepdims=True))


class _Exchange:
    def __init__(self, items):
        self.arrays = [a for a, _ in items]
        self.scatter = [s for _, s in items]
        self.n = len(items)
        self.out_shape = [jax.ShapeDtypeStruct((N_DEV,) + tuple(a.shape[1:] if s else a.shape), a.dtype)
                          for a, s in items]
        self.specs = [pl.BlockSpec(memory_space=pl.ANY)] * self.n
        self.scratch = ([pltpu.SemaphoreType.DMA((N_DEV - 1, self.n)), pltpu.SemaphoreType.DMA((N_DEV - 1, self.n)),
                         pltpu.SemaphoreType.DMA((self.n,))] if self.n else [])

    def _copies(self, ins, outs, sems, with_receives):
        send_sems, recv_sems, local_sems = sems
        x, y, c = lax.axis_index("x"), lax.axis_index("y"), lax.axis_index("c")
        me = 4 * x + 2 * y + c

        def src(t, slot):
            return ins[t].at[slot] if self.scatter[t] else ins[t]

        local = [pltpu.make_async_copy(src(t, me), outs[t].at[me], local_sems.at[t]) for t in range(self.n)]
        sends, recvs = [], []
        for k in range(1, N_DEV):
            px = 1 - x if k & 4 else x
            py = 1 - y if k & 2 else y
            pc = 1 - c if k & 1 else c
            peer = 4 * px + 2 * py + pc
            for t in range(self.n):
                common = dict(src_ref=src(t, peer), send_sem=send_sems.at[k - 1, t], recv_sem=recv_sems.at[k - 1, t],
                              device_id=(px, py, pc), device_id_type=pl.DeviceIdType.MESH)
                sends.append(pltpu.make_async_remote_copy(dst_ref=outs[t].at[me], **common))
                if with_receives:
                    recvs.append(pltpu.make_async_remote_copy(dst_ref=outs[t].at[peer], **common))
        return local, sends, recvs

    def start(self, ins, outs, sems):
        if self.n:
            local, sends, _ = self._copies(ins, outs, sems, False)
            for cp in local + sends:
                cp.start()

    def wait(self, ins, outs, sems):
        if self.n:
            local, sends, recvs = self._copies(ins, outs, sems, True)
            for cp in recvs:
                cp.wait_recv()
            for cp in sends:
                cp.wait_send()
            for cp in local:
                cp.wait()


def _exchange(items, name):
    ex = _Exchange(items)
    n = ex.n

    def body(*refs):
        ins, outs, sems = refs[:n], refs[n:2 * n], refs[2 * n:]
        ex.start(ins, outs, sems)
        ex.wait(ins, outs, sems)

    return pl.pallas_call(
        body, name=name, out_shape=ex.out_shape, in_specs=ex.specs, out_specs=ex.specs, scratch_shapes=ex.scratch,
        compiler_params=pltpu.CompilerParams(has_side_effects=True),
    )(*ex.arrays)


def _gather_two_level(arrays, rel_bias_padded, name):
    n = len(arrays)
    out_shape = [jax.ShapeDtypeStruct((N_DEV,) + a.shape, a.dtype) for a in arrays]
    own_sib, own_x, own_y, half_via_x, half_via_y, x_sib, y_sib, diag_sib = range(8)

    def body(*refs):
        ins, rb_ref, outs, tab_ref = refs[:n], refs[n], refs[n + 1:2 * n + 1], refs[2 * n + 1]
        send_sems, recv_sems, local_sems = refs[2 * n + 2:]
        x, y, c = lax.axis_index("x"), lax.axis_index("y"), lax.axis_index("c")
        me, sibling = (x, y, c), (x, y, 1 - c)
        x_nbr, y_nbr, diag = (1 - x, y, c), (x, 1 - y, c), (1 - x, 1 - y, c)

        def slot(pos):
            return 4 * pos[0] + 2 * pos[1] + pos[2]

        def other_core(pos):
            return (pos[0], pos[1], 1 - pos[2])

        def copy(kind, t, block, to, own=False, half=None):
            dst = outs[t].at[slot(block)]
            if half is not None:
                rows = arrays[t].shape[0] // 2
                dst = dst.at[pl.ds(half * rows, rows)]
            return pltpu.make_async_remote_copy(
                src_ref=ins[t] if own else dst, dst_ref=dst,
                send_sem=send_sems.at[kind, t], recv_sem=recv_sems.at[kind, t],
                device_id=to, device_id_type=pl.DeviceIdType.MESH)

        local = [pltpu.make_async_copy(ins[t], outs[t].at[slot(me)], local_sems.at[t]) for t in range(n)]
        sent = [copy(kind, t, me, to, own=True)
                for t in range(n) for kind, to in ((own_x, x_nbr), (own_y, y_nbr), (own_sib, sibling))]
        for cp in local + sent:
            cp.start()
        _fill_bias_table(rb_ref, tab_ref)

        def start(cp):
            cp.start()
            sent.append(cp)

        for t in range(n):
            copy(own_x, t, x_nbr, me).wait_recv()
            start(copy(half_via_x, t, x_nbr, y_nbr, half=0))
            start(copy(x_sib, t, x_nbr, sibling))
        for t in range(n):
            copy(own_y, t, y_nbr, me).wait_recv()
            start(copy(half_via_y, t, y_nbr, x_nbr, half=1))
            start(copy(y_sib, t, y_nbr, sibling))
        for t in range(n):
            copy(half_via_x, t, diag, me, half=0).wait_recv()
            copy(half_via_y, t, diag, me, half=1).wait_recv()
            start(copy(diag_sib, t, diag, sibling))
        for t in range(n):
            for kind, block in ((own_sib, sibling), (x_sib, other_core(x_nbr)), (y_sib, other_core(y_nbr)),
                                (diag_sib, other_core(diag))):
                copy(kind, t, block, me).wait_recv()
        for cp in sent:
            cp.wait_send()
        for cp in local:
            cp.wait()

    any_spec = pl.BlockSpec(memory_space=pl.ANY)
    vmem_spec = pl.BlockSpec(memory_space=pltpu.VMEM)
    return pl.pallas_call(
        body, name=name, out_shape=out_shape + [jax.ShapeDtypeStruct((N_HEADS, TM, TKW), F32)],
        in_specs=[any_spec] * n + [vmem_spec], out_specs=[any_spec] * n + [vmem_spec],
        scratch_shapes=[pltpu.SemaphoreType.DMA((8, n)), pltpu.SemaphoreType.DMA((8, n)), pltpu.SemaphoreType.DMA((n,))],
        compiler_params=pltpu.CompilerParams(has_side_effects=True, vmem_limit_bytes=VMEM_LIMIT),
    )(*arrays, rel_bias_padded)


def _inv_count(row, window):
    return 1.0 / jnp.minimum(row + 1, window).astype(F32)


def _pool_fwd(x, g_pre, g_post, w_in, w_group, scale, w_out, exchange_items):
    n_tok = x.shape[0]
    nt = n_tok // TM
    ex = _Exchange(exchange_items)

    def body(x_ref, gpre_ref, gpost_ref, win_ref, wg_ref, sc_ref, wout_ref, *rest):
        ex_in, rest = rest[:ex.n], rest[ex.n:]
        x1_ref, y_ref, z_ref, mixed_ref, mg_ref, prod_ref, ht_ref = rest[:7]
        ex_out, carry_ref, ex_sems = rest[7:7 + ex.n], rest[7 + ex.n], rest[8 + ex.n:]
        i = pl.program_id(0)

        @pl.when(i == 0)
        def _():
            ex.start(ex_in, ex_out, ex_sems)
            carry_ref[...] = jnp.zeros_like(carry_ref)

        xv = x_ref[...]
        r, xhat = _rms_fwd(xv)
        hf = xhat * gpre_ref[...]
        h = hf.astype(MXU)
        ht_ref[...] = hf.T.astype(MXU)
        row = i * TM + lax.broadcasted_iota(jnp.int32, (TM, 1), 0)
        y = None
        for g in range(N_GROUPS):
            cols = slice(g * GROUP, (g + 1) * GROUP)
            a = _dot(h, win_ref[g])
            z = _dot(h, win_ref[N_GROUPS + g])
            s = jnp.concatenate([carry_ref[g], a], axis=0)
            carry_ref[g] = a[TM - HALO:, :]
            w = 1
            while w < POOL_WINDOWS[g]:
                s = s + pltpu.roll(s, w, 0)
                w *= 2
            mixed = (s[HALO:, :] * _inv_count(row, POOL_WINDOWS[g]) - a).astype(MXU)
            mg = _dot(mixed, wg_ref[g])
            prod = (mg * sc_ref[:, cols] * (z * _sigmoid(z))).astype(MXU)
            z_ref[:, cols] = z
            mixed_ref[:, cols] = mixed
            mg_ref[:, cols] = mg
            prod_ref[:, cols] = prod
            part = _dot(prod, wout_ref[cols, :])
            y = part if y is None else y + part
        y_ref[...] = y
        _, yhat = _rms_fwd(y)
        x1_ref[...] = xv + yhat * gpost_ref[...]

        @pl.when(i == nt - 1)
        def _():
            ex.wait(ex_in, ex_out, ex_sems)

    tok = lambda w: pl.BlockSpec((TM, w), lambda i: (i, 0))
    return pl.pallas_call(
        body, name="pool_fwd", grid=(nt,),
        in_specs=[tok(D_MODEL), _const_spec((1, D_MODEL)), _const_spec((1, D_MODEL)),
                  _const_spec((N_DEV, D_MODEL, W_BLOCK)), _const_spec((N_GROUPS, GROUP, GROUP)),
                  _const_spec((1, POOL_WIDTH)), _const_spec((POOL_WIDTH, D_MODEL))] + ex.specs,
        out_specs=[tok(D_MODEL), tok(D_MODEL), tok(POOL_WIDTH), tok(POOL_WIDTH), tok(POOL_WIDTH), tok(POOL_WIDTH),
                   pl.BlockSpec((D_MODEL, TM), lambda i: (0, i))] + ex.specs,
        out_shape=[jax.ShapeDtypeStruct((n_tok, D_MODEL), F32), jax.ShapeDtypeStruct((n_tok, D_MODEL), F32),
                   jax.ShapeDtypeStruct((n_tok, POOL_WIDTH), F32), jax.ShapeDtypeStruct((n_tok, POOL_WIDTH), MXU),
                   jax.ShapeDtypeStruct((n_tok, POOL_WIDTH), F32), jax.ShapeDtypeStruct((n_tok, POOL_WIDTH), MXU),
                   jax.ShapeDtypeStruct((D_MODEL, n_tok), MXU)] + ex.out_shape,
        scratch_shapes=[pltpu.VMEM((N_GROUPS, HALO, GROUP), F32)] + ex.scratch,
        compiler_params=_params(1),
    )(x, g_pre, g_post, w_in, w_group, scale, w_out, *ex.arrays)


def _flush(acc_ref, out_hbm, stage_ref):
    for j in range(acc_ref.shape[0]):
        stage_ref[...] = acc_ref[j].astype(stage_ref.dtype)
        pltpu.sync_copy(stage_ref, out_hbm.at[j])


def _pool_bwd(dx1, y, z, mg, mixed, prod, g_post, scale, w_group, w_out, exchange_items):
    n_tok = dx1.shape[0]
    nt = n_tok // TM
    ex = _Exchange(exchange_items)

    def body(dx1_ref, y_ref, z_ref, mg_ref, mixed_ref, prod_ref, gpost_ref, sc_ref, wg_ref, wout_ref, *rest):
        ex_in, rest = rest[:ex.n], rest[ex.n:]
        du_ref, dsc_ref, dgpost_ref, dwg_hbm, dwout_hbm = rest[:5]
        ex_out, rest = rest[5:5 + ex.n], rest[5 + ex.n:]
        carry_ref, dwg_acc, dwout_acc, stage_g, stage_o = rest[:5]
        ex_sems = rest[5:]
        i = pl.program_id(0)

        @pl.when(i == 0)
        def _():
            ex.start(ex_in, ex_out, ex_sems)
            carry_ref[...] = jnp.zeros_like(carry_ref)
            dwg_acc[...] = jnp.zeros_like(dwg_acc)
            dwout_acc[...] = jnp.zeros_like(dwout_acc)
            dsc_ref[...] = jnp.zeros_like(dsc_ref)
            dgpost_ref[...] = jnp.zeros_like(dgpost_ref)

        dn = dx1_ref[...]
        r, yhat = _rms_fwd(y_ref[...])
        dgpost_ref[...] += jnp.sum(dn * yhat, axis=0, keepdims=True)
        dy = _rms_bwd(dn, yhat, r, gpost_ref[...]).astype(MXU)
        row = (nt - 1 - i) * TM + lax.broadcasted_iota(jnp.int32, (TM, 1), 0)
        n_ext = TM + HALO
        for g in range(N_GROUPS):
            cols = slice(g * GROUP, (g + 1) * GROUP)
            dwout_acc[g] += _dot_tn(prod_ref[:, cols], dy)
            dprod = _dot_nt(dy, wout_ref[cols, :])
            zv = z_ref[:, cols]
            sig = _sigmoid(zv)
            silu = zv * sig
            mgv = mg_ref[:, cols]
            sc = sc_ref[:, cols]
            dsc_ref[:, cols] += jnp.sum(dprod * silu * mgv, axis=0, keepdims=True)
            dmg = (dprod * silu * sc).astype(MXU)
            dz = dprod * (mgv * sc) * (sig * (1.0 + zv * (1.0 - sig)))
            dwg_acc[g] += _dot_tn(mixed_ref[:, cols], dmg)
            dmixed = _dot_nt(dmg, wg_ref[g])
            e = dmixed * _inv_count(row, POOL_WINDOWS[g])
            s = jnp.concatenate([e, carry_ref[g]], axis=0)
            carry_ref[g] = e[:HALO, :]
            w = 1
            while w < POOL_WINDOWS[g]:
                s = s + pltpu.roll(s, n_ext - w, 0)
                w *= 2
            du_ref[:, cols] = (s[:TM, :] - dmixed).astype(MXU)
            du_ref[:, POOL_WIDTH + g * GROUP:POOL_WIDTH + (g + 1) * GROUP] = dz.astype(MXU)

        @pl.when(i == nt - 1)
        def _():
            _flush(dwg_acc, dwg_hbm, stage_g)
            _flush(dwout_acc, dwout_hbm, stage_o)
            ex.wait(ex_in, ex_out, ex_sems)

    rev = lambda w: pl.BlockSpec((TM, w), lambda i: (nt - 1 - i, 0))
    any_spec = pl.BlockSpec(memory_space=pl.ANY)
    return pl.pallas_call(
        body, name="pool_bwd", grid=(nt,),
        in_specs=[rev(D_MODEL), rev(D_MODEL), rev(POOL_WIDTH), rev(POOL_WIDTH), rev(POOL_WIDTH), rev(POOL_WIDTH),
                  _const_spec((1, D_MODEL)), _const_spec((1, POOL_WIDTH)),
                  _const_spec((N_GROUPS, GROUP, GROUP)), _const_spec((POOL_WIDTH, D_MODEL))] + ex.specs,
        out_specs=[rev(2 * POOL_WIDTH), pl.BlockSpec((1, POOL_WIDTH), lambda i: (0, 0)),
                   pl.BlockSpec((1, D_MODEL), lambda i: (0, 0)), any_spec, any_spec] + ex.specs,
        out_shape=[jax.ShapeDtypeStruct((n_tok, 2 * POOL_WIDTH), MXU), jax.ShapeDtypeStruct((1, POOL_WIDTH), F32),
                   jax.ShapeDtypeStruct((1, D_MODEL), F32), jax.ShapeDtypeStruct((N_GROUPS, GROUP, GROUP), MXU),
                   jax.ShapeDtypeStruct((N_GROUPS, GROUP, D_MODEL), MXU)] + ex.out_shape,
        scratch_shapes=[pltpu.VMEM((N_GROUPS, HALO, GROUP), F32), pltpu.VMEM((N_GROUPS, GROUP, GROUP), F32),
                        pltpu.VMEM((N_GROUPS, GROUP, D_MODEL), F32), pltpu.VMEM((GROUP, GROUP), MXU),
                        pltpu.VMEM((GROUP, D_MODEL), MXU)] + ex.scratch,
        compiler_params=_params(1),
    )(dx1, y, z, mg, mixed, prod, g_post, scale, w_group, w_out, *ex.arrays)


def _in_proj_bwd(parts, x, dres, g_pre, w_in, name, tm, with_dw, exchange_items):
    n_tok = x.shape[0]
    nt = n_tok // tm
    half = D_MODEL // W_BLOCK
    ex = _Exchange(exchange_items)
    n_dw = 1 if with_dw else 0

    def body(p0, p1, p2, p3, x_ref, dres_ref, g_ref, w_ref, *rest):
        ex_in, rest = rest[:ex.n], rest[ex.n:]
        dx_ref, dg_ref = rest[:2]
        dw_hbm = rest[2:2 + n_dw]
        ex_out, rest = rest[2 + n_dw:2 + n_dw + ex.n], rest[2 + n_dw + ex.n:]
        dw_scratch, ex_sems = rest[:2 * n_dw], rest[2 * n_dw:]
        i = pl.program_id(0)

        @pl.when(i == 0)
        def _():
            ex.start(ex_in, ex_out, ex_sems)
            dg_ref[...] = jnp.zeros_like(dg_ref)
            if with_dw:
                dw_scratch[0][...] = jnp.zeros_like(dw_scratch[0])

        r, xhat = _rms_fwd(x_ref[...])
        g = g_ref[...]
        h = (xhat * g).astype(MXU)
        dh = None
        for p, part_ref in enumerate((p0, p1, p2, p3)):
            for jj in range(half):
                j = half * p + jj
                if len(part_ref.shape) == 3:
                    per_block = W_BLOCK // PAIR
                    du = jnp.concatenate([part_ref[jj * per_block + pp] for pp in range(per_block)], axis=1)
                else:
                    du = part_ref[:, jj * W_BLOCK:(jj + 1) * W_BLOCK]
                t = _dot_nt(du, w_ref[j])
                dh = t if dh is None else dh + t
                if with_dw:
                    dw_scratch[0][j] += _dot_tn(h, du)
        dg_ref[...] += jnp.sum(dh * xhat, axis=0, keepdims=True)
        dx_ref[...] = dres_ref[...] + _rms_bwd(dh, xhat, r, g)

        @pl.when(i == nt - 1)
        def _():
            if with_dw:
                _flush(dw_scratch[0], dw_hbm[0], dw_scratch[1])
            ex.wait(ex_in, ex_out, ex_sems)

    tok = pl.BlockSpec((tm, D_MODEL), lambda i: (i, 0))
    return pl.pallas_call(
        body, name=name, grid=(nt,),
        in_specs=[pl.BlockSpec(shape, m) for _, shape, m in parts]
        + [tok, tok, _const_spec((1, D_MODEL)), _const_spec((N_DEV, D_MODEL, W_BLOCK))] + ex.specs,
        out_specs=[tok, pl.BlockSpec((1, D_MODEL), lambda i: (0, 0))]
        + [pl.BlockSpec(memory_space=pl.ANY)] * n_dw + ex.specs,
        out_shape=[jax.ShapeDtypeStruct((n_tok, D_MODEL), F32), jax.ShapeDtypeStruct((1, D_MODEL), F32)]
        + [jax.ShapeDtypeStruct((N_DEV, D_MODEL, W_BLOCK), MXU)] * n_dw + ex.out_shape,
        scratch_shapes=[pltpu.VMEM((N_DEV, D_MODEL, W_BLOCK), F32), pltpu.VMEM((D_MODEL, W_BLOCK), MXU)][:2 * n_dw]
        + ex.scratch,
        compiler_params=_params(1),
    )(*[a for a, _, _ in parts], x, dres, g_pre, w_in, *ex.arrays)


def _flip(x, y, c, bits):
    return (1 - x if bits & 4 else x, 1 - y if bits & 2 else y, 1 - c if bits & 1 else c)


def _w_in_grad_scatter(h_t, du, exchange_items, name):
    n_tok = du.shape[0]
    ni = n_tok // TMB
    ex = _Exchange(exchange_items)
    me_out = 4 * lax.axis_index("x") + 2 * lax.axis_index("y") + lax.axis_index("c")
    order = (me_out ^ jnp.array(TAIL_OWNER_BITS, jnp.int32)).astype(jnp.int32)
    n_pairs = N_DEV // 2

    def body(order_ref, h_ref, du_ref, *rest):
        ex_in, rest = rest[:ex.n], rest[ex.n:]
        part_hbm, ex_out, rest = rest[0], rest[1:1 + ex.n], rest[1 + ex.n:]
        acc_ref, stage_ref, pair_ref, pair_send, pair_recv, chip_send, chip_recv = rest[:7]
        ex_sems = rest[7:]
        s = pl.program_id(0)
        i = pl.program_id(1)
        x, y, c = lax.axis_index("x"), lax.axis_index("y"), lax.axis_index("c")
        my_chip = 2 * x + y

        def to_sibling(j):
            return pltpu.make_async_remote_copy(
                src_ref=stage_ref.at[2 * j], dst_ref=pair_ref.at[j], send_sem=pair_send.at[j], recv_sem=pair_recv.at[j],
                device_id=(x, y, 1 - c), device_id_type=pl.DeviceIdType.MESH)

        def to_owner(j, from_chip):
            return pltpu.make_async_remote_copy(
                src_ref=stage_ref.at[2 * j + 1], dst_ref=part_hbm.at[from_chip],
                send_sem=chip_send.at[j], recv_sem=chip_recv.at[j],
                device_id=_flip(x, y, c, TAIL_OWNER_BITS[2 * j + 1]), device_id_type=pl.DeviceIdType.MESH)

        @pl.when((s == 0) & (i == 0))
        def _():
            ex.start(ex_in, ex_out, ex_sems)

        @pl.when(i == 0)
        def _():
            acc_ref[...] = jnp.zeros_like(acc_ref)

        acc_ref[...] += _dot(h_ref[:, pl.ds(pl.multiple_of(i * TMB, TMB), TMB)], du_ref[...])

        @pl.when(i == ni - 1)
        def _():
            for j in range(n_pairs):
                @pl.when(s == 2 * j)
                def _(j=j):
                    stage_ref[2 * j] = acc_ref[...].astype(MXU)
                    to_sibling(j).start()

                @pl.when(s == 2 * j + 1)
                def _(j=j):
                    to_sibling(j).wait_recv()
                    stage_ref[2 * j + 1] = (acc_ref[...] + pair_ref[j].astype(F32)).astype(MXU)
                    if j < n_pairs - 1:
                        to_owner(j, my_chip).start()
                    else:
                        pltpu.sync_copy(stage_ref.at[2 * j + 1], part_hbm.at[my_chip])
                        for jj in range(n_pairs - 1):
                            sx, sy, _ = _flip(x, y, c, TAIL_OWNER_BITS[2 * jj + 1])
                            to_owner(jj, 2 * sx + sy).wait_recv()
                            to_owner(jj, my_chip).wait_send()
                        for jj in range(n_pairs):
                            to_sibling(jj).wait_send()
                        ex.wait(ex_in, ex_out, ex_sems)

    grid_spec = pltpu.PrefetchScalarGridSpec(
        num_scalar_prefetch=1, grid=(N_DEV, ni),
        in_specs=[pl.BlockSpec((D_MODEL, n_tok), lambda s, i, order: (0, 0), pipeline_mode=pl.Buffered(1)),
                  pl.BlockSpec((TMB, W_BLOCK), lambda s, i, order: (i, order[s]))] + ex.specs,
        out_specs=[pl.BlockSpec(memory_space=pl.ANY)] + ex.specs,
        scratch_shapes=[pltpu.VMEM((D_MODEL, W_BLOCK), F32), pltpu.VMEM((N_DEV, D_MODEL, W_BLOCK), MXU),
                        pltpu.VMEM((n_pairs, D_MODEL, W_BLOCK), MXU),
                        pltpu.SemaphoreType.DMA((n_pairs,)), pltpu.SemaphoreType.DMA((n_pairs,)),
                        pltpu.SemaphoreType.DMA((n_pairs - 1,)), pltpu.SemaphoreType.DMA((n_pairs - 1,))] + ex.scratch)
    return pl.pallas_call(
        body, name=name, grid_spec=grid_spec,
        out_shape=[jax.ShapeDtypeStruct((n_pairs, D_MODEL, W_BLOCK), MXU)] + ex.out_shape,
        compiler_params=_params(2),
    )(order, h_t, du, *ex.arrays)


def _rel_onehot():
    rel = lax.broadcasted_iota(jnp.int32, (REL_PAD, ROLL_W), 0)
    col = lax.broadcasted_iota(jnp.int32, (REL_PAD, ROLL_W), 1)
    return (rel == jnp.minimum(BAND + MAX_REL - col, 2 * MAX_REL)).astype(MXU)


def _split3(v):
    hi = v.astype(MXU)
    r1 = v - hi.astype(F32)
    mid = r1.astype(MXU)
    lo = (r1 - mid.astype(F32)).astype(MXU)
    return hi, mid, lo


def _fill_bias_table(rb_ref, out_ref):
    onehot = _rel_onehot()
    base = None
    for term in _split3(rb_ref[...]):
        t = _dot(term, onehot)
        base = t if base is None else base + t
    qi = lax.broadcasted_iota(jnp.int32, (CHUNK, ROLL_W), 0)
    kk = lax.broadcasted_iota(jnp.int32, (CHUNK, TKW), 1)
    for h in range(N_HEADS):
        t = jnp.broadcast_to(base[h:h + 1, :], (CHUNK, ROLL_W))
        for bit in range(6):
            t = jnp.where(((qi >> bit) & 1) == 1, pltpu.roll(t, 1 << bit, 1), t)
        for rr in range(TM // CHUNK):
            shifted = pltpu.roll(t, (CHUNK * rr - CHUNK) % ROLL_W, 1)[:, :TKW]
            band = kk - CHUNK * rr
            out_ref[h, rr * CHUNK:(rr + 1) * CHUNK, :] = jnp.where((band >= 0) & (band < BAND), shifted, NEG)


def _bias_grad(dtab):
    def body(dt_ref, out_ref, dbase_ref):
        qi = lax.broadcasted_iota(jnp.int32, (CHUNK, ROLL_W), 0)
        zeros = jnp.zeros((CHUNK, ROLL_W - TKW), F32)
        for h in range(N_HEADS):
            t = None
            for rr in range(TM // CHUNK):
                blk = jnp.concatenate([dt_ref[h, rr * CHUNK:(rr + 1) * CHUNK, :], zeros], axis=1)
                blk = pltpu.roll(blk, (CHUNK - CHUNK * rr) % ROLL_W, 1)
                t = blk if t is None else t + blk
            for bit in range(6):
                t = jnp.where(((qi >> bit) & 1) == 1, pltpu.roll(t, ROLL_W - (1 << bit), 1), t)
            dbase_ref[h:h + 1, :] = jnp.sum(t, axis=0, keepdims=True)
        onehot = _rel_onehot()
        acc = None
        for term in _split3(dbase_ref[...]):
            t = _dot_nt(term, onehot)
            acc = t if acc is None else acc + t
        out_ref[...] = acc

    return pl.pallas_call(
        body, name="bias_grad", out_shape=jax.ShapeDtypeStruct((N_HEADS, REL_PAD), F32),
        scratch_shapes=[pltpu.VMEM((N_HEADS, ROLL_W), F32)],
        compiler_params=pltpu.CompilerParams(vmem_limit_bytes=VMEM_LIMIT),
    )(dtab)


def _att_in(x1, g_pre, w_in):
    n_tok = x1.shape[0]
    tm = TM_WIDE
    nt = n_tok // tm
    lead = PAD // tm
    per_block = W_BLOCK // PAIR

    def body(x_ref, g_ref, w_ref, q_ref, k_ref, v_ref, z_ref):
        i = pl.program_id(0)

        @pl.when(i < lead)
        def _():
            k_ref[...] = jnp.zeros_like(k_ref)
            v_ref[...] = jnp.zeros_like(v_ref)

        @pl.when(i >= lead)
        def _():
            _, xhat = _rms_fwd(x_ref[...])
            h = (xhat * g_ref[...]).astype(MXU)
            for j in range(N_DEV):
                u = _dot(h, w_ref[j])
                if j >= 6:
                    z_ref[:, (j % 2) * W_BLOCK:(j % 2 + 1) * W_BLOCK] = u
                    continue
                dst = (q_ref, k_ref, v_ref)[j // 2]
                if j < 2:
                    u = u * QK_SCALE
                for pp in range(per_block):
                    dst[(j % 2) * per_block + pp] = u[:, pp * PAIR:(pp + 1) * PAIR].astype(MXU)

    late = pl.BlockSpec((tm, D_MODEL), lambda i: (jnp.maximum(i - lead, 0), 0))
    late3 = pl.BlockSpec((N_PAIRS, tm, PAIR), lambda i: (0, jnp.maximum(i - lead, 0), 0))
    padded3 = pl.BlockSpec((N_PAIRS, tm, PAIR), lambda i: (0, i, 0))
    return pl.pallas_call(
        body, name="att_in", grid=(nt + lead,),
        in_specs=[late, _const_spec((1, D_MODEL)), _const_spec((N_DEV, D_MODEL, W_BLOCK))],
        out_specs=[late3, padded3, padded3, late],
        out_shape=[jax.ShapeDtypeStruct((N_PAIRS, n_tok, PAIR), MXU),
                   jax.ShapeDtypeStruct((N_PAIRS, n_tok + PAD, PAIR), MXU),
                   jax.ShapeDtypeStruct((N_PAIRS, n_tok + PAD, PAIR), MXU),
                   jax.ShapeDtypeStruct((n_tok, ATT_WIDTH), F32)],
        compiler_params=_params(1),
    )(x1, g_pre, w_in)


def _head_masks():
    lane = lax.broadcasted_iota(jnp.int32, (1, PAIR), 1)
    first = lane < HEAD_DIM
    return first, jnp.logical_not(first)


def _pair_loop(one_pair, unroll):
    def loop_pass(t, carry):
        for u in range(unroll):
            one_pair(t * unroll + u)
        return carry

    lax.fori_loop(0, PAIRS_PER_STEP // unroll, loop_pass, 0)


def _scores(qh, k_refs, bias_ref, p, hh, tile, masked):
    ss = []
    for b in range(KB):
        s = _dot_nt(qh, k_refs[b][p]) + bias_ref[2 * p + hh, :, b * TM:(b + 1) * TM]
        if masked:
            s = s + jnp.where(tile + b < PAD // TM, NEG, 0.0).astype(F32)
        ss.append(s)
    return ss


def _pair_specs(index_map):
    return pl.BlockSpec((PAIRS_PER_STEP, TM, PAIR), index_map)


def _att_fwd(q, kpad, vpad, bias_tab):
    n_tok = q.shape[1]
    nt = n_tok // TM
    lead = PAD // TM

    def body(q_ref, k0, k1, k2, v0, v1, v2, bias_ref, o_ref, lse_ref):
        i = pl.program_id(1)
        masks = _head_masks()

        def pairs(masked):
            def one_pair(p):
                qv = q_ref[p]
                outs, ms = [], []
                for hh, mask in enumerate(masks):
                    qh = jnp.where(mask, qv, jnp.zeros_like(qv))
                    ss = _scores(qh, (k0, k1, k2), bias_ref, p, hh, i, masked)
                    m = None
                    for s in ss:
                        mb = jnp.max(s, axis=-1, keepdims=True)
                        m = mb if m is None else jnp.maximum(m, mb)
                    out = None
                    for b, v_ref in enumerate((v0, v1, v2)):
                        vb = v_ref[p]
                        t = _dot(jnp.exp(ss[b] - m).astype(MXU), jnp.where(mask, vb, jnp.ones_like(vb)))
                        out = t if out is None else out + t
                    outs.append(out)
                    ms.append(m)
                num = jnp.where(masks[0], outs[0], outs[1])
                den = jnp.where(masks[0], pltpu.roll(outs[0], HEAD_DIM, 1), pltpu.roll(outs[1], HEAD_DIM, 1))
                o_ref[p] = num / den
                lse_ref[p] = jnp.where(masks[0], ms[0], ms[1]) + jnp.log(den)

            _pair_loop(one_pair, FWD_UNROLL)

        @pl.when(i < lead)
        def _():
            pairs(True)

        @pl.when(i >= lead)
        def _():
            pairs(False)

    qspec = _pair_specs(lambda g, i: (g, i, 0))
    kspecs = [_pair_specs(functools.partial(lambda g, i, b: (g, i + b, 0), b=b)) for b in range(KB)]
    vspecs = [_pair_specs(functools.partial(lambda g, i, b: (g, i + b, 0), b=b)) for b in range(KB)]
    shape = jax.ShapeDtypeStruct((N_PAIRS, n_tok, PAIR), F32)
    return pl.pallas_call(
        body, name="att_fwd", grid=(N_PAIRS // PAIRS_PER_STEP, nt),
        in_specs=[qspec] + kspecs + vspecs + [pl.BlockSpec((2 * PAIRS_PER_STEP, TM, TKW), lambda g, i: (g, 0, 0))],
        out_specs=[qspec, qspec], out_shape=[shape, shape],
        compiler_params=_params(2),
    )(q, kpad, kpad, kpad, vpad, vpad, vpad, bias_tab)


def _att_bwd(q, kpad, vpad, do, o, lse, bias_tab):
    n_tok = q.shape[1]
    nt = n_tok // TM
    lead = PAD // TM

    def body(q_ref, do_ref, o_ref, lse_ref, k0, k1, k2, v0, v1, v2, bias_ref,
             dq_ref, dk_ref, dv_ref, dtab_ref, rk0, rk1, rv0, rv1):
        i = pl.program_id(1)
        masks = _head_masks()

        @pl.when(i == 0)
        def _():
            for ref in (rk0, rk1, rv0, rv1):
                ref[...] = jnp.zeros_like(ref)
            dtab_ref[...] = jnp.zeros_like(dtab_ref)

        def pairs(masked):
            def one_pair(p):
                qv = q_ref[p]
                dov = do_ref[p]
                doo = dov.astype(F32) * o_ref[p]
                lse_pair = lse_ref[p]
                dks = [None] * KB
                dvs = [None] * KB
                dqs = []
                for hh, mask in enumerate(masks):
                    qh = jnp.where(mask, qv, jnp.zeros_like(qv))
                    doh = jnp.where(mask, dov, jnp.zeros_like(dov))
                    dsum = jnp.sum(jnp.where(mask, doo, 0.0), axis=-1, keepdims=True)
                    lse_h = lse_pair[:, hh * HEAD_DIM:hh * HEAD_DIM + 1]
                    ss = _scores(qh, (k0, k1, k2), bias_ref, p, hh, i, masked)
                    dq = None
                    for b, (k_ref, v_ref) in enumerate(zip((k0, k1, k2), (v0, v1, v2))):
                        prob = jnp.exp(ss[b] - lse_h)
                        ds = prob * (_dot_nt(doh, v_ref[p]) - dsum)
                        dtab_ref[2 * p + hh, :, b * TM:(b + 1) * TM] += ds
                        dsb = ds.astype(MXU)
                        t = lax.dot_general(k_ref[p], dsb, (((0,), (1,)), ((), ())), preferred_element_type=F32)
                        dq = t if dq is None else dq + t
                        t = _dot_tn(qh, dsb)
                        dks[b] = t if dks[b] is None else dks[b] + t
                        t = _dot_tn(doh, prob.astype(MXU))
                        dvs[b] = t if dvs[b] is None else dvs[b] + t
                    dqs.append(dq)
                first_rows = lax.broadcasted_iota(jnp.int32, (PAIR, 1), 0) < HEAD_DIM
                dq_ref[p] = (jnp.where(first_rows, dqs[0], dqs[1]).T * QK_SCALE).astype(MXU)
                dk_ref[p] = (rk0[p] + dks[0].T).astype(MXU)
                dv_ref[p] = (rv0[p] + dvs[0].T).astype(MXU)
                rk0[p] = rk1[p] + dks[1].T
                rv0[p] = rv1[p] + dvs[1].T
                rk1[p] = dks[2].T
                rv1[p] = dvs[2].T

            _pair_loop(one_pair, BWD_UNROLL)

        @pl.when(i < lead)
        def _():
            pairs(True)

        @pl.when((i >= lead) & (i < nt))
        def _():
            pairs(False)

        @pl.when(i >= nt)
        def _():
            dk_ref[...] = rk0[...].astype(MXU)
            dv_ref[...] = rv0[...].astype(MXU)
            rk0[...] = rk1[...]
            rv0[...] = rv1[...]

    last = nt - 1
    qspec = _pair_specs(lambda g, i: (g, jnp.minimum(i, last), 0))
    kspecs = [_pair_specs(functools.partial(lambda g, i, b: (g, jnp.minimum(i, last) + b, 0), b=b)) for b in range(KB)]
    vspecs = [_pair_specs(functools.partial(lambda g, i, b: (g, jnp.minimum(i, last) + b, 0), b=b)) for b in range(KB)]
    pspec = _pair_specs(lambda g, i: (g, i, 0))
    tspec = pl.BlockSpec((2 * PAIRS_PER_STEP, TM, TKW), lambda g, i: (g, 0, 0))
    ring = pltpu.VMEM((PAIRS_PER_STEP, TM, PAIR), F32)
    return pl.pallas_call(
        body, name="att_bwd", grid=(N_PAIRS // PAIRS_PER_STEP, nt + KB - 1),
        in_specs=[qspec, qspec, qspec, qspec] + kspecs + vspecs + [tspec],
        out_specs=[qspec, pspec, pspec, tspec],
        out_shape=[jax.ShapeDtypeStruct((N_PAIRS, n_tok, PAIR), MXU),
                   jax.ShapeDtypeStruct((N_PAIRS, n_tok + PAD, PAIR), MXU),
                   jax.ShapeDtypeStruct((N_PAIRS, n_tok + PAD, PAIR), MXU),
                   jax.ShapeDtypeStruct((N_HEADS, TM, TKW), F32)],
        scratch_shapes=[ring, ring, ring, ring],
        compiler_params=_params(2),
    )(q, do, o, lse, kpad, kpad, kpad, vpad, vpad, vpad, bias_tab)


def _att_out(o, z, x1, target, g_post, w_out):
    n_tok = z.shape[0]
    nt = n_tok // TM_WIDE

    def body(o_ref, z_ref, x1_ref, tgt_ref, gpost_ref, w_ref,
             loss_ref, dx2_ref, do_ref, dz_ref, dgpost_ref, dw_hbm, acc_ref, loss_acc, stage_ref):
        i = pl.program_id(0)

        @pl.when(i == 0)
        def _():
            acc_ref[...] = jnp.zeros_like(acc_ref)
            loss_acc[...] = jnp.zeros_like(loss_acc)
            dgpost_ref[...] = jnp.zeros_like(dgpost_ref)

        ov = jnp.concatenate([o_ref[p] for p in range(N_PAIRS)], axis=1)
        zv = z_ref[...]
        sig = _sigmoid(zv)
        silu = zv * sig
        gated = (ov * silu).astype(MXU)
        y = _dot(gated, w_ref[...])
        r, yhat = _rms_fwd(y)
        gpost = gpost_ref[...]
        diff = x1_ref[...] + yhat * gpost - tgt_ref[...]
        loss_acc[...] += jnp.sum(diff * diff, axis=0, keepdims=True)
        dn = diff * (1.0 / D_MODEL)
        dx2_ref[...] = dn
        dgpost_ref[...] += jnp.sum(dn * yhat, axis=0, keepdims=True)
        dy = _rms_bwd(dn, yhat, r, gpost).astype(MXU)
        for j in range(ATT_WIDTH // W_BLOCK):
            acc_ref[j] += _dot_tn(gated[:, j * W_BLOCK:(j + 1) * W_BLOCK], dy)
        dgated = _dot_nt(dy, w_ref[...])
        dob = (dgated * silu).astype(MXU)
        for p in range(N_PAIRS):
            do_ref[p] = dob[:, p * PAIR:(p + 1) * PAIR]
        dz_ref[...] = (dgated * ov * (sig * (1.0 + zv * (1.0 - sig)))).astype(MXU)

        @pl.when(i == nt - 1)
        def _():
            total = jnp.sum(loss_acc[...], axis=-1, keepdims=True) * (0.5 / D_MODEL)
            loss_ref[...] = jnp.broadcast_to(total, loss_ref.shape)
            _flush(acc_ref, dw_hbm, stage_ref)

    tok = pl.BlockSpec((TM_WIDE, D_MODEL), lambda i: (i, 0))
    tok3 = pl.BlockSpec((N_PAIRS, TM_WIDE, PAIR), lambda i: (0, i, 0))
    return pl.pallas_call(
        body, name="att_out", grid=(nt,),
        in_specs=[tok3, tok, tok, tok, _const_spec((1, D_MODEL)), _const_spec((ATT_WIDTH, D_MODEL))],
        out_specs=[pl.BlockSpec((1, 128), lambda i: (0, 0)), tok, tok3, tok,
                   pl.BlockSpec((1, D_MODEL), lambda i: (0, 0)), pl.BlockSpec(memory_space=pl.ANY)],
        out_shape=[jax.ShapeDtypeStruct((1, 128), F32), jax.ShapeDtypeStruct((n_tok, D_MODEL), F32),
                   jax.ShapeDtypeStruct((N_PAIRS, n_tok, PAIR), MXU), jax.ShapeDtypeStruct((n_tok, ATT_WIDTH), MXU),
                   jax.ShapeDtypeStruct((1, D_MODEL), F32),
                   jax.ShapeDtypeStruct((ATT_WIDTH // W_BLOCK, W_BLOCK, D_MODEL), MXU)],
        scratch_shapes=[pltpu.VMEM((ATT_WIDTH // W_BLOCK, W_BLOCK, D_MODEL), F32), pltpu.VMEM((1, D_MODEL), F32),
                        pltpu.VMEM((W_BLOCK, D_MODEL), MXU)],
        compiler_params=_params(1),
    )(o, z, x1, target, g_post, w_out)


def _adamw(parts, w, m, v, name):
    rows, cols = w.shape
    tr = min(rows, 256)

    def body(p_ref, w_ref, m_ref, v_ref, g_ref, d_ref, mo_ref, vo_ref):
        g = p_ref[0].astype(F32)
        for s in range(1, parts.shape[0]):
            g = g + p_ref[s].astype(F32)
        m_new = ADAM_B1 * m_ref[...] + (1.0 - ADAM_B1) * g
        v_new = ADAM_B2 * v_ref[...] + (1.0 - ADAM_B2) * (g * g)
        m_hat = m_new / (1.0 - ADAM_B1 ** ADAM_STEP)
        v_hat = v_new / (1.0 - ADAM_B2 ** ADAM_STEP)
        g_ref[...] = g
        d_ref[...] = -ADAM_LR * (m_hat / (jnp.sqrt(v_hat) + ADAM_EPS) + ADAM_WD * w_ref[...])
        mo_ref[...] = m_new
        vo_ref[...] = v_new

    blk = pl.BlockSpec((tr, cols), lambda i: (i, 0))
    shape = jax.ShapeDtypeStruct((rows, cols), F32)
    return pl.pallas_call(
        body, name=name, grid=(rows // tr,),
        in_specs=[pl.BlockSpec((parts.shape[0], tr, cols), lambda i: (0, i, 0)), blk, blk, blk],
        out_specs=[blk, blk, blk, blk], out_shape=[shape, shape, shape, shape],
        compiler_params=_params(1),
    )(parts, w, m, v)


SMALL_ROWS = 16
LOSS_AT = (6, N_REL)


def _pack_small(norm_pre, norm_post, pool_scale, rel_bias_padded):
    return jnp.concatenate([norm_pre, norm_post, pool_scale.reshape(2, D_MODEL),
                            rel_bias_padded.reshape(SMALL_ROWS - 6, D_MODEL)], axis=0)


def _unpack_small(packed):
    rel = packed[6:].reshape(N_HEADS, REL_PAD)[:, :N_REL]
    return packed[0:2], packed[2:4], packed[4:6].reshape(1, POOL_WIDTH), rel.reshape(1, N_HEADS, N_REL)


def _pad_rel(rel_bias):
    return jnp.pad(rel_bias.reshape(N_HEADS, N_REL), ((0, 0), (0, REL_PAD - N_REL)))


def kernel(x, norm_pre, norm_post, pool_w_in, pool_w_group, pool_scale, pool_w_out, att_w_in, att_rel_bias, att_w_out, loss_target, m_norm_pre, m_norm_post, m_pool_w_in, m_pool_w_group, m_pool_scale, m_pool_w_out, m_att_w_in, m_att_rel_bias, m_att_w_out, v_norm_pre, v_norm_post, v_pool_w_in, v_pool_w_group, v_pool_scale, v_pool_w_out, v_att_w_in, v_att_rel_bias, v_att_w_out):
    xt = x[0]
    target = loss_target[0]
    n_tok = xt.shape[0]
    lead = PAD // TM_WIDE
    rows_g = GROUP // N_DEV

    rel_padded = _pad_rel(att_rel_bias[0])
    gathered = _gather_two_level([pool_w_in[0].astype(MXU), pool_w_group[0].astype(MXU), pool_w_out[0].astype(MXU)],
                                 rel_padded, "gather_pool_weights")
    bias_tab = gathered[3]
    w_in_p = gathered[0]
    w_group = gathered[1].transpose(1, 0, 2, 3).reshape(N_GROUPS, GROUP, GROUP)
    w_out_p = gathered[2].reshape(POOL_WIDTH, D_MODEL)

    x1, y0, z0, mixed, mg, prod, h0_t, w_in_a, w_out_a = _pool_fwd(
        xt, norm_pre[0:1], norm_post[0:1], w_in_p, w_group, pool_scale, w_out_p,
        [(att_w_in[0].astype(MXU), False), (att_w_out[0].astype(MXU), False)])
    w_out_a = w_out_a.reshape(ATT_WIDTH, D_MODEL)
    q, kpad, vpad, z1 = _att_in(x1, norm_pre[1:2], w_in_a)
    o, lse = _att_fwd(q, kpad, vpad, bias_tab)
    loss_part, dx2, do, dz1, d_gpost1, d_w_out_a = _att_out(o, z1, x1, target, norm_post[1:2], w_out_a)
    dq, dkpad, dvpad, dtab = _att_bwd(q, kpad, vpad, do, o, lse, bias_tab)
    d_rel = _bias_grad(dtab)
    pairs = (N_PAIRS, TM_WIDE, PAIR)
    flat = (TM_WIDE, D_MODEL)
    dx1, d_gpre1, d_w_in_a = _in_proj_bwd(
        [(dq, pairs, lambda i: (0, i, 0)), (dkpad, pairs, lambda i: (0, i + lead, 0)),
         (dvpad, pairs, lambda i: (0, i + lead, 0)), (dz1, flat, lambda i: (i, 0))],
        x1, dx2, norm_pre[1:2], w_in_a, "att_in_bwd", TM_WIDE, True, [])
    du0, d_scale, d_gpost0, d_w_group, d_w_out_p, part_w_in_a, part_w_out_a = _pool_bwd(
        dx1, y0, z0, mg, mixed, prod, norm_post[0:1], pool_scale, w_group, w_out_p,
        [(d_w_in_a, True), (d_w_out_a.reshape(N_DEV, ATT_WIDTH // N_DEV, D_MODEL), True)])
    col = lambda p: (lambda i: (i, p))
    grad_x, d_gpre0, part_w_group, part_w_out_p = _in_proj_bwd(
        [(du0, (TM_WIDE, D_MODEL), col(p)) for p in range(4)], xt, dx1, norm_pre[0:1], w_in_p, "pool_in_bwd", TM_WIDE,
        False,
        [(d_w_group.reshape(N_GROUPS, N_DEV, rows_g, GROUP).transpose(1, 0, 2, 3), True),
         (d_w_out_p.reshape(N_DEV, POOL_WIDTH // N_DEV, D_MODEL), True)])
    d_small = _pack_small(jnp.concatenate([d_gpre0, d_gpre1], axis=0), jnp.concatenate([d_gpost0, d_gpost1], axis=0),
                          d_scale, d_rel).at[LOSS_AT].set(loss_part[0, 0])
    part_w_in_p, part_small = _w_in_grad_scatter(h0_t, du0, [(d_small, False)], "pool_w_in_grad")

    def update(part, w, m, v, name):
        shape = w.shape
        flat = lambda a: a.reshape(-1, shape[-1])
        outs = _adamw(part.reshape(part.shape[0], -1, shape[-1]), flat(w), flat(m), flat(v), name)
        return [a.reshape(shape) for a in outs]

    u_att_w_in = update(part_w_in_a, att_w_in, m_att_w_in, v_att_w_in, "adamw_att_w_in")
    u_att_w_out = update(part_w_out_a, att_w_out, m_att_w_out, v_att_w_out, "adamw_att_w_out")
    u_pool_w_group = update(part_w_group, pool_w_group, m_pool_w_group, v_pool_w_group, "adamw_pool_w_group")
    u_pool_w_out = update(part_w_out_p, pool_w_out, m_pool_w_out, v_pool_w_out, "adamw_pool_w_out")
    u_pool_w_in = update(part_w_in_p, pool_w_in, m_pool_w_in, v_pool_w_in, "adamw_pool_w_in")
    small = _adamw(part_small, _pack_small(norm_pre, norm_post, pool_scale, rel_padded),
                   _pack_small(m_norm_pre, m_norm_post, m_pool_scale, _pad_rel(m_att_rel_bias[0])),
                   _pack_small(v_norm_pre, v_norm_post, v_pool_scale, _pad_rel(v_att_rel_bias[0])), "adamw_small")
    u_small = [_unpack_small(a) for a in small]

    loss = small[0][LOSS_AT]
    outs = [loss, grad_x.reshape(1, n_tok, D_MODEL)]
    for kind in range(4):
        outs += [u_small[kind][0], u_small[kind][1], u_pool_w_in[kind], u_pool_w_group[kind], u_small[kind][2],
                 u_pool_w_out[kind], u_att_w_in[kind], u_small[kind][3], u_att_w_out[kind]]
    return tuple(outs)
```

```python
import functools

import jax
import jax.numpy as jnp
from jax import lax
from jax.experimental import pallas as pl
from jax.experimental.pallas import tpu as pltpu

F32 = jnp.float32
MXU = jnp.bfloat16

D_MODEL = 1024
POOL_WIDTH = 2048
POOL_WINDOWS = (2, 4, 8, 16)
N_GROUPS = 4
GROUP = 512
HALO = 16
N_HEADS = 16
HEAD_DIM = 64
CHUNK = 64
LEFT_CHUNKS = 8
PAD = LEFT_CHUNKS * CHUNK
BAND = PAD + CHUNK
MAX_REL = 256
N_REL = 2 * MAX_REL + 1
REL_PAD = 640
ATT_WIDTH = 1024
PAIR = 2 * HEAD_DIM
N_PAIRS = N_HEADS // 2
N_DEV = 8
W_BLOCK = 512
RMS_EPS = 1e-6
QK_SCALE = 0.125
NEG = -1e30

TM = 256
TM_WIDE = 512
TMB = 1024
TAIL_OWNER_BITS = (7, 6, 5, 4, 3, 2, 1, 0)
KB = 3
TKW = KB * TM
FWD_PAIRS = 8
BWD_PAIRS = 4
FWD_UNROLL = 4
BWD_UNROLL = 2
ROLL_W = 1024

VMEM_LIMIT = 56 * 1024 * 1024

ADAM_LR = 0.001
ADAM_B1 = 0.9
ADAM_B2 = 0.999
ADAM_EPS = 1e-08
ADAM_WD = 0.01
ADAM_STEP = 10

NT_DIMS = (((1,), (1,)), ((), ()))
TN_DIMS = (((0,), (0,)), ((), ()))


def _params(n_grid):
    return pltpu.CompilerParams(dimension_semantics=("arbitrary",) * n_grid, vmem_limit_bytes=VMEM_LIMIT)


def _const_spec(shape):
    nd = len(shape)
    return pl.BlockSpec(shape, lambda *_: (0,) * nd, pipeline_mode=pl.Buffered(1))


def _dot(a, b):
    return jnp.dot(a, b, preferred_element_type=F32)


def _dot_nt(a, b):
    return lax.dot_general(a, b, NT_DIMS, preferred_element_type=F32)


def _dot_tn(a, b):
    return lax.dot_general(a, b, TN_DIMS, preferred_element_type=F32)


def _sigmoid(z):
    return 1.0 / (1.0 + jnp.exp(-z))


def _rms_fwd(xv):
    r = lax.rsqrt(jnp.mean(xv * xv, axis=-1, keepdims=True) + RMS_EPS)
    return r, xv * r


def _rms_bwd(dn, xhat, r, g):
    dng = dn * g
    return r * (dng - xhat * jnp.mean(dng * xhat, axis=-1, keepdims=True))


class _Exchange:
    def __init__(self, items):
        self.arrays = [a for a, _ in items]
        self.scatter = [s for _, s in items]
        self.n = len(items)
        self.out_shape = [jax.ShapeDtypeStruct((N_DEV,) + tuple(a.shape[1:] if s else a.shape), a.dtype)
                          for a, s in items]
        self.specs = [pl.BlockSpec(memory_space=pl.ANY)] * self.n
        self.scratch = ([pltpu.SemaphoreType.DMA((N_DEV - 1, self.n)), pltpu.SemaphoreType.DMA((N_DEV - 1, self.n)),
                         pltpu.SemaphoreType.DMA((self.n,))] if self.n else [])

    def _copies(self, ins, outs, sems, with_receives):
        send_sems, recv_sems, local_sems = sems
        x, y, c = lax.axis_index("x"), lax.axis_index("y"), lax.axis_index("c")
        me = 4 * x + 2 * y + c

        def src(t, slot):
            return ins[t].at[slot] if self.scatter[t] else ins[t]

        local = [pltpu.make_async_copy(src(t, me), outs[t].at[me], local_sems.at[t]) for t in range(self.n)]
        sends, recvs = [], []
        for k in range(1, N_DEV):
            px = 1 - x if k & 4 else x
            py = 1 - y if k & 2 else y
            pc = 1 - c if k & 1 else c
            peer = 4 * px + 2 * py + pc
            for t in range(self.n):
                common = dict(src_ref=src(t, peer), send_sem=send_sems.at[k - 1, t], recv_sem=recv_sems.at[k - 1, t],
                              device_id=(px, py, pc), device_id_type=pl.DeviceIdType.MESH)
                sends.append(pltpu.make_async_remote_copy(dst_ref=outs[t].at[me], **common))
                if with_receives:
                    recvs.append(pltpu.make_async_remote_copy(dst_ref=outs[t].at[peer], **common))
        return local, sends, recvs

    def start(self, ins, outs, sems):
        if self.n:
            local, sends, _ = self._copies(ins, outs, sems, False)
            for cp in local + sends:
                cp.start()

    def wait(self, ins, outs, sems):
        if self.n:
            local, sends, recvs = self._copies(ins, outs, sems, True)
            for cp in recvs:
                cp.wait_recv()
            for cp in sends:
                cp.wait_send()
            for cp in local:
                cp.wait()


def _exchange(items, name):
    ex = _Exchange(items)
    n = ex.n

    def body(*refs):
        ins, outs, sems = refs[:n], refs[n:2 * n], refs[2 * n:]
        ex.start(ins, outs, sems)
        ex.wait(ins, outs, sems)

    return pl.pallas_call(
        body, name=name, out_shape=ex.out_shape, in_specs=ex.specs, out_specs=ex.specs, scratch_shapes=ex.scratch,
        compiler_params=pltpu.CompilerParams(has_side_effects=True),
    )(*ex.arrays)


def _gather_two_level(arrays, rel_bias_padded, name):
    n = len(arrays)
    out_shape = [jax.ShapeDtypeStruct((N_DEV,) + a.shape, a.dtype) for a in arrays]
    own_sib, own_x, own_y, half_via_x, half_via_y, x_sib, y_sib, diag_sib = range(8)

    def body(*refs):
        ins, rb_ref, outs, tab_ref = refs[:n], refs[n], refs[n + 1:2 * n + 1], refs[2 * n + 1]
        send_sems, recv_sems, local_sems = refs[2 * n + 2:]
        x, y, c = lax.axis_index("x"), lax.axis_index("y"), lax.axis_index("c")
        me, sibling = (x, y, c), (x, y, 1 - c)
        x_nbr, y_nbr, diag = (1 - x, y, c), (x, 1 - y, c), (1 - x, 1 - y, c)

        def slot(pos):
            return 4 * pos[0] + 2 * pos[1] + pos[2]

        def other_core(pos):
            return (pos[0], pos[1], 1 - pos[2])

        def copy(kind, t, block, to, own=False, half=None):
            dst = outs[t].at[slot(block)]
            if half is not None:
                rows = arrays[t].shape[0] // 2
                dst = dst.at[pl.ds(half * rows, rows)]
            return pltpu.make_async_remote_copy(
                src_ref=ins[t] if own else dst, dst_ref=dst,
                send_sem=send_sems.at[kind, t], recv_sem=recv_sems.at[kind, t],
                device_id=to, device_id_type=pl.DeviceIdType.MESH)

        local = [pltpu.make_async_copy(ins[t], outs[t].at[slot(me)], local_sems.at[t]) for t in range(n)]
        sent = [copy(kind, t, me, to, own=True)
                for t in range(n) for kind, to in ((own_x, x_nbr), (own_y, y_nbr), (own_sib, sibling))]
        for cp in local + sent:
            cp.start()
        _fill_bias_table(rb_ref, tab_ref)

        def start(cp):
            cp.start()
            sent.append(cp)

        for t in range(n):
            copy(own_x, t, x_nbr, me).wait_recv()
            start(copy(half_via_x, t, x_nbr, y_nbr, half=0))
            start(copy(x_sib, t, x_nbr, sibling))
        for t in range(n):
            copy(own_y, t, y_nbr, me).wait_recv()
            start(copy(half_via_y, t, y_nbr, x_nbr, half=1))
            start(copy(y_sib, t, y_nbr, sibling))
        for t in range(n):
            copy(half_via_x, t, diag, me, half=0).wait_recv()
            copy(half_via_y, t, diag, me, half=1).wait_recv()
            start(copy(diag_sib, t, diag, sibling))
        for t in range(n):
            for kind, block in ((own_sib, sibling), (x_sib, other_core(x_nbr)), (y_sib, other_core(y_nbr)),
                                (diag_sib, other_core(diag))):
                copy(kind, t, block, me).wait_recv()
        for cp in sent:
            cp.wait_send()
        for cp in local:
            cp.wait()

    any_spec = pl.BlockSpec(memory_space=pl.ANY)
    vmem_spec = pl.BlockSpec(memory_space=pltpu.VMEM)
    return pl.pallas_call(
        body, name=name, out_shape=out_shape + [jax.ShapeDtypeStruct((N_HEADS, TM, TKW), F32)],
        in_specs=[any_spec] * n + [vmem_spec], out_specs=[any_spec] * n + [vmem_spec],
        scratch_shapes=[pltpu.SemaphoreType.DMA((8, n)), pltpu.SemaphoreType.DMA((8, n)), pltpu.SemaphoreType.DMA((n,))],
        compiler_params=pltpu.CompilerParams(has_side_effects=True, vmem_limit_bytes=VMEM_LIMIT),
    )(*arrays, rel_bias_padded)


def _inv_count(row, window):
    return 1.0 / jnp.minimum(row + 1, window).astype(F32)


def _pool_fwd(x, g_pre, g_post, w_in, w_group, scale, w_out, exchange_items):
    n_tok = x.shape[0]
    nt = n_tok // TM
    ex = _Exchange(exchange_items)

    def body(x_ref, gpre_ref, gpost_ref, win_ref, wg_ref, sc_ref, wout_ref, *rest):
        ex_in, rest = rest[:ex.n], rest[ex.n:]
        x1_ref, y_ref, z_ref, mixed_ref, mg_ref, prod_ref, ht_ref = rest[:7]
        ex_out, carry_ref, ex_sems = rest[7:7 + ex.n], rest[7 + ex.n], rest[8 + ex.n:]
        i = pl.program_id(0)

        @pl.when(i == 0)
        def _():
            ex.start(ex_in, ex_out, ex_sems)
            carry_ref[...] = jnp.zeros_like(carry_ref)

        xv = x_ref[...]
        r, xhat = _rms_fwd(xv)
        hf = xhat * gpre_ref[...]
        h = hf.astype(MXU)
        ht_ref[...] = hf.T.astype(MXU)
        row = i * TM + lax.broadcasted_iota(jnp.int32, (TM, 1), 0)
        y = None
        for g in range(N_GROUPS):
            cols = slice(g * GROUP, (g + 1) * GROUP)
            a = _dot(h, win_ref[g])
            z = _dot(h, win_ref[N_GROUPS + g])
            s = jnp.concatenate([carry_ref[g], a], axis=0)
            carry_ref[g] = a[TM - HALO:, :]
            w = 1
            while w < POOL_WINDOWS[g]:
                s = s + pltpu.roll(s, w, 0)
                w *= 2
            mixed = (s[HALO:, :] * _inv_count(row, POOL_WINDOWS[g]) - a).astype(MXU)
            mg = _dot(mixed, wg_ref[g])
            prod = (mg * sc_ref[:, cols] * (z * _sigmoid(z))).astype(MXU)
            z_ref[:, cols] = z
            mixed_ref[:, cols] = mixed
            mg_ref[:, cols] = mg
            prod_ref[:, cols] = prod
            part = _dot(prod, wout_ref[cols, :])
            y = part if y is None else y + part
        y_ref[...] = y
        _, yhat = _rms_fwd(y)
        x1_ref[...] = xv + yhat * gpost_ref[...]

        @pl.when(i == nt - 1)
        def _():
            ex.wait(ex_in, ex_out, ex_sems)

    tok = lambda w: pl.BlockSpec((TM, w), lambda i: (i, 0))
    return pl.pallas_call(
        body, name="pool_fwd", grid=(nt,),
        in_specs=[tok(D_MODEL), _const_spec((1, D_MODEL)), _const_spec((1, D_MODEL)),
                  _const_spec((N_DEV, D_MODEL, W_BLOCK)), _const_spec((N_GROUPS, GROUP, GROUP)),
                  _const_spec((1, POOL_WIDTH)), _const_spec((POOL_WIDTH, D_MODEL))] + ex.specs,
        out_specs=[tok(D_MODEL), tok(D_MODEL), tok(POOL_WIDTH), tok(POOL_WIDTH), tok(POOL_WIDTH), tok(POOL_WIDTH),
                   pl.BlockSpec((D_MODEL, TM), lambda i: (0, i))] + ex.specs,
        out_shape=[jax.ShapeDtypeStruct((n_tok, D_MODEL), F32), jax.ShapeDtypeStruct((n_tok, D_MODEL), F32),
                   jax.ShapeDtypeStruct((n_tok, POOL_WIDTH), F32), jax.ShapeDtypeStruct((n_tok, POOL_WIDTH), MXU),
                   jax.ShapeDtypeStruct((n_tok, POOL_WIDTH), F32), jax.ShapeDtypeStruct((n_tok, POOL_WIDTH), MXU),
                   jax.ShapeDtypeStruct((D_MODEL, n_tok), MXU)] + ex.out_shape,
        scratch_shapes=[pltpu.VMEM((N_GROUPS, HALO, GROUP), F32)] + ex.scratch,
        compiler_params=_params(1),
    )(x, g_pre, g_post, w_in, w_group, scale, w_out, *ex.arrays)


def _flush(acc_ref, out_hbm, stage_ref):
    for j in range(acc_ref.shape[0]):
        stage_ref[...] = acc_ref[j].astype(stage_ref.dtype)
        pltpu.sync_copy(stage_ref, out_hbm.at[j])


def _pool_bwd(dx1, y, z, mg, mixed, prod, g_post, scale, w_group, w_out, exchange_items):
    n_tok = dx1.shape[0]
    nt = n_tok // TM
    ex = _Exchange(exchange_items)

    def body(dx1_ref, y_ref, z_ref, mg_ref, mixed_ref, prod_ref, gpost_ref, sc_ref, wg_ref, wout_ref, *rest):
        ex_in, rest = rest[:ex.n], rest[ex.n:]
        du_ref, dsc_ref, dgpost_ref, dwg_hbm, dwout_hbm = rest[:5]
        ex_out, rest = rest[5:5 + ex.n], rest[5 + ex.n:]
        carry_ref, dwg_acc, dwout_acc, stage_g, stage_o = rest[:5]
        ex_sems = rest[5:]
        i = pl.program_id(0)

        @pl.when(i == 0)
        def _():
            ex.start(ex_in, ex_out, ex_sems)
            carry_ref[...] = jnp.zeros_like(carry_ref)
            dwg_acc[...] = jnp.zeros_like(dwg_acc)
            dwout_acc[...] = jnp.zeros_like(dwout_acc)
            dsc_ref[...] = jnp.zeros_like(dsc_ref)
            dgpost_ref[...] = jnp.zeros_like(dgpost_ref)

        dn = dx1_ref[...]
        r, yhat = _rms_fwd(y_ref[...])
        dgpost_ref[...] += jnp.sum(dn * yhat, axis=0, keepdims=True)
        dy = _rms_bwd(dn, yhat, r, gpost_ref[...]).astype(MXU)
        row = (nt - 1 - i) * TM + lax.broadcasted_iota(jnp.int32, (TM, 1), 0)
        n_ext = TM + HALO
        for g in range(N_GROUPS):
            cols = slice(g * GROUP, (g + 1) * GROUP)
            dwout_acc[g] += _dot_tn(prod_ref[:, cols], dy)
            dprod = _dot_nt(dy, wout_ref[cols, :])
            zv = z_ref[:, cols]
            sig = _sigmoid(zv)
            silu = zv * sig
            mgv = mg_ref[:, cols]
            sc = sc_ref[:, cols]
            dsc_ref[:, cols] += jnp.sum(dprod * silu * mgv, axis=0, keepdims=True)
            dmg = (dprod * silu * sc).astype(MXU)
            dz = dprod * (mgv * sc) * (sig * (1.0 + zv * (1.0 - sig)))
            dwg_acc[g] += _dot_tn(mixed_ref[:, cols], dmg)
            dmixed = _dot_nt(dmg, wg_ref[g])
            e = dmixed * _inv_count(row, POOL_WINDOWS[g])
            s = jnp.concatenate([e, carry_ref[g]], axis=0)
            carry_ref[g] = e[:HALO, :]
            w = 1
            while w < POOL_WINDOWS[g]:
                s = s + pltpu.roll(s, n_ext - w, 0)
                w *= 2
            du_ref[:, cols] = (s[:TM, :] - dmixed).astype(MXU)
            du_ref[:, POOL_WIDTH + g * GROUP:POOL_WIDTH + (g + 1) * GROUP] = dz.astype(MXU)

        @pl.when(i == nt - 1)
        def _():
            _flush(dwg_acc, dwg_hbm, stage_g)
            _flush(dwout_acc, dwout_hbm, stage_o)
            ex.wait(ex_in, ex_out, ex_sems)

    rev = lambda w: pl.BlockSpec((TM, w), lambda i: (nt - 1 - i, 0))
    any_spec = pl.BlockSpec(memory_space=pl.ANY)
    return pl.pallas_call(
        body, name="pool_bwd", grid=(nt,),
        in_specs=[rev(D_MODEL), rev(D_MODEL), rev(POOL_WIDTH), rev(POOL_WIDTH), rev(POOL_WIDTH), rev(POOL_WIDTH),
                  _const_spec((1, D_MODEL)), _const_spec((1, POOL_WIDTH)),
                  _const_spec((N_GROUPS, GROUP, GROUP)), _const_spec((POOL_WIDTH, D_MODEL))] + ex.specs,
        out_specs=[rev(2 * POOL_WIDTH), pl.BlockSpec((1, POOL_WIDTH), lambda i: (0, 0)),
                   pl.BlockSpec((1, D_MODEL), lambda i: (0, 0)), any_spec, any_spec] + ex.specs,
        out_shape=[jax.ShapeDtypeStruct((n_tok, 2 * POOL_WIDTH), MXU), jax.ShapeDtypeStruct((1, POOL_WIDTH), F32),
                   jax.ShapeDtypeStruct((1, D_MODEL), F32), jax.ShapeDtypeStruct((N_GROUPS, GROUP, GROUP), MXU),
                   jax.ShapeDtypeStruct((N_GROUPS, GROUP, D_MODEL), MXU)] + ex.out_shape,
        scratch_shapes=[pltpu.VMEM((N_GROUPS, HALO, GROUP), F32), pltpu.VMEM((N_GROUPS, GROUP, GROUP), F32),
                        pltpu.VMEM((N_GROUPS, GROUP, D_MODEL), F32), pltpu.VMEM((GROUP, GROUP), MXU),
                        pltpu.VMEM((GROUP, D_MODEL), MXU)] + ex.scratch,
        compiler_params=_params(1),
    )(dx1, y, z, mg, mixed, prod, g_post, scale, w_group, w_out, *ex.arrays)


def _in_proj_bwd(parts, x, dres, g_pre, w_in, name, tm, with_dw, exchange_items):
    n_tok = x.shape[0]
    nt = n_tok // tm
    half = D_MODEL // W_BLOCK
    ex = _Exchange(exchange_items)
    n_dw = 1 if with_dw else 0

    def body(p0, p1, p2, p3, x_ref, dres_ref, g_ref, w_ref, *rest):
        ex_in, rest = rest[:ex.n], rest[ex.n:]
        dx_ref, dg_ref = rest[:2]
        dw_hbm = rest[2:2 + n_dw]
        ex_out, rest = rest[2 + n_dw:2 + n_dw + ex.n], rest[2 + n_dw + ex.n:]
        dw_scratch, ex_sems = rest[:2 * n_dw], rest[2 * n_dw:]
        i = pl.program_id(0)

        @pl.when(i == 0)
        def _():
            ex.start(ex_in, ex_out, ex_sems)
            dg_ref[...] = jnp.zeros_like(dg_ref)
            if with_dw:
                dw_scratch[0][...] = jnp.zeros_like(dw_scratch[0])

        r, xhat = _rms_fwd(x_ref[...])
        g = g_ref[...]
        h = (xhat * g).astype(MXU)
        dh = None
        for p, part_ref in enumerate((p0, p1, p2, p3)):
            for jj in range(half):
                j = half * p + jj
                if len(part_ref.shape) == 3:
                    per_block = W_BLOCK // PAIR
                    du = jnp.concatenate([part_ref[jj * per_block + pp] for pp in range(per_block)], axis=1)
                else:
                    du = part_ref[:, jj * W_BLOCK:(jj + 1) * W_BLOCK]
                t = _dot_nt(du, w_ref[j])
                dh = t if dh is None else dh + t
                if with_dw:
                    dw_scratch[0][j] += _dot_tn(h, du)
        dg_ref[...] += jnp.sum(dh * xhat, axis=0, keepdims=True)
        dx_ref[...] = dres_ref[...] + _rms_bwd(dh, xhat, r, g)

        @pl.when(i == nt - 1)
        def _():
            if with_dw:
                _flush(dw_scratch[0], dw_hbm[0], dw_scratch[1])
            ex.wait(ex_in, ex_out, ex_sems)

    tok = pl.BlockSpec((tm, D_MODEL), lambda i: (i, 0))
    return pl.pallas_call(
        body, name=name, grid=(nt,),
        in_specs=[pl.BlockSpec(shape, m) for _, shape, m in parts]
        + [tok, tok, _const_spec((1, D_MODEL)), _const_spec((N_DEV, D_MODEL, W_BLOCK))] + ex.specs,
        out_specs=[tok, pl.BlockSpec((1, D_MODEL), lambda i: (0, 0))]
        + [pl.BlockSpec(memory_space=pl.ANY)] * n_dw + ex.specs,
        out_shape=[jax.ShapeDtypeStruct((n_tok, D_MODEL), F32), jax.ShapeDtypeStruct((1, D_MODEL), F32)]
        + [jax.ShapeDtypeStruct((N_DEV, D_MODEL, W_BLOCK), MXU)] * n_dw + ex.out_shape,
        scratch_shapes=[pltpu.VMEM((N_DEV, D_MODEL, W_BLOCK), F32), pltpu.VMEM((D_MODEL, W_BLOCK), MXU)][:2 * n_dw]
        + ex.scratch,
        compiler_params=_params(1),
    )(*[a for a, _, _ in parts], x, dres, g_pre, w_in, *ex.arrays)


def _flip(x, y, c, bits):
    return (1 - x if bits & 4 else x, 1 - y if bits & 2 else y, 1 - c if bits & 1 else c)


def _w_in_grad_scatter(h_t, du, exchange_items, name):
    n_tok = du.shape[0]
    ni = n_tok // TMB
    ex = _Exchange(exchange_items)
    me_out = 4 * lax.axis_index("x") + 2 * lax.axis_index("y") + lax.axis_index("c")
    order = (me_out ^ jnp.array(TAIL_OWNER_BITS, jnp.int32)).astype(jnp.int32)
    n_pairs = N_DEV // 2

    def body(order_ref, h_ref, du_ref, *rest):
        ex_in, rest = rest[:ex.n], rest[ex.n:]
        part_hbm, ex_out, rest = rest[0], rest[1:1 + ex.n], rest[1 + ex.n:]
        acc_ref, stage_ref, pair_ref, pair_send, pair_recv, chip_send, chip_recv = rest[:7]
        ex_sems = rest[7:]
        s = pl.program_id(0)
        i = pl.program_id(1)
        x, y, c = lax.axis_index("x"), lax.axis_index("y"), lax.axis_index("c")
        my_chip = 2 * x + y

        def to_sibling(j):
            return pltpu.make_async_remote_copy(
                src_ref=stage_ref.at[2 * j], dst_ref=pair_ref.at[j], send_sem=pair_send.at[j], recv_sem=pair_recv.at[j],
                device_id=(x, y, 1 - c), device_id_type=pl.DeviceIdType.MESH)

        def to_owner(j, from_chip):
            return pltpu.make_async_remote_copy(
                src_ref=stage_ref.at[2 * j + 1], dst_ref=part_hbm.at[from_chip],
                send_sem=chip_send.at[j], recv_sem=chip_recv.at[j],
                device_id=_flip(x, y, c, TAIL_OWNER_BITS[2 * j + 1]), device_id_type=pl.DeviceIdType.MESH)

        @pl.when((s == 0) & (i == 0))
        def _():
            ex.start(ex_in, ex_out, ex_sems)

        @pl.when(i == 0)
        def _():
            acc_ref[...] = jnp.zeros_like(acc_ref)

        acc_ref[...] += _dot(h_ref[:, pl.ds(pl.multiple_of(i * TMB, TMB), TMB)], du_ref[...])

        @pl.when(i == ni - 1)
        def _():
            for j in range(n_pairs):
                @pl.when(s == 2 * j)
                def _(j=j):
                    stage_ref[2 * j] = acc_ref[...].astype(MXU)
                    to_sibling(j).start()

                @pl.when(s == 2 * j + 1)
                def _(j=j):
                    to_sibling(j).wait_recv()
                    stage_ref[2 * j + 1] = (acc_ref[...] + pair_ref[j].astype(F32)).astype(MXU)
                    if j < n_pairs - 1:
                        to_owner(j, my_chip).start()
                    else:
                        pltpu.sync_copy(stage_ref.at[2 * j + 1], part_hbm.at[my_chip])
                        for jj in range(n_pairs - 1):
                            sx, sy, _ = _flip(x, y, c, TAIL_OWNER_BITS[2 * jj + 1])
                            to_owner(jj, 2 * sx + sy).wait_recv()
                            to_owner(jj, my_chip).wait_send()
                        for jj in range(n_pairs):
                            to_sibling(jj).wait_send()
                        ex.wait(ex_in, ex_out, ex_sems)

    grid_spec = pltpu.PrefetchScalarGridSpec(
        num_scalar_prefetch=1, grid=(N_DEV, ni),
        in_specs=[pl.BlockSpec((D_MODEL, n_tok), lambda s, i, order: (0, 0), pipeline_mode=pl.Buffered(1)),
                  pl.BlockSpec((TMB, W_BLOCK), lambda s, i, order: (i, order[s]))] + ex.specs,
        out_specs=[pl.BlockSpec(memory_space=pl.ANY)] + ex.specs,
        scratch_shapes=[pltpu.VMEM((D_MODEL, W_BLOCK), F32), pltpu.VMEM((N_DEV, D_MODEL, W_BLOCK), MXU),
                        pltpu.VMEM((n_pairs, D_MODEL, W_BLOCK), MXU),
                        pltpu.SemaphoreType.DMA((n_pairs,)), pltpu.SemaphoreType.DMA((n_pairs,)),
                        pltpu.SemaphoreType.DMA((n_pairs - 1,)), pltpu.SemaphoreType.DMA((n_pairs - 1,))] + ex.scratch)
    return pl.pallas_call(
        body, name=name, grid_spec=grid_spec,
        out_shape=[jax.ShapeDtypeStruct((n_pairs, D_MODEL, W_BLOCK), MXU)] + ex.out_shape,
        compiler_params=_params(2),
    )(order, h_t, du, *ex.arrays)


def _rel_onehot():
    rel = lax.broadcasted_iota(jnp.int32, (REL_PAD, ROLL_W), 0)
    col = lax.broadcasted_iota(jnp.int32, (REL_PAD, ROLL_W), 1)
    return (rel == jnp.minimum(BAND + MAX_REL - col, 2 * MAX_REL)).astype(MXU)


def _split3(v):
    hi = v.astype(MXU)
    r1 = v - hi.astype(F32)
    mid = r1.astype(MXU)
    lo = (r1 - mid.astype(F32)).astype(MXU)
    return hi, mid, lo


def _fill_bias_table(rb_ref, out_ref):
    onehot = _rel_onehot()
    base = None
    for term in _split3(rb_ref[...]):
        t = _dot(term, onehot)
        base = t if base is None else base + t
    qi = lax.broadcasted_iota(jnp.int32, (CHUNK, ROLL_W), 0)
    kk = lax.broadcasted_iota(jnp.int32, (CHUNK, TKW), 1)
    for h in range(N_HEADS):
        t = jnp.broadcast_to(base[h:h + 1, :], (CHUNK, ROLL_W))
        for bit in range(6):
            t = jnp.where(((qi >> bit) & 1) == 1, pltpu.roll(t, 1 << bit, 1), t)
        for rr in range(TM // CHUNK):
            shifted = pltpu.roll(t, (CHUNK * rr - CHUNK) % ROLL_W, 1)[:, :TKW]
            band = kk - CHUNK * rr
            out_ref[h, rr * CHUNK:(rr + 1) * CHUNK, :] = jnp.where((band >= 0) & (band < BAND), shifted, NEG)


def _bias_grad(dtab):
    def body(dt_ref, out_ref, dbase_ref):
        qi = lax.broadcasted_iota(jnp.int32, (CHUNK, ROLL_W), 0)
        zeros = jnp.zeros((CHUNK, ROLL_W - TKW), F32)
        for h in range(N_HEADS):
            t = None
            for rr in range(TM // CHUNK):
                blk = jnp.concatenate([dt_ref[h, rr * CHUNK:(rr + 1) * CHUNK, :], zeros], axis=1)
                blk = pltpu.roll(blk, (CHUNK - CHUNK * rr) % ROLL_W, 1)
                t = blk if t is None else t + blk
            for bit in range(6):
                t = jnp.where(((qi >> bit) & 1) == 1, pltpu.roll(t, ROLL_W - (1 << bit), 1), t)
            dbase_ref[h:h + 1, :] = jnp.sum(t, axis=0, keepdims=True)
        onehot = _rel_onehot()
        acc = None
        for term in _split3(dbase_ref[...]):
            t = _dot_nt(term, onehot)
            acc = t if acc is None else acc + t
        out_ref[...] = acc

    return pl.pallas_call(
        body, name="bias_grad", out_shape=jax.ShapeDtypeStruct((N_HEADS, REL_PAD), F32),
        scratch_shapes=[pltpu.VMEM((N_HEADS, ROLL_W), F32)],
        compiler_params=pltpu.CompilerParams(vmem_limit_bytes=VMEM_LIMIT),
    )(dtab)


def _att_in(x1, g_pre, w_in):
    n_tok = x1.shape[0]
    tm = TM_WIDE
    nt = n_tok // tm
    lead = PAD // tm
    per_block = W_BLOCK // PAIR

    def body(x_ref, g_ref, w_ref, q_ref, k_ref, v_ref, z_ref):
        i = pl.program_id(0)

        @pl.when(i < lead)
        def _():
            k_ref[...] = jnp.zeros_like(k_ref)
            v_ref[...] = jnp.zeros_like(v_ref)

        @pl.when(i >= lead)
        def _():
            _, xhat = _rms_fwd(x_ref[...])
            h = (xhat * g_ref[...]).astype(MXU)
            for j in range(N_DEV):
                u = _dot(h, w_ref[j])
                if j >= 6:
                    z_ref[:, (j % 2) * W_BLOCK:(j % 2 + 1) * W_BLOCK] = u
                    continue
                dst = (q_ref, k_ref, v_ref)[j // 2]
                if j < 2:
                    u = u * QK_SCALE
                for pp in range(per_block):
                    dst[(j % 2) * per_block + pp] = u[:, pp * PAIR:(pp + 1) * PAIR].astype(MXU)

    late = pl.BlockSpec((tm, D_MODEL), lambda i: (jnp.maximum(i - lead, 0), 0))
    late3 = pl.BlockSpec((N_PAIRS, tm, PAIR), lambda i: (0, jnp.maximum(i - lead, 0), 0))
    padded3 = pl.BlockSpec((N_PAIRS, tm, PAIR), lambda i: (0, i, 0))
    return pl.pallas_call(
        body, name="att_in", grid=(nt + lead,),
        in_specs=[late, _const_spec((1, D_MODEL)), _const_spec((N_DEV, D_MODEL, W_BLOCK))],
        out_specs=[late3, padded3, padded3, late],
        out_shape=[jax.ShapeDtypeStruct((N_PAIRS, n_tok, PAIR), MXU),
                   jax.ShapeDtypeStruct((N_PAIRS, n_tok + PAD, PAIR), MXU),
                   jax.ShapeDtypeStruct((N_PAIRS, n_tok + PAD, PAIR), MXU),
                   jax.ShapeDtypeStruct((n_tok, ATT_WIDTH), F32)],
        compiler_params=_params(1),
    )(x1, g_pre, w_in)


def _head_masks():
    lane = lax.broadcasted_iota(jnp.int32, (1, PAIR), 1)
    first = lane < HEAD_DIM
    return first, jnp.logical_not(first)


def _pair_loop(one_pair, n, unroll):
    def loop_pass(t, carry):
        for u in range(unroll):
            one_pair(t * unroll + u)
        return carry

    lax.fori_loop(0, n // unroll, loop_pass, 0)


def _scores(qh, k_refs, bias_ref, p, hh, tile, masked):
    ss = []
    for b in range(KB):
        s = _dot_nt(qh, k_refs[b][p]) + bias_ref[2 * p + hh, :, b * TM:(b + 1) * TM]
        if masked:
            s = s + jnp.where(tile + b < PAD // TM, NEG, 0.0).astype(F32)
        ss.append(s)
    return ss


def _pair_specs(n, index_map):
    return pl.BlockSpec((n, TM, PAIR), index_map)


def _att_fwd(q, kpad, vpad, bias_tab):
    n_tok = q.shape[1]
    nt = n_tok // TM
    lead = PAD // TM

    def body(q_ref, k0, k1, k2, v0, v1, v2, bias_ref, o_ref, lse_ref):
        i = pl.program_id(1)
        masks = _head_masks()

        def pairs(masked):
            def one_pair(p):
                qv = q_ref[p]
                outs, ms = [], []
                for hh, mask in enumerate(masks):
                    qh = jnp.where(mask, qv, jnp.zeros_like(qv))
                    ss = _scores(qh, (k0, k1, k2), bias_ref, p, hh, i, masked)
                    m = None
                    for s in ss:
                        mb = jnp.max(s, axis=-1, keepdims=True)
                        m = mb if m is None else jnp.maximum(m, mb)
                    out = None
                    for b, v_ref in enumerate((v0, v1, v2)):
                        vb = v_ref[p]
                        t = _dot(jnp.exp(ss[b] - m).astype(MXU), jnp.where(mask, vb, jnp.ones_like(vb)))
                        out = t if out is None else out + t
                    outs.append(out)
                    ms.append(m)
                num = jnp.where(masks[0], outs[0], outs[1])
                den = jnp.where(masks[0], pltpu.roll(outs[0], HEAD_DIM, 1), pltpu.roll(outs[1], HEAD_DIM, 1))
                o_ref[p] = num / den
                lse_ref[p] = jnp.where(masks[0], ms[0], ms[1]) + jnp.log(den)

            _pair_loop(one_pair, FWD_PAIRS, FWD_UNROLL)

        @pl.when(i < lead)
        def _():
            pairs(True)

        @pl.when(i >= lead)
        def _():
            pairs(False)

    qspec = _pair_specs(FWD_PAIRS, lambda g, i: (g, i, 0))
    kspecs = [_pair_specs(FWD_PAIRS, functools.partial(lambda g, i, b: (g, i + b, 0), b=b)) for b in range(KB)]
    vspecs = [_pair_specs(FWD_PAIRS, functools.partial(lambda g, i, b: (g, i + b, 0), b=b)) for b in range(KB)]
    shape = jax.ShapeDtypeStruct((N_PAIRS, n_tok, PAIR), F32)
    return pl.pallas_call(
        body, name="att_fwd", grid=(N_PAIRS // FWD_PAIRS, nt),
        in_specs=[qspec] + kspecs + vspecs
        + [pl.BlockSpec((2 * FWD_PAIRS, TM, TKW), lambda g, i: (g, 0, 0), pipeline_mode=pl.Buffered(1))],
        out_specs=[qspec, qspec], out_shape=[shape, shape],
        compiler_params=_params(2),
    )(q, kpad, kpad, kpad, vpad, vpad, vpad, bias_tab)


def _att_bwd(q, kpad, vpad, do, o, lse, bias_tab):
    n_tok = q.shape[1]
    nt = n_tok // TM
    lead = PAD // TM

    def body(q_ref, do_ref, o_ref, lse_ref, k0, k1, k2, v0, v1, v2, bias_ref,
             dq_ref, dk_ref, dv_ref, dtab_ref, rk0, rk1, rv0, rv1):
        i = pl.program_id(1)
        masks = _head_masks()

        @pl.when(i == 0)
        def _():
            for ref in (rk0, rk1, rv0, rv1):
                ref[...] = jnp.zeros_like(ref)
            dtab_ref[...] = jnp.zeros_like(dtab_ref)

        def pairs(masked):
            def one_pair(p):
                qv = q_ref[p]
                dov = do_ref[p]
                doo = dov.astype(F32) * o_ref[p]
                lse_pair = lse_ref[p]
                dks = [None] * KB
                dvs = [None] * KB
                dqs = []
                for hh, mask in enumerate(masks):
                    qh = jnp.where(mask, qv, jnp.zeros_like(qv))
                    doh = jnp.where(mask, dov, jnp.zeros_like(dov))
                    dsum = jnp.sum(jnp.where(mask, doo, 0.0), axis=-1, keepdims=True)
                    lse_h = lse_pair[:, hh * HEAD_DIM:hh * HEAD_DIM + 1]
                    ss = _scores(qh, (k0, k1, k2), bias_ref, p, hh, i, masked)
                    dq = None
                    for b, (k_ref, v_ref) in enumerate(zip((k0, k1, k2), (v0, v1, v2))):
                        prob = jnp.exp(ss[b] - lse_h)
                        ds = prob * (_dot_nt(doh, v_ref[p]) - dsum)
                        dtab_ref[2 * p + hh, :, b * TM:(b + 1) * TM] += ds
                        dsb = ds.astype(MXU)
                        t = lax.dot_general(k_ref[p], dsb, (((0,), (1,)), ((), ())), preferred_element_type=F32)
                        dq = t if dq is None else dq + t
                        t = _dot_tn(qh, dsb)
                        dks[b] = t if dks[b] is None else dks[b] + t
                        t = _dot_tn(doh, prob.astype(MXU))
                        dvs[b] = t if dvs[b] is None else dvs[b] + t
                    dqs.append(dq)
                first_rows = lax.broadcasted_iota(jnp.int32, (PAIR, 1), 0) < HEAD_DIM
                dq_ref[p] = (jnp.where(first_rows, dqs[0], dqs[1]).T * QK_SCALE).astype(MXU)
                dk_ref[p] = (rk0[p] + dks[0].T).astype(MXU)
                dv_ref[p] = (rv0[p] + dvs[0].T).astype(MXU)
                rk0[p] = rk1[p] + dks[1].T
                rv0[p] = rv1[p] + dvs[1].T
                rk1[p] = dks[2].T
                rv1[p] = dvs[2].T

            _pair_loop(one_pair, BWD_PAIRS, BWD_UNROLL)

        @pl.when(i < lead)
        def _():
            pairs(True)

        @pl.when((i >= lead) & (i < nt))
        def _():
            pairs(False)

        @pl.when(i >= nt)
        def _():
            dk_ref[...] = rk0[...].astype(MXU)
            dv_ref[...] = rv0[...].astype(MXU)
            rk0[...] = rk1[...]
            rv0[...] = rv1[...]

    last = nt - 1
    clamped = lambda b: functools.partial(lambda g, i, b: (g, jnp.minimum(i, last) + b, 0), b=b)
    qspec = _pair_specs(BWD_PAIRS, clamped(0))
    kspecs = [_pair_specs(BWD_PAIRS, clamped(b)) for b in range(KB)]
    vspecs = [_pair_specs(BWD_PAIRS, clamped(b)) for b in range(KB)]
    pspec = _pair_specs(BWD_PAIRS, lambda g, i: (g, i, 0))
    tspec = pl.BlockSpec((2 * BWD_PAIRS, TM, TKW), lambda g, i: (g, 0, 0))
    ring = pltpu.VMEM((BWD_PAIRS, TM, PAIR), F32)
    return pl.pallas_call(
        body, name="att_bwd", grid=(N_PAIRS // BWD_PAIRS, nt + KB - 1),
        in_specs=[qspec, qspec, qspec, qspec] + kspecs + vspecs + [tspec],
        out_specs=[qspec, pspec, pspec, tspec],
        out_shape=[jax.ShapeDtypeStruct((N_PAIRS, n_tok, PAIR), MXU),
                   jax.ShapeDtypeStruct((N_PAIRS, n_tok + PAD, PAIR), MXU),
                   jax.ShapeDtypeStruct((N_PAIRS, n_tok + PAD, PAIR), MXU),
                   jax.ShapeDtypeStruct((N_HEADS, TM, TKW), F32)],
        scratch_shapes=[ring, ring, ring, ring],
        compiler_params=_params(2),
    )(q, do, o, lse, kpad, kpad, kpad, vpad, vpad, vpad, bias_tab)


def _att_out(o, z, x1, target, g_post, w_out):
    n_tok = z.shape[0]
    nt = n_tok // TM_WIDE

    def body(o_ref, z_ref, x1_ref, tgt_ref, gpost_ref, w_ref,
             loss_ref, dx2_ref, do_ref, dz_ref, dgpost_ref, dw_hbm, acc_ref, loss_acc, stage_ref):
        i = pl.program_id(0)

        @pl.when(i == 0)
        def _():
            acc_ref[...] = jnp.zeros_like(acc_ref)
            loss_acc[...] = jnp.zeros_like(loss_acc)
            dgpost_ref[...] = jnp.zeros_like(dgpost_ref)

        ov = jnp.concatenate([o_ref[p] for p in range(N_PAIRS)], axis=1)
        zv = z_ref[...]
        sig = _sigmoid(zv)
        silu = zv * sig
        gated = (ov * silu).astype(MXU)
        y = _dot(gated, w_ref[...])
        r, yhat = _rms_fwd(y)
        gpost = gpost_ref[...]
        diff = x1_ref[...] + yhat * gpost - tgt_ref[...]
        loss_acc[...] += jnp.sum(diff * diff, axis=0, keepdims=True)
        dn = diff * (1.0 / D_MODEL)
        dx2_ref[...] = dn
        dgpost_ref[...] += jnp.sum(dn * yhat, axis=0, keepdims=True)
        dy = _rms_bwd(dn, yhat, r, gpost).astype(MXU)
        for j in range(ATT_WIDTH // W_BLOCK):
            acc_ref[j] += _dot_tn(gated[:, j * W_BLOCK:(j + 1) * W_BLOCK], dy)
        dgated = _dot_nt(dy, w_ref[...])
        dob = (dgated * silu).astype(MXU)
        for p in range(N_PAIRS):
            do_ref[p] = dob[:, p * PAIR:(p + 1) * PAIR]
        dz_ref[...] = (dgated * ov * (sig * (1.0 + zv * (1.0 - sig)))).astype(MXU)

        @pl.when(i == nt - 1)
        def _():
            total = jnp.sum(loss_acc[...], axis=-1, keepdims=True) * (0.5 / D_MODEL)
            loss_ref[...] = jnp.broadcast_to(total, loss_ref.shape)
            _flush(acc_ref, dw_hbm, stage_ref)

    tok = pl.BlockSpec((TM_WIDE, D_MODEL), lambda i: (i, 0))
    tok3 = pl.BlockSpec((N_PAIRS, TM_WIDE, PAIR), lambda i: (0, i, 0))
    return pl.pallas_call(
        body, name="att_out", grid=(nt,),
        in_specs=[tok3, tok, tok, tok, _const_spec((1, D_MODEL)), _const_spec((ATT_WIDTH, D_MODEL))],
        out_specs=[pl.BlockSpec((1, 128), lambda i: (0, 0)), tok, tok3, tok,
                   pl.BlockSpec((1, D_MODEL), lambda i: (0, 0)), pl.BlockSpec(memory_space=pl.ANY)],
        out_shape=[jax.ShapeDtypeStruct((1, 128), F32), jax.ShapeDtypeStruct((n_tok, D_MODEL), F32),
                   jax.ShapeDtypeStruct((N_PAIRS, n_tok, PAIR), MXU), jax.ShapeDtypeStruct((n_tok, ATT_WIDTH), MXU),
                   jax.ShapeDtypeStruct((1, D_MODEL), F32),
                   jax.ShapeDtypeStruct((ATT_WIDTH // W_BLOCK, W_BLOCK, D_MODEL), MXU)],
        scratch_shapes=[pltpu.VMEM((ATT_WIDTH // W_BLOCK, W_BLOCK, D_MODEL), F32), pltpu.VMEM((1, D_MODEL), F32),
                        pltpu.VMEM((W_BLOCK, D_MODEL), MXU)],
        compiler_params=_params(1),
    )(o, z, x1, target, g_post, w_out)


def _adamw(parts, w, m, v, name):
    rows, cols = w.shape
    tr = min(rows, 512)

    def body(p_ref, w_ref, m_ref, v_ref, g_ref, d_ref, mo_ref, vo_ref):
        g = p_ref[0].astype(F32)
        for s in range(1, parts.shape[0]):
            g = g + p_ref[s].astype(F32)
        m_new = ADAM_B1 * m_ref[...] + (1.0 - ADAM_B1) * g
        v_new = ADAM_B2 * v_ref[...] + (1.0 - ADAM_B2) * (g * g)
        m_hat = m_new / (1.0 - ADAM_B1 ** ADAM_STEP)
        v_hat = v_new / (1.0 - ADAM_B2 ** ADAM_STEP)
        g_ref[...] = g
        d_ref[...] = -ADAM_LR * (m_hat / (jnp.sqrt(v_hat) + ADAM_EPS) + ADAM_WD * w_ref[...])
        mo_ref[...] = m_new
        vo_ref[...] = v_new

    blk = pl.BlockSpec((tr, cols), lambda i: (i, 0))
    shape = jax.ShapeDtypeStruct((rows, cols), F32)
    return pl.pallas_call(
        body, name=name, grid=(rows // tr,),
        in_specs=[pl.BlockSpec((parts.shape[0], tr, cols), lambda i: (0, i, 0)), blk, blk, blk],
        out_specs=[blk, blk, blk, blk], out_shape=[shape, shape, shape, shape],
        compiler_params=_params(1),
    )(parts, w, m, v)


SMALL_ROWS = 16
LOSS_AT = (6, N_REL)


def _pack_small(norm_pre, norm_post, pool_scale, rel_bias_padded):
    return jnp.concatenate([norm_pre, norm_post, pool_scale.reshape(2, D_MODEL),
                            rel_bias_padded.reshape(SMALL_ROWS - 6, D_MODEL)], axis=0)


def _unpack_small(packed):
    rel = packed[6:].reshape(N_HEADS, REL_PAD)[:, :N_REL]
    return packed[0:2], packed[2:4], packed[4:6].reshape(1, POOL_WIDTH), rel.reshape(1, N_HEADS, N_REL)


def _pad_rel(rel_bias):
    return jnp.pad(rel_bias.reshape(N_HEADS, N_REL), ((0, 0), (0, REL_PAD - N_REL)))


def kernel(x, norm_pre, norm_post, pool_w_in, pool_w_group, pool_scale, pool_w_out, att_w_in, att_rel_bias, att_w_out, loss_target, m_norm_pre, m_norm_post, m_pool_w_in, m_pool_w_group, m_pool_scale, m_pool_w_out, m_att_w_in, m_att_rel_bias, m_att_w_out, v_norm_pre, v_norm_post, v_pool_w_in, v_pool_w_group, v_pool_scale, v_pool_w_out, v_att_w_in, v_att_rel_bias, v_att_w_out):
    xt = x[0]
    target = loss_target[0]
    n_tok = xt.shape[0]
    lead = PAD // TM_WIDE
    rows_g = GROUP // N_DEV

    rel_padded = _pad_rel(att_rel_bias[0])
    gathered = _gather_two_level([pool_w_in[0].astype(MXU), pool_w_group[0].astype(MXU), pool_w_out[0].astype(MXU)],
                                 rel_padded, "gather_pool_weights")
    bias_tab = gathered[3]
    w_in_p = gathered[0]
    w_group = gathered[1].transpose(1, 0, 2, 3).reshape(N_GROUPS, GROUP, GROUP)
    w_out_p = gathered[2].reshape(POOL_WIDTH, D_MODEL)

    x1, y0, z0, mixed, mg, prod, h0_t, w_in_a, w_out_a = _pool_fwd(
        xt, norm_pre[0:1], norm_post[0:1], w_in_p, w_group, pool_scale, w_out_p,
        [(att_w_in[0].astype(MXU), False), (att_w_out[0].astype(MXU), False)])
    w_out_a = w_out_a.reshape(ATT_WIDTH, D_MODEL)
    q, kpad, vpad, z1 = _att_in(x1, norm_pre[1:2], w_in_a)
    o, lse = _att_fwd(q, kpad, vpad, bias_tab)
    loss_part, dx2, do, dz1, d_gpost1, d_w_out_a = _att_out(o, z1, x1, target, norm_post[1:2], w_out_a)
    dq, dkpad, dvpad, dtab = _att_bwd(q, kpad, vpad, do, o, lse, bias_tab)
    d_rel = _bias_grad(dtab)
    pairs = (N_PAIRS, TM_WIDE, PAIR)
    flat = (TM_WIDE, D_MODEL)
    dx1, d_gpre1, d_w_in_a = _in_proj_bwd(
        [(dq, pairs, lambda i: (0, i, 0)), (dkpad, pairs, lambda i: (0, i + lead, 0)),
         (dvpad, pairs, lambda i: (0, i + lead, 0)), (dz1, flat, lambda i: (i, 0))],
        x1, dx2, norm_pre[1:2], w_in_a, "att_in_bwd", TM_WIDE, True, [])
    du0, d_scale, d_gpost0, d_w_group, d_w_out_p, part_w_in_a, part_w_out_a = _pool_bwd(
        dx1, y0, z0, mg, mixed, prod, norm_post[0:1], pool_scale, w_group, w_out_p,
        [(d_w_in_a, True), (d_w_out_a.reshape(N_DEV, ATT_WIDTH // N_DEV, D_MODEL), True)])
    col = lambda p: (lambda i: (i, p))
    grad_x, d_gpre0, part_w_group, part_w_out_p = _in_proj_bwd(
        [(du0, (TM_WIDE, D_MODEL), col(p)) for p in range(4)], xt, dx1, norm_pre[0:1], w_in_p, "pool_in_bwd", TM_WIDE,
        False,
        [(d_w_group.reshape(N_GROUPS, N_DEV, rows_g, GROUP).transpose(1, 0, 2, 3), True),
         (d_w_out_p.reshape(N_DEV, POOL_WIDTH // N_DEV, D_MODEL), True)])
    d_small = _pack_small(jnp.concatenate([d_gpre0, d_gpre1], axis=0), jnp.concatenate([d_gpost0, d_gpost1], axis=0),
                          d_scale, d_rel).at[LOSS_AT].set(loss_part[0, 0])
    part_w_in_p, part_small = _w_in_grad_scatter(h0_t, du0, [(d_small, False)], "pool_w_in_grad")

    def update(part, w, m, v, name):
        shape = w.shape
        flat = lambda a: a.reshape(-1, shape[-1])
        outs = _adamw(part.reshape(part.shape[0], -1, shape[-1]), flat(w), flat(m), flat(v), name)
        return [a.reshape(shape) for a in outs]

    u_att_w_in = update(part_w_in_a, att_w_in, m_att_w_in, v_att_w_in, "adamw_att_w_in")
    u_att_w_out = update(part_w_out_a, att_w_out, m_att_w_out, v_att_w_out, "adamw_att_w_out")
    u_pool_w_group = update(part_w_group, pool_w_group, m_pool_w_group, v_pool_w_group, "adamw_pool_w_group")
    u_pool_w_out = update(part_w_out_p, pool_w_out, m_pool_w_out, v_pool_w_out, "adamw_pool_w_out")
    u_pool_w_in = update(part_w_in_p, pool_w_in, m_pool_w_in, v_pool_w_in, "adamw_pool_w_in")
    small = _adamw(part_small, _pack_small(norm_pre, norm_post, pool_scale, rel_padded),
                   _pack_small(m_norm_pre, m_norm_post, m_pool_scale, _pad_rel(m_att_rel_bias[0])),
                   _pack_small(v_norm_pre, v_norm_post, v_pool_scale, _pad_rel(v_att_rel_bias[0])), "adamw_small")
    u_small = [_unpack_small(a) for a in small]

    loss = small[0][LOSS_AT]
    outs = [loss, grad_x.reshape(1, n_tok, D_MODEL)]
    for kind in range(4):
        outs += [u_small[kind][0], u_small[kind][1], u_pool_w_in[kind], u_pool_w_group[kind], u_small[kind][2],
                 u_pool_w_out[kind], u_att_w_in[kind], u_small[kind][3], u_att_w_out[kind]]
    return tuple(outs)
```

```python
import functools

import jax
import jax.numpy as jnp
from jax import lax
from jax.experimental import pallas as pl
from jax.experimental.pallas import tpu as pltpu

F32 = jnp.float32
MXU = jnp.bfloat16

D_MODEL = 1024
POOL_WIDTH = 2048
POOL_WINDOWS = (2, 4, 8, 16)
N_GROUPS = 4
GROUP = 512
HALO = 16
N_HEADS = 16
HEAD_DIM = 64
CHUNK = 64
LEFT_CHUNKS = 8
PAD = LEFT_CHUNKS * CHUNK
BAND = PAD + CHUNK
MAX_REL = 256
N_REL = 2 * MAX_REL + 1
REL_PAD = 640
ATT_WIDTH = 1024
PAIR = 2 * HEAD_DIM
N_PAIRS = N_HEADS // 2
N_DEV = 8
W_BLOCK = 512
RMS_EPS = 1e-6
QK_SCALE = 0.125
NEG = -1e30

TM = 256
TM_WIDE = 512
TMB = 1024
TAIL_OWNER_BITS = (7, 6, 5, 4, 3, 2, 1, 0)
KB = 3
TKW = KB * TM
FWD_PAIRS = 8
BWD_PAIRS = 4
FWD_UNROLL = 8
BWD_UNROLL = 2
IN_BUFFERS = 2
ROLL_W = 1024

VMEM_LIMIT = 56 * 1024 * 1024

ADAM_LR = 0.001
ADAM_B1 = 0.9
ADAM_B2 = 0.999
ADAM_EPS = 1e-08
ADAM_WD = 0.01
ADAM_STEP = 10

NT_DIMS = (((1,), (1,)), ((), ()))
TN_DIMS = (((0,), (0,)), ((), ()))


def _params(n_grid):
    return pltpu.CompilerParams(dimension_semantics=("arbitrary",) * n_grid, vmem_limit_bytes=VMEM_LIMIT)


def _const_spec(shape):
    nd = len(shape)
    return pl.BlockSpec(shape, lambda *_: (0,) * nd, pipeline_mode=pl.Buffered(1))


def _dot(a, b):
    return jnp.dot(a, b, preferred_element_type=F32)


def _dot_nt(a, b):
    return lax.dot_general(a, b, NT_DIMS, preferred_element_type=F32)


def _dot_tn(a, b):
    return lax.dot_general(a, b, TN_DIMS, preferred_element_type=F32)


def _sigmoid(z):
    return 1.0 / (1.0 + jnp.exp(-z))


def _rms_fwd(xv):
    r = lax.rsqrt(jnp.mean(xv * xv, axis=-1, keepdims=True) + RMS_EPS)
    return r, xv * r


def _rms_bwd(dn, xhat, r, g):
    dng = dn * g
    return r * (dng - xhat * jnp.mean(dng * xhat, axis=-1, keepdims=True))


class _Exchange:
    def __init__(self, items):
        self.arrays = [a for a, _ in items]
        self.scatter = [s for _, s in items]
        self.n = len(items)
        self.out_shape = [jax.ShapeDtypeStruct((N_DEV,) + tuple(a.shape[1:] if s else a.shape), a.dtype)
                          for a, s in items]
        self.specs = [pl.BlockSpec(memory_space=pl.ANY)] * self.n
        self.scratch = ([pltpu.SemaphoreType.DMA((N_DEV - 1, self.n)), pltpu.SemaphoreType.DMA((N_DEV - 1, self.n)),
                         pltpu.SemaphoreType.DMA((self.n,))] if self.n else [])

    def _copies(self, ins, outs, sems, with_receives):
        send_sems, recv_sems, local_sems = sems
        x, y, c = lax.axis_index("x"), lax.axis_index("y"), lax.axis_index("c")
        me = 4 * x + 2 * y + c

        def src(t, slot):
            return ins[t].at[slot] if self.scatter[t] else ins[t]

        local = [pltpu.make_async_copy(src(t, me), outs[t].at[me], local_sems.at[t]) for t in range(self.n)]
        sends, recvs = [], []
        for k in range(1, N_DEV):
            px = 1 - x if k & 4 else x
            py = 1 - y if k & 2 else y
            pc = 1 - c if k & 1 else c
            peer = 4 * px + 2 * py + pc
            for t in range(self.n):
                common = dict(src_ref=src(t, peer), send_sem=send_sems.at[k - 1, t], recv_sem=recv_sems.at[k - 1, t],
                              device_id=(px, py, pc), device_id_type=pl.DeviceIdType.MESH)
                sends.append(pltpu.make_async_remote_copy(dst_ref=outs[t].at[me], **common))
                if with_receives:
                    recvs.append(pltpu.make_async_remote_copy(dst_ref=outs[t].at[peer], **common))
        return local, sends, recvs

    def start(self, ins, outs, sems):
        if self.n:
            local, sends, _ = self._copies(ins, outs, sems, False)
            for cp in local + sends:
                cp.start()

    def wait(self, ins, outs, sems):
        if self.n:
            local, sends, recvs = self._copies(ins, outs, sems, True)
            for cp in recvs:
                cp.wait_recv()
            for cp in sends:
                cp.wait_send()
            for cp in local:
                cp.wait()


def _exchange(items, name):
    ex = _Exchange(items)
    n = ex.n

    def body(*refs):
        ins, outs, sems = refs[:n], refs[n:2 * n], refs[2 * n:]
        ex.start(ins, outs, sems)
        ex.wait(ins, outs, sems)

    return pl.pallas_call(
        body, name=name, out_shape=ex.out_shape, in_specs=ex.specs, out_specs=ex.specs, scratch_shapes=ex.scratch,
        compiler_params=pltpu.CompilerParams(has_side_effects=True),
    )(*ex.arrays)


def _gather_two_level(arrays, rel_bias_padded, name):
    n = len(arrays)
    out_shape = [jax.ShapeDtypeStruct((N_DEV,) + a.shape, a.dtype) for a in arrays]
    own_sib, own_x, own_y, half_via_x, half_via_y, x_sib, y_sib, diag_sib = range(8)

    def body(*refs):
        ins, rb_ref, outs, tab_ref = refs[:n], refs[n], refs[n + 1:2 * n + 1], refs[2 * n + 1]
        send_sems, recv_sems, local_sems = refs[2 * n + 2:]
        x, y, c = lax.axis_index("x"), lax.axis_index("y"), lax.axis_index("c")
        me, sibling = (x, y, c), (x, y, 1 - c)
        x_nbr, y_nbr, diag = (1 - x, y, c), (x, 1 - y, c), (1 - x, 1 - y, c)

        def slot(pos):
            return 4 * pos[0] + 2 * pos[1] + pos[2]

        def other_core(pos):
            return (pos[0], pos[1], 1 - pos[2])

        def copy(kind, t, block, to, own=False, half=None):
            dst = outs[t].at[slot(block)]
            if half is not None:
                rows = arrays[t].shape[0] // 2
                dst = dst.at[pl.ds(half * rows, rows)]
            return pltpu.make_async_remote_copy(
                src_ref=ins[t] if own else dst, dst_ref=dst,
                send_sem=send_sems.at[kind, t], recv_sem=recv_sems.at[kind, t],
                device_id=to, device_id_type=pl.DeviceIdType.MESH)

        local = [pltpu.make_async_copy(ins[t], outs[t].at[slot(me)], local_sems.at[t]) for t in range(n)]
        sent = [copy(kind, t, me, to, own=True)
                for t in range(n) for kind, to in ((own_x, x_nbr), (own_y, y_nbr), (own_sib, sibling))]
        for cp in local + sent:
            cp.start()
        _fill_bias_table(rb_ref, tab_ref)

        def start(cp):
            cp.start()
            sent.append(cp)

        for t in range(n):
            copy(own_x, t, x_nbr, me).wait_recv()
            start(copy(half_via_x, t, x_nbr, y_nbr, half=0))
            start(copy(x_sib, t, x_nbr, sibling))
        for t in range(n):
            copy(own_y, t, y_nbr, me).wait_recv()
            start(copy(half_via_y, t, y_nbr, x_nbr, half=1))
            start(copy(y_sib, t, y_nbr, sibling))
        for t in range(n):
            copy(half_via_x, t, diag, me, half=0).wait_recv()
            copy(half_via_y, t, diag, me, half=1).wait_recv()
            start(copy(diag_sib, t, diag, sibling))
        for t in range(n):
            for kind, block in ((own_sib, sibling), (x_sib, other_core(x_nbr)), (y_sib, other_core(y_nbr)),
                                (diag_sib, other_core(diag))):
                copy(kind, t, block, me).wait_recv()
        for cp in sent:
            cp.wait_send()
        for cp in local:
            cp.wait()

    any_spec = pl.BlockSpec(memory_space=pl.ANY)
    vmem_spec = pl.BlockSpec(memory_space=pltpu.VMEM)
    return pl.pallas_call(
        body, name=name, out_shape=out_shape + [jax.ShapeDtypeStruct((N_HEADS, TM, TKW), F32)],
        in_specs=[any_spec] * n + [vmem_spec], out_specs=[any_spec] * n + [vmem_spec],
        scratch_shapes=[pltpu.SemaphoreType.DMA((8, n)), pltpu.SemaphoreType.DMA((8, n)), pltpu.SemaphoreType.DMA((n,))],
        compiler_params=pltpu.CompilerParams(has_side_effects=True, vmem_limit_bytes=VMEM_LIMIT),
    )(*arrays, rel_bias_padded)


def _inv_count(row, window):
    return 1.0 / jnp.minimum(row + 1, window).astype(F32)


def _pool_fwd(x, g_pre, g_post, w_in, w_group, scale, w_out, exchange_items):
    n_tok = x.shape[0]
    nt = n_tok // TM
    ex = _Exchange(exchange_items)

    def body(x_ref, gpre_ref, gpost_ref, win_ref, wg_ref, sc_ref, wout_ref, *rest):
        ex_in, rest = rest[:ex.n], rest[ex.n:]
        x1_ref, y_ref, z_ref, mixed_ref, mg_ref, prod_ref, ht_ref = rest[:7]
        ex_out, carry_ref, ex_sems = rest[7:7 + ex.n], rest[7 + ex.n], rest[8 + ex.n:]
        i = pl.program_id(0)

        @pl.when(i == 0)
        def _():
            ex.start(ex_in, ex_out, ex_sems)
            carry_ref[...] = jnp.zeros_like(carry_ref)

        xv = x_ref[...]
        r, xhat = _rms_fwd(xv)
        hf = xhat * gpre_ref[...]
        h = hf.astype(MXU)
        ht_ref[...] = hf.T.astype(MXU)
        row = i * TM + lax.broadcasted_iota(jnp.int32, (TM, 1), 0)
        y = None
        for g in range(N_GROUPS):
            cols = slice(g * GROUP, (g + 1) * GROUP)
            a = _dot(h, win_ref[g])
            z = _dot(h, win_ref[N_GROUPS + g])
            s = jnp.concatenate([carry_ref[g], a], axis=0)
            carry_ref[g] = a[TM - HALO:, :]
            w = 1
            while w < POOL_WINDOWS[g]:
                s = s + pltpu.roll(s, w, 0)
                w *= 2
            mixed = (s[HALO:, :] * _inv_count(row, POOL_WINDOWS[g]) - a).astype(MXU)
            mg = _dot(mixed, wg_ref[g])
            prod = (mg * sc_ref[:, cols] * (z * _sigmoid(z))).astype(MXU)
            z_ref[:, cols] = z
            mixed_ref[:, cols] = mixed
            mg_ref[:, cols] = mg
            prod_ref[:, cols] = prod
            part = _dot(prod, wout_ref[cols, :])
            y = part if y is None else y + part
        y_ref[...] = y
        _, yhat = _rms_fwd(y)
        x1_ref[...] = xv + yhat * gpost_ref[...]

        @pl.when(i == nt - 1)
        def _():
            ex.wait(ex_in, ex_out, ex_sems)

    tok = lambda w: pl.BlockSpec((TM, w), lambda i: (i, 0))
    return pl.pallas_call(
        body, name="pool_fwd", grid=(nt,),
        in_specs=[tok(D_MODEL), _const_spec((1, D_MODEL)), _const_spec((1, D_MODEL)),
                  _const_spec((N_DEV, D_MODEL, W_BLOCK)), _const_spec((N_GROUPS, GROUP, GROUP)),
                  _const_spec((1, POOL_WIDTH)), _const_spec((POOL_WIDTH, D_MODEL))] + ex.specs,
        out_specs=[tok(D_MODEL), tok(D_MODEL), tok(POOL_WIDTH), tok(POOL_WIDTH), tok(POOL_WIDTH), tok(POOL_WIDTH),
                   pl.BlockSpec((D_MODEL, TM), lambda i: (0, i))] + ex.specs,
        out_shape=[jax.ShapeDtypeStruct((n_tok, D_MODEL), F32), jax.ShapeDtypeStruct((n_tok, D_MODEL), F32),
                   jax.ShapeDtypeStruct((n_tok, POOL_WIDTH), F32), jax.ShapeDtypeStruct((n_tok, POOL_WIDTH), MXU),
                   jax.ShapeDtypeStruct((n_tok, POOL_WIDTH), F32), jax.ShapeDtypeStruct((n_tok, POOL_WIDTH), MXU),
                   jax.ShapeDtypeStruct((D_MODEL, n_tok), MXU)] + ex.out_shape,
        scratch_shapes=[pltpu.VMEM((N_GROUPS, HALO, GROUP), F32)] + ex.scratch,
        compiler_params=_params(1),
    )(x, g_pre, g_post, w_in, w_group, scale, w_out, *ex.arrays)


def _flush(acc_ref, out_hbm, stage_ref):
    for j in range(acc_ref.shape[0]):
        stage_ref[...] = acc_ref[j].astype(stage_ref.dtype)
        pltpu.sync_copy(stage_ref, out_hbm.at[j])


def _pool_bwd(dx1, y, z, mg, mixed, prod, g_post, scale, w_group, w_out, exchange_items):
    n_tok = dx1.shape[0]
    nt = n_tok // TM
    ex = _Exchange(exchange_items)

    def body(dx1_ref, y_ref, z_ref, mg_ref, mixed_ref, prod_ref, gpost_ref, sc_ref, wg_ref, wout_ref, *rest):
        ex_in, rest = rest[:ex.n], rest[ex.n:]
        du_ref, dsc_ref, dgpost_ref, dwg_hbm, dwout_hbm = rest[:5]
        ex_out, rest = rest[5:5 + ex.n], rest[5 + ex.n:]
        carry_ref, dwg_acc, dwout_acc, stage_g, stage_o = rest[:5]
        ex_sems = rest[5:]
        i = pl.program_id(0)

        @pl.when(i == 0)
        def _():
            ex.start(ex_in, ex_out, ex_sems)
            carry_ref[...] = jnp.zeros_like(carry_ref)
            dwg_acc[...] = jnp.zeros_like(dwg_acc)
            dwout_acc[...] = jnp.zeros_like(dwout_acc)
            dsc_ref[...] = jnp.zeros_like(dsc_ref)
            dgpost_ref[...] = jnp.zeros_like(dgpost_ref)

        dn = dx1_ref[...]
        r, yhat = _rms_fwd(y_ref[...])
        dgpost_ref[...] += jnp.sum(dn * yhat, axis=0, keepdims=True)
        dy = _rms_bwd(dn, yhat, r, gpost_ref[...]).astype(MXU)
        row = (nt - 1 - i) * TM + lax.broadcasted_iota(jnp.int32, (TM, 1), 0)
        n_ext = TM + HALO
        for g in range(N_GROUPS):
            cols = slice(g * GROUP, (g + 1) * GROUP)
            dwout_acc[g] += _dot_tn(prod_ref[:, cols], dy)
            dprod = _dot_nt(dy, wout_ref[cols, :])
            zv = z_ref[:, cols]
            sig = _sigmoid(zv)
            silu = zv * sig
            mgv = mg_ref[:, cols]
            sc = sc_ref[:, cols]
            dsc_ref[:, cols] += jnp.sum(dprod * silu * mgv, axis=0, keepdims=True)
            dmg = (dprod * silu * sc).astype(MXU)
            dz = dprod * (mgv * sc) * (sig * (1.0 + zv * (1.0 - sig)))
            dwg_acc[g] += _dot_tn(mixed_ref[:, cols], dmg)
            dmixed = _dot_nt(dmg, wg_ref[g])
            e = dmixed * _inv_count(row, POOL_WINDOWS[g])
            s = jnp.concatenate([e, carry_ref[g]], axis=0)
            carry_ref[g] = e[:HALO, :]
            w = 1
            while w < POOL_WINDOWS[g]:
                s = s + pltpu.roll(s, n_ext - w, 0)
                w *= 2
            du_ref[:, cols] = (s[:TM, :] - dmixed).astype(MXU)
            du_ref[:, POOL_WIDTH + g * GROUP:POOL_WIDTH + (g + 1) * GROUP] = dz.astype(MXU)

        @pl.when(i == nt - 1)
        def _():
            _flush(dwg_acc, dwg_hbm, stage_g)
            _flush(dwout_acc, dwout_hbm, stage_o)
            ex.wait(ex_in, ex_out, ex_sems)

    rev = lambda w: pl.BlockSpec((TM, w), lambda i: (nt - 1 - i, 0))
    any_spec = pl.BlockSpec(memory_space=pl.ANY)
    return pl.pallas_call(
        body, name="pool_bwd", grid=(nt,),
        in_specs=[rev(D_MODEL), rev(D_MODEL), rev(POOL_WIDTH), rev(POOL_WIDTH), rev(POOL_WIDTH), rev(POOL_WIDTH),
                  _const_spec((1, D_MODEL)), _const_spec((1, POOL_WIDTH)),
                  _const_spec((N_GROUPS, GROUP, GROUP)), _const_spec((POOL_WIDTH, D_MODEL))] + ex.specs,
        out_specs=[rev(2 * POOL_WIDTH), pl.BlockSpec((1, POOL_WIDTH), lambda i: (0, 0)),
                   pl.BlockSpec((1, D_MODEL), lambda i: (0, 0)), any_spec, any_spec] + ex.specs,
        out_shape=[jax.ShapeDtypeStruct((n_tok, 2 * POOL_WIDTH), MXU), jax.ShapeDtypeStruct((1, POOL_WIDTH), F32),
                   jax.ShapeDtypeStruct((1, D_MODEL), F32), jax.ShapeDtypeStruct((N_GROUPS, GROUP, GROUP), MXU),
                   jax.ShapeDtypeStruct((N_GROUPS, GROUP, D_MODEL), MXU)] + ex.out_shape,
        scratch_shapes=[pltpu.VMEM((N_GROUPS, HALO, GROUP), F32), pltpu.VMEM((N_GROUPS, GROUP, GROUP), F32),
                        pltpu.VMEM((N_GROUPS, GROUP, D_MODEL), F32), pltpu.VMEM((GROUP, GROUP), MXU),
                        pltpu.VMEM((GROUP, D_MODEL), MXU)] + ex.scratch,
        compiler_params=_params(1),
    )(dx1, y, z, mg, mixed, prod, g_post, scale, w_group, w_out, *ex.arrays)


def _in_proj_bwd(parts, x, dres, g_pre, w_in, name, tm, with_dw, exchange_items):
    n_tok = x.shape[0]
    nt = n_tok // tm
    half = D_MODEL // W_BLOCK
    ex = _Exchange(exchange_items)
    n_dw = 1 if with_dw else 0

    def body(p0, p1, p2, p3, x_ref, dres_ref, g_ref, w_ref, *rest):
        ex_in, rest = rest[:ex.n], rest[ex.n:]
        dx_ref, dg_ref = rest[:2]
        dw_hbm = rest[2:2 + n_dw]
        ex_out, rest = rest[2 + n_dw:2 + n_dw + ex.n], rest[2 + n_dw + ex.n:]
        dw_scratch, ex_sems = rest[:2 * n_dw], rest[2 * n_dw:]
        i = pl.program_id(0)

        @pl.when(i == 0)
        def _():
            ex.start(ex_in, ex_out, ex_sems)
            dg_ref[...] = jnp.zeros_like(dg_ref)
            if with_dw:
                dw_scratch[0][...] = jnp.zeros_like(dw_scratch[0])

        r, xhat = _rms_fwd(x_ref[...])
        g = g_ref[...]
        h = (xhat * g).astype(MXU)
        dh = None
        for p, part_ref in enumerate((p0, p1, p2, p3)):
            for jj in range(half):
                j = half * p + jj
                if len(part_ref.shape) == 3:
                    per_block = W_BLOCK // PAIR
                    du = jnp.concatenate([part_ref[jj * per_block + pp] for pp in range(per_block)], axis=1)
                else:
                    du = part_ref[:, jj * W_BLOCK:(jj + 1) * W_BLOCK]
                t = _dot_nt(du, w_ref[j])
                dh = t if dh is None else dh + t
                if with_dw:
                    dw_scratch[0][j] += _dot_tn(h, du)
        dg_ref[...] += jnp.sum(dh * xhat, axis=0, keepdims=True)
        dx_ref[...] = dres_ref[...] + _rms_bwd(dh, xhat, r, g)

        @pl.when(i == nt - 1)
        def _():
            if with_dw:
                _flush(dw_scratch[0], dw_hbm[0], dw_scratch[1])
            ex.wait(ex_in, ex_out, ex_sems)

    tok = pl.BlockSpec((tm, D_MODEL), lambda i: (i, 0))
    return pl.pallas_call(
        body, name=name, grid=(nt,),
        in_specs=[pl.BlockSpec(shape, m) for _, shape, m in parts]
        + [tok, tok, _const_spec((1, D_MODEL)), _const_spec((N_DEV, D_MODEL, W_BLOCK))] + ex.specs,
        out_specs=[tok, pl.BlockSpec((1, D_MODEL), lambda i: (0, 0))]
        + [pl.BlockSpec(memory_space=pl.ANY)] * n_dw + ex.specs,
        out_shape=[jax.ShapeDtypeStruct((n_tok, D_MODEL), F32), jax.ShapeDtypeStruct((1, D_MODEL), F32)]
        + [jax.ShapeDtypeStruct((N_DEV, D_MODEL, W_BLOCK), MXU)] * n_dw + ex.out_shape,
        scratch_shapes=[pltpu.VMEM((N_DEV, D_MODEL, W_BLOCK), F32), pltpu.VMEM((D_MODEL, W_BLOCK), MXU)][:2 * n_dw]
        + ex.scratch,
        compiler_params=_params(1),
    )(*[a for a, _, _ in parts], x, dres, g_pre, w_in, *ex.arrays)


def _flip(x, y, c, bits):
    return (1 - x if bits & 4 else x, 1 - y if bits & 2 else y, 1 - c if bits & 1 else c)


def _w_in_grad_scatter(h_t, du, exchange_items, name):
    n_tok = du.shape[0]
    ni = n_tok // TMB
    ex = _Exchange(exchange_items)
    me_out = 4 * lax.axis_index("x") + 2 * lax.axis_index("y") + lax.axis_index("c")
    order = (me_out ^ jnp.array(TAIL_OWNER_BITS, jnp.int32)).astype(jnp.int32)
    n_pairs = N_DEV // 2

    def body(order_ref, h_ref, du_ref, *rest):
        ex_in, rest = rest[:ex.n], rest[ex.n:]
        part_hbm, ex_out, rest = rest[0], rest[1:1 + ex.n], rest[1 + ex.n:]
        acc_ref, stage_ref, pair_ref, pair_send, pair_recv, chip_send, chip_recv = rest[:7]
        ex_sems = rest[7:]
        s = pl.program_id(0)
        i = pl.program_id(1)
        x, y, c = lax.axis_index("x"), lax.axis_index("y"), lax.axis_index("c")
        my_chip = 2 * x + y

        def to_sibling(j):
            return pltpu.make_async_remote_copy(
                src_ref=stage_ref.at[2 * j], dst_ref=pair_ref.at[j], send_sem=pair_send.at[j], recv_sem=pair_recv.at[j],
                device_id=(x, y, 1 - c), device_id_type=pl.DeviceIdType.MESH)

        def to_owner(j, from_chip):
            return pltpu.make_async_remote_copy(
                src_ref=stage_ref.at[2 * j + 1], dst_ref=part_hbm.at[from_chip],
                send_sem=chip_send.at[j], recv_sem=chip_recv.at[j],
                device_id=_flip(x, y, c, TAIL_OWNER_BITS[2 * j + 1]), device_id_type=pl.DeviceIdType.MESH)

        @pl.when((s == 0) & (i == 0))
        def _():
            ex.start(ex_in, ex_out, ex_sems)

        @pl.when(i == 0)
        def _():
            acc_ref[...] = jnp.zeros_like(acc_ref)

        acc_ref[...] += _dot(h_ref[:, pl.ds(pl.multiple_of(i * TMB, TMB), TMB)], du_ref[...])

        @pl.when(i == ni - 1)
        def _():
            for j in range(n_pairs):
                @pl.when(s == 2 * j)
                def _(j=j):
                    stage_ref[2 * j] = acc_ref[...].astype(MXU)
                    to_sibling(j).start()

                @pl.when(s == 2 * j + 1)
                def _(j=j):
                    to_sibling(j).wait_recv()
                    stage_ref[2 * j + 1] = (acc_ref[...] + pair_ref[j].astype(F32)).astype(MXU)
                    if j < n_pairs - 1:
                        to_owner(j, my_chip).start()
                    else:
                        pltpu.sync_copy(stage_ref.at[2 * j + 1], part_hbm.at[my_chip])
                        for jj in range(n_pairs - 1):
                            sx, sy, _ = _flip(x, y, c, TAIL_OWNER_BITS[2 * jj + 1])
                            to_owner(jj, 2 * sx + sy).wait_recv()
                            to_owner(jj, my_chip).wait_send()
                        for jj in range(n_pairs):
                            to_sibling(jj).wait_send()
                        ex.wait(ex_in, ex_out, ex_sems)

    grid_spec = pltpu.PrefetchScalarGridSpec(
        num_scalar_prefetch=1, grid=(N_DEV, ni),
        in_specs=[pl.BlockSpec((D_MODEL, n_tok), lambda s, i, order: (0, 0), pipeline_mode=pl.Buffered(1)),
                  pl.BlockSpec((TMB, W_BLOCK), lambda s, i, order: (i, order[s]))] + ex.specs,
        out_specs=[pl.BlockSpec(memory_space=pl.ANY)] + ex.specs,
        scratch_shapes=[pltpu.VMEM((D_MODEL, W_BLOCK), F32), pltpu.VMEM((N_DEV, D_MODEL, W_BLOCK), MXU),
                        pltpu.VMEM((n_pairs, D_MODEL, W_BLOCK), MXU),
                        pltpu.SemaphoreType.DMA((n_pairs,)), pltpu.SemaphoreType.DMA((n_pairs,)),
                        pltpu.SemaphoreType.DMA((n_pairs - 1,)), pltpu.SemaphoreType.DMA((n_pairs - 1,))] + ex.scratch)
    return pl.pallas_call(
        body, name=name, grid_spec=grid_spec,
        out_shape=[jax.ShapeDtypeStruct((n_pairs, D_MODEL, W_BLOCK), MXU)] + ex.out_shape,
        compiler_params=_params(2),
    )(order, h_t, du, *ex.arrays)


def _rel_onehot():
    rel = lax.broadcasted_iota(jnp.int32, (REL_PAD, ROLL_W), 0)
    col = lax.broadcasted_iota(jnp.int32, (REL_PAD, ROLL_W), 1)
    return (rel == jnp.minimum(BAND + MAX_REL - col, 2 * MAX_REL)).astype(MXU)


def _split3(v):
    hi = v.astype(MXU)
    r1 = v - hi.astype(F32)
    mid = r1.astype(MXU)
    lo = (r1 - mid.astype(F32)).astype(MXU)
    return hi, mid, lo


def _fill_bias_table(rb_ref, out_ref):
    onehot = _rel_onehot()
    base = None
    for term in _split3(rb_ref[...]):
        t = _dot(term, onehot)
        base = t if base is None else base + t
    qi = lax.broadcasted_iota(jnp.int32, (CHUNK, ROLL_W), 0)
    kk = lax.broadcasted_iota(jnp.int32, (CHUNK, TKW), 1)
    for h in range(N_HEADS):
        t = jnp.broadcast_to(base[h:h + 1, :], (CHUNK, ROLL_W))
        for bit in range(6):
            t = jnp.where(((qi >> bit) & 1) == 1, pltpu.roll(t, 1 << bit, 1), t)
        for rr in range(TM // CHUNK):
            shifted = pltpu.roll(t, (CHUNK * rr - CHUNK) % ROLL_W, 1)[:, :TKW]
            band = kk - CHUNK * rr
            out_ref[h, rr * CHUNK:(rr + 1) * CHUNK, :] = jnp.where((band >= 0) & (band < BAND), shifted, NEG)


def _bias_grad(dtab):
    def body(dt_ref, out_ref, dbase_ref):
        qi = lax.broadcasted_iota(jnp.int32, (CHUNK, ROLL_W), 0)
        zeros = jnp.zeros((CHUNK, ROLL_W - TKW), F32)
        for h in range(N_HEADS):
            t = None
            for rr in range(TM // CHUNK):
                blk = jnp.concatenate([dt_ref[h, rr * CHUNK:(rr + 1) * CHUNK, :], zeros], axis=1)
                blk = pltpu.roll(blk, (CHUNK - CHUNK * rr) % ROLL_W, 1)
                t = blk if t is None else t + blk
            for bit in range(6):
                t = jnp.where(((qi >> bit) & 1) == 1, pltpu.roll(t, ROLL_W - (1 << bit), 1), t)
            dbase_ref[h:h + 1, :] = jnp.sum(t, axis=0, keepdims=True)
        onehot = _rel_onehot()
        acc = None
        for term in _split3(dbase_ref[...]):
            t = _dot_nt(term, onehot)
            acc = t if acc is None else acc + t
        out_ref[...] = acc

    return pl.pallas_call(
        body, name="bias_grad", out_shape=jax.ShapeDtypeStruct((N_HEADS, REL_PAD), F32),
        scratch_shapes=[pltpu.VMEM((N_HEADS, ROLL_W), F32)],
        compiler_params=pltpu.CompilerParams(vmem_limit_bytes=VMEM_LIMIT),
    )(dtab)


def _att_in(x1, g_pre, w_in):
    n_tok = x1.shape[0]
    tm = TM_WIDE
    nt = n_tok // tm
    lead = PAD // tm
    per_block = W_BLOCK // PAIR

    def body(x_ref, g_ref, w_ref, q_ref, k_ref, v_ref, z_ref):
        i = pl.program_id(0)

        @pl.when(i < lead)
        def _():
            k_ref[...] = jnp.zeros_like(k_ref)
            v_ref[...] = jnp.zeros_like(v_ref)

        @pl.when(i >= lead)
        def _():
            _, xhat = _rms_fwd(x_ref[...])
            h = (xhat * g_ref[...]).astype(MXU)
            for j in range(N_DEV):
                u = _dot(h, w_ref[j])
                if j >= 6:
                    z_ref[:, (j % 2) * W_BLOCK:(j % 2 + 1) * W_BLOCK] = u
                    continue
                dst = (q_ref, k_ref, v_ref)[j // 2]
                if j < 2:
                    u = u * QK_SCALE
                for pp in range(per_block):
                    dst[(j % 2) * per_block + pp] = u[:, pp * PAIR:(pp + 1) * PAIR].astype(MXU)

    late = pl.BlockSpec((tm, D_MODEL), lambda i: (jnp.maximum(i - lead, 0), 0))
    late3 = pl.BlockSpec((N_PAIRS, tm, PAIR), lambda i: (0, jnp.maximum(i - lead, 0), 0))
    padded3 = pl.BlockSpec((N_PAIRS, tm, PAIR), lambda i: (0, i, 0))
    return pl.pallas_call(
        body, name="att_in", grid=(nt + lead,),
        in_specs=[late, _const_spec((1, D_MODEL)), _const_spec((N_DEV, D_MODEL, W_BLOCK))],
        out_specs=[late3, padded3, padded3, late],
        out_shape=[jax.ShapeDtypeStruct((N_PAIRS, n_tok, PAIR), MXU),
                   jax.ShapeDtypeStruct((N_PAIRS, n_tok + PAD, PAIR), MXU),
                   jax.ShapeDtypeStruct((N_PAIRS, n_tok + PAD, PAIR), MXU),
                   jax.ShapeDtypeStruct((n_tok, ATT_WIDTH), F32)],
        compiler_params=_params(1),
    )(x1, g_pre, w_in)


def _head_masks():
    lane = lax.broadcasted_iota(jnp.int32, (1, PAIR), 1)
    first = lane < HEAD_DIM
    return first, jnp.logical_not(first)


def _pair_loop(one_pair, n, unroll):
    def loop_pass(t, carry):
        for u in range(unroll):
            one_pair(t * unroll + u)
        return carry

    lax.fori_loop(0, n // unroll, loop_pass, 0)


def _scores(qh, k_refs, bias_ref, p, hh, tile, masked):
    ss = []
    for b in range(KB):
        s = _dot_nt(qh, k_refs[b][p]) + bias_ref[2 * p + hh, :, b * TM:(b + 1) * TM]
        if masked:
            s = s + jnp.where(tile + b < PAD // TM, NEG, 0.0).astype(F32)
        ss.append(s)
    return ss


def _pair_specs(n, index_map, buffers=2):
    return pl.BlockSpec((n, TM, PAIR), index_map, pipeline_mode=pl.Buffered(buffers))


def _att_fwd(q, kpad, vpad, bias_tab):
    n_tok = q.shape[1]
    nt = n_tok // TM
    lead = PAD // TM

    def body(q_ref, k0, k1, k2, v0, v1, v2, bias_ref, o_ref, lse_ref):
        i = pl.program_id(1)
        masks = _head_masks()

        def pairs(masked):
            def one_pair(p):
                qv = q_ref[p]
                outs, ms = [], []
                for hh, mask in enumerate(masks):
                    qh = jnp.where(mask, qv, jnp.zeros_like(qv))
                    ss = _scores(qh, (k0, k1, k2), bias_ref, p, hh, i, masked)
                    m = None
                    for s in ss:
                        mb = jnp.max(s, axis=-1, keepdims=True)
                        m = mb if m is None else jnp.maximum(m, mb)
                    out = None
                    for b, v_ref in enumerate((v0, v1, v2)):
                        vb = v_ref[p]
                        t = _dot(jnp.exp(ss[b] - m).astype(MXU), jnp.where(mask, vb, jnp.ones_like(vb)))
                        out = t if out is None else out + t
                    outs.append(out)
                    ms.append(m)
                num = jnp.where(masks[0], outs[0], outs[1])
                den = jnp.where(masks[0], pltpu.roll(outs[0], HEAD_DIM, 1), pltpu.roll(outs[1], HEAD_DIM, 1))
                o_ref[p] = num / den
                lse_ref[p] = jnp.where(masks[0], ms[0], ms[1]) + jnp.log(den)

            _pair_loop(one_pair, FWD_PAIRS, FWD_UNROLL)

        @pl.when(i < lead)
        def _():
            pairs(True)

        @pl.when(i >= lead)
        def _():
            pairs(False)

    qspec = _pair_specs(FWD_PAIRS, lambda g, i: (g, i, 0))
    shifted = lambda b: functools.partial(lambda g, i, b: (g, i + b, 0), b=b)
    kspecs = [_pair_specs(FWD_PAIRS, shifted(b), IN_BUFFERS) for b in range(KB)]
    vspecs = [_pair_specs(FWD_PAIRS, shifted(b), IN_BUFFERS) for b in range(KB)]
    shape = jax.ShapeDtypeStruct((N_PAIRS, n_tok, PAIR), F32)
    return pl.pallas_call(
        body, name="att_fwd", grid=(N_PAIRS // FWD_PAIRS, nt),
        in_specs=[_pair_specs(FWD_PAIRS, shifted(0), IN_BUFFERS)] + kspecs + vspecs
        + [pl.BlockSpec((2 * FWD_PAIRS, TM, TKW), lambda g, i: (g, 0, 0), pipeline_mode=pl.Buffered(1))],
        out_specs=[qspec, qspec], out_shape=[shape, shape],
        compiler_params=_params(2),
    )(q, kpad, kpad, kpad, vpad, vpad, vpad, bias_tab)


def _att_bwd(q, kpad, vpad, do, o, lse, bias_tab):
    n_tok = q.shape[1]
    nt = n_tok // TM
    lead = PAD // TM

    def body(q_ref, do_ref, o_ref, lse_ref, k0, k1, k2, v0, v1, v2, bias_ref,
             dq_ref, dk_ref, dv_ref, dtab_ref, rk0, rk1, rv0, rv1):
        i = pl.program_id(1)
        masks = _head_masks()

        @pl.when(i == 0)
        def _():
            for ref in (rk0, rk1, rv0, rv1):
                ref[...] = jnp.zeros_like(ref)
            dtab_ref[...] = jnp.zeros_like(dtab_ref)

        def pairs(masked):
            def one_pair(p):
                qv = q_ref[p]
                dov = do_ref[p]
                doo = dov.astype(F32) * o_ref[p]
                lse_pair = lse_ref[p]
                dks = [None] * KB
                dvs = [None] * KB
                dqs = []
                for hh, mask in enumerate(masks):
                    qh = jnp.where(mask, qv, jnp.zeros_like(qv))
                    doh = jnp.where(mask, dov, jnp.zeros_like(dov))
                    dsum = jnp.sum(jnp.where(mask, doo, 0.0), axis=-1, keepdims=True)
                    lse_h = lse_pair[:, hh * HEAD_DIM:hh * HEAD_DIM + 1]
                    ss = _scores(qh, (k0, k1, k2), bias_ref, p, hh, i, masked)
                    dq = None
                    for b, (k_ref, v_ref) in enumerate(zip((k0, k1, k2), (v0, v1, v2))):
                        prob = jnp.exp(ss[b] - lse_h)
                        ds = prob * (_dot_nt(doh, v_ref[p]) - dsum)
                        dtab_ref[2 * p + hh, :, b * TM:(b + 1) * TM] += ds
                        dsb = ds.astype(MXU)
                        t = lax.dot_general(k_ref[p], dsb, (((0,), (1,)), ((), ())), preferred_element_type=F32)
                        dq = t if dq is None else dq + t
                        t = _dot_tn(qh, dsb)
                        dks[b] = t if dks[b] is None else dks[b] + t
                        t = _dot_tn(doh, prob.astype(MXU))
                        dvs[b] = t if dvs[b] is None else dvs[b] + t
                    dqs.append(dq)
                first_rows = lax.broadcasted_iota(jnp.int32, (PAIR, 1), 0) < HEAD_DIM
                dq_ref[p] = (jnp.where(first_rows, dqs[0], dqs[1]).T * QK_SCALE).astype(MXU)
                dk_ref[p] = (rk0[p] + dks[0].T).astype(MXU)
                dv_ref[p] = (rv0[p] + dvs[0].T).astype(MXU)
                rk0[p] = rk1[p] + dks[1].T
                rv0[p] = rv1[p] + dvs[1].T
                rk1[p] = dks[2].T
                rv1[p] = dvs[2].T

            _pair_loop(one_pair, BWD_PAIRS, BWD_UNROLL)

        @pl.when(i < lead)
        def _():
            pairs(True)

        @pl.when((i >= lead) & (i < nt))
        def _():
            pairs(False)

        @pl.when(i >= nt)
        def _():
            dk_ref[...] = rk0[...].astype(MXU)
            dv_ref[...] = rv0[...].astype(MXU)
            rk0[...] = rk1[...]
            rv0[...] = rv1[...]

    last = nt - 1
    clamped = lambda b: functools.partial(lambda g, i, b: (g, jnp.minimum(i, last) + b, 0), b=b)
    qspec = _pair_specs(BWD_PAIRS, clamped(0))
    deep = _pair_specs(BWD_PAIRS, clamped(0), IN_BUFFERS)
    kspecs = [_pair_specs(BWD_PAIRS, clamped(b), IN_BUFFERS) for b in range(KB)]
    vspecs = [_pair_specs(BWD_PAIRS, clamped(b), IN_BUFFERS) for b in range(KB)]
    pspec = _pair_specs(BWD_PAIRS, lambda g, i: (g, i, 0))
    tspec = pl.BlockSpec((2 * BWD_PAIRS, TM, TKW), lambda g, i: (g, 0, 0))
    ring = pltpu.VMEM((BWD_PAIRS, TM, PAIR), F32)
    return pl.pallas_call(
        body, name="att_bwd", grid=(N_PAIRS // BWD_PAIRS, nt + KB - 1),
        in_specs=[deep, deep, deep, deep] + kspecs + vspecs + [tspec],
        out_specs=[qspec, pspec, pspec, tspec],
        out_shape=[jax.ShapeDtypeStruct((N_PAIRS, n_tok, PAIR), MXU),
                   jax.ShapeDtypeStruct((N_PAIRS, n_tok + PAD, PAIR), MXU),
                   jax.ShapeDtypeStruct((N_PAIRS, n_tok + PAD, PAIR), MXU),
                   jax.ShapeDtypeStruct((N_HEADS, TM, TKW), F32)],
        scratch_shapes=[ring, ring, ring, ring],
        compiler_params=_params(2),
    )(q, do, o, lse, kpad, kpad, kpad, vpad, vpad, vpad, bias_tab)


def _att_out(o, z, x1, target, g_post, w_out):
    n_tok = z.shape[0]
    nt = n_tok // TM_WIDE

    def body(o_ref, z_ref, x1_ref, tgt_ref, gpost_ref, w_ref,
             loss_ref, dx2_ref, do_ref, dz_ref, dgpost_ref, dw_hbm, acc_ref, loss_acc, stage_ref):
        i = pl.program_id(0)

        @pl.when(i == 0)
        def _():
            acc_ref[...] = jnp.zeros_like(acc_ref)
            loss_acc[...] = jnp.zeros_like(loss_acc)
            dgpost_ref[...] = jnp.zeros_like(dgpost_ref)

        ov = jnp.concatenate([o_ref[p] for p in range(N_PAIRS)], axis=1)
        zv = z_ref[...]
        sig = _sigmoid(zv)
        silu = zv * sig
        gated = (ov * silu).astype(MXU)
        y = _dot(gated, w_ref[...])
        r, yhat = _rms_fwd(y)
        gpost = gpost_ref[...]
        diff = x1_ref[...] + yhat * gpost - tgt_ref[...]
        loss_acc[...] += jnp.sum(diff * diff, axis=0, keepdims=True)
        dn = diff * (1.0 / D_MODEL)
        dx2_ref[...] = dn
        dgpost_ref[...] += jnp.sum(dn * yhat, axis=0, keepdims=True)
        dy = _rms_bwd(dn, yhat, r, gpost).astype(MXU)
        for j in range(ATT_WIDTH // W_BLOCK):
            acc_ref[j] += _dot_tn(gated[:, j * W_BLOCK:(j + 1) * W_BLOCK], dy)
        dgated = _dot_nt(dy, w_ref[...])
        dob = (dgated * silu).astype(MXU)
        for p in range(N_PAIRS):
            do_ref[p] = dob[:, p * PAIR:(p + 1) * PAIR]
        dz_ref[...] = (dgated * ov * (sig * (1.0 + zv * (1.0 - sig)))).astype(MXU)

        @pl.when(i == nt - 1)
        def _():
            total = jnp.sum(loss_acc[...], axis=-1, keepdims=True) * (0.5 / D_MODEL)
            loss_ref[...] = jnp.broadcast_to(total, loss_ref.shape)
            _flush(acc_ref, dw_hbm, stage_ref)

    tok = pl.BlockSpec((TM_WIDE, D_MODEL), lambda i: (i, 0))
    tok3 = pl.BlockSpec((N_PAIRS, TM_WIDE, PAIR), lambda i: (0, i, 0))
    return pl.pallas_call(
        body, name="att_out", grid=(nt,),
        in_specs=[tok3, tok, tok, tok, _const_spec((1, D_MODEL)), _const_spec((ATT_WIDTH, D_MODEL))],
        out_specs=[pl.BlockSpec((1, 128), lambda i: (0, 0)), tok, tok3, tok,
                   pl.BlockSpec((1, D_MODEL), lambda i: (0, 0)), pl.BlockSpec(memory_space=pl.ANY)],
        out_shape=[jax.ShapeDtypeStruct((1, 128), F32), jax.ShapeDtypeStruct((n_tok, D_MODEL), F32),
                   jax.ShapeDtypeStruct((N_PAIRS, n_tok, PAIR), MXU), jax.ShapeDtypeStruct((n_tok, ATT_WIDTH), MXU),
                   jax.ShapeDtypeStruct((1, D_MODEL), F32),
                   jax.ShapeDtypeStruct((ATT_WIDTH // W_BLOCK, W_BLOCK, D_MODEL), MXU)],
        scratch_shapes=[pltpu.VMEM((ATT_WIDTH // W_BLOCK, W_BLOCK, D_MODEL), F32), pltpu.VMEM((1, D_MODEL), F32),
                        pltpu.VMEM((W_BLOCK, D_MODEL), MXU)],
        compiler_params=_params(1),
    )(o, z, x1, target, g_post, w_out)


def _adamw(parts, w, m, v, name):
    rows, cols = w.shape
    tr = min(rows, 512)

    def body(p_ref, w_ref, m_ref, v_ref, g_ref, d_ref, mo_ref, vo_ref):
        g = p_ref[0].astype(F32)
        for s in range(1, parts.shape[0]):
            g = g + p_ref[s].astype(F32)
        m_new = ADAM_B1 * m_ref[...] + (1.0 - ADAM_B1) * g
        v_new = ADAM_B2 * v_ref[...] + (1.0 - ADAM_B2) * (g * g)
        m_hat = m_new / (1.0 - ADAM_B1 ** ADAM_STEP)
        v_hat = v_new / (1.0 - ADAM_B2 ** ADAM_STEP)
        g_ref[...] = g
        d_ref[...] = -ADAM_LR * (m_hat / (jnp.sqrt(v_hat) + ADAM_EPS) + ADAM_WD * w_ref[...])
        mo_ref[...] = m_new
        vo_ref[...] = v_new

    blk = pl.BlockSpec((tr, cols), lambda i: (i, 0))
    shape = jax.ShapeDtypeStruct((rows, cols), F32)
    return pl.pallas_call(
        body, name=name, grid=(rows // tr,),
        in_specs=[pl.BlockSpec((parts.shape[0], tr, cols), lambda i: (0, i, 0)), blk, blk, blk],
        out_specs=[blk, blk, blk, blk], out_shape=[shape, shape, shape, shape],
        compiler_params=_params(1),
    )(parts, w, m, v)


SMALL_ROWS = 16
LOSS_AT = (6, N_REL)


def _pack_small(norm_pre, norm_post, pool_scale, rel_bias_padded):
    return jnp.concatenate([norm_pre, norm_post, pool_scale.reshape(2, D_MODEL),
                            rel_bias_padded.reshape(SMALL_ROWS - 6, D_MODEL)], axis=0)


def _unpack_small(packed):
    rel = packed[6:].reshape(N_HEADS, REL_PAD)[:, :N_REL]
    return packed[0:2], packed[2:4], packed[4:6].reshape(1, POOL_WIDTH), rel.reshape(1, N_HEADS, N_REL)


def _pad_rel(rel_bias):
    return jnp.pad(rel_bias.reshape(N_HEADS, N_REL), ((0, 0), (0, REL_PAD - N_REL)))


def kernel(x, norm_pre, norm_post, pool_w_in, pool_w_group, pool_scale, pool_w_out, att_w_in, att_rel_bias, att_w_out, loss_target, m_norm_pre, m_norm_post, m_pool_w_in, m_pool_w_group, m_pool_scale, m_pool_w_out, m_att_w_in, m_att_rel_bias, m_att_w_out, v_norm_pre, v_norm_post, v_pool_w_in, v_pool_w_group, v_pool_scale, v_pool_w_out, v_att_w_in, v_att_rel_bias, v_att_w_out):
    xt = x[0]
    target = loss_target[0]
    n_tok = xt.shape[0]
    lead = PAD // TM_WIDE
    rows_g = GROUP // N_DEV

    rel_padded = _pad_rel(att_rel_bias[0])
    gathered = _gather_two_level([pool_w_in[0].astype(MXU), pool_w_group[0].astype(MXU), pool_w_out[0].astype(MXU)],
                                 rel_padded, "gather_pool_weights")
    bias_tab = gathered[3]
    w_in_p = gathered[0]
    w_group = gathered[1].transpose(1, 0, 2, 3).reshape(N_GROUPS, GROUP, GROUP)
    w_out_p = gathered[2].reshape(POOL_WIDTH, D_MODEL)

    x1, y0, z0, mixed, mg, prod, h0_t, w_in_a, w_out_a = _pool_fwd(
        xt, norm_pre[0:1], norm_post[0:1], w_in_p, w_group, pool_scale, w_out_p,
        [(att_w_in[0].astype(MXU), False), (att_w_out[0].astype(MXU), False)])
    w_out_a = w_out_a.reshape(ATT_WIDTH, D_MODEL)
    q, kpad, vpad, z1 = _att_in(x1, norm_pre[1:2], w_in_a)
    o, lse = _att_fwd(q, kpad, vpad, bias_tab)
    loss_part, dx2, do, dz1, d_gpost1, d_w_out_a = _att_out(o, z1, x1, target, norm_post[1:2], w_out_a)
    dq, dkpad, dvpad, dtab = _att_bwd(q, kpad, vpad, do, o, lse, bias_tab)
    d_rel = _bias_grad(dtab)
    pairs = (N_PAIRS, TM_WIDE, PAIR)
    flat = (TM_WIDE, D_MODEL)
    dx1, d_gpre1, d_w_in_a = _in_proj_bwd(
        [(dq, pairs, lambda i: (0, i, 0)), (dkpad, pairs, lambda i: (0, i + lead, 0)),
         (dvpad, pairs, lambda i: (0, i + lead, 0)), (dz1, flat, lambda i: (i, 0))],
        x1, dx2, norm_pre[1:2], w_in_a, "att_in_bwd", TM_WIDE, True, [])
    du0, d_scale, d_gpost0, d_w_group, d_w_out_p, part_w_in_a, part_w_out_a = _pool_bwd(
        dx1, y0, z0, mg, mixed, prod, norm_post[0:1], pool_scale, w_group, w_out_p,
        [(d_w_in_a, True), (d_w_out_a.reshape(N_DEV, ATT_WIDTH // N_DEV, D_MODEL), True)])
    col = lambda p: (lambda i: (i, p))
    grad_x, d_gpre0, part_w_group, part_w_out_p = _in_proj_bwd(
        [(du0, (TM_WIDE, D_MODEL), col(p)) for p in range(4)], xt, dx1, norm_pre[0:1], w_in_p, "pool_in_bwd", TM_WIDE,
        False,
        [(d_w_group.reshape(N_GROUPS, N_DEV, rows_g, GROUP).transpose(1, 0, 2, 3), True),
         (d_w_out_p.reshape(N_DEV, POOL_WIDTH // N_DEV, D_MODEL), True)])
    d_small = _pack_small(jnp.concatenate([d_gpre0, d_gpre1], axis=0), jnp.concatenate([d_gpost0, d_gpost1], axis=0),
                          d_scale, d_rel).at[LOSS_AT].set(loss_part[0, 0])
    part_w_in_p, part_small = _w_in_grad_scatter(h0_t, du0, [(d_small, False)], "pool_w_in_grad")

    def update(part, w, m, v, name):
        shape = w.shape
        flat = lambda a: a.reshape(-1, shape[-1])
        outs = _adamw(part.reshape(part.shape[0], -1, shape[-1]), flat(w), flat(m), flat(v), name)
        return [a.reshape(shape) for a in outs]

    u_att_w_in = update(part_w_in_a, att_w_in, m_att_w_in, v_att_w_in, "adamw_att_w_in")
    u_att_w_out = update(part_w_out_a, att_w_out, m_att_w_out, v_att_w_out, "adamw_att_w_out")
    u_pool_w_group = update(part_w_group, pool_w_group, m_pool_w_group, v_pool_w_group, "adamw_pool_w_group")
    u_pool_w_out = update(part_w_out_p, pool_w_out, m_pool_w_out, v_pool_w_out, "adamw_pool_w_out")
    u_pool_w_in = update(part_w_in_p, pool_w_in, m_pool_w_in, v_pool_w_in, "adamw_pool_w_in")
    small = _adamw(part_small, _pack_small(norm_pre, norm_post, pool_scale, rel_padded),
                   _pack_small(m_norm_pre, m_norm_post, m_pool_scale, _pad_rel(m_att_rel_bias[0])),
                   _pack_small(v_norm_pre, v_norm_post, v_pool_scale, _pad_rel(v_att_rel_bias[0])), "adamw_small")
    u_small = [_unpack_small(a) for a in small]

    loss = small[0][LOSS_AT]
    outs = [loss, grad_x.reshape(1, n_tok, D_MODEL)]
    for kind in range(4):
        outs += [u_small[kind][0], u_small[kind][1], u_pool_w_in[kind], u_pool_w_group[kind], u_small[kind][2],
                 u_pool_w_out[kind], u_att_w_in[kind], u_small[kind][3], u_att_w_out[kind]]
    return tuple(outs)
```

```python
import functools

import jax
import jax.numpy as jnp
from jax import lax
from jax.experimental import pallas as pl
from jax.experimental.pallas import tpu as pltpu

F32 = jnp.float32
MXU = jnp.bfloat16

D_MODEL = 1024
POOL_WIDTH = 2048
POOL_WINDOWS = (2, 4, 8, 16)
N_GROUPS = 4
GROUP = 512
HALO = 16
N_HEADS = 16
HEAD_DIM = 64
CHUNK = 64
LEFT_CHUNKS = 8
PAD = LEFT_CHUNKS * CHUNK
BAND = PAD + CHUNK
MAX_REL = 256
N_REL = 2 * MAX_REL + 1
REL_PAD = 640
ATT_WIDTH = 1024
PAIR = 2 * HEAD_DIM
N_PAIRS = N_HEADS // 2
N_DEV = 8
W_BLOCK = 512
RMS_EPS = 1e-6
QK_SCALE = 0.125
NEG = -1e30

TM = 256
TM_WIDE = 512
TMB = 2048
TAIL_OWNER_BITS = (7, 6, 5, 4, 3, 2, 1, 0)
KB = 3
TKW = KB * TM
FWD_PAIRS = 8
BWD_PAIRS = 4
FWD_UNROLL = 4
BWD_UNROLL = 2
ROLL_W = 1024

VMEM_LIMIT = 56 * 1024 * 1024

ADAM_LR = 0.001
ADAM_B1 = 0.9
ADAM_B2 = 0.999
ADAM_EPS = 1e-08
ADAM_WD = 0.01
ADAM_STEP = 10

NT_DIMS = (((1,), (1,)), ((), ()))
TN_DIMS = (((0,), (0,)), ((), ()))


def _params(n_grid):
    return pltpu.CompilerParams(dimension_semantics=("arbitrary",) * n_grid, vmem_limit_bytes=VMEM_LIMIT)


def _const_spec(shape):
    nd = len(shape)
    return pl.BlockSpec(shape, lambda *_: (0,) * nd, pipeline_mode=pl.Buffered(1))


def _dot(a, b):
    return jnp.dot(a, b, preferred_element_type=F32)


def _dot_nt(a, b):
    return lax.dot_general(a, b, NT_DIMS, preferred_element_type=F32)


def _dot_tn(a, b):
    return lax.dot_general(a, b, TN_DIMS, preferred_element_type=F32)


def _sigmoid(z):
    return 1.0 / (1.0 + jnp.exp(-z))


def _rms_fwd(xv):
    r = lax.rsqrt(jnp.mean(xv * xv, axis=-1, keepdims=True) + RMS_EPS)
    return r, xv * r


def _rms_bwd(dn, xhat, r, g):
    dng = dn * g
    return r * (dng - xhat * jnp.mean(dng * xhat, axis=-1, keepdims=True))


class _Exchange:
    def __init__(self, items):
        self.arrays = [a for a, _ in items]
        self.scatter = [s for _, s in items]
        self.n = len(items)
        self.out_shape = [jax.ShapeDtypeStruct((N_DEV,) + tuple(a.shape[1:] if s else a.shape), a.dtype)
                          for a, s in items]
        self.specs = [pl.BlockSpec(memory_space=pl.ANY)] * self.n
        self.scratch = ([pltpu.SemaphoreType.DMA((N_DEV - 1, self.n)), pltpu.SemaphoreType.DMA((N_DEV - 1, self.n)),
                         pltpu.SemaphoreType.DMA((self.n,))] if self.n else [])

    def _copies(self, ins, outs, sems, with_receives):
        send_sems, recv_sems, local_sems = sems
        x, y, c = lax.axis_index("x"), lax.axis_index("y"), lax.axis_index("c")
        me = 4 * x + 2 * y + c

        def src(t, slot):
            return ins[t].at[slot] if self.scatter[t] else ins[t]

        local = [pltpu.make_async_copy(src(t, me), outs[t].at[me], local_sems.at[t]) for t in range(self.n)]
        sends, recvs = [], []
        for k in range(1, N_DEV):
            px = 1 - x if k & 4 else x
            py = 1 - y if k & 2 else y
            pc = 1 - c if k & 1 else c
            peer = 4 * px + 2 * py + pc
            for t in range(self.n):
                common = dict(src_ref=src(t, peer), send_sem=send_sems.at[k - 1, t], recv_sem=recv_sems.at[k - 1, t],
                              device_id=(px, py, pc), device_id_type=pl.DeviceIdType.MESH)
                sends.append(pltpu.make_async_remote_copy(dst_ref=outs[t].at[me], **common))
                if with_receives:
                    recvs.append(pltpu.make_async_remote_copy(dst_ref=outs[t].at[peer], **common))
        return local, sends, recvs

    def start(self, ins, outs, sems):
        if self.n:
            local, sends, _ = self._copies(ins, outs, sems, False)
            for cp in local + sends:
                cp.start()

    def wait(self, ins, outs, sems):
        if self.n:
            local, sends, recvs = self._copies(ins, outs, sems, True)
            for cp in recvs:
                cp.wait_recv()
            for cp in sends:
                cp.wait_send()
            for cp in local:
                cp.wait()


def _exchange(items, name):
    ex = _Exchange(items)
    n = ex.n

    def body(*refs):
        ins, outs, sems = refs[:n], refs[n:2 * n], refs[2 * n:]
        ex.start(ins, outs, sems)
        ex.wait(ins, outs, sems)

    return pl.pallas_call(
        body, name=name, out_shape=ex.out_shape, in_specs=ex.specs, out_specs=ex.specs, scratch_shapes=ex.scratch,
        compiler_params=pltpu.CompilerParams(has_side_effects=True),
    )(*ex.arrays)


def _gather_two_level(arrays, rel_bias_padded, name):
    n = len(arrays)
    out_shape = [jax.ShapeDtypeStruct((N_DEV,) + a.shape, a.dtype) for a in arrays]
    own_sib, own_x, own_y, half_via_x, half_via_y, x_sib, y_sib, diag_sib = range(8)

    def body(*refs):
        ins, rb_ref, outs, tab_ref = refs[:n], refs[n], refs[n + 1:2 * n + 1], refs[2 * n + 1]
        send_sems, recv_sems, local_sems = refs[2 * n + 2:]
        x, y, c = lax.axis_index("x"), lax.axis_index("y"), lax.axis_index("c")
        me, sibling = (x, y, c), (x, y, 1 - c)
        x_nbr, y_nbr, diag = (1 - x, y, c), (x, 1 - y, c), (1 - x, 1 - y, c)

        def slot(pos):
            return 4 * pos[0] + 2 * pos[1] + pos[2]

        def other_core(pos):
            return (pos[0], pos[1], 1 - pos[2])

        def copy(kind, t, block, to, own=False, half=None):
            dst = outs[t].at[slot(block)]
            if half is not None:
                rows = arrays[t].shape[0] // 2
                dst = dst.at[pl.ds(half * rows, rows)]
            return pltpu.make_async_remote_copy(
                src_ref=ins[t] if own else dst, dst_ref=dst,
                send_sem=send_sems.at[kind, t], recv_sem=recv_sems.at[kind, t],
                device_id=to, device_id_type=pl.DeviceIdType.MESH)

        local = [pltpu.make_async_copy(ins[t], outs[t].at[slot(me)], local_sems.at[t]) for t in range(n)]
        sent = [copy(kind, t, me, to, own=True)
                for t in range(n) for kind, to in ((own_x, x_nbr), (own_y, y_nbr), (own_sib, sibling))]
        for cp in local + sent:
            cp.start()
        _fill_bias_table(rb_ref, tab_ref)

        def start(cp):
            cp.start()
            sent.append(cp)

        for t in range(n):
            copy(own_x, t, x_nbr, me).wait_recv()
            start(copy(half_via_x, t, x_nbr, y_nbr, half=0))
            start(copy(x_sib, t, x_nbr, sibling))
        for t in range(n):
            copy(own_y, t, y_nbr, me).wait_recv()
            start(copy(half_via_y, t, y_nbr, x_nbr, half=1))
            start(copy(y_sib, t, y_nbr, sibling))
        for t in range(n):
            copy(half_via_x, t, diag, me, half=0).wait_recv()
            copy(half_via_y, t, diag, me, half=1).wait_recv()
            start(copy(diag_sib, t, diag, sibling))
        for t in range(n):
            for kind, block in ((own_sib, sibling), (x_sib, other_core(x_nbr)), (y_sib, other_core(y_nbr)),
                                (diag_sib, other_core(diag))):
                copy(kind, t, block, me).wait_recv()
        for cp in sent:
            cp.wait_send()
        for cp in local:
            cp.wait()

    any_spec = pl.BlockSpec(memory_space=pl.ANY)
    vmem_spec = pl.BlockSpec(memory_space=pltpu.VMEM)
    return pl.pallas_call(
        body, name=name, out_shape=out_shape + [jax.ShapeDtypeStruct((N_HEADS, TM, TKW), F32)],
        in_specs=[any_spec] * n + [vmem_spec], out_specs=[any_spec] * n + [vmem_spec],
        scratch_shapes=[pltpu.SemaphoreType.DMA((8, n)), pltpu.SemaphoreType.DMA((8, n)), pltpu.SemaphoreType.DMA((n,))],
        compiler_params=pltpu.CompilerParams(has_side_effects=True, vmem_limit_bytes=VMEM_LIMIT),
    )(*arrays, rel_bias_padded)


def _inv_count(row, window):
    return 1.0 / jnp.minimum(row + 1, window).astype(F32)


def _pool_fwd(x, g_pre, g_post, w_in, w_group, scale, w_out, exchange_items):
    n_tok = x.shape[0]
    nt = n_tok // TM
    ex = _Exchange(exchange_items)

    def body(x_ref, gpre_ref, gpost_ref, win_ref, wg_ref, sc_ref, wout_ref, *rest):
        ex_in, rest = rest[:ex.n], rest[ex.n:]
        x1_ref, y_ref, z_ref, mixed_ref, mg_ref, prod_ref, ht_ref = rest[:7]
        ex_out, carry_ref, ex_sems = rest[7:7 + ex.n], rest[7 + ex.n], rest[8 + ex.n:]
        i = pl.program_id(0)

        @pl.when(i == 0)
        def _():
            ex.start(ex_in, ex_out, ex_sems)
            carry_ref[...] = jnp.zeros_like(carry_ref)

        xv = x_ref[...]
        r, xhat = _rms_fwd(xv)
        hf = xhat * gpre_ref[...]
        h = hf.astype(MXU)
        ht_ref[...] = hf.T.astype(MXU)
        row = i * TM + lax.broadcasted_iota(jnp.int32, (TM, 1), 0)
        y = None
        for g in range(N_GROUPS):
            cols = slice(g * GROUP, (g + 1) * GROUP)
            a = _dot(h, win_ref[g])
            z = _dot(h, win_ref[N_GROUPS + g])
            s = jnp.concatenate([carry_ref[g], a], axis=0)
            carry_ref[g] = a[TM - HALO:, :]
            w = 1
            while w < POOL_WINDOWS[g]:
                s = s + pltpu.roll(s, w, 0)
                w *= 2
            mixed = (s[HALO:, :] * _inv_count(row, POOL_WINDOWS[g]) - a).astype(MXU)
            mg = _dot(mixed, wg_ref[g])
            prod = (mg * sc_ref[:, cols] * (z * _sigmoid(z))).astype(MXU)
            z_ref[:, cols] = z
            mixed_ref[:, cols] = mixed
            mg_ref[:, cols] = mg
            prod_ref[:, cols] = prod
            part = _dot(prod, wout_ref[cols, :])
            y = part if y is None else y + part
        y_ref[...] = y
        _, yhat = _rms_fwd(y)
        x1_ref[...] = xv + yhat * gpost_ref[...]

        @pl.when(i == nt - 1)
        def _():
            ex.wait(ex_in, ex_out, ex_sems)

    tok = lambda w: pl.BlockSpec((TM, w), lambda i: (i, 0))
    return pl.pallas_call(
        body, name="pool_fwd", grid=(nt,),
        in_specs=[tok(D_MODEL), _const_spec((1, D_MODEL)), _const_spec((1, D_MODEL)),
                  _const_spec((N_DEV, D_MODEL, W_BLOCK)), _const_spec((N_GROUPS, GROUP, GROUP)),
                  _const_spec((1, POOL_WIDTH)), _const_spec((POOL_WIDTH, D_MODEL))] + ex.specs,
        out_specs=[tok(D_MODEL), tok(D_MODEL), tok(POOL_WIDTH), tok(POOL_WIDTH), tok(POOL_WIDTH), tok(POOL_WIDTH),
                   pl.BlockSpec((D_MODEL, TM), lambda i: (0, i))] + ex.specs,
        out_shape=[jax.ShapeDtypeStruct((n_tok, D_MODEL), F32), jax.ShapeDtypeStruct((n_tok, D_MODEL), F32),
                   jax.ShapeDtypeStruct((n_tok, POOL_WIDTH), F32), jax.ShapeDtypeStruct((n_tok, POOL_WIDTH), MXU),
                   jax.ShapeDtypeStruct((n_tok, POOL_WIDTH), F32), jax.ShapeDtypeStruct((n_tok, POOL_WIDTH), MXU),
                   jax.ShapeDtypeStruct((D_MODEL, n_tok), MXU)] + ex.out_shape,
        scratch_shapes=[pltpu.VMEM((N_GROUPS, HALO, GROUP), F32)] + ex.scratch,
        compiler_params=_params(1),
    )(x, g_pre, g_post, w_in, w_group, scale, w_out, *ex.arrays)


def _flush(acc_ref, out_hbm, stage_ref):
    for j in range(acc_ref.shape[0]):
        stage_ref[...] = acc_ref[j].astype(stage_ref.dtype)
        pltpu.sync_copy(stage_ref, out_hbm.at[j])


def _pool_bwd(dx1, y, z, mg, mixed, prod, g_post, scale, w_group, w_out, exchange_items):
    n_tok = dx1.shape[0]
    nt = n_tok // TM
    ex = _Exchange(exchange_items)

    def body(dx1_ref, y_ref, z_ref, mg_ref, mixed_ref, prod_ref, gpost_ref, sc_ref, wg_ref, wout_ref, *rest):
        ex_in, rest = rest[:ex.n], rest[ex.n:]
        du_ref, dsc_ref, dgpost_ref, dwg_hbm, dwout_hbm = rest[:5]
        ex_out, rest = rest[5:5 + ex.n], rest[5 + ex.n:]
        carry_ref, dwg_acc, dwout_acc, stage_g, stage_o = rest[:5]
        ex_sems = rest[5:]
        i = pl.program_id(0)

        @pl.when(i == 0)
        def _():
            ex.start(ex_in, ex_out, ex_sems)
            carry_ref[...] = jnp.zeros_like(carry_ref)
            dwg_acc[...] = jnp.zeros_like(dwg_acc)
            dwout_acc[...] = jnp.zeros_like(dwout_acc)
            dsc_ref[...] = jnp.zeros_like(dsc_ref)
            dgpost_ref[...] = jnp.zeros_like(dgpost_ref)

        dn = dx1_ref[...]
        r, yhat = _rms_fwd(y_ref[...])
        dgpost_ref[...] += jnp.sum(dn * yhat, axis=0, keepdims=True)
        dy = _rms_bwd(dn, yhat, r, gpost_ref[...]).astype(MXU)
        row = (nt - 1 - i) * TM + lax.broadcasted_iota(jnp.int32, (TM, 1), 0)
        n_ext = TM + HALO
        for g in range(N_GROUPS):
            cols = slice(g * GROUP, (g + 1) * GROUP)
            dwout_acc[g] += _dot_tn(prod_ref[:, cols], dy)
            dprod = _dot_nt(dy, wout_ref[cols, :])
            zv = z_ref[:, cols]
            sig = _sigmoid(zv)
            silu = zv * sig
            mgv = mg_ref[:, cols]
            sc = sc_ref[:, cols]
            dsc_ref[:, cols] += jnp.sum(dprod * silu * mgv, axis=0, keepdims=True)
            dmg = (dprod * silu * sc).astype(MXU)
            dz = dprod * (mgv * sc) * (sig * (1.0 + zv * (1.0 - sig)))
            dwg_acc[g] += _dot_tn(mixed_ref[:, cols], dmg)
            dmixed = _dot_nt(dmg, wg_ref[g])
            e = dmixed * _inv_count(row, POOL_WINDOWS[g])
            s = jnp.concatenate([e, carry_ref[g]], axis=0)
            carry_ref[g] = e[:HALO, :]
            w = 1
            while w < POOL_WINDOWS[g]:
                s = s + pltpu.roll(s, n_ext - w, 0)
                w *= 2
            du_ref[:, cols] = (s[:TM, :] - dmixed).astype(MXU)
            du_ref[:, POOL_WIDTH + g * GROUP:POOL_WIDTH + (g + 1) * GROUP] = dz.astype(MXU)

        @pl.when(i == nt - 1)
        def _():
            _flush(dwg_acc, dwg_hbm, stage_g)
            _flush(dwout_acc, dwout_hbm, stage_o)
            ex.wait(ex_in, ex_out, ex_sems)

    rev = lambda w: pl.BlockSpec((TM, w), lambda i: (nt - 1 - i, 0))
    any_spec = pl.BlockSpec(memory_space=pl.ANY)
    return pl.pallas_call(
        body, name="pool_bwd", grid=(nt,),
        in_specs=[rev(D_MODEL), rev(D_MODEL), rev(POOL_WIDTH), rev(POOL_WIDTH), rev(POOL_WIDTH), rev(POOL_WIDTH),
                  _const_spec((1, D_MODEL)), _const_spec((1, POOL_WIDTH)),
                  _const_spec((N_GROUPS, GROUP, GROUP)), _const_spec((POOL_WIDTH, D_MODEL))] + ex.specs,
        out_specs=[rev(2 * POOL_WIDTH), pl.BlockSpec((1, POOL_WIDTH), lambda i: (0, 0)),
                   pl.BlockSpec((1, D_MODEL), lambda i: (0, 0)), any_spec, any_spec] + ex.specs,
        out_shape=[jax.ShapeDtypeStruct((n_tok, 2 * POOL_WIDTH), MXU), jax.ShapeDtypeStruct((1, POOL_WIDTH), F32),
                   jax.ShapeDtypeStruct((1, D_MODEL), F32), jax.ShapeDtypeStruct((N_GROUPS, GROUP, GROUP), MXU),
                   jax.ShapeDtypeStruct((N_GROUPS, GROUP, D_MODEL), MXU)] + ex.out_shape,
        scratch_shapes=[pltpu.VMEM((N_GROUPS, HALO, GROUP), F32), pltpu.VMEM((N_GROUPS, GROUP, GROUP), F32),
                        pltpu.VMEM((N_GROUPS, GROUP, D_MODEL), F32), pltpu.VMEM((GROUP, GROUP), MXU),
                        pltpu.VMEM((GROUP, D_MODEL), MXU)] + ex.scratch,
        compiler_params=_params(1),
    )(dx1, y, z, mg, mixed, prod, g_post, scale, w_group, w_out, *ex.arrays)


def _in_proj_bwd(parts, x, dres, g_pre, w_in, name, tm, with_dw, exchange_items):
    n_tok = x.shape[0]
    nt = n_tok // tm
    half = D_MODEL // W_BLOCK
    ex = _Exchange(exchange_items)
    n_dw = 1 if with_dw else 0

    def body(p0, p1, p2, p3, x_ref, dres_ref, g_ref, w_ref, *rest):
        ex_in, rest = rest[:ex.n], rest[ex.n:]
        dx_ref, dg_ref = rest[:2]
        dw_hbm = rest[2:2 + n_dw]
        ex_out, rest = rest[2 + n_dw:2 + n_dw + ex.n], rest[2 + n_dw + ex.n:]
        dw_scratch, ex_sems = rest[:2 * n_dw], rest[2 * n_dw:]
        i = pl.program_id(0)

        @pl.when(i == 0)
        def _():
            ex.start(ex_in, ex_out, ex_sems)
            dg_ref[...] = jnp.zeros_like(dg_ref)
            if with_dw:
                dw_scratch[0][...] = jnp.zeros_like(dw_scratch[0])

        r, xhat = _rms_fwd(x_ref[...])
        g = g_ref[...]
        h = (xhat * g).astype(MXU)
        dh = None
        for p, part_ref in enumerate((p0, p1, p2, p3)):
            for jj in range(half):
                j = half * p + jj
                if len(part_ref.shape) == 3:
                    per_block = W_BLOCK // PAIR
                    du = jnp.concatenate([part_ref[jj * per_block + pp] for pp in range(per_block)], axis=1)
                else:
                    du = part_ref[:, jj * W_BLOCK:(jj + 1) * W_BLOCK]
                t = _dot_nt(du, w_ref[j])
                dh = t if dh is None else dh + t
                if with_dw:
                    dw_scratch[0][j] += _dot_tn(h, du)
        dg_ref[...] += jnp.sum(dh * xhat, axis=0, keepdims=True)
        dx_ref[...] = dres_ref[...] + _rms_bwd(dh, xhat, r, g)

        @pl.when(i == nt - 1)
        def _():
            if with_dw:
                _flush(dw_scratch[0], dw_hbm[0], dw_scratch[1])
            ex.wait(ex_in, ex_out, ex_sems)

    tok = pl.BlockSpec((tm, D_MODEL), lambda i: (i, 0))
    return pl.pallas_call(
        body, name=name, grid=(nt,),
        in_specs=[pl.BlockSpec(shape, m) for _, shape, m in parts]
        + [tok, tok, _const_spec((1, D_MODEL)), _const_spec((N_DEV, D_MODEL, W_BLOCK))] + ex.specs,
        out_specs=[tok, pl.BlockSpec((1, D_MODEL), lambda i: (0, 0))]
        + [pl.BlockSpec(memory_space=pl.ANY)] * n_dw + ex.specs,
        out_shape=[jax.ShapeDtypeStruct((n_tok, D_MODEL), F32), jax.ShapeDtypeStruct((1, D_MODEL), F32)]
        + [jax.ShapeDtypeStruct((N_DEV, D_MODEL, W_BLOCK), MXU)] * n_dw + ex.out_shape,
        scratch_shapes=[pltpu.VMEM((N_DEV, D_MODEL, W_BLOCK), F32), pltpu.VMEM((D_MODEL, W_BLOCK), MXU)][:2 * n_dw]
        + ex.scratch,
        compiler_params=_params(1),
    )(*[a for a, _, _ in parts], x, dres, g_pre, w_in, *ex.arrays)


def _flip(x, y, c, bits):
    return (1 - x if bits & 4 else x, 1 - y if bits & 2 else y, 1 - c if bits & 1 else c)


def _w_in_grad_scatter(h_t, du, exchange_items, name):
    n_tok = du.shape[0]
    ni = n_tok // TMB
    ex = _Exchange(exchange_items)
    me_out = 4 * lax.axis_index("x") + 2 * lax.axis_index("y") + lax.axis_index("c")
    order = (me_out ^ jnp.array(TAIL_OWNER_BITS, jnp.int32)).astype(jnp.int32)
    n_pairs = N_DEV // 2

    def body(order_ref, h_ref, du_ref, *rest):
        ex_in, rest = rest[:ex.n], rest[ex.n:]
        part_hbm, ex_out, rest = rest[0], rest[1:1 + ex.n], rest[1 + ex.n:]
        acc_ref, stage_ref, pair_ref, pair_send, pair_recv, chip_send, chip_recv = rest[:7]
        ex_sems = rest[7:]
        s = pl.program_id(0)
        i = pl.program_id(1)
        x, y, c = lax.axis_index("x"), lax.axis_index("y"), lax.axis_index("c")
        my_chip = 2 * x + y

        def to_sibling(j):
            return pltpu.make_async_remote_copy(
                src_ref=stage_ref.at[2 * j], dst_ref=pair_ref.at[j], send_sem=pair_send.at[j], recv_sem=pair_recv.at[j],
                device_id=(x, y, 1 - c), device_id_type=pl.DeviceIdType.MESH)

        def to_owner(j, from_chip):
            return pltpu.make_async_remote_copy(
                src_ref=stage_ref.at[2 * j + 1], dst_ref=part_hbm.at[from_chip],
                send_sem=chip_send.at[j], recv_sem=chip_recv.at[j],
                device_id=_flip(x, y, c, TAIL_OWNER_BITS[2 * j + 1]), device_id_type=pl.DeviceIdType.MESH)

        @pl.when((s == 0) & (i == 0))
        def _():
            ex.start(ex_in, ex_out, ex_sems)

        @pl.when(i == 0)
        def _():
            acc_ref[...] = jnp.zeros_like(acc_ref)

        acc_ref[...] += _dot(h_ref[:, pl.ds(pl.multiple_of(i * TMB, TMB), TMB)], du_ref[...])

        @pl.when(i == ni - 1)
        def _():
            for j in range(n_pairs):
                @pl.when(s == 2 * j)
                def _(j=j):
                    stage_ref[2 * j] = acc_ref[...].astype(MXU)
                    to_sibling(j).start()

                @pl.when(s == 2 * j + 1)
                def _(j=j):
                    to_sibling(j).wait_recv()
                    stage_ref[2 * j + 1] = (acc_ref[...] + pair_ref[j].astype(F32)).astype(MXU)
                    if j < n_pairs - 1:
                        to_owner(j, my_chip).start()
                    else:
                        pltpu.sync_copy(stage_ref.at[2 * j + 1], part_hbm.at[my_chip])
                        for jj in range(n_pairs - 1):
                            sx, sy, _ = _flip(x, y, c, TAIL_OWNER_BITS[2 * jj + 1])
                            to_owner(jj, 2 * sx + sy).wait_recv()
                            to_owner(jj, my_chip).wait_send()
                        for jj in range(n_pairs):
                            to_sibling(jj).wait_send()
                        ex.wait(ex_in, ex_out, ex_sems)

    grid_spec = pltpu.PrefetchScalarGridSpec(
        num_scalar_prefetch=1, grid=(N_DEV, ni),
        in_specs=[pl.BlockSpec((D_MODEL, n_tok), lambda s, i, order: (0, 0), pipeline_mode=pl.Buffered(1)),
                  pl.BlockSpec((TMB, W_BLOCK), lambda s, i, order: (i, order[s]))] + ex.specs,
        out_specs=[pl.BlockSpec(memory_space=pl.ANY)] + ex.specs,
        scratch_shapes=[pltpu.VMEM((D_MODEL, W_BLOCK), F32), pltpu.VMEM((N_DEV, D_MODEL, W_BLOCK), MXU),
                        pltpu.VMEM((n_pairs, D_MODEL, W_BLOCK), MXU),
                        pltpu.SemaphoreType.DMA((n_pairs,)), pltpu.SemaphoreType.DMA((n_pairs,)),
                        pltpu.SemaphoreType.DMA((n_pairs - 1,)), pltpu.SemaphoreType.DMA((n_pairs - 1,))] + ex.scratch)
    return pl.pallas_call(
        body, name=name, grid_spec=grid_spec,
        out_shape=[jax.ShapeDtypeStruct((n_pairs, D_MODEL, W_BLOCK), MXU)] + ex.out_shape,
        compiler_params=_params(2),
    )(order, h_t, du, *ex.arrays)


def _rel_onehot():
    rel = lax.broadcasted_iota(jnp.int32, (REL_PAD, ROLL_W), 0)
    col = lax.broadcasted_iota(jnp.int32, (REL_PAD, ROLL_W), 1)
    return (rel == jnp.minimum(BAND + MAX_REL - col, 2 * MAX_REL)).astype(MXU)


def _split3(v):
    hi = v.astype(MXU)
    r1 = v - hi.astype(F32)
    mid = r1.astype(MXU)
    lo = (r1 - mid.astype(F32)).astype(MXU)
    return hi, mid, lo


def _fill_bias_table(rb_ref, out_ref):
    onehot = _rel_onehot()
    base = None
    for term in _split3(rb_ref[...]):
        t = _dot(term, onehot)
        base = t if base is None else base + t
    qi = lax.broadcasted_iota(jnp.int32, (CHUNK, ROLL_W), 0)
    kk = lax.broadcasted_iota(jnp.int32, (CHUNK, TKW), 1)
    for h in range(N_HEADS):
        t = jnp.broadcast_to(base[h:h + 1, :], (CHUNK, ROLL_W))
        for bit in range(6):
            t = jnp.where(((qi >> bit) & 1) == 1, pltpu.roll(t, 1 << bit, 1), t)
        for rr in range(TM // CHUNK):
            shifted = pltpu.roll(t, (CHUNK * rr - CHUNK) % ROLL_W, 1)[:, :TKW]
            band = kk - CHUNK * rr
            out_ref[h, rr * CHUNK:(rr + 1) * CHUNK, :] = jnp.where((band >= 0) & (band < BAND), shifted, NEG)


def _bias_grad(dtab):
    def body(dt_ref, out_ref, dbase_ref):
        qi = lax.broadcasted_iota(jnp.int32, (CHUNK, ROLL_W), 0)
        zeros = jnp.zeros((CHUNK, ROLL_W - TKW), F32)
        for h in range(N_HEADS):
            t = None
            for rr in range(TM // CHUNK):
                blk = jnp.concatenate([dt_ref[h, rr * CHUNK:(rr + 1) * CHUNK, :], zeros], axis=1)
                blk = pltpu.roll(blk, (CHUNK - CHUNK * rr) % ROLL_W, 1)
                t = blk if t is None else t + blk
            for bit in range(6):
                t = jnp.where(((qi >> bit) & 1) == 1, pltpu.roll(t, ROLL_W - (1 << bit), 1), t)
            dbase_ref[h:h + 1, :] = jnp.sum(t, axis=0, keepdims=True)
        onehot = _rel_onehot()
        acc = None
        for term in _split3(dbase_ref[...]):
            t = _dot_nt(term, onehot)
            acc = t if acc is None else acc + t
        out_ref[...] = acc

    return pl.pallas_call(
        body, name="bias_grad", out_shape=jax.ShapeDtypeStruct((N_HEADS, REL_PAD), F32),
        scratch_shapes=[pltpu.VMEM((N_HEADS, ROLL_W), F32)],
        compiler_params=pltpu.CompilerParams(vmem_limit_bytes=VMEM_LIMIT),
    )(dtab)


def _att_in(x1, g_pre, w_in):
    n_tok = x1.shape[0]
    tm = TM_WIDE
    nt = n_tok // tm
    lead = PAD // tm
    per_block = W_BLOCK // PAIR

    def body(x_ref, g_ref, w_ref, q_ref, k_ref, v_ref, z_ref):
        i = pl.program_id(0)

        @pl.when(i < lead)
        def _():
            k_ref[...] = jnp.zeros_like(k_ref)
            v_ref[...] = jnp.zeros_like(v_ref)

        @pl.when(i >= lead)
        def _():
            _, xhat = _rms_fwd(x_ref[...])
            h = (xhat * g_ref[...]).astype(MXU)
            for j in range(N_DEV):
                u = _dot(h, w_ref[j])
                if j >= 6:
                    z_ref[:, (j % 2) * W_BLOCK:(j % 2 + 1) * W_BLOCK] = u
                    continue
                dst = (q_ref, k_ref, v_ref)[j // 2]
                if j < 2:
                    u = u * QK_SCALE
                for pp in range(per_block):
                    dst[(j % 2) * per_block + pp] = u[:, pp * PAIR:(pp + 1) * PAIR].astype(MXU)

    late = pl.BlockSpec((tm, D_MODEL), lambda i: (jnp.maximum(i - lead, 0), 0))
    late3 = pl.BlockSpec((N_PAIRS, tm, PAIR), lambda i: (0, jnp.maximum(i - lead, 0), 0))
    padded3 = pl.BlockSpec((N_PAIRS, tm, PAIR), lambda i: (0, i, 0))
    return pl.pallas_call(
        body, name="att_in", grid=(nt + lead,),
        in_specs=[late, _const_spec((1, D_MODEL)), _const_spec((N_DEV, D_MODEL, W_BLOCK))],
        out_specs=[late3, padded3, padded3, late],
        out_shape=[jax.ShapeDtypeStruct((N_PAIRS, n_tok, PAIR), MXU),
                   jax.ShapeDtypeStruct((N_PAIRS, n_tok + PAD, PAIR), MXU),
                   jax.ShapeDtypeStruct((N_PAIRS, n_tok + PAD, PAIR), MXU),
                   jax.ShapeDtypeStruct((n_tok, ATT_WIDTH), F32)],
        compiler_params=_params(1),
    )(x1, g_pre, w_in)


def _head_masks():
    lane = lax.broadcasted_iota(jnp.int32, (1, PAIR), 1)
    first = lane < HEAD_DIM
    return first, jnp.logical_not(first)


def _pair_loop(one_pair, n, unroll):
    def loop_pass(t, carry):
        for u in range(unroll):
            one_pair(t * unroll + u)
        return carry

    lax.fori_loop(0, n // unroll, loop_pass, 0)


def _scores(qh, k_refs, bias_ref, p, hh, tile, masked):
    ss = []
    for b in range(KB):
        s = _dot_nt(qh, k_refs[b][p]) + bias_ref[2 * p + hh, :, b * TM:(b + 1) * TM]
        if masked:
            s = s + jnp.where(tile + b < PAD // TM, NEG, 0.0).astype(F32)
        ss.append(s)
    return ss


def _pair_specs(n, index_map):
    return pl.BlockSpec((n, TM, PAIR), index_map)


def _att_fwd(q, kpad, vpad, bias_tab):
    n_tok = q.shape[1]
    nt = n_tok // TM
    lead = PAD // TM

    def body(q_ref, k0, k1, k2, v0, v1, v2, bias_ref, o_ref, lse_ref):
        i = pl.program_id(1)
        masks = _head_masks()

        def pairs(masked):
            def one_pair(p):
                qv = q_ref[p]
                outs, ms = [], []
                for hh, mask in enumerate(masks):
                    qh = jnp.where(mask, qv, jnp.zeros_like(qv))
                    ss = _scores(qh, (k0, k1, k2), bias_ref, p, hh, i, masked)
                    m = None
                    for s in ss:
                        mb = jnp.max(s, axis=-1, keepdims=True)
                        m = mb if m is None else jnp.maximum(m, mb)
                    out = None
                    for b, v_ref in enumerate((v0, v1, v2)):
                        vb = v_ref[p]
                        t = _dot(jnp.exp(ss[b] - m).astype(MXU), jnp.where(mask, vb, jnp.ones_like(vb)))
                        out = t if out is None else out + t
                    outs.append(out)
                    ms.append(m)
                num = jnp.where(masks[0], outs[0], outs[1])
                den = jnp.where(masks[0], pltpu.roll(outs[0], HEAD_DIM, 1), pltpu.roll(outs[1], HEAD_DIM, 1))
                o_ref[p] = num / den
                lse_ref[p] = jnp.where(masks[0], ms[0], ms[1]) + jnp.log(den)

            _pair_loop(one_pair, FWD_PAIRS, FWD_UNROLL)

        @pl.when(i < lead)
        def _():
            pairs(True)

        @pl.when(i >= lead)
        def _():
            pairs(False)

    qspec = _pair_specs(FWD_PAIRS, lambda g, i: (g, i, 0))
    kspecs = [_pair_specs(FWD_PAIRS, functools.partial(lambda g, i, b: (g, i + b, 0), b=b)) for b in range(KB)]
    vspecs = [_pair_specs(FWD_PAIRS, functools.partial(lambda g, i, b: (g, i + b, 0), b=b)) for b in range(KB)]
    shape = jax.ShapeDtypeStruct((N_PAIRS, n_tok, PAIR), F32)
    return pl.pallas_call(
        body, name="att_fwd", grid=(N_PAIRS // FWD_PAIRS, nt),
        in_specs=[qspec] + kspecs + vspecs
        + [pl.BlockSpec((2 * FWD_PAIRS, TM, TKW), lambda g, i: (g, 0, 0), pipeline_mode=pl.Buffered(1))],
        out_specs=[qspec, qspec], out_shape=[shape, shape],
        compiler_params=_params(2),
    )(q, kpad, kpad, kpad, vpad, vpad, vpad, bias_tab)


def _att_bwd(q, kpad, vpad, do, o, lse, bias_tab):
    n_tok = q.shape[1]
    nt = n_tok // TM
    lead = PAD // TM

    def body(q_ref, do_ref, o_ref, lse_ref, k0, k1, k2, v0, v1, v2, bias_ref,
             dq_ref, dk_ref, dv_ref, dtab_ref, rk0, rk1, rv0, rv1):
        i = pl.program_id(1)
        masks = _head_masks()

        @pl.when(i == 0)
        def _():
            for ref in (rk0, rk1, rv0, rv1):
                ref[...] = jnp.zeros_like(ref)
            dtab_ref[...] = jnp.zeros_like(dtab_ref)

        def pairs(masked):
            def one_pair(p):
                qv = q_ref[p]
                dov = do_ref[p]
                doo = dov.astype(F32) * o_ref[p]
                lse_pair = lse_ref[p]
                dks = [None] * KB
                dvs = [None] * KB
                dqs = []
                for hh, mask in enumerate(masks):
                    qh = jnp.where(mask, qv, jnp.zeros_like(qv))
                    doh = jnp.where(mask, dov, jnp.zeros_like(dov))
                    dsum = jnp.sum(jnp.where(mask, doo, 0.0), axis=-1, keepdims=True)
                    lse_h = lse_pair[:, hh * HEAD_DIM:hh * HEAD_DIM + 1]
                    ss = _scores(qh, (k0, k1, k2), bias_ref, p, hh, i, masked)
                    dq = None
                    for b, (k_ref, v_ref) in enumerate(zip((k0, k1, k2), (v0, v1, v2))):
                        prob = jnp.exp(ss[b] - lse_h)
                        ds = prob * (_dot_nt(doh, v_ref[p]) - dsum)
                        dtab_ref[2 * p + hh, :, b * TM:(b + 1) * TM] += ds
                        dsb = ds.astype(MXU)
                        t = lax.dot_general(k_ref[p], dsb, (((0,), (1,)), ((), ())), preferred_element_type=F32)
                        dq = t if dq is None else dq + t
                        t = _dot_tn(qh, dsb)
                        dks[b] = t if dks[b] is None else dks[b] + t
                        t = _dot_tn(doh, prob.astype(MXU))
                        dvs[b] = t if dvs[b] is None else dvs[b] + t
                    dqs.append(dq)
                first_rows = lax.broadcasted_iota(jnp.int32, (PAIR, 1), 0) < HEAD_DIM
                dq_ref[p] = (jnp.where(first_rows, dqs[0], dqs[1]).T * QK_SCALE).astype(MXU)
                dk_ref[p] = (rk0[p] + dks[0].T).astype(MXU)
                dv_ref[p] = (rv0[p] + dvs[0].T).astype(MXU)
                rk0[p] = rk1[p] + dks[1].T
                rv0[p] = rv1[p] + dvs[1].T
                rk1[p] = dks[2].T
                rv1[p] = dvs[2].T

            _pair_loop(one_pair, BWD_PAIRS, BWD_UNROLL)

        @pl.when(i < lead)
        def _():
            pairs(True)

        @pl.when((i >= lead) & (i < nt))
        def _():
            pairs(False)

        @pl.when(i >= nt)
        def _():
            dk_ref[...] = rk0[...].astype(MXU)
            dv_ref[...] = rv0[...].astype(MXU)
            rk0[...] = rk1[...]
            rv0[...] = rv1[...]

    last = nt - 1
    clamped = lambda b: functools.partial(lambda g, i, b: (g, jnp.minimum(i, last) + b, 0), b=b)
    qspec = _pair_specs(BWD_PAIRS, clamped(0))
    kspecs = [_pair_specs(BWD_PAIRS, clamped(b)) for b in range(KB)]
    vspecs = [_pair_specs(BWD_PAIRS, clamped(b)) for b in range(KB)]
    pspec = _pair_specs(BWD_PAIRS, lambda g, i: (g, i, 0))
    tspec = pl.BlockSpec((2 * BWD_PAIRS, TM, TKW), lambda g, i: (g, 0, 0))
    ring = pltpu.VMEM((BWD_PAIRS, TM, PAIR), F32)
    return pl.pallas_call(
        body, name="att_bwd", grid=(N_PAIRS // BWD_PAIRS, nt + KB - 1),
        in_specs=[qspec, qspec, qspec, qspec] + kspecs + vspecs + [tspec],
        out_specs=[qspec, pspec, pspec, tspec],
        out_shape=[jax.ShapeDtypeStruct((N_PAIRS, n_tok, PAIR), MXU),
                   jax.ShapeDtypeStruct((N_PAIRS, n_tok + PAD, PAIR), MXU),
                   jax.ShapeDtypeStruct((N_PAIRS, n_tok + PAD, PAIR), MXU),
                   jax.ShapeDtypeStruct((N_HEADS, TM, TKW), F32)],
        scratch_shapes=[ring, ring, ring, ring],
        compiler_params=_params(2),
    )(q, do, o, lse, kpad, kpad, kpad, vpad, vpad, vpad, bias_tab)


def _att_out(o, z, x1, target, g_post, w_out):
    n_tok = z.shape[0]
    nt = n_tok // TM_WIDE

    def body(o_ref, z_ref, x1_ref, tgt_ref, gpost_ref, w_ref,
             loss_ref, dx2_ref, do_ref, dz_ref, dgpost_ref, dw_hbm, acc_ref, loss_acc, stage_ref):
        i = pl.program_id(0)

        @pl.when(i == 0)
        def _():
            acc_ref[...] = jnp.zeros_like(acc_ref)
            loss_acc[...] = jnp.zeros_like(loss_acc)
            dgpost_ref[...] = jnp.zeros_like(dgpost_ref)

        ov = jnp.concatenate([o_ref[p] for p in range(N_PAIRS)], axis=1)
        zv = z_ref[...]
        sig = _sigmoid(zv)
        silu = zv * sig
        gated = (ov * silu).astype(MXU)
        y = _dot(gated, w_ref[...])
        r, yhat = _rms_fwd(y)
        gpost = gpost_ref[...]
        diff = x1_ref[...] + yhat * gpost - tgt_ref[...]
        loss_acc[...] += jnp.sum(diff * diff, axis=0, keepdims=True)
        dn = diff * (1.0 / D_MODEL)
        dx2_ref[...] = dn
        dgpost_ref[...] += jnp.sum(dn * yhat, axis=0, keepdims=True)
        dy = _rms_bwd(dn, yhat, r, gpost).astype(MXU)
        for j in range(ATT_WIDTH // W_BLOCK):
            acc_ref[j] += _dot_tn(gated[:, j * W_BLOCK:(j + 1) * W_BLOCK], dy)
        dgated = _dot_nt(dy, w_ref[...])
        dob = (dgated * silu).astype(MXU)
        for p in range(N_PAIRS):
            do_ref[p] = dob[:, p * PAIR:(p + 1) * PAIR]
        dz_ref[...] = (dgated * ov * (sig * (1.0 + zv * (1.0 - sig)))).astype(MXU)

        @pl.when(i == nt - 1)
        def _():
            total = jnp.sum(loss_acc[...], axis=-1, keepdims=True) * (0.5 / D_MODEL)
            loss_ref[...] = jnp.broadcast_to(total, loss_ref.shape)
            _flush(acc_ref, dw_hbm, stage_ref)

    tok = pl.BlockSpec((TM_WIDE, D_MODEL), lambda i: (i, 0))
    tok3 = pl.BlockSpec((N_PAIRS, TM_WIDE, PAIR), lambda i: (0, i, 0))
    return pl.pallas_call(
        body, name="att_out", grid=(nt,),
        in_specs=[tok3, tok, tok, tok, _const_spec((1, D_MODEL)), _const_spec((ATT_WIDTH, D_MODEL))],
        out_specs=[pl.BlockSpec((1, 128), lambda i: (0, 0)), tok, tok3, tok,
                   pl.BlockSpec((1, D_MODEL), lambda i: (0, 0)), pl.BlockSpec(memory_space=pl.ANY)],
        out_shape=[jax.ShapeDtypeStruct((1, 128), F32), jax.ShapeDtypeStruct((n_tok, D_MODEL), F32),
                   jax.ShapeDtypeStruct((N_PAIRS, n_tok, PAIR), MXU), jax.ShapeDtypeStruct((n_tok, ATT_WIDTH), MXU),
                   jax.ShapeDtypeStruct((1, D_MODEL), F32),
                   jax.ShapeDtypeStruct((ATT_WIDTH // W_BLOCK, W_BLOCK, D_MODEL), MXU)],
        scratch_shapes=[pltpu.VMEM((ATT_WIDTH // W_BLOCK, W_BLOCK, D_MODEL), F32), pltpu.VMEM((1, D_MODEL), F32),
                        pltpu.VMEM((W_BLOCK, D_MODEL), MXU)],
        compiler_params=_params(1),
    )(o, z, x1, target, g_post, w_out)


def _adamw(parts, w, m, v, name):
    rows, cols = w.shape
    tr = min(rows, 512)

    def body(p_ref, w_ref, m_ref, v_ref, g_ref, d_ref, mo_ref, vo_ref):
        g = p_ref[0].astype(F32)
        for s in range(1, parts.shape[0]):
            g = g + p_ref[s].astype(F32)
        m_new = ADAM_B1 * m_ref[...] + (1.0 - ADAM_B1) * g
        v_new = ADAM_B2 * v_ref[...] + (1.0 - ADAM_B2) * (g * g)
        m_hat = m_new / (1.0 - ADAM_B1 ** ADAM_STEP)
        v_hat = v_new / (1.0 - ADAM_B2 ** ADAM_STEP)
        g_ref[...] = g
        d_ref[...] = -ADAM_LR * (m_hat / (jnp.sqrt(v_hat) + ADAM_EPS) + ADAM_WD * w_ref[...])
        mo_ref[...] = m_new
        vo_ref[...] = v_new

    blk = pl.BlockSpec((tr, cols), lambda i: (i, 0))
    shape = jax.ShapeDtypeStruct((rows, cols), F32)
    return pl.pallas_call(
        body, name=name, grid=(rows // tr,),
        in_specs=[pl.BlockSpec((parts.shape[0], tr, cols), lambda i: (0, i, 0)), blk, blk, blk],
        out_specs=[blk, blk, blk, blk], out_shape=[shape, shape, shape, shape],
        compiler_params=_params(1),
    )(parts, w, m, v)


SMALL_ROWS = 16
LOSS_AT = (6, N_REL)


def _pack_small(norm_pre, norm_post, pool_scale, rel_bias_padded):
    return jnp.concatenate([norm_pre, norm_post, pool_scale.reshape(2, D_MODEL),
                            rel_bias_padded.reshape(SMALL_ROWS - 6, D_MODEL)], axis=0)


def _unpack_small(packed):
    rel = packed[6:].reshape(N_HEADS, REL_PAD)[:, :N_REL]
    return packed[0:2], packed[2:4], packed[4:6].reshape(1, POOL_WIDTH), rel.reshape(1, N_HEADS, N_REL)


def _pad_rel(rel_bias):
    return jnp.pad(rel_bias.reshape(N_HEADS, N_REL), ((0, 0), (0, REL_PAD - N_REL)))


def kernel(x, norm_pre, norm_post, pool_w_in, pool_w_group, pool_scale, pool_w_out, att_w_in, att_rel_bias, att_w_out, loss_target, m_norm_pre, m_norm_post, m_pool_w_in, m_pool_w_group, m_pool_scale, m_pool_w_out, m_att_w_in, m_att_rel_bias, m_att_w_out, v_norm_pre, v_norm_post, v_pool_w_in, v_pool_w_group, v_pool_scale, v_pool_w_out, v_att_w_in, v_att_rel_bias, v_att_w_out):
    xt = x[0]
    target = loss_target[0]
    n_tok = xt.shape[0]
    lead = PAD // TM_WIDE
    rows_g = GROUP // N_DEV

    rel_padded = _pad_rel(att_rel_bias[0])
    gathered = _gather_two_level([pool_w_in[0].astype(MXU), pool_w_group[0].astype(MXU), pool_w_out[0].astype(MXU)],
                                 rel_padded, "gather_pool_weights")
    bias_tab = gathered[3]
    w_in_p = gathered[0]
    w_group = gathered[1].transpose(1, 0, 2, 3).reshape(N_GROUPS, GROUP, GROUP)
    w_out_p = gathered[2].reshape(POOL_WIDTH, D_MODEL)

    x1, y0, z0, mixed, mg, prod, h0_t, w_in_a, w_out_a = _pool_fwd(
        xt, norm_pre[0:1], norm_post[0:1], w_in_p, w_group, pool_scale, w_out_p,
        [(att_w_in[0].astype(MXU), False), (att_w_out[0].astype(MXU), False)])
    w_out_a = w_out_a.reshape(ATT_WIDTH, D_MODEL)
    q, kpad, vpad, z1 = _att_in(x1, norm_pre[1:2], w_in_a)
    o, lse = _att_fwd(q, kpad, vpad, bias_tab)
    loss_part, dx2, do, dz1, d_gpost1, d_w_out_a = _att_out(o, z1, x1, target, norm_post[1:2], w_out_a)
    dq, dkpad, dvpad, dtab = _att_bwd(q, kpad, vpad, do, o, lse, bias_tab)
    d_rel = _bias_grad(dtab)
    pairs = (N_PAIRS, TM_WIDE, PAIR)
    flat = (TM_WIDE, D_MODEL)
    dx1, d_gpre1, d_w_in_a = _in_proj_bwd(
        [(dq, pairs, lambda i: (0, i, 0)), (dkpad, pairs, lambda i: (0, i + lead, 0)),
         (dvpad, pairs, lambda i: (0, i + lead, 0)), (dz1, flat, lambda i: (i, 0))],
        x1, dx2, norm_pre[1:2], w_in_a, "att_in_bwd", TM_WIDE, True, [])
    du0, d_scale, d_gpost0, d_w_group, d_w_out_p, part_w_in_a, part_w_out_a = _pool_bwd(
        dx1, y0, z0, mg, mixed, prod, norm_post[0:1], pool_scale, w_group, w_out_p,
        [(d_w_in_a, True), (d_w_out_a.reshape(N_DEV, ATT_WIDTH // N_DEV, D_MODEL), True)])
    col = lambda p: (lambda i: (i, p))
    grad_x, d_gpre0, part_w_group, part_w_out_p = _in_proj_bwd(
        [(du0, (TM_WIDE, D_MODEL), col(p)) for p in range(4)], xt, dx1, norm_pre[0:1], w_in_p, "pool_in_bwd", TM_WIDE,
        False,
        [(d_w_group.reshape(N_GROUPS, N_DEV, rows_g, GROUP).transpose(1, 0, 2, 3), True),
         (d_w_out_p.reshape(N_DEV, POOL_WIDTH // N_DEV, D_MODEL), True)])
    d_small = _pack_small(jnp.concatenate([d_gpre0, d_gpre1], axis=0), jnp.concatenate([d_gpost0, d_gpost1], axis=0),
                          d_scale, d_rel).at[LOSS_AT].set(loss_part[0, 0])
    part_w_in_p, part_small = _w_in_grad_scatter(h0_t, du0, [(d_small, False)], "pool_w_in_grad")

    def update(part, w, m, v, name):
        shape = w.shape
        flat = lambda a: a.reshape(-1, shape[-1])
        outs = _adamw(part.reshape(part.shape[0], -1, shape[-1]), flat(w), flat(m), flat(v), name)
        return [a.reshape(shape) for a in outs]

    u_att_w_in = update(part_w_in_a, att_w_in, m_att_w_in, v_att_w_in, "adamw_att_w_in")
    u_att_w_out = update(part_w_out_a, att_w_out, m_att_w_out, v_att_w_out, "adamw_att_w_out")
    u_pool_w_group = update(part_w_group, pool_w_group, m_pool_w_group, v_pool_w_group, "adamw_pool_w_group")
    u_pool_w_out = update(part_w_out_p, pool_w_out, m_pool_w_out, v_pool_w_out, "adamw_pool_w_out")
    u_pool_w_in = update(part_w_in_p, pool_w_in, m_pool_w_in, v_pool_w_in, "adamw_pool_w_in")
    small = _adamw(part_small, _pack_small(norm_pre, norm_post, pool_scale, rel_padded),
                   _pack_small(m_norm_pre, m_norm_post, m_pool_scale, _pad_rel(m_att_rel_bias[0])),
                   _pack_small(v_norm_pre, v_norm_post, v_pool_scale, _pad_rel(v_att_rel_bias[0])), "adamw_small")
    u_small = [_unpack_small(a) for a in small]

    loss = small[0][LOSS_AT]
    outs = [loss, grad_x.reshape(1, n_tok, D_MODEL)]
    for kind in range(4):
        outs += [u_small[kind][0], u_small[kind][1], u_pool_w_in[kind], u_pool_w_group[kind], u_small[kind][2],
                 u_pool_w_out[kind], u_att_w_in[kind], u_small[kind][3], u_att_w_out[kind]]
    return tuple(outs)
```
